```python
import math
import jax
import jax.numpy as jnp
from jax import lax
import numpy as np

D_MODEL = 1024
BATCH = 16
SEQ = 256
DEPTH = 4
DEC_BATCH = 2
DEC_SEQ = 1024
PAST_LEN = 512

GRID_W = 64
HEAD_DIM = 64
MIX_WIDTH = D_MODEL
A_WIDTH = MIX_WIDTH // 2
A_HEADS = A_WIDTH // HEAD_DIM
B_WIDTH = MIX_WIDTH - A_WIDTH
POOL_WINDOWS = (2, 4, 8, 16)
B_GROUPS = len(POOL_WINDOWS)
B_GROUP_DIM = B_WIDTH // B_GROUPS
NA_ROWS = 8
NA_COLS = 16
NA_QCOLS = 16
NA_SPAN = NA_QCOLS + NA_COLS
C_HEADS = MIX_WIDTH // HEAD_DIM
C_KV_HEADS = 4
C_Q_WIDTH = C_HEADS * HEAD_DIM
C_KV_WIDTH = C_KV_HEADS * HEAD_DIM
C_WINDOW = 128
C_BLOCK = 128
FFN_HIDDEN = -(-8 * D_MODEL // (3 * 256)) * 256
N_AB_LAYERS = (DEPTH + 1) // 2
N_C_LAYERS = DEPTH // 2
ROPE_BASE = 10000.0
ROPE_AXIS_DIM = HEAD_DIM // 2
EPS = 1e-6
NEG_INF = -1e30

kernel_name = 'hybrid_diffusion_natten_pool_swa_step'


def rms_norm(x, g):
    x32 = x.astype(jnp.float32)
    y = x32 * lax.rsqrt(jnp.mean(x32 * x32, axis=-1, keepdims=True) + EPS)
    return (y * g.astype(jnp.float32)).astype(x.dtype)


def modulate(h, shift, scale):
    return h * (1 + scale) + shift


def axial_rope(x):
    n = x.shape[1]
    t = jnp.arange(n)
    pos = jnp.stack([t // GRID_W, t % GRID_W], axis=-1).astype(jnp.float32)
    half = ROPE_AXIS_DIM // 2
    inv = ROPE_BASE ** (-jnp.arange(half, dtype=jnp.float32) / half)
    ang = pos[:, :, None] * inv
    cos = jnp.cos(ang)[None, :, None].astype(x.dtype)
    sin = jnp.sin(ang)[None, :, None].astype(x.dtype)
    xa = x.reshape(*x.shape[:-1], 2, 2, half)
    x1, x2 = xa[..., 0, :], xa[..., 1, :]
    out = jnp.stack([x1 * cos - x2 * sin, x1 * sin + x2 * cos], axis=-2)
    return out.reshape(x.shape)


def dense_attention(q, k, v, sink):
    bsz, s, hq, hd = q.shape
    hkv = k.shape[2]
    g = hq // hkv
    qg = (q * hd ** -0.5).reshape(bsz, s, hkv, g, hd)
    logits = jnp.einsum('bqkgd,bpkd->bkgqp', qg, k).astype(jnp.float32)
    if sink is None:
        p = jax.nn.softmax(logits, axis=-1)
    else:
        sl = jnp.broadcast_to(sink.reshape(hkv, g)[None, :, :, None, None].astype(jnp.float32),
                              logits.shape[:-1] + (1,))
        p = jax.nn.softmax(jnp.concatenate([sl, logits], axis=-1), axis=-1)[..., 1:]
    out = jnp.einsum('bkgqp,bpkd->bqkgd', p.astype(v.dtype), v)
    return out.reshape(bsz, s, hq * hd)


def neighbourhood_attention(q, k, v, rpb, k_ctx, v_ctx):
    bsz, n, h, hd = q.shape
    rows = n // GRID_W
    kr = min(NA_ROWS, rows)
    n_cb = GRID_W // NA_QCOLS
    r = jnp.arange(rows)
    row_idx = jnp.clip(r - kr // 2, 0, rows - kr)[:, None] + jnp.arange(kr)[None, :]
    cb = jnp.arange(n_cb)
    col_idx = (jnp.clip(cb * NA_QCOLS - NA_COLS // 2, 0, GRID_W - NA_SPAN)[:, None]
               + jnp.arange(NA_SPAN)[None, :])
    qcol = cb[:, None] * NA_QCOLS + jnp.arange(NA_QCOLS)[None, :]
    qstart = jnp.clip(qcol - NA_COLS // 2, 0, GRID_W - NA_COLS)
    key_col = col_idx[:, None, :]
    valid = (key_col >= qstart[..., None]) & (key_col < qstart[..., None] + NA_COLS)
    d_row = row_idx - r[:, None] + (NA_ROWS - 1)
    d_col = jnp.clip(key_col - qcol[..., None] + (NA_COLS - 1), 0, 2 * NA_COLS - 2)
    bias = rpb[:, d_row[:, None, None, :, None], d_col[None, :, :, None, :]]
    bias = jnp.where(valid[None, None, :, :, None, :], bias.astype(jnp.float32), NEG_INF)
    bias = bias.reshape(h, rows, n_cb, NA_QCOLS, kr * NA_SPAN).transpose(1, 2, 0, 3, 4)

    def gather_band(t):
        g = t.reshape(bsz, rows, GRID_W, h, hd)[:, row_idx]
        g = g[:, :, :, col_idx]
        return g.transpose(0, 1, 3, 2, 4, 5, 6).reshape(bsz, rows, n_cb, kr * NA_SPAN, h, hd)

    kb = gather_band(k)
    vb = gather_band(v)
    qb = (q * hd ** -0.5).reshape(bsz, rows, n_cb, NA_QCOLS, h, hd)
    s_nb = jnp.einsum('brcqhd,brckhd->brchqk', qb, kb).astype(jnp.float32) + bias[None]
    s_ctx = jnp.einsum('brcqhd,bphd->brchqp', qb, k_ctx).astype(jnp.float32)
    p = jax.nn.softmax(jnp.concatenate([s_ctx, s_nb], axis=-1), axis=-1).astype(v.dtype)
    n_ctx = k_ctx.shape[1]
    out = (jnp.einsum('brchqp,bphd->brcqhd', p[..., :n_ctx], v_ctx)
           + jnp.einsum('brchqk,brckhd->brcqhd', p[..., n_ctx:], vb))
    return out.reshape(bsz, n, h * hd)


def window_attention(q, k, v, sink, k_ctx, v_ctx):
    bsz, n, hq, hd = q.shape
    hkv = k.shape[2]
    g = hq // hkv
    nb = n // C_BLOCK
    pad = ((0, 0), (C_BLOCK, C_BLOCK), (0, 0), (0, 0))

    def band(t):
        tp = jnp.pad(t, pad).reshape(bsz, nb + 2, C_BLOCK, hkv, hd)
        return jnp.concatenate([tp[:, :nb], tp[:, 1:nb + 1], tp[:, 2:]], axis=2)

    kband = band(k)
    vband = band(v)
    qb = (q * hd ** -0.5).reshape(bsz, nb, C_BLOCK, hkv, g, hd)
    qpos = jnp.arange(n).reshape(nb, C_BLOCK)
    kpos = jnp.arange(nb)[:, None] * C_BLOCK - C_BLOCK + jnp.arange(3 * C_BLOCK)[None, :]
    valid = ((jnp.abs(qpos[:, :, None] - kpos[:, None, :]) <= C_WINDOW)
             & (kpos[:, None, :] >= 0) & (kpos[:, None, :] < n))
    s_nb = jnp.einsum('bnqkgd,bnjkd->bnkgqj', qb, kband).astype(jnp.float32)
    s_nb = jnp.where(valid[None, :, None, None], s_nb, NEG_INF)
    s_ctx = jnp.einsum('bnqkgd,bpkd->bnkgqp', qb, k_ctx).astype(jnp.float32)
    sl = jnp.broadcast_to(sink.reshape(hkv, g)[None, None, :, :, None, None].astype(jnp.float32),
                          s_nb.shape[:-1] + (1,))
    p = jax.nn.softmax(jnp.concatenate([sl, s_ctx, s_nb], axis=-1), axis=-1).astype(v.dtype)
    n_ctx = k_ctx.shape[1]
    out = (jnp.einsum('bnkgqp,bpkd->bnqkgd', p[..., 1:1 + n_ctx], v_ctx)
           + jnp.einsum('bnkgqj,bnjkd->bnqkgd', p[..., 1 + n_ctx:], vband))
    return out.reshape(bsz, n, hq * hd)


def multiscale_pool(u, w_pool, pool_scale):
    bsz, n, _ = u.shape
    ug = u.reshape(bsz, n, B_GROUPS, B_GROUP_DIM).astype(jnp.float32)
    csum = jnp.concatenate([jnp.zeros_like(ug[:, :1]), jnp.cumsum(ug, axis=1)], axis=1)
    win = jnp.array(POOL_WINDOWS, dtype=jnp.int32)
    t = jnp.arange(n, dtype=jnp.int32)[:, None]
    lo = jnp.clip(t - win // 2, 0, n)
    hi = jnp.clip(t - win // 2 + win, 0, n)
    gi = jnp.arange(B_GROUPS)
    total = csum[:, hi, gi] - csum[:, lo, gi]
    count = (hi - lo).astype(jnp.float32)[None, :, :, None]
    pooled = (total / count - ug).astype(u.dtype)
    y = jnp.einsum('bngc,gcd->bngd', pooled, w_pool)
    return y.reshape(bsz, n, B_WIDTH) * pool_scale


def split_ab(h, w_in):
    bsz, n, _ = h.shape
    proj = h @ w_in
    q = proj[..., :A_WIDTH].reshape(bsz, n, A_HEADS, HEAD_DIM)
    k = proj[..., A_WIDTH:2 * A_WIDTH].reshape(bsz, n, A_HEADS, HEAD_DIM)
    v = proj[..., 2 * A_WIDTH:3 * A_WIDTH].reshape(bsz, n, A_HEADS, HEAD_DIM)
    u = proj[..., 3 * A_WIDTH:]
    return q, k, v, u


def split_c(h, w_in):
    bsz, n, _ = h.shape
    proj = h @ w_in
    q = proj[..., :C_Q_WIDTH].reshape(bsz, n, C_HEADS, HEAD_DIM)
    k = proj[..., C_Q_WIDTH:C_Q_WIDTH + C_KV_WIDTH].reshape(bsz, n, C_KV_HEADS, HEAD_DIM)
    v = proj[..., C_Q_WIDTH + C_KV_WIDTH:].reshape(bsz, n, C_KV_HEADS, HEAD_DIM)
    return q, k, v


def swiglu(h, w_gate_up, w_down):
    gu = h @ w_gate_up
    return (jax.nn.silu(gu[..., :FFN_HIDDEN]) * gu[..., FFN_HIDDEN:]) @ w_down


def setup_inputs(seed: int = 0) -> dict:
    key = jax.random.key(seed)
    ks = jax.random.split(key, 24)

    def nrm(k, shape, scale):
        return jax.random.normal(k, shape, jnp.float32) * scale

    return {
        'x_prompt': nrm(ks[0], (BATCH, SEQ, D_MODEL), 1.0),
        'x_sample': nrm(ks[1], (DEC_BATCH, DEC_SEQ, D_MODEL), 1.0),
        'cache_a_k': nrm(ks[2], (DEC_BATCH, N_AB_LAYERS, PAST_LEN, A_HEADS, HEAD_DIM), 1.0),
        'cache_a_v': nrm(ks[3], (DEC_BATCH, N_AB_LAYERS, PAST_LEN, A_HEADS, HEAD_DIM), 1.0),
        'cache_c_k': nrm(ks[4], (DEC_BATCH, N_C_LAYERS, PAST_LEN, C_KV_HEADS, HEAD_DIM), 1.0),
        'cache_c_v': nrm(ks[5], (DEC_BATCH, N_C_LAYERS, PAST_LEN, C_KV_HEADS, HEAD_DIM), 1.0),
        'c': nrm(ks[6], (DEC_BATCH, D_MODEL), 1.0),
        'c_ctx': nrm(ks[7], (D_MODEL,), 1.0),
        'w_mod': nrm(ks[8], (DEPTH, D_MODEL, 6 * D_MODEL), D_MODEL ** -0.5),
        'b_mod': nrm(ks[9], (DEPTH, 6 * D_MODEL), 0.02),
        'norm_mix': 1.0 + nrm(ks[10], (DEPTH, D_MODEL), 0.05),
        'norm_ffn': 1.0 + nrm(ks[11], (DEPTH, D_MODEL), 0.05),
        'w_in_ab': nrm(ks[12], (N_AB_LAYERS, D_MODEL, 3 * A_WIDTH + B_WIDTH), D_MODEL ** -0.5),
        'rpb_a': nrm(ks[13], (N_AB_LAYERS, A_HEADS, 2 * NA_ROWS - 1, 2 * NA_COLS - 1), 0.5),
        'w_pool': nrm(ks[14], (N_AB_LAYERS, B_GROUPS, B_GROUP_DIM, B_GROUP_DIM), B_GROUP_DIM ** -0.5),
        'pool_scale': 1.0 + nrm(ks[15], (N_AB_LAYERS, B_WIDTH), 0.05),
        'w_out_ab': nrm(ks[16], (N_AB_LAYERS, MIX_WIDTH, D_MODEL), MIX_WIDTH ** -0.5),
        'w_in_c': nrm(ks[17], (N_C_LAYERS, D_MODEL, C_Q_WIDTH + 2 * C_KV_WIDTH), D_MODEL ** -0.5),
        'sink_c': nrm(ks[18], (N_C_LAYERS, C_HEADS), 1.0),
        'w_out_c': nrm(ks[19], (N_C_LAYERS, C_Q_WIDTH, D_MODEL), C_Q_WIDTH ** -0.5),
        'w_gate_up': nrm(ks[20], (DEPTH, D_MODEL, 2 * FFN_HIDDEN), D_MODEL ** -0.5),
        'w_down': nrm(ks[21], (DEPTH, FFN_HIDDEN, D_MODEL), FFN_HIDDEN ** -0.5),
        'norm_final': 1.0 + nrm(ks[22], (D_MODEL,), 0.05),
    }


def reference(x_prompt, x_sample, cache_a_k, cache_a_v, cache_c_k, cache_c_v, c, c_ctx,
              w_mod, b_mod, norm_mix, norm_ffn, w_in_ab, rpb_a, w_pool, pool_scale, w_out_ab,
              w_in_c, sink_c, w_out_c, w_gate_up, w_down, norm_final):
    xp = x_prompt
    xs = x_sample
    a_k_list, a_v_list, c_k_list, c_v_list = [], [], [], []
    for l in range(DEPTH):
        mod_p = jax.nn.silu(c_ctx) @ w_mod[l] + b_mod[l]
        mod_s = (jax.nn.silu(c) @ w_mod[l] + b_mod[l])[:, None, :]
        sh1_p, sc1_p, g1_p, sh2_p, sc2_p, g2_p = jnp.split(mod_p, 6, axis=-1)
        sh1_s, sc1_s, g1_s, sh2_s, sc2_s, g2_s = jnp.split(mod_s, 6, axis=-1)

        hp = modulate(rms_norm(xp, norm_mix[l]), sh1_p, sc1_p)
        hs = modulate(rms_norm(xs, norm_mix[l]), sh1_s, sc1_s)
        if l % 2 == 0:
            i = l // 2
            qp, kp, vp, up = split_ab(hp, w_in_ab[i])
            op = jnp.concatenate([dense_attention(qp, kp, vp, None),
                                  multiscale_pool(up, w_pool[i], pool_scale[i])], axis=-1) @ w_out_ab[i]
            a_k_list.append(kp)
            a_v_list.append(vp)
            qs, ks_, vs, us = split_ab(hs, w_in_ab[i])
            os_ = jnp.concatenate([neighbourhood_attention(qs, ks_, vs, rpb_a[i], cache_a_k[:, i], cache_a_v[:, i]),
                                   multiscale_pool(us, w_pool[i], pool_scale[i])], axis=-1) @ w_out_ab[i]
        else:
            j = l // 2
            qp, kp, vp = split_c(hp, w_in_c[j])
            op = dense_attention(qp, kp, vp, sink_c[j]) @ w_out_c[j]
            c_k_list.append(kp)
            c_v_list.append(vp)
            qs, ks_, vs = split_c(hs, w_in_c[j])
            os_ = window_attention(axial_rope(qs), axial_rope(ks_), vs, sink_c[j],
                                   cache_c_k[:, j], cache_c_v[:, j]) @ w_out_c[j]
        xp = xp + g1_p * op
        xs = xs + g1_s * os_

        hp = modulate(rms_norm(xp, norm_ffn[l]), sh2_p, sc2_p)
        hs = modulate(rms_norm(xs, norm_ffn[l]), sh2_s, sc2_s)
        xp = xp + g2_p * swiglu(hp, w_gate_up[l], w_down[l])
        xs = xs + g2_s * swiglu(hs, w_gate_up[l], w_down[l])

    y_prompt = rms_norm(xp, norm_final)
    y_sample = rms_norm(xs, norm_final)
    new_a_k = jnp.stack(a_k_list, axis=1)
    new_a_v = jnp.stack(a_v_list, axis=1)
    new_c_k = jnp.stack(c_k_list, axis=1)
    new_c_v = jnp.stack(c_v_list, axis=1)
    return (y_prompt, y_sample, new_a_k, new_a_v, new_c_k, new_c_v)
```

```python
import functools

import jax
import jax.numpy as jnp
from jax import lax
from jax.experimental import pallas as pl
from jax.experimental.pallas import tpu as pltpu

D_MODEL = 1024
BATCH = 16
SEQ = 256
DEPTH = 4
DEC_BATCH = 2
DEC_SEQ = 1024
PAST_LEN = 512
GRID_W = 64
HEAD_DIM = 64
A_WIDTH = 512
A_HEADS = 8
B_WIDTH = 512
POOL_WINDOWS = (2, 4, 8, 16)
B_GROUP_DIM = 128
NA_ROWS = 8
NA_COLS = 16
C_HEADS = 16
C_KV_HEADS = 4
C_GROUP = C_HEADS // C_KV_HEADS
C_Q_WIDTH = 1024
C_KV_WIDTH = 256
C_BLOCK = 128
FFN_HIDDEN = 2816
ROPE_BASE = 10000.0
EPS = 1e-6
NEG_INF = -1e30

N_CTX_TOK = BATCH * SEQ
N_LAT_TOK = DEC_BATCH * DEC_SEQ
N_TOK = N_CTX_TOK + N_LAT_TOK
GRID_ROWS = DEC_SEQ // GRID_W
N_GROUPS_PAD = 8

VMEM_LIMIT = 56 * 1024 * 1024

F32 = jnp.float32
BF16 = jnp.bfloat16


def _params(n_axes):
    return pltpu.CompilerParams(dimension_semantics=("arbitrary",) * n_axes,
                                vmem_limit_bytes=VMEM_LIMIT)


def _group_of_tile(i, tm):
    row0 = i * tm
    return jnp.where(row0 < N_CTX_TOK, 0, 1 + (row0 - N_CTX_TOK) // DEC_SEQ)


def _dot_nt(a, b):
    return lax.dot_general(a, b, (((1,), (1,)), ((), ())), preferred_element_type=F32)


def _dot(a, b):
    return jnp.dot(a, b, preferred_element_type=F32)


def _mod_kernel(cond_ref, w_ref, b_ref, o_ref):
    c = cond_ref[...]
    s = c / (1.0 + jnp.exp(-c))
    o_ref[...] = _dot(s.astype(BF16), w_ref[...].astype(BF16)) + b_ref[...]


def _modulation(cond8, w_mod, b_mod):
    tn = 1536
    return pl.pallas_call(
        _mod_kernel,
        out_shape=jax.ShapeDtypeStruct((DEPTH, N_GROUPS_PAD, 6 * D_MODEL), F32),
        grid=(DEPTH, 6 * D_MODEL // tn),
        in_specs=[
            pl.BlockSpec((N_GROUPS_PAD, D_MODEL), lambda l, j: (0, 0)),
            pl.BlockSpec((None, D_MODEL, tn), lambda l, j: (l, 0, j)),
            pl.BlockSpec((None, 1, tn), lambda l, j: (l, 0, j)),
        ],
        out_specs=pl.BlockSpec((None, N_GROUPS_PAD, tn), lambda l, j: (l, 0, j)),
        compiler_params=_params(2),
        name="modulation",
    )(cond8, w_mod, b_mod.reshape(DEPTH, 1, 6 * D_MODEL))


def _norm_mod(x, g, mod_ref, shift_idx):
    var = jnp.mean(x * x, axis=-1, keepdims=True)
    y = x * lax.rsqrt(var + EPS) * g
    shift = mod_ref[:, shift_idx * D_MODEL:(shift_idx + 1) * D_MODEL]
    scale = mod_ref[:, (shift_idx + 1) * D_MODEL:(shift_idx + 2) * D_MODEL]
    return y * (1.0 + scale) + shift


def _inproj_kernel(x_ref, g_ref, mod_ref, w_ref, o_ref, wbf_ref):
    @pl.when(pl.program_id(0) == 0)
    def _():
        wbf_ref[...] = w_ref[...].astype(BF16)

    h = _norm_mod(x_ref[...], g_ref[...], mod_ref, 0)
    o_ref[...] = _dot(h.astype(BF16), wbf_ref[...])


def _inproj(x, g, mod4, layer, w):
    tm = 512
    n_out = w.shape[1]
    return pl.pallas_call(
        _inproj_kernel,
        out_shape=jax.ShapeDtypeStruct((N_TOK, n_out), F32),
        grid=(N_TOK // tm,),
        in_specs=[
            pl.BlockSpec((tm, D_MODEL), lambda i: (i, 0)),
            pl.BlockSpec((1, D_MODEL), lambda i: (0, 0)),
            pl.BlockSpec((None, None, 1, 6 * D_MODEL),
                         lambda i: (layer, _group_of_tile(i, tm), 0, 0)),
            pl.BlockSpec((D_MODEL, n_out), lambda i: (0, 0)),
        ],
        out_specs=pl.BlockSpec((tm, n_out), lambda i: (i, 0)),
        scratch_shapes=[pltpu.VMEM((D_MODEL, n_out), BF16)],
        compiler_params=_params(1),
        name="inproj",
    )(x, g.reshape(1, D_MODEL), mod4, w)


def _ctx_attn_a_kernel(p_ref, o_ref):
    for h in range(A_HEADS):
        c0 = h * HEAD_DIM
        q = (p_ref[:, c0:c0 + HEAD_DIM] * HEAD_DIM ** -0.5).astype(BF16)
        k = p_ref[:, A_WIDTH + c0:A_WIDTH + c0 + HEAD_DIM].astype(BF16)
        v = p_ref[:, 2 * A_WIDTH + c0:2 * A_WIDTH + c0 + HEAD_DIM].astype(BF16)
        s = _dot_nt(q, k)
        m = jnp.max(s, axis=-1, keepdims=True)
        p = jnp.exp(s - m)
        l = jnp.sum(p, axis=-1, keepdims=True)
        o_ref[:, c0:c0 + HEAD_DIM] = _dot(p.astype(BF16), v) / l


def _ctx_attn_a(proj):
    return pl.pallas_call(
        _ctx_attn_a_kernel,
        out_shape=jax.ShapeDtypeStruct((N_CTX_TOK, A_WIDTH), F32),
        grid=(BATCH,),
        in_specs=[pl.BlockSpec((SEQ, 4 * A_WIDTH), lambda b: (b, 0))],
        out_specs=pl.BlockSpec((SEQ, A_WIDTH), lambda b: (b, 0)),
        compiler_params=_params(1),
        name="ctx_attn_a",
    )(proj)


def _na_bias_kernel(rpb_ref, o_ref):
    h = pl.program_id(0)
    qi = lax.broadcasted_iota(jnp.int32, (GRID_W, GRID_W), 0)
    kc = lax.broadcasted_iota(jnp.int32, (GRID_W, GRID_W), 1)
    rel = kc - qi + (NA_COLS - 1)
    qstart = jnp.clip(qi - NA_COLS // 2, 0, GRID_W - NA_COLS)
    valid = (kc >= qstart) & (kc < qstart + NA_COLS)
    n_dr = 2 * NA_ROWS - 1
    n_dc = 2 * NA_COLS - 1
    for dr in range(n_dr):
        acc = jnp.zeros((GRID_W, GRID_W), F32)
        for d in range(n_dc):
            acc = jnp.where(rel == d, rpb_ref[(h * n_dr + dr) * n_dc + d], acc)
        o_ref[dr] = jnp.where(valid, acc, NEG_INF)


def _na_bias_tiles(rpb):
    n_dr = 2 * NA_ROWS - 1
    return pl.pallas_call(
        _na_bias_kernel,
        out_shape=jax.ShapeDtypeStruct((A_HEADS, n_dr, GRID_W, GRID_W), F32),
        grid=(A_HEADS,),
        in_specs=[pl.BlockSpec(memory_space=pltpu.SMEM)],
        out_specs=pl.BlockSpec((None, n_dr, GRID_W, GRID_W), lambda h: (h, 0, 0, 0)),
        compiler_params=_params(1),
        name="na_bias",
    )(rpb.reshape(-1))


def _na_bias_table(rpb):
    tiles = _na_bias_tiles(rpb)
    t = jnp.stack([tiles[:, s:s + NA_ROWS] for s in range(NA_ROWS)])
    return t.transpose(0, 1, 3, 2, 4).reshape(NA_ROWS, A_HEADS, GRID_W, NA_ROWS * GRID_W)


def _lat_attn_a_kernel(q_ref, k_ref, v_ref, kc_ref, vc_ref, bias_ref, o_ref):
    r = pl.program_id(1)
    start = pl.multiple_of(jnp.clip(r - NA_ROWS // 2, 0, GRID_ROWS - NA_ROWS) * GRID_W, GRID_W)
    n_loc = NA_ROWS * GRID_W
    for h in range(A_HEADS):
        c0 = h * HEAD_DIM
        q = (q_ref[:, c0:c0 + HEAD_DIM] * HEAD_DIM ** -0.5).astype(BF16)
        kc = kc_ref[:, c0:c0 + HEAD_DIM].astype(BF16)
        vc = vc_ref[:, c0:c0 + HEAD_DIM].astype(BF16)
        kl = k_ref[pl.ds(start, n_loc), c0:c0 + HEAD_DIM].astype(BF16)
        vl = v_ref[pl.ds(start, n_loc), c0:c0 + HEAD_DIM].astype(BF16)
        s_c = _dot_nt(q, kc)
        s_l = _dot_nt(q, kl) + bias_ref[h]
        m = jnp.maximum(jnp.max(s_c, axis=-1, keepdims=True),
                        jnp.max(s_l, axis=-1, keepdims=True))
        p_c = jnp.exp(s_c - m)
        p_l = jnp.exp(s_l - m)
        l = jnp.sum(p_c, axis=-1, keepdims=True) + jnp.sum(p_l, axis=-1, keepdims=True)
        o = _dot(p_c.astype(BF16), vc) + _dot(p_l.astype(BF16), vl)
        o_ref[:, c0:c0 + HEAD_DIM] = o / l


def _lat_attn_a(proj, cache_k, cache_v, bias_table, i_layer):
    q_row0 = N_CTX_TOK // GRID_W
    kv_row0 = N_CTX_TOK // DEC_SEQ

    def bias_index(b, r):
        return (jnp.clip(r - NA_ROWS // 2, 0, GRID_ROWS - NA_ROWS) - r + NA_ROWS - 1, 0, 0, 0)

    return pl.pallas_call(
        _lat_attn_a_kernel,
        out_shape=jax.ShapeDtypeStruct((N_LAT_TOK, A_WIDTH), F32),
        grid=(DEC_BATCH, GRID_ROWS),
        in_specs=[
            pl.BlockSpec((GRID_W, A_WIDTH), lambda b, r: (q_row0 + b * GRID_ROWS + r, 0)),
            pl.BlockSpec((DEC_SEQ, A_WIDTH), lambda b, r: (kv_row0 + b, 1)),
            pl.BlockSpec((DEC_SEQ, A_WIDTH), lambda b, r: (kv_row0 + b, 2)),
            pl.BlockSpec((None, None, PAST_LEN, A_WIDTH), lambda b, r: (b, i_layer, 0, 0)),
            pl.BlockSpec((None, None, PAST_LEN, A_WIDTH), lambda b, r: (b, i_layer, 0, 0)),
            pl.BlockSpec((None, A_HEADS, GRID_W, NA_ROWS * GRID_W), bias_index),
        ],
        out_specs=pl.BlockSpec((GRID_W, A_WIDTH), lambda b, r: (b * GRID_ROWS + r, 0)),
        compiler_params=_params(2),
        name="lat_attn_a",
    )(proj, proj, proj, cache_k, cache_v, bias_table)


def _pool_kernel(u_ref, wp_ref, ps_ref, o_ref, *, n):
    t = lax.broadcasted_iota(jnp.int32, (n, n), 0)
    j = lax.broadcasted_iota(jnp.int32, (n, n), 1)
    tc = lax.broadcasted_iota(jnp.int32, (n, 1), 0)
    for g, w in enumerate(POOL_WINDOWS):
        c0 = g * B_GROUP_DIM
        lo = jnp.maximum(t - w // 2, 0)
        hi = jnp.minimum(t - w // 2 + w, n)
        band = jnp.where((j >= lo) & (j < hi), 1.0, 0.0).astype(BF16)
        count = (jnp.minimum(tc - w // 2 + w, n) - jnp.maximum(tc - w // 2, 0)).astype(F32)
        u = u_ref[:, c0:c0 + B_GROUP_DIM]
        u_hi = u.astype(BF16)
        u_lo = (u - u_hi.astype(F32)).astype(BF16)
        total = _dot(band, u_hi) + _dot(band, u_lo)
        pooled = total / count - u
        y = _dot(pooled.astype(BF16), wp_ref[g].astype(BF16))
        o_ref[:, c0:c0 + B_GROUP_DIM] = y * ps_ref[:, c0:c0 + B_GROUP_DIM]


def _pool(proj, w_pool, pool_scale, n, n_seq, row_block0):
    return pl.pallas_call(
        functools.partial(_pool_kernel, n=n),
        out_shape=jax.ShapeDtypeStruct((n_seq * n, B_WIDTH), F32),
        grid=(n_seq,),
        in_specs=[
            pl.BlockSpec((n, B_WIDTH), lambda b: (row_block0 + b, 3)),
            pl.BlockSpec((len(POOL_WINDOWS), B_GROUP_DIM, B_GROUP_DIM), lambda b: (0, 0, 0)),
            pl.BlockSpec((1, B_WIDTH), lambda b: (0, 0)),
        ],
        out_specs=pl.BlockSpec((n, B_WIDTH), lambda b: (b, 0)),
        compiler_params=_params(1),
        name="pool_mixer",
    )(proj, w_pool, pool_scale.reshape(1, B_WIDTH))


def _sink_softmax_pv(scores, sinks, values):
    m = sinks
    for s in scores:
        m = jnp.maximum(m, jnp.max(s, axis=-1, keepdims=True))
    l = jnp.exp(sinks - m)
    o = None
    for s, v in zip(scores, values):
        p = jnp.exp(s - m)
        l = l + jnp.sum(p, axis=-1, keepdims=True)
        pv = _dot(p.astype(BF16), v)
        o = pv if o is None else o + pv
    return o / l


def _sink_column(sink_ref, kk, rows_per_head):
    rows = lax.broadcasted_iota(jnp.int32, (C_GROUP * rows_per_head, 1), 0)
    col = jnp.full((C_GROUP * rows_per_head, 1), sink_ref[kk * C_GROUP], F32)
    for g in range(1, C_GROUP):
        col = jnp.where(rows >= g * rows_per_head, sink_ref[kk * C_GROUP + g], col)
    return col


def _ctx_attn_c_kernel(sink_ref, p_ref, o_ref):
    for kk in range(C_KV_HEADS):
        heads = [kk * C_GROUP + g for g in range(C_GROUP)]
        q = jnp.concatenate(
            [p_ref[:, h * HEAD_DIM:(h + 1) * HEAD_DIM] for h in heads], axis=0)
        q = (q * HEAD_DIM ** -0.5).astype(BF16)
        k0 = C_Q_WIDTH + kk * HEAD_DIM
        v0 = C_Q_WIDTH + C_KV_WIDTH + kk * HEAD_DIM
        k = p_ref[:, k0:k0 + HEAD_DIM].astype(BF16)
        v = p_ref[:, v0:v0 + HEAD_DIM].astype(BF16)
        o = _sink_softmax_pv([_dot_nt(q, k)], _sink_column(sink_ref, kk, SEQ), [v])
        for g, h in enumerate(heads):
            o_ref[:, h * HEAD_DIM:(h + 1) * HEAD_DIM] = o[g * SEQ:(g + 1) * SEQ]


def _ctx_attn_c(proj, sink):
    return pl.pallas_call(
        _ctx_attn_c_kernel,
        out_shape=jax.ShapeDtypeStruct((N_CTX_TOK, C_Q_WIDTH), F32),
        grid=(BATCH,),
        in_specs=[
            pl.BlockSpec(memory_space=pltpu.SMEM),
            pl.BlockSpec((SEQ, C_Q_WIDTH + 2 * C_KV_WIDTH), lambda b: (b, 0)),
        ],
        out_specs=pl.BlockSpec((SEQ, C_Q_WIDTH), lambda b: (b, 0)),
        compiler_params=_params(1),
        name="ctx_attn_c",
    )(sink, proj)


def _rope(x, cos, sin_signed):
    n = x.shape[-1]
    lane = lax.broadcasted_iota(jnp.int32, x.shape, x.ndim - 1)
    first = (lane % 32) < 16
    partner = jnp.where(first, pltpu.roll(x, n - 16, axis=x.ndim - 1),
                        pltpu.roll(x, 16, axis=x.ndim - 1))
    return x * cos + partner * sin_signed


def _lat_attn_c_kernel(sink_ref, q_ref, k_ref, v_ref, kc_ref, vc_ref, cosq_ref, sinq_ref,
                       cosk_ref, sink_rot_ref, o_ref, kb_ref, vb_ref):
    j = pl.program_id(1)
    n_blocks = DEC_SEQ // C_BLOCK

    @pl.when(j == 0)
    def _():
        zeros = jnp.zeros((C_BLOCK, C_KV_WIDTH), BF16)
        kb_ref[0:C_BLOCK] = zeros
        vb_ref[0:C_BLOCK] = zeros
        kb_ref[C_BLOCK + DEC_SEQ:2 * C_BLOCK + DEC_SEQ] = zeros
        vb_ref[C_BLOCK + DEC_SEQ:2 * C_BLOCK + DEC_SEQ] = zeros
        kb_ref[C_BLOCK:C_BLOCK + DEC_SEQ] = _rope(
            k_ref[...], cosk_ref[...], sink_rot_ref[...]).astype(BF16)
        vb_ref[C_BLOCK:C_BLOCK + DEC_SEQ] = v_ref[...].astype(BF16)

    q = (_rope(q_ref[...], cosq_ref[...], sinq_ref[...]) * HEAD_DIM ** -0.5).astype(BF16)

    rows = C_GROUP * C_BLOCK
    qi = lax.broadcasted_iota(jnp.int32, (rows, 3 * C_BLOCK), 0) % C_BLOCK
    jj = lax.broadcasted_iota(jnp.int32, (rows, 3 * C_BLOCK), 1)
    kpos = (j - 1) * C_BLOCK + jj
    valid = (jj >= qi) & (jj <= qi + 2 * C_BLOCK) & (kpos >= 0) & (kpos < DEC_SEQ)
    band0 = pl.multiple_of(j * C_BLOCK, C_BLOCK)

    for kk in range(C_KV_HEADS):
        heads = [kk * C_GROUP + g for g in range(C_GROUP)]
        qs = jnp.concatenate([q[:, h * HEAD_DIM:(h + 1) * HEAD_DIM] for h in heads], axis=0)
        c0 = kk * HEAD_DIM
        k_ctx = kc_ref[:, c0:c0 + HEAD_DIM].astype(BF16)
        v_ctx = vc_ref[:, c0:c0 + HEAD_DIM].astype(BF16)
        k_band = kb_ref[pl.ds(band0, 3 * C_BLOCK), c0:c0 + HEAD_DIM]
        v_band = vb_ref[pl.ds(band0, 3 * C_BLOCK), c0:c0 + HEAD_DIM]
        s_ctx = _dot_nt(qs, k_ctx)
        s_band = jnp.where(valid, _dot_nt(qs, k_band), NEG_INF)
        o = _sink_softmax_pv([s_ctx, s_band], _sink_column(sink_ref, kk, C_BLOCK),
                             [v_ctx, v_band])
        for g, h in enumerate(heads):
            o_ref[:, h * HEAD_DIM:(h + 1) * HEAD_DIM] = o[g * C_BLOCK:(g + 1) * C_BLOCK]


def _lat_attn_c(proj, cache_k, cache_v, sink, cos_t, sin_t, j_layer):
    n_blocks = DEC_SEQ // C_BLOCK
    q_row0 = N_CTX_TOK // C_BLOCK
    kv_row0 = N_CTX_TOK // DEC_SEQ
    k_col = C_Q_WIDTH // C_KV_WIDTH
    return pl.pallas_call(
        _lat_attn_c_kernel,
        out_shape=jax.ShapeDtypeStruct((N_LAT_TOK, C_Q_WIDTH), F32),
        grid=(DEC_BATCH, n_blocks),
        in_specs=[
            pl.BlockSpec(memory_space=pltpu.SMEM),
            pl.BlockSpec((C_BLOCK, C_Q_WIDTH), lambda b, j: (q_row0 + b * n_blocks + j, 0)),
            pl.BlockSpec((DEC_SEQ, C_KV_WIDTH), lambda b, j: (kv_row0 + b, k_col)),
            pl.BlockSpec((DEC_SEQ, C_KV_WIDTH), lambda b, j: (kv_row0 + b, k_col + 1)),
            pl.BlockSpec((None, None, PAST_LEN, C_KV_WIDTH), lambda b, j: (b, j_layer, 0, 0)),
            pl.BlockSpec((None, None, PAST_LEN, C_KV_WIDTH), lambda b, j: (b, j_layer, 0, 0)),
            pl.BlockSpec((C_BLOCK, C_Q_WIDTH), lambda b, j: (j, 0)),
            pl.BlockSpec((C_BLOCK, C_Q_WIDTH), lambda b, j: (j, 0)),
            pl.BlockSpec((DEC_SEQ, C_KV_WIDTH), lambda b, j: (0, 0)),
            pl.BlockSpec((DEC_SEQ, C_KV_WIDTH), lambda b, j: (0, 0)),
        ],
        out_specs=pl.BlockSpec((C_BLOCK, C_Q_WIDTH), lambda b, j: (b * n_blocks + j, 0)),
        scratch_shapes=[pltpu.VMEM((DEC_SEQ + 2 * C_BLOCK, C_KV_WIDTH), BF16),
                        pltpu.VMEM((DEC_SEQ + 2 * C_BLOCK, C_KV_WIDTH), BF16)],
        compiler_params=_params(2),
        name="lat_attn_c",
    )(sink, proj, proj, proj, cache_k, cache_v, cos_t, sin_t, cos_t, sin_t)


def _rope_tables():
    t = jnp.arange(DEC_SEQ)
    pos = jnp.stack([t // GRID_W, t % GRID_W], axis=-1).astype(F32)
    half = HEAD_DIM // 4
    inv = ROPE_BASE ** (-jnp.arange(half, dtype=F32) / half)
    ang = pos[:, :, None] * inv
    cos = jnp.cos(ang)
    sin = jnp.sin(ang)
    cos64 = jnp.stack([cos, cos], axis=2).reshape(DEC_SEQ, HEAD_DIM)
    sin64 = jnp.stack([-sin, sin], axis=2).reshape(DEC_SEQ, HEAD_DIM)
    return jnp.tile(cos64, (1, C_HEADS)), jnp.tile(sin64, (1, C_HEADS))


def _outproj_kernel(*refs, n_in):
    x_ref, mod_ref = refs[0], refs[1]
    in_refs = refs[2:2 + n_in]
    w_ref = refs[2 + n_in]
    o_ref = refs[3 + n_in]
    wbf_ref = refs[4 + n_in]

    @pl.when(pl.program_id(0) == 0)
    def _():
        wbf_ref[...] = w_ref[...].astype(BF16)

    acc = None
    row = 0
    for r in in_refs:
        width = r.shape[1]
        part = _dot(r[...].astype(BF16), wbf_ref[row:row + width, :])
        acc = part if acc is None else acc + part
        row += width
    gate = mod_ref[:, 2 * D_MODEL:3 * D_MODEL]
    o_ref[...] = x_ref[...] + gate * acc


def _outproj(x, mod4, layer, mixer_outs, w):
    tm = 512
    n_in = len(mixer_outs)
    in_specs = [
        pl.BlockSpec((tm, D_MODEL), lambda i: (i, 0)),
        pl.BlockSpec((None, None, 1, 6 * D_MODEL),
                     lambda i: (layer, _group_of_tile(i, tm), 0, 0)),
    ]
    in_specs += [pl.BlockSpec((tm, m.shape[1]), lambda i: (i, 0)) for m in mixer_outs]
    in_specs.append(pl.BlockSpec((D_MODEL, D_MODEL), lambda i: (0, 0)))
    return pl.pallas_call(
        functools.partial(_outproj_kernel, n_in=n_in),
        out_shape=jax.ShapeDtypeStruct((N_TOK, D_MODEL), F32),
        grid=(N_TOK // tm,),
        in_specs=in_specs,
        out_specs=pl.BlockSpec((tm, D_MODEL), lambda i: (i, 0)),
        scratch_shapes=[pltpu.VMEM((D_MODEL, D_MODEL), BF16)],
        compiler_params=_params(1),
        name="outproj",
    )(x, mod4, *mixer_outs, w)


def _ffn_kernel(x_ref, g_ref, mod_ref, wg_ref, wu_ref, wd_ref, o_ref, h_ref, acc_ref):
    c = pl.program_id(1)

    @pl.when(c == 0)
    def _():
        h_ref[...] = _norm_mod(x_ref[...], g_ref[...], mod_ref, 3).astype(BF16)
        acc_ref[...] = jnp.zeros_like(acc_ref)

    h = h_ref[...]
    gate = _dot(h, wg_ref[...].astype(BF16))
    up = _dot(h, wu_ref[...].astype(BF16))
    act = gate / (1.0 + jnp.exp(-gate)) * up
    acc_ref[...] += _dot(act.astype(BF16), wd_ref[...].astype(BF16))

    @pl.when(c == pl.num_programs(1) - 1)
    def _():
        o_ref[...] = x_ref[...] + mod_ref[:, 5 * D_MODEL:6 * D_MODEL] * acc_ref[...]


def _ffn(x, g, mod4, layer, w_gate_up, w_down):
    tm = 1024
    th = 256
    n_chunks = FFN_HIDDEN // th
    return pl.pallas_call(
        _ffn_kernel,
        out_shape=jax.ShapeDtypeStruct((N_TOK, D_MODEL), F32),
        grid=(N_TOK // tm, n_chunks),
        in_specs=[
            pl.BlockSpec((tm, D_MODEL), lambda i, c: (i, 0)),
            pl.BlockSpec((1, D_MODEL), lambda i, c: (0, 0)),
            pl.BlockSpec((None, None, 1, 6 * D_MODEL),
                         lambda i, c: (layer, _group_of_tile(i, tm), 0, 0)),
            pl.BlockSpec((D_MODEL, th), lambda i, c: (0, c)),
            pl.BlockSpec((D_MODEL, th), lambda i, c: (0, n_chunks + c)),
            pl.BlockSpec((th, D_MODEL), lambda i, c: (c, 0)),
        ],
        out_specs=pl.BlockSpec((tm, D_MODEL), lambda i, c: (i, 0)),
        scratch_shapes=[pltpu.VMEM((tm, D_MODEL), BF16), pltpu.VMEM((tm, D_MODEL), F32)],
        compiler_params=_params(2),
        name="ffn",
    )(x, g.reshape(1, D_MODEL), mod4, w_gate_up, w_gate_up, w_down)


def _final_norm_kernel(x_ref, g_ref, o_ref):
    x = x_ref[...]
    var = jnp.mean(x * x, axis=-1, keepdims=True)
    o_ref[...] = x * lax.rsqrt(var + EPS) * g_ref[...]


def _final_norm(x, g):
    tm = 1024
    return pl.pallas_call(
        _final_norm_kernel,
        out_shape=jax.ShapeDtypeStruct((N_TOK, D_MODEL), F32),
        grid=(N_TOK // tm,),
        in_specs=[pl.BlockSpec((tm, D_MODEL), lambda i: (i, 0)),
                  pl.BlockSpec((1, D_MODEL), lambda i: (0, 0))],
        out_specs=pl.BlockSpec((tm, D_MODEL), lambda i: (i, 0)),
        compiler_params=_params(1),
        name="final_norm",
    )(x, g.reshape(1, D_MODEL))


def kernel(x_prompt, x_sample, cache_a_k, cache_a_v, cache_c_k, cache_c_v, c, c_ctx, w_mod, b_mod, norm_mix, norm_ffn, w_in_ab, rpb_a, w_pool, pool_scale, w_out_ab, w_in_c, sink_c, w_out_c, w_gate_up, w_down, norm_final):
    x = jnp.concatenate([x_prompt.reshape(N_CTX_TOK, D_MODEL),
                         x_sample.reshape(N_LAT_TOK, D_MODEL)], axis=0)
    cond8 = jnp.concatenate(
        [c_ctx[None], c, jnp.zeros((N_GROUPS_PAD - 1 - DEC_BATCH, D_MODEL), F32)], axis=0)
    mod4 = _modulation(cond8, w_mod, b_mod).reshape(DEPTH, N_GROUPS_PAD, 1, 6 * D_MODEL)

    n_ab = cache_a_k.shape[1]
    n_c = cache_c_k.shape[1]
    cache_a_k2 = cache_a_k.reshape(DEC_BATCH, n_ab, PAST_LEN, A_WIDTH)
    cache_a_v2 = cache_a_v.reshape(DEC_BATCH, n_ab, PAST_LEN, A_WIDTH)
    cache_c_k2 = cache_c_k.reshape(DEC_BATCH, n_c, PAST_LEN, C_KV_WIDTH)
    cache_c_v2 = cache_c_v.reshape(DEC_BATCH, n_c, PAST_LEN, C_KV_WIDTH)
    cos_t, sin_t = _rope_tables()

    a_k, a_v, c_k, c_v = [], [], [], []
    for l in range(DEPTH):
        if l % 2 == 0:
            i = l // 2
            proj = _inproj(x, norm_mix[l], mod4, l, w_in_ab[i])
            a_k.append(proj[:N_CTX_TOK, A_WIDTH:2 * A_WIDTH])
            a_v.append(proj[:N_CTX_TOK, 2 * A_WIDTH:3 * A_WIDTH])
            attn = jnp.concatenate(
                [_ctx_attn_a(proj),
                 _lat_attn_a(proj, cache_a_k2, cache_a_v2, _na_bias_table(rpb_a[i]), i)], axis=0)
            pooled = jnp.concatenate(
                [_pool(proj, w_pool[i], pool_scale[i], SEQ, BATCH, 0),
                 _pool(proj, w_pool[i], pool_scale[i], DEC_SEQ, DEC_BATCH, N_CTX_TOK // DEC_SEQ)],
                axis=0)
            x = _outproj(x, mod4, l, [attn, pooled], w_out_ab[i])
        else:
            j = l // 2
            proj = _inproj(x, norm_mix[l], mod4, l, w_in_c[j])
            c_k.append(proj[:N_CTX_TOK, C_Q_WIDTH:C_Q_WIDTH + C_KV_WIDTH])
            c_v.append(proj[:N_CTX_TOK, C_Q_WIDTH + C_KV_WIDTH:])
            attn = jnp.concatenate(
                [_ctx_attn_c(proj, sink_c[j]),
                 _lat_attn_c(proj, cache_c_k2, cache_c_v2, sink_c[j], cos_t, sin_t, j)], axis=0)
            x = _outproj(x, mod4, l, [attn], w_out_c[j])
        x = _ffn(x, norm_ffn[l], mod4, l, w_gate_up[l], w_down[l])

    y = _final_norm(x, norm_final)
    y_prompt = y[:N_CTX_TOK].reshape(BATCH, SEQ, D_MODEL)
    y_sample = y[N_CTX_TOK:].reshape(DEC_BATCH, DEC_SEQ, D_MODEL)

    def stack_cache(parts, heads):
        per_layer = [p.reshape(BATCH, SEQ, heads, HEAD_DIM) for p in parts]
        return jnp.stack(per_layer, axis=1)

    return (y_prompt, y_sample, stack_cache(a_k, A_HEADS), stack_cache(a_v, A_HEADS),
            stack_cache(c_k, C_KV_HEADS), stack_cache(c_v, C_KV_HEADS))
```

```python
import functools

import jax
import jax.numpy as jnp
from jax import lax
from jax.experimental import pallas as pl
from jax.experimental.pallas import tpu as pltpu

D_MODEL = 1024
BATCH = 16
SEQ = 256
DEPTH = 4
DEC_BATCH = 2
DEC_SEQ = 1024
PAST_LEN = 512
GRID_W = 64
HEAD_DIM = 64
A_WIDTH = 512
A_HEADS = 8
B_WIDTH = 512
POOL_WINDOWS = (2, 4, 8, 16)
B_GROUP_DIM = 128
NA_ROWS = 8
NA_COLS = 16
C_HEADS = 16
C_KV_HEADS = 4
C_GROUP = C_HEADS // C_KV_HEADS
C_Q_WIDTH = 1024
C_KV_WIDTH = 256
C_BLOCK = 128
FFN_HIDDEN = 2816
ROPE_BASE = 10000.0
EPS = 1e-6
NEG_INF = -1e30

N_CTX_TOK = BATCH * SEQ
N_LAT_TOK = DEC_BATCH * DEC_SEQ
N_TOK = N_CTX_TOK + N_LAT_TOK
GRID_ROWS = DEC_SEQ // GRID_W
N_GROUPS_PAD = 8

VMEM_LIMIT = 56 * 1024 * 1024

F32 = jnp.float32
BF16 = jnp.bfloat16


def _params(n_axes):
    return pltpu.CompilerParams(dimension_semantics=("arbitrary",) * n_axes,
                                vmem_limit_bytes=VMEM_LIMIT)


def _group_of_tile(i, tm):
    row0 = i * tm
    return jnp.where(row0 < N_CTX_TOK, 0, 1 + (row0 - N_CTX_TOK) // DEC_SEQ)


def _dot_nt(a, b):
    return lax.dot_general(a, b, (((1,), (1,)), ((), ())), preferred_element_type=F32)


def _dot(a, b):
    return jnp.dot(a, b, preferred_element_type=F32)


def _mod_kernel(cond_ref, w_ref, b_ref, o_ref):
    c = cond_ref[...]
    s = c / (1.0 + jnp.exp(-c))
    o_ref[...] = _dot(s.astype(BF16), w_ref[...].astype(BF16)) + b_ref[...]


def _modulation(cond8, w_mod, b_mod):
    tn = 1536
    return pl.pallas_call(
        _mod_kernel,
        out_shape=jax.ShapeDtypeStruct((DEPTH, N_GROUPS_PAD, 6 * D_MODEL), F32),
        grid=(DEPTH, 6 * D_MODEL // tn),
        in_specs=[
            pl.BlockSpec((N_GROUPS_PAD, D_MODEL), lambda l, j: (0, 0)),
            pl.BlockSpec((None, D_MODEL, tn), lambda l, j: (l, 0, j)),
            pl.BlockSpec((None, 1, tn), lambda l, j: (l, 0, j)),
        ],
        out_specs=pl.BlockSpec((None, N_GROUPS_PAD, tn), lambda l, j: (l, 0, j)),
        compiler_params=_params(2),
        name="modulation",
    )(cond8, w_mod, b_mod.reshape(DEPTH, 1, 6 * D_MODEL))


def _norm_mod(x, g, mod_ref, shift_idx):
    var = jnp.mean(x * x, axis=-1, keepdims=True)
    y = x * lax.rsqrt(var + EPS) * g
    shift = mod_ref[:, shift_idx * D_MODEL:(shift_idx + 1) * D_MODEL]
    scale = mod_ref[:, (shift_idx + 1) * D_MODEL:(shift_idx + 2) * D_MODEL]
    return y * (1.0 + scale) + shift


def _inproj_kernel(*refs, tm, n_heads, k_col, v_col, n_prev):
    x_ref, g_ref, mod_ref, w_ref = refs[:4]
    o_ref, ck_ref, cv_ref, wbf_ref = refs[4 + n_prev:]
    i = pl.program_id(0)

    @pl.when(i == 0)
    def _():
        wbf_ref[...] = w_ref[...].astype(BF16)

    h = _norm_mod(x_ref[...], g_ref[...], mod_ref, 0)
    o_ref[...] = _dot(h.astype(BF16), wbf_ref[...])

    @pl.when(i < N_CTX_TOK // tm)
    def _():
        rows = SEQ * n_heads
        for col, c_ref in ((k_col, ck_ref), (v_col, cv_ref)):
            flat = c_ref.reshape(tm // SEQ * rows, HEAD_DIM)
            for r in range(tm // SEQ):
                for hd in range(n_heads):
                    c0 = col + hd * HEAD_DIM
                    flat[pl.ds(r * rows + hd, SEQ, stride=n_heads), :] = (
                        o_ref[r * SEQ:(r + 1) * SEQ, c0:c0 + HEAD_DIM])


def _inproj(x, g, mod4, layer, w, n_heads, k_col, v_col, slot, n_slots, prev_caches):
    tm = 512
    n_out = w.shape[1]
    n_ctx_tiles = N_CTX_TOK // tm
    cache_shape = jax.ShapeDtypeStruct((BATCH, n_slots, SEQ, n_heads, HEAD_DIM), F32)
    cache_spec = pl.BlockSpec(
        (tm // SEQ, None, SEQ, n_heads, HEAD_DIM),
        lambda i: (jnp.minimum(i, n_ctx_tiles - 1), slot, 0, 0, 0))
    n_prev = len(prev_caches)
    return pl.pallas_call(
        functools.partial(_inproj_kernel, tm=tm, n_heads=n_heads, k_col=k_col, v_col=v_col,
                          n_prev=n_prev),
        out_shape=(jax.ShapeDtypeStruct((N_TOK, n_out), F32), cache_shape, cache_shape),
        grid=(N_TOK // tm,),
        in_specs=[
            pl.BlockSpec((tm, D_MODEL), lambda i: (i, 0)),
            pl.BlockSpec((1, D_MODEL), lambda i: (0, 0)),
            pl.BlockSpec((None, None, 1, 6 * D_MODEL),
                         lambda i: (layer, _group_of_tile(i, tm), 0, 0)),
            pl.BlockSpec((D_MODEL, n_out), lambda i: (0, 0)),
        ] + [pl.BlockSpec(memory_space=pl.ANY)] * n_prev,
        out_specs=(pl.BlockSpec((tm, n_out), lambda i: (i, 0)), cache_spec, cache_spec),
        scratch_shapes=[pltpu.VMEM((D_MODEL, n_out), BF16)],
        input_output_aliases={4 + k: 1 + k for k in range(n_prev)},
        compiler_params=_params(1),
        name="inproj",
    )(x, g.reshape(1, D_MODEL), mod4, w, *prev_caches)


def _ctx_attn_a_kernel(p_ref, o_ref):
    for h in range(A_HEADS):
        c0 = h * HEAD_DIM
        q = (p_ref[:, c0:c0 + HEAD_DIM] * HEAD_DIM ** -0.5).astype(BF16)
        k = p_ref[:, A_WIDTH + c0:A_WIDTH + c0 + HEAD_DIM].astype(BF16)
        v = p_ref[:, 2 * A_WIDTH + c0:2 * A_WIDTH + c0 + HEAD_DIM].astype(BF16)
        s = _dot_nt(q, k)
        m = jnp.max(s, axis=-1, keepdims=True)
        p = jnp.exp(s - m)
        l = jnp.sum(p, axis=-1, keepdims=True)
        o_ref[:, c0:c0 + HEAD_DIM] = _dot(p.astype(BF16), v) / l


def _ctx_attn_a(proj):
    return pl.pallas_call(
        _ctx_attn_a_kernel,
        out_shape=jax.ShapeDtypeStruct((N_TOK, D_MODEL), F32),
        grid=(BATCH,),
        in_specs=[pl.BlockSpec((SEQ, 4 * A_WIDTH), lambda b: (b, 0))],
        out_specs=pl.BlockSpec((SEQ, A_WIDTH), lambda b: (b, 0)),
        compiler_params=_params(1),
        name="ctx_attn_a",
    )(proj)


def _na_bias_kernel(rpb_ref, o_ref):
    h = pl.program_id(0)
    qi = lax.broadcasted_iota(jnp.int32, (GRID_W, GRID_W), 0)
    kc = lax.broadcasted_iota(jnp.int32, (GRID_W, GRID_W), 1)
    rel = kc - qi + (NA_COLS - 1)
    qstart = jnp.clip(qi - NA_COLS // 2, 0, GRID_W - NA_COLS)
    valid = (kc >= qstart) & (kc < qstart + NA_COLS)
    n_dr = 2 * NA_ROWS - 1
    n_dc = 2 * NA_COLS - 1
    for dr in range(n_dr):
        acc = jnp.zeros((GRID_W, GRID_W), F32)
        for d in range(n_dc):
            acc = jnp.where(rel == d, rpb_ref[(h * n_dr + dr) * n_dc + d], acc)
        o_ref[dr] = jnp.where(valid, acc, NEG_INF)


def _na_bias_tiles(rpb):
    n_dr = 2 * NA_ROWS - 1
    return pl.pallas_call(
        _na_bias_kernel,
        out_shape=jax.ShapeDtypeStruct((A_HEADS, n_dr, GRID_W, GRID_W), F32),
        grid=(A_HEADS,),
        in_specs=[pl.BlockSpec(memory_space=pltpu.SMEM)],
        out_specs=pl.BlockSpec((None, n_dr, GRID_W, GRID_W), lambda h: (h, 0, 0, 0)),
        compiler_params=_params(1),
        name="na_bias",
    )(rpb.reshape(-1))


def _na_bias_table(rpb):
    tiles = _na_bias_tiles(rpb)
    t = jnp.stack([tiles[:, s:s + NA_ROWS] for s in range(NA_ROWS)])
    return t.transpose(0, 1, 3, 2, 4).reshape(NA_ROWS, A_HEADS, GRID_W, NA_ROWS * GRID_W)


def _lat_attn_a_kernel(q_ref, k_ref, v_ref, kc_ref, vc_ref, bias_ref, mix_ref, o_ref):
    del mix_ref
    r = pl.program_id(1)
    kc_flat = kc_ref.reshape(PAST_LEN * A_HEADS, HEAD_DIM)
    vc_flat = vc_ref.reshape(PAST_LEN * A_HEADS, HEAD_DIM)
    start = pl.multiple_of(jnp.clip(r - NA_ROWS // 2, 0, GRID_ROWS - NA_ROWS) * GRID_W, GRID_W)
    n_loc = NA_ROWS * GRID_W
    for h in range(A_HEADS):
        c0 = h * HEAD_DIM
        q = (q_ref[:, c0:c0 + HEAD_DIM] * HEAD_DIM ** -0.5).astype(BF16)
        kc = kc_flat[pl.ds(h, PAST_LEN, stride=A_HEADS), :].astype(BF16)
        vc = vc_flat[pl.ds(h, PAST_LEN, stride=A_HEADS), :].astype(BF16)
        kl = k_ref[pl.ds(start, n_loc), c0:c0 + HEAD_DIM].astype(BF16)
        vl = v_ref[pl.ds(start, n_loc), c0:c0 + HEAD_DIM].astype(BF16)
        s_c = _dot_nt(q, kc)
        s_l = _dot_nt(q, kl) + bias_ref[h]
        m = jnp.maximum(jnp.max(s_c, axis=-1, keepdims=True),
                        jnp.max(s_l, axis=-1, keepdims=True))
        p_c = jnp.exp(s_c - m)
        p_l = jnp.exp(s_l - m)
        l = jnp.sum(p_c, axis=-1, keepdims=True) + jnp.sum(p_l, axis=-1, keepdims=True)
        o = _dot(p_c.astype(BF16), vc) + _dot(p_l.astype(BF16), vl)
        o_ref[:, c0:c0 + HEAD_DIM] = o / l


def _lat_attn_a(proj, cache_k, cache_v, bias_table, i_layer, mix):
    q_row0 = N_CTX_TOK // GRID_W
    kv_row0 = N_CTX_TOK // DEC_SEQ

    def bias_index(b, r):
        return (jnp.clip(r - NA_ROWS // 2, 0, GRID_ROWS - NA_ROWS) - r + NA_ROWS - 1, 0, 0, 0)

    return pl.pallas_call(
        _lat_attn_a_kernel,
        out_shape=jax.ShapeDtypeStruct((N_TOK, D_MODEL), F32),
        grid=(DEC_BATCH, GRID_ROWS),
        in_specs=[
            pl.BlockSpec((GRID_W, A_WIDTH), lambda b, r: (q_row0 + b * GRID_ROWS + r, 0)),
            pl.BlockSpec((DEC_SEQ, A_WIDTH), lambda b, r: (kv_row0 + b, 1)),
            pl.BlockSpec((DEC_SEQ, A_WIDTH), lambda b, r: (kv_row0 + b, 2)),
            pl.BlockSpec((None, None, PAST_LEN, A_HEADS, HEAD_DIM),
                         lambda b, r: (b, i_layer, 0, 0, 0)),
            pl.BlockSpec((None, None, PAST_LEN, A_HEADS, HEAD_DIM),
                         lambda b, r: (b, i_layer, 0, 0, 0)),
            pl.BlockSpec((None, A_HEADS, GRID_W, NA_ROWS * GRID_W), bias_index),
            pl.BlockSpec(memory_space=pl.ANY),
        ],
        out_specs=pl.BlockSpec((GRID_W, A_WIDTH),
                               lambda b, r: (q_row0 + b * GRID_ROWS + r, 0)),
        input_output_aliases={6: 0},
        compiler_params=_params(2),
        name="lat_attn_a",
    )(proj, proj, proj, cache_k, cache_v, bias_table, mix)


def _pool_kernel(u_ref, wp_ref, ps_ref, mix_ref, o_ref, *, n):
    del mix_ref
    t = lax.broadcasted_iota(jnp.int32, (n, n), 0)
    j = lax.broadcasted_iota(jnp.int32, (n, n), 1)
    tc = lax.broadcasted_iota(jnp.int32, (n, 1), 0)
    for g, w in enumerate(POOL_WINDOWS):
        c0 = g * B_GROUP_DIM
        lo = jnp.maximum(t - w // 2, 0)
        hi = jnp.minimum(t - w // 2 + w, n)
        band = jnp.where((j >= lo) & (j < hi), 1.0, 0.0).astype(BF16)
        count = (jnp.minimum(tc - w // 2 + w, n) - jnp.maximum(tc - w // 2, 0)).astype(F32)
        u = u_ref[:, c0:c0 + B_GROUP_DIM]
        u_hi = u.astype(BF16)
        u_lo = (u - u_hi.astype(F32)).astype(BF16)
        total = _dot(band, u_hi) + _dot(band, u_lo)
        pooled = total / count - u
        y = _dot(pooled.astype(BF16), wp_ref[g].astype(BF16))
        o_ref[:, c0:c0 + B_GROUP_DIM] = y * ps_ref[:, c0:c0 + B_GROUP_DIM]


def _pool(proj, w_pool, pool_scale, n, n_seq, row_block0, mix):
    return pl.pallas_call(
        functools.partial(_pool_kernel, n=n),
        out_shape=jax.ShapeDtypeStruct((N_TOK, D_MODEL), F32),
        grid=(n_seq,),
        in_specs=[
            pl.BlockSpec((n, B_WIDTH), lambda b: (row_block0 + b, 3)),
            pl.BlockSpec((len(POOL_WINDOWS), B_GROUP_DIM, B_GROUP_DIM), lambda b: (0, 0, 0)),
            pl.BlockSpec((1, B_WIDTH), lambda b: (0, 0)),
            pl.BlockSpec(memory_space=pl.ANY),
        ],
        out_specs=pl.BlockSpec((n, B_WIDTH), lambda b: (row_block0 + b, 1)),
        input_output_aliases={3: 0},
        compiler_params=_params(1),
        name="pool_mixer",
    )(proj, w_pool, pool_scale.reshape(1, B_WIDTH), mix)


def _sink_softmax_pv(scores, sinks, values):
    m = sinks
    for s in scores:
        m = jnp.maximum(m, jnp.max(s, axis=-1, keepdims=True))
    l = jnp.exp(sinks - m)
    o = None
    for s, v in zip(scores, values):
        p = jnp.exp(s - m)
        l = l + jnp.sum(p, axis=-1, keepdims=True)
        pv = _dot(p.astype(BF16), v)
        o = pv if o is None else o + pv
    return o / l


def _sink_column(sink_ref, kk, rows_per_head):
    rows = lax.broadcasted_iota(jnp.int32, (C_GROUP * rows_per_head, 1), 0)
    col = jnp.full((C_GROUP * rows_per_head, 1), sink_ref[kk * C_GROUP], F32)
    for g in range(1, C_GROUP):
        col = jnp.where(rows >= g * rows_per_head, sink_ref[kk * C_GROUP + g], col)
    return col


def _ctx_attn_c_kernel(sink_ref, p_ref, o_ref):
    for kk in range(C_KV_HEADS):
        heads = [kk * C_GROUP + g for g in range(C_GROUP)]
        q = jnp.concatenate(
            [p_ref[:, h * HEAD_DIM:(h + 1) * HEAD_DIM] for h in heads], axis=0)
        q = (q * HEAD_DIM ** -0.5).astype(BF16)
        k0 = C_Q_WIDTH + kk * HEAD_DIM
        v0 = C_Q_WIDTH + C_KV_WIDTH + kk * HEAD_DIM
        k = p_ref[:, k0:k0 + HEAD_DIM].astype(BF16)
        v = p_ref[:, v0:v0 + HEAD_DIM].astype(BF16)
        o = _sink_softmax_pv([_dot_nt(q, k)], _sink_column(sink_ref, kk, SEQ), [v])
        for g, h in enumerate(heads):
            o_ref[:, h * HEAD_DIM:(h + 1) * HEAD_DIM] = o[g * SEQ:(g + 1) * SEQ]


def _ctx_attn_c(proj, sink):
    return pl.pallas_call(
        _ctx_attn_c_kernel,
        out_shape=jax.ShapeDtypeStruct((N_TOK, D_MODEL), F32),
        grid=(BATCH,),
        in_specs=[
            pl.BlockSpec(memory_space=pltpu.SMEM),
            pl.BlockSpec((SEQ, C_Q_WIDTH + 2 * C_KV_WIDTH), lambda b: (b, 0)),
        ],
        out_specs=pl.BlockSpec((SEQ, C_Q_WIDTH), lambda b: (b, 0)),
        compiler_params=_params(1),
        name="ctx_attn_c",
    )(sink, proj)


def _rope(x, cos, sin_signed):
    n = x.shape[-1]
    lane = lax.broadcasted_iota(jnp.int32, x.shape, x.ndim - 1)
    first = (lane % 32) < 16
    partner = jnp.where(first, pltpu.roll(x, n - 16, axis=x.ndim - 1),
                        pltpu.roll(x, 16, axis=x.ndim - 1))
    return x * cos + partner * sin_signed


def _lat_attn_c_kernel(sink_ref, q_ref, k_ref, v_ref, kc_ref, vc_ref, cosq_ref, sinq_ref,
                       cosk_ref, sink_rot_ref, mix_ref, o_ref, kb_ref, vb_ref):
    del mix_ref
    j = pl.program_id(1)
    kc_flat = kc_ref.reshape(PAST_LEN * C_KV_HEADS, HEAD_DIM)
    vc_flat = vc_ref.reshape(PAST_LEN * C_KV_HEADS, HEAD_DIM)
    n_blocks = DEC_SEQ // C_BLOCK

    @pl.when(j == 0)
    def _():
        zeros = jnp.zeros((C_BLOCK, C_KV_WIDTH), BF16)
        kb_ref[0:C_BLOCK] = zeros
        vb_ref[0:C_BLOCK] = zeros
        kb_ref[C_BLOCK + DEC_SEQ:2 * C_BLOCK + DEC_SEQ] = zeros
        vb_ref[C_BLOCK + DEC_SEQ:2 * C_BLOCK + DEC_SEQ] = zeros
        kb_ref[C_BLOCK:C_BLOCK + DEC_SEQ] = _rope(
            k_ref[...], cosk_ref[...], sink_rot_ref[...]).astype(BF16)
        vb_ref[C_BLOCK:C_BLOCK + DEC_SEQ] = v_ref[...].astype(BF16)

    q = (_rope(q_ref[...], cosq_ref[...], sinq_ref[...]) * HEAD_DIM ** -0.5).astype(BF16)

    rows = C_GROUP * C_BLOCK
    qi = lax.broadcasted_iota(jnp.int32, (rows, 3 * C_BLOCK), 0) % C_BLOCK
    jj = lax.broadcasted_iota(jnp.int32, (rows, 3 * C_BLOCK), 1)
    kpos = (j - 1) * C_BLOCK + jj
    valid = (jj >= qi) & (jj <= qi + 2 * C_BLOCK) & (kpos >= 0) & (kpos < DEC_SEQ)
    band0 = pl.multiple_of(j * C_BLOCK, C_BLOCK)

    for kk in range(C_KV_HEADS):
        heads = [kk * C_GROUP + g for g in range(C_GROUP)]
        qs = jnp.concatenate([q[:, h * HEAD_DIM:(h + 1) * HEAD_DIM] for h in heads], axis=0)
        c0 = kk * HEAD_DIM
        k_ctx = kc_flat[pl.ds(kk, PAST_LEN, stride=C_KV_HEADS), :].astype(BF16)
        v_ctx = vc_flat[pl.ds(kk, PAST_LEN, stride=C_KV_HEADS), :].astype(BF16)
        k_band = kb_ref[pl.ds(band0, 3 * C_BLOCK), c0:c0 + HEAD_DIM]
        v_band = vb_ref[pl.ds(band0, 3 * C_BLOCK), c0:c0 + HEAD_DIM]
        s_ctx = _dot_nt(qs, k_ctx)
        s_band = jnp.where(valid, _dot_nt(qs, k_band), NEG_INF)
        o = _sink_softmax_pv([s_ctx, s_band], _sink_column(sink_ref, kk, C_BLOCK),
                             [v_ctx, v_band])
        for g, h in enumerate(heads):
            o_ref[:, h * HEAD_DIM:(h + 1) * HEAD_DIM] = o[g * C_BLOCK:(g + 1) * C_BLOCK]


def _lat_attn_c(proj, cache_k, cache_v, sink, cos_t, sin_t, j_layer, mix):
    n_blocks = DEC_SEQ // C_BLOCK
    q_row0 = N_CTX_TOK // C_BLOCK
    kv_row0 = N_CTX_TOK // DEC_SEQ
    k_col = C_Q_WIDTH // C_KV_WIDTH
    return pl.pallas_call(
        _lat_attn_c_kernel,
        out_shape=jax.ShapeDtypeStruct((N_TOK, D_MODEL), F32),
        grid=(DEC_BATCH, n_blocks),
        in_specs=[
            pl.BlockSpec(memory_space=pltpu.SMEM),
            pl.BlockSpec((C_BLOCK, C_Q_WIDTH), lambda b, j: (q_row0 + b * n_blocks + j, 0)),
            pl.BlockSpec((DEC_SEQ, C_KV_WIDTH), lambda b, j: (kv_row0 + b, k_col)),
            pl.BlockSpec((DEC_SEQ, C_KV_WIDTH), lambda b, j: (kv_row0 + b, k_col + 1)),
            pl.BlockSpec((None, None, PAST_LEN, C_KV_HEADS, HEAD_DIM),
                         lambda b, j: (b, j_layer, 0, 0, 0)),
            pl.BlockSpec((None, None, PAST_LEN, C_KV_HEADS, HEAD_DIM),
                         lambda b, j: (b, j_layer, 0, 0, 0)),
            pl.BlockSpec((C_BLOCK, C_Q_WIDTH), lambda b, j: (j, 0)),
            pl.BlockSpec((C_BLOCK, C_Q_WIDTH), lambda b, j: (j, 0)),
            pl.BlockSpec((DEC_SEQ, C_KV_WIDTH), lambda b, j: (0, 0)),
            pl.BlockSpec((DEC_SEQ, C_KV_WIDTH), lambda b, j: (0, 0)),
            pl.BlockSpec(memory_space=pl.ANY),
        ],
        out_specs=pl.BlockSpec((C_BLOCK, C_Q_WIDTH),
                               lambda b, j: (q_row0 + b * n_blocks + j, 0)),
        input_output_aliases={10: 0},
        scratch_shapes=[pltpu.VMEM((DEC_SEQ + 2 * C_BLOCK, C_KV_WIDTH), BF16),
                        pltpu.VMEM((DEC_SEQ + 2 * C_BLOCK, C_KV_WIDTH), BF16)],
        compiler_params=_params(2),
        name="lat_attn_c",
    )(sink, proj, proj, proj, cache_k, cache_v, cos_t, sin_t, cos_t, sin_t, mix)


def _rope_tables():
    t = jnp.arange(DEC_SEQ)
    pos = jnp.stack([t // GRID_W, t % GRID_W], axis=-1).astype(F32)
    half = HEAD_DIM // 4
    inv = ROPE_BASE ** (-jnp.arange(half, dtype=F32) / half)
    ang = pos[:, :, None] * inv
    cos = jnp.cos(ang)
    sin = jnp.sin(ang)
    cos64 = jnp.stack([cos, cos], axis=2).reshape(DEC_SEQ, HEAD_DIM)
    sin64 = jnp.stack([-sin, sin], axis=2).reshape(DEC_SEQ, HEAD_DIM)
    return jnp.tile(cos64, (1, C_HEADS)), jnp.tile(sin64, (1, C_HEADS))


def _outproj_kernel(x_ref, mod_ref, mix_ref, w_ref, o_ref, wbf_ref):
    @pl.when(pl.program_id(0) == 0)
    def _():
        wbf_ref[...] = w_ref[...].astype(BF16)

    gate = mod_ref[:, 2 * D_MODEL:3 * D_MODEL]
    o_ref[...] = x_ref[...] + gate * _dot(mix_ref[...].astype(BF16), wbf_ref[...])


def _outproj(x, mod4, layer, mix, w):
    tm = 512
    return pl.pallas_call(
        _outproj_kernel,
        out_shape=jax.ShapeDtypeStruct((N_TOK, D_MODEL), F32),
        grid=(N_TOK // tm,),
        in_specs=[
            pl.BlockSpec((tm, D_MODEL), lambda i: (i, 0)),
            pl.BlockSpec((None, None, 1, 6 * D_MODEL),
                         lambda i: (layer, _group_of_tile(i, tm), 0, 0)),
            pl.BlockSpec((tm, D_MODEL), lambda i: (i, 0)),
            pl.BlockSpec((D_MODEL, D_MODEL), lambda i: (0, 0)),
        ],
        out_specs=pl.BlockSpec((tm, D_MODEL), lambda i: (i, 0)),
        scratch_shapes=[pltpu.VMEM((D_MODEL, D_MODEL), BF16)],
        compiler_params=_params(1),
        name="outproj",
    )(x, mod4, mix, w)


def _ffn_kernel(x_ref, g_ref, mod_ref, wg_ref, wu_ref, wd_ref, o_ref, h_ref, acc_ref):
    c = pl.program_id(1)

    @pl.when(c == 0)
    def _():
        h_ref[...] = _norm_mod(x_ref[...], g_ref[...], mod_ref, 3).astype(BF16)
        acc_ref[...] = jnp.zeros_like(acc_ref)

    h = h_ref[...]
    gate = _dot(h, wg_ref[...].astype(BF16))
    up = _dot(h, wu_ref[...].astype(BF16))
    act = gate / (1.0 + jnp.exp(-gate)) * up
    acc_ref[...] += _dot(act.astype(BF16), wd_ref[...].astype(BF16))

    @pl.when(c == pl.num_programs(1) - 1)
    def _():
        o_ref[...] = x_ref[...] + mod_ref[:, 5 * D_MODEL:6 * D_MODEL] * acc_ref[...]


def _ffn(x, g, mod4, layer, w_gate_up, w_down):
    tm = 1024
    th = 256
    n_chunks = FFN_HIDDEN // th
    return pl.pallas_call(
        _ffn_kernel,
        out_shape=jax.ShapeDtypeStruct((N_TOK, D_MODEL), F32),
        grid=(N_TOK // tm, n_chunks),
        in_specs=[
            pl.BlockSpec((tm, D_MODEL), lambda i, c: (i, 0)),
            pl.BlockSpec((1, D_MODEL), lambda i, c: (0, 0)),
            pl.BlockSpec((None, None, 1, 6 * D_MODEL),
                         lambda i, c: (layer, _group_of_tile(i, tm), 0, 0)),
            pl.BlockSpec((D_MODEL, th), lambda i, c: (0, c)),
            pl.BlockSpec((D_MODEL, th), lambda i, c: (0, n_chunks + c)),
            pl.BlockSpec((th, D_MODEL), lambda i, c: (c, 0)),
        ],
        out_specs=pl.BlockSpec((tm, D_MODEL), lambda i, c: (i, 0)),
        scratch_shapes=[pltpu.VMEM((tm, D_MODEL), BF16), pltpu.VMEM((tm, D_MODEL), F32)],
        compiler_params=_params(2),
        name="ffn",
    )(x, g.reshape(1, D_MODEL), mod4, w_gate_up, w_gate_up, w_down)


def _final_norm_kernel(x_ref, g_ref, ctx_ref, lat_ref, *, n_ctx_tiles):
    x = x_ref[...]
    var = jnp.mean(x * x, axis=-1, keepdims=True)
    y = x * lax.rsqrt(var + EPS) * g_ref[...]
    is_ctx = pl.program_id(0) < n_ctx_tiles

    @pl.when(is_ctx)
    def _():
        ctx_ref[...] = y

    @pl.when(jnp.logical_not(is_ctx))
    def _():
        lat_ref[...] = y


def _final_norm(x, g):
    tm = 1024
    n_ctx_tiles = N_CTX_TOK // tm
    return pl.pallas_call(
        functools.partial(_final_norm_kernel, n_ctx_tiles=n_ctx_tiles),
        out_shape=(jax.ShapeDtypeStruct((N_CTX_TOK, D_MODEL), F32),
                   jax.ShapeDtypeStruct((N_LAT_TOK, D_MODEL), F32)),
        grid=(N_TOK // tm,),
        in_specs=[pl.BlockSpec((tm, D_MODEL), lambda i: (i, 0)),
                  pl.BlockSpec((1, D_MODEL), lambda i: (0, 0))],
        out_specs=(pl.BlockSpec((tm, D_MODEL), lambda i: (jnp.minimum(i, n_ctx_tiles - 1), 0)),
                   pl.BlockSpec((tm, D_MODEL), lambda i: (jnp.maximum(i - n_ctx_tiles, 0), 0))),
        compiler_params=_params(1),
        name="final_norm",
    )(x, g.reshape(1, D_MODEL))


def kernel(x_prompt, x_sample, cache_a_k, cache_a_v, cache_c_k, cache_c_v, c, c_ctx, w_mod, b_mod, norm_mix, norm_ffn, w_in_ab, rpb_a, w_pool, pool_scale, w_out_ab, w_in_c, sink_c, w_out_c, w_gate_up, w_down, norm_final):
    x = jnp.concatenate([x_prompt.reshape(N_CTX_TOK, D_MODEL),
                         x_sample.reshape(N_LAT_TOK, D_MODEL)], axis=0)
    cond8 = jnp.concatenate(
        [c_ctx[None], c, jnp.zeros((N_GROUPS_PAD - 1 - DEC_BATCH, D_MODEL), F32)], axis=0)
    mod4 = _modulation(cond8, w_mod, b_mod).reshape(DEPTH, N_GROUPS_PAD, 1, 6 * D_MODEL)

    n_ab = cache_a_k.shape[1]
    n_c = cache_c_k.shape[1]
    cos_t, sin_t = _rope_tables()

    new_a = ()
    new_c = ()
    for l in range(DEPTH):
        if l % 2 == 0:
            i = l // 2
            proj, *new_a = _inproj(x, norm_mix[l], mod4, l, w_in_ab[i], A_HEADS,
                                   A_WIDTH, 2 * A_WIDTH, i, n_ab, new_a)
            mix = _ctx_attn_a(proj)
            mix = _lat_attn_a(proj, cache_a_k, cache_a_v, _na_bias_table(rpb_a[i]), i, mix)
            mix = _pool(proj, w_pool[i], pool_scale[i], SEQ, BATCH, 0, mix)
            mix = _pool(proj, w_pool[i], pool_scale[i], DEC_SEQ, DEC_BATCH,
                        N_CTX_TOK // DEC_SEQ, mix)
            x = _outproj(x, mod4, l, mix, w_out_ab[i])
        else:
            j = l // 2
            proj, *new_c = _inproj(x, norm_mix[l], mod4, l, w_in_c[j], C_KV_HEADS,
                                   C_Q_WIDTH, C_Q_WIDTH + C_KV_WIDTH, j, n_c, new_c)
            mix = _ctx_attn_c(proj, sink_c[j])
            mix = _lat_attn_c(proj, cache_c_k, cache_c_v, sink_c[j], cos_t, sin_t, j, mix)
            x = _outproj(x, mod4, l, mix, w_out_c[j])
        x = _ffn(x, norm_ffn[l], mod4, l, w_gate_up[l], w_down[l])

    y_ctx, y_lat = _final_norm(x, norm_final)
    return (y_ctx.reshape(BATCH, SEQ, D_MODEL), y_lat.reshape(DEC_BATCH, DEC_SEQ, D_MODEL),
            new_a[0], new_a[1], new_c[0], new_c[1])
```

```python
import functools

import jax
import jax.numpy as jnp
from jax import lax
from jax.experimental import pallas as pl
from jax.experimental.pallas import tpu as pltpu

D_MODEL = 1024
BATCH = 16
SEQ = 256
DEPTH = 4
DEC_BATCH = 2
DEC_SEQ = 1024
PAST_LEN = 512
GRID_W = 64
HEAD_DIM = 64
A_WIDTH = 512
A_HEADS = 8
B_WIDTH = 512
POOL_WINDOWS = (2, 4, 8, 16)
B_GROUP_DIM = 128
NA_ROWS = 8
NA_COLS = 16
C_HEADS = 16
C_KV_HEADS = 4
C_GROUP = C_HEADS // C_KV_HEADS
C_Q_WIDTH = 1024
C_KV_WIDTH = 256
C_BLOCK = 128
FFN_HIDDEN = 2816
ROPE_BASE = 10000.0
EPS = 1e-6
NEG_INF = -1e30

N_CTX_TOK = BATCH * SEQ
N_LAT_TOK = DEC_BATCH * DEC_SEQ
N_TOK = N_CTX_TOK + N_LAT_TOK
GRID_ROWS = DEC_SEQ // GRID_W
N_GROUPS_PAD = 8

VMEM_LIMIT = 56 * 1024 * 1024

F32 = jnp.float32
BF16 = jnp.bfloat16


def _params(n_axes):
    return pltpu.CompilerParams(dimension_semantics=("arbitrary",) * n_axes,
                                vmem_limit_bytes=VMEM_LIMIT)


def _group_of_tile(i, tm):
    row0 = i * tm
    return jnp.where(row0 < N_CTX_TOK, 0, 1 + (row0 - N_CTX_TOK) // DEC_SEQ)


def _dot_nt(a, b):
    return lax.dot_general(a, b, (((1,), (1,)), ((), ())), preferred_element_type=F32)


def _dot(a, b):
    return jnp.dot(a, b, preferred_element_type=F32)


def _mod_kernel(cond_ref, w_ref, b_ref, o_ref):
    c = cond_ref[...]
    s = c / (1.0 + jnp.exp(-c))
    o_ref[...] = _dot(s.astype(BF16), w_ref[...].astype(BF16)) + b_ref[...]


def _modulation(cond8, w_mod, b_mod):
    tn = 1536
    return pl.pallas_call(
        _mod_kernel,
        out_shape=jax.ShapeDtypeStruct((DEPTH, N_GROUPS_PAD, 6 * D_MODEL), F32),
        grid=(DEPTH, 6 * D_MODEL // tn),
        in_specs=[
            pl.BlockSpec((N_GROUPS_PAD, D_MODEL), lambda l, j: (0, 0)),
            pl.BlockSpec((None, D_MODEL, tn), lambda l, j: (l, 0, j)),
            pl.BlockSpec((None, 1, tn), lambda l, j: (l, 0, j)),
        ],
        out_specs=pl.BlockSpec((None, N_GROUPS_PAD, tn), lambda l, j: (l, 0, j)),
        compiler_params=_params(2),
        name="modulation",
    )(cond8, w_mod, b_mod.reshape(DEPTH, 1, 6 * D_MODEL))


def _norm_mod(x, g, mod_ref, shift_idx):
    var = jnp.mean(x * x, axis=-1, keepdims=True)
    y = x * lax.rsqrt(var + EPS) * g
    shift = mod_ref[:, shift_idx * D_MODEL:(shift_idx + 1) * D_MODEL]
    scale = mod_ref[:, (shift_idx + 1) * D_MODEL:(shift_idx + 2) * D_MODEL]
    return y * (1.0 + scale) + shift


def _inproj_kernel(*refs, tm, n_heads, k_col, v_col, n_prev, split_x):
    n_x = 2 if split_x else 1
    g_ref, mod_ref, w_ref = refs[n_x:n_x + 3]
    outs = refs[n_x + 3 + n_prev:]
    o_ref, ck_ref, cv_ref = outs[:3]
    wbf_ref = outs[-1]
    i = pl.program_id(0)
    is_ctx = i < N_CTX_TOK // tm

    @pl.when(i == 0)
    def _():
        wbf_ref[...] = w_ref[...].astype(BF16)

    if split_x:
        x = jnp.where(is_ctx, refs[0][...], refs[1][...])
        outs[3][...] = x
    else:
        x = refs[0][...]
    h = _norm_mod(x, g_ref[...], mod_ref, 0)
    o_ref[...] = _dot(h.astype(BF16), wbf_ref[...])

    @pl.when(is_ctx)
    def _():
        rows = SEQ * n_heads
        for col, c_ref in ((k_col, ck_ref), (v_col, cv_ref)):
            flat = c_ref.reshape(tm // SEQ * rows, HEAD_DIM)
            for r in range(tm // SEQ):
                for hd in range(n_heads):
                    c0 = col + hd * HEAD_DIM
                    flat[pl.ds(r * rows + hd, SEQ, stride=n_heads), :] = (
                        o_ref[r * SEQ:(r + 1) * SEQ, c0:c0 + HEAD_DIM])


def _inproj(xs, g_all, mod4, layer, w_all, w_idx, n_heads, k_col, v_col, n_slots, prev_caches):
    tm = 512
    n_out = w_all.shape[2]
    n_ctx_tiles = N_CTX_TOK // tm
    split_x = len(xs) == 2
    cache_shape = jax.ShapeDtypeStruct((BATCH, n_slots, SEQ, n_heads, HEAD_DIM), F32)
    cache_spec = pl.BlockSpec(
        (tm // SEQ, None, SEQ, n_heads, HEAD_DIM),
        lambda i: (jnp.minimum(i, n_ctx_tiles - 1), w_idx, 0, 0, 0))
    row_spec = pl.BlockSpec((tm, D_MODEL), lambda i: (i, 0))
    if split_x:
        x_specs = [
            pl.BlockSpec((tm, D_MODEL), lambda i: (jnp.minimum(i, n_ctx_tiles - 1), 0)),
            pl.BlockSpec((tm, D_MODEL), lambda i: (jnp.maximum(i - n_ctx_tiles, 0), 0)),
        ]
    else:
        x_specs = [row_spec]
    out_shape = [jax.ShapeDtypeStruct((N_TOK, n_out), F32), cache_shape, cache_shape]
    out_specs = [pl.BlockSpec((tm, n_out), lambda i: (i, 0)), cache_spec, cache_spec]
    if split_x:
        out_shape.append(jax.ShapeDtypeStruct((N_TOK, D_MODEL), F32))
        out_specs.append(row_spec)
    n_prev = len(prev_caches)
    n_x = len(xs)
    return pl.pallas_call(
        functools.partial(_inproj_kernel, tm=tm, n_heads=n_heads, k_col=k_col, v_col=v_col,
                          n_prev=n_prev, split_x=split_x),
        out_shape=out_shape,
        grid=(N_TOK // tm,),
        in_specs=x_specs + [
            pl.BlockSpec((None, 1, D_MODEL), lambda i: (layer, 0, 0)),
            pl.BlockSpec((None, None, 1, 6 * D_MODEL),
                         lambda i: (layer, _group_of_tile(i, tm), 0, 0)),
            pl.BlockSpec((None, D_MODEL, n_out), lambda i: (w_idx, 0, 0),
                         pipeline_mode=pl.Buffered(1)),
        ] + [pl.BlockSpec(memory_space=pl.ANY)] * n_prev,
        out_specs=out_specs,
        scratch_shapes=[pltpu.VMEM((D_MODEL, n_out), BF16)],
        input_output_aliases={n_x + 3 + k: 1 + k for k in range(n_prev)},
        compiler_params=_params(1),
        name="inproj",
    )(*xs, g_all.reshape(DEPTH, 1, D_MODEL), mod4, w_all, *prev_caches)


def _ctx_attn_a_kernel(p_ref, o_ref):
    for h in range(A_HEADS):
        c0 = h * HEAD_DIM
        q = (p_ref[:, c0:c0 + HEAD_DIM] * HEAD_DIM ** -0.5).astype(BF16)
        k = p_ref[:, A_WIDTH + c0:A_WIDTH + c0 + HEAD_DIM].astype(BF16)
        v = p_ref[:, 2 * A_WIDTH + c0:2 * A_WIDTH + c0 + HEAD_DIM].astype(BF16)
        s = _dot_nt(q, k)
        m = jnp.max(s, axis=-1, keepdims=True)
        p = jnp.exp(s - m)
        l = jnp.sum(p, axis=-1, keepdims=True)
        o_ref[:, c0:c0 + HEAD_DIM] = _dot(p.astype(BF16), v) / l


def _ctx_attn_a(proj):
    return pl.pallas_call(
        _ctx_attn_a_kernel,
        out_shape=jax.ShapeDtypeStruct((N_TOK, D_MODEL), F32),
        grid=(BATCH,),
        in_specs=[pl.BlockSpec((SEQ, 4 * A_WIDTH), lambda b: (b, 0))],
        out_specs=pl.BlockSpec((SEQ, A_WIDTH), lambda b: (b, 0)),
        compiler_params=_params(1),
        name="ctx_attn_a",
    )(proj)


N_DROW = 2 * NA_ROWS - 1
N_DCOL = 2 * NA_COLS - 1
N_BIAS_TILES = 16
BIAS_TILE_LEFT_PAD = 14
BIAS_TILE_RIGHT_PAD = 15
MID_DROW = NA_ROWS - 1 - NA_ROWS // 2


def _na_bias_kernel(rpb_ref, o_ref):
    base = (pl.program_id(0) * A_HEADS + pl.program_id(1)) * (N_DROW * N_DCOL)
    qi = lax.broadcasted_iota(jnp.int32, (GRID_W, 2 * GRID_W), 0)
    lane = lax.broadcasted_iota(jnp.int32, (GRID_W, 2 * GRID_W), 1)
    right = lane >= GRID_W
    kc = jnp.where(right, lane - GRID_W, lane)
    rel = kc - qi + (NA_COLS - 1)
    qstart = jnp.clip(qi - NA_COLS // 2, 0, GRID_W - NA_COLS)
    valid = (kc >= qstart) & (kc < qstart + NA_COLS)

    def tile(dr_left, dr_right):
        acc = jnp.zeros((GRID_W, 2 * GRID_W), F32)
        for d in range(N_DCOL):
            if dr_left is None:
                val = rpb_ref[base + dr_right * N_DCOL + d]
            elif dr_right is None:
                val = rpb_ref[base + dr_left * N_DCOL + d]
            else:
                val = jnp.where(right, rpb_ref[base + dr_right * N_DCOL + d],
                                rpb_ref[base + dr_left * N_DCOL + d])
            acc = jnp.where(rel == d, val, acc)
        ok = valid
        if dr_left is None:
            ok = ok & right
        if dr_right is None:
            ok = ok & jnp.logical_not(right)
        return jnp.where(ok, acc, NEG_INF)

    for t in range(N_DROW - 1):
        o_ref[t] = tile(t, t + 1)
    o_ref[BIAS_TILE_LEFT_PAD] = tile(None, MID_DROW)
    o_ref[BIAS_TILE_RIGHT_PAD] = tile(MID_DROW + NA_ROWS - 1, None)


def _na_bias_tiles(rpb_a):
    n_layers = rpb_a.shape[0]
    return pl.pallas_call(
        _na_bias_kernel,
        out_shape=jax.ShapeDtypeStruct(
            (n_layers, A_HEADS, N_BIAS_TILES, GRID_W, 2 * GRID_W), F32),
        grid=(n_layers, A_HEADS),
        in_specs=[pl.BlockSpec(memory_space=pltpu.SMEM)],
        out_specs=pl.BlockSpec((None, None, N_BIAS_TILES, GRID_W, 2 * GRID_W),
                               lambda l, h: (l, h, 0, 0, 0)),
        compiler_params=_params(2),
        name="na_bias",
    )(rpb_a.reshape(-1))


def _na_window(r):
    start = min(max(r - NA_ROWS // 2, 0), GRID_ROWS - NA_ROWS)
    first_drow = start - r + NA_ROWS - 1
    if start % 2 == 0:
        return start, [first_drow + 2 * p for p in range(NA_ROWS // 2)]
    assert first_drow == MID_DROW
    inner = [first_drow + 1 + 2 * p for p in range(NA_ROWS // 2 - 1)]
    return start - 1, [BIAS_TILE_LEFT_PAD] + inner + [BIAS_TILE_RIGHT_PAD]


def _lat_attn_a_kernel(q_ref, k_ref, v_ref, kc_ref, vc_ref, bias_ref, mix_ref, o_ref,
                       s_loc, s_ctx, p_loc, p_ctx):
    del mix_ref
    hp = pl.program_id(1)

    @pl.when((pl.program_id(0) == 0) & (hp == 0))
    def _():
        p_loc[...] = jnp.zeros_like(p_loc)

    kc_flat = kc_ref.reshape(PAST_LEN * A_HEADS, HEAD_DIM)
    vc_flat = vc_ref.reshape(PAST_LEN * A_HEADS, HEAD_DIM)
    for hh in range(2):
        c0 = hh * HEAD_DIM
        head = hp * 2 + hh
        q = (q_ref[:, c0:c0 + HEAD_DIM] * HEAD_DIM ** -0.5).astype(BF16)
        k = k_ref[:, c0:c0 + HEAD_DIM].astype(BF16)
        v = v_ref[:, c0:c0 + HEAD_DIM].astype(BF16)
        kc = kc_flat[pl.ds(head, PAST_LEN, stride=A_HEADS), :].astype(BF16)
        vc = vc_flat[pl.ds(head, PAST_LEN, stride=A_HEADS), :].astype(BF16)
        s_loc[...] = _dot_nt(q, k)
        s_ctx[...] = _dot_nt(q, kc)
        inv_l = []
        for r in range(GRID_ROWS):
            rows = slice(r * GRID_W, (r + 1) * GRID_W)
            first_row, tiles = _na_window(r)
            cols = slice(first_row * GRID_W, (first_row + 2 * len(tiles)) * GRID_W)
            bias = jnp.concatenate([bias_ref[hh, t] for t in tiles], axis=1)
            sc = s_ctx[rows, :]
            sl = s_loc[rows, cols] + bias
            m = jnp.maximum(jnp.max(sc, axis=-1, keepdims=True),
                            jnp.max(sl, axis=-1, keepdims=True))
            pc = jnp.exp(sc - m)
            pw = jnp.exp(sl - m)
            inv_l.append(1.0 / (jnp.sum(pc, axis=-1, keepdims=True)
                                + jnp.sum(pw, axis=-1, keepdims=True)))
            p_ctx[rows, :] = pc.astype(BF16)
            p_loc[rows, cols] = pw.astype(BF16)
        o = _dot(p_ctx[...], vc) + _dot(p_loc[...], v)
        o_ref[:, c0:c0 + HEAD_DIM] = o * jnp.concatenate(inv_l, axis=0)


def _lat_attn_a(proj, cache_k, cache_v, bias_tiles, i_layer, mix):
    row0 = N_CTX_TOK // DEC_SEQ
    pair = 2 * HEAD_DIM
    k_col0 = A_WIDTH // pair
    v_col0 = 2 * A_WIDTH // pair
    return pl.pallas_call(
        _lat_attn_a_kernel,
        out_shape=jax.ShapeDtypeStruct((N_TOK, D_MODEL), F32),
        grid=(DEC_BATCH, A_HEADS // 2),
        in_specs=[
            pl.BlockSpec((DEC_SEQ, pair), lambda b, hp: (row0 + b, hp)),
            pl.BlockSpec((DEC_SEQ, pair), lambda b, hp: (row0 + b, k_col0 + hp)),
            pl.BlockSpec((DEC_SEQ, pair), lambda b, hp: (row0 + b, v_col0 + hp)),
            pl.BlockSpec((None, None, PAST_LEN, A_HEADS, HEAD_DIM),
                         lambda b, hp: (b, i_layer, 0, 0, 0)),
            pl.BlockSpec((None, None, PAST_LEN, A_HEADS, HEAD_DIM),
                         lambda b, hp: (b, i_layer, 0, 0, 0)),
            pl.BlockSpec((None, 2, N_BIAS_TILES, GRID_W, 2 * GRID_W),
                         lambda b, hp: (i_layer, hp, 0, 0, 0)),
            pl.BlockSpec(memory_space=pl.ANY),
        ],
        out_specs=pl.BlockSpec((DEC_SEQ, pair), lambda b, hp: (row0 + b, hp)),
        scratch_shapes=[pltpu.VMEM((DEC_SEQ, DEC_SEQ), F32),
                        pltpu.VMEM((DEC_SEQ, PAST_LEN), F32),
                        pltpu.VMEM((DEC_SEQ, DEC_SEQ), BF16),
                        pltpu.VMEM((DEC_SEQ, PAST_LEN), BF16)],
        input_output_aliases={6: 0},
        compiler_params=_params(2),
        name="lat_attn_a",
    )(proj, proj, proj, cache_k, cache_v, bias_tiles, mix)


def _pool_kernel(u_ref, wp_ref, ps_ref, mix_ref, o_ref, *, n):
    del mix_ref
    t = lax.broadcasted_iota(jnp.int32, (n, n), 0)
    j = lax.broadcasted_iota(jnp.int32, (n, n), 1)
    tc = lax.broadcasted_iota(jnp.int32, (n, 1), 0)
    for g, w in enumerate(POOL_WINDOWS):
        c0 = g * B_GROUP_DIM
        lo = jnp.maximum(t - w // 2, 0)
        hi = jnp.minimum(t - w // 2 + w, n)
        band = jnp.where((j >= lo) & (j < hi), 1.0, 0.0).astype(BF16)
        count = (jnp.minimum(tc - w // 2 + w, n) - jnp.maximum(tc - w // 2, 0)).astype(F32)
        u = u_ref[:, c0:c0 + B_GROUP_DIM]
        u_hi = u.astype(BF16)
        u_lo = (u - u_hi.astype(F32)).astype(BF16)
        total = _dot(band, u_hi) + _dot(band, u_lo)
        pooled = total / count - u
        y = _dot(pooled.astype(BF16), wp_ref[g].astype(BF16))
        o_ref[:, c0:c0 + B_GROUP_DIM] = y * ps_ref[:, c0:c0 + B_GROUP_DIM]


def _pool(proj, w_pool, pool_scale, i_layer, n, n_seq, row_block0, mix):
    n_groups = len(POOL_WINDOWS)
    return pl.pallas_call(
        functools.partial(_pool_kernel, n=n),
        out_shape=jax.ShapeDtypeStruct((N_TOK, D_MODEL), F32),
        grid=(n_seq,),
        in_specs=[
            pl.BlockSpec((n, B_WIDTH), lambda b: (row_block0 + b, 3)),
            pl.BlockSpec((None, n_groups, B_GROUP_DIM, B_GROUP_DIM),
                         lambda b: (i_layer, 0, 0, 0)),
            pl.BlockSpec((None, 1, B_WIDTH), lambda b: (i_layer, 0, 0)),
            pl.BlockSpec(memory_space=pl.ANY),
        ],
        out_specs=pl.BlockSpec((n, B_WIDTH), lambda b: (row_block0 + b, 1)),
        input_output_aliases={3: 0},
        compiler_params=_params(1),
        name="pool_mixer",
    )(proj, w_pool, pool_scale.reshape(-1, 1, B_WIDTH), mix)


def _sink_softmax_pv(scores, sinks, values):
    m = sinks
    for s in scores:
        m = jnp.maximum(m, jnp.max(s, axis=-1, keepdims=True))
    l = jnp.exp(sinks - m)
    o = None
    for s, v in zip(scores, values):
        p = jnp.exp(s - m)
        l = l + jnp.sum(p, axis=-1, keepdims=True)
        pv = _dot(p.astype(BF16), v)
        o = pv if o is None else o + pv
    return o / l


def _sink_column(sink_ref, head0, rows_per_head):
    rows = lax.broadcasted_iota(jnp.int32, (C_GROUP * rows_per_head, 1), 0)
    col = jnp.full((C_GROUP * rows_per_head, 1), sink_ref[head0], F32)
    for g in range(1, C_GROUP):
        col = jnp.where(rows >= g * rows_per_head, sink_ref[head0 + g], col)
    return col


def _ctx_attn_c_kernel(sink_ref, p_ref, o_ref, *, sink0):
    for kk in range(C_KV_HEADS):
        heads = [kk * C_GROUP + g for g in range(C_GROUP)]
        q = jnp.concatenate(
            [p_ref[:, h * HEAD_DIM:(h + 1) * HEAD_DIM] for h in heads], axis=0)
        q = (q * HEAD_DIM ** -0.5).astype(BF16)
        k0 = C_Q_WIDTH + kk * HEAD_DIM
        v0 = C_Q_WIDTH + C_KV_WIDTH + kk * HEAD_DIM
        k = p_ref[:, k0:k0 + HEAD_DIM].astype(BF16)
        v = p_ref[:, v0:v0 + HEAD_DIM].astype(BF16)
        o = _sink_softmax_pv([_dot_nt(q, k)],
                             _sink_column(sink_ref, sink0 + kk * C_GROUP, SEQ), [v])
        for g, h in enumerate(heads):
            o_ref[:, h * HEAD_DIM:(h + 1) * HEAD_DIM] = o[g * SEQ:(g + 1) * SEQ]


def _ctx_attn_c(proj, sink_all, j_layer):
    return pl.pallas_call(
        functools.partial(_ctx_attn_c_kernel, sink0=j_layer * C_HEADS),
        out_shape=jax.ShapeDtypeStruct((N_TOK, D_MODEL), F32),
        grid=(BATCH,),
        in_specs=[
            pl.BlockSpec(memory_space=pltpu.SMEM),
            pl.BlockSpec((SEQ, C_Q_WIDTH + 2 * C_KV_WIDTH), lambda b: (b, 0)),
        ],
        out_specs=pl.BlockSpec((SEQ, C_Q_WIDTH), lambda b: (b, 0)),
        compiler_params=_params(1),
        name="ctx_attn_c",
    )(sink_all.reshape(-1), proj)


def _rope(x, cos, sin_signed):
    n = x.shape[-1]
    lane = lax.broadcasted_iota(jnp.int32, x.shape, x.ndim - 1)
    first = (lane % 32) < 16
    partner = jnp.where(first, pltpu.roll(x, n - 16, axis=x.ndim - 1),
                        pltpu.roll(x, 16, axis=x.ndim - 1))
    return x * cos + partner * sin_signed


def _lat_attn_c_kernel(sink_ref, q_ref, k_ref, v_ref, kc_ref, vc_ref, cosq_ref, sinq_ref,
                       cosk_ref, sink_rot_ref, mix_ref, o_ref, kb_ref, vb_ref, *, sink0):
    del mix_ref
    j = pl.program_id(1)
    kc_flat = kc_ref.reshape(PAST_LEN * C_KV_HEADS, HEAD_DIM)
    vc_flat = vc_ref.reshape(PAST_LEN * C_KV_HEADS, HEAD_DIM)
    n_blocks = DEC_SEQ // C_BLOCK

    @pl.when(j == 0)
    def _():
        zeros = jnp.zeros((C_BLOCK, C_KV_WIDTH), BF16)
        kb_ref[0:C_BLOCK] = zeros
        vb_ref[0:C_BLOCK] = zeros
        kb_ref[C_BLOCK + DEC_SEQ:2 * C_BLOCK + DEC_SEQ] = zeros
        vb_ref[C_BLOCK + DEC_SEQ:2 * C_BLOCK + DEC_SEQ] = zeros
        kb_ref[C_BLOCK:C_BLOCK + DEC_SEQ] = _rope(
            k_ref[...], cosk_ref[...], sink_rot_ref[...]).astype(BF16)
        vb_ref[C_BLOCK:C_BLOCK + DEC_SEQ] = v_ref[...].astype(BF16)

    q = (_rope(q_ref[...], cosq_ref[...], sinq_ref[...]) * HEAD_DIM ** -0.5).astype(BF16)

    rows = C_GROUP * C_BLOCK
    qi = lax.broadcasted_iota(jnp.int32, (rows, 3 * C_BLOCK), 0) % C_BLOCK
    jj = lax.broadcasted_iota(jnp.int32, (rows, 3 * C_BLOCK), 1)
    kpos = (j - 1) * C_BLOCK + jj
    valid = (jj >= qi) & (jj <= qi + 2 * C_BLOCK) & (kpos >= 0) & (kpos < DEC_SEQ)
    band0 = pl.multiple_of(j * C_BLOCK, C_BLOCK)

    for kk in range(C_KV_HEADS):
        heads = [kk * C_GROUP + g for g in range(C_GROUP)]
        qs = jnp.concatenate([q[:, h * HEAD_DIM:(h + 1) * HEAD_DIM] for h in heads], axis=0)
        c0 = kk * HEAD_DIM
        k_ctx = kc_flat[pl.ds(kk, PAST_LEN, stride=C_KV_HEADS), :].astype(BF16)
        v_ctx = vc_flat[pl.ds(kk, PAST_LEN, stride=C_KV_HEADS), :].astype(BF16)
        k_band = kb_ref[pl.ds(band0, 3 * C_BLOCK), c0:c0 + HEAD_DIM]
        v_band = vb_ref[pl.ds(band0, 3 * C_BLOCK), c0:c0 + HEAD_DIM]
        s_ctx = _dot_nt(qs, k_ctx)
        s_band = jnp.where(valid, _dot_nt(qs, k_band), NEG_INF)
        o = _sink_softmax_pv([s_ctx, s_band],
                             _sink_column(sink_ref, sink0 + kk * C_GROUP, C_BLOCK),
                             [v_ctx, v_band])
        for g, h in enumerate(heads):
            o_ref[:, h * HEAD_DIM:(h + 1) * HEAD_DIM] = o[g * C_BLOCK:(g + 1) * C_BLOCK]


def _lat_attn_c(proj, cache_k, cache_v, sink, cos_t, sin_t, j_layer, mix):
    n_blocks = DEC_SEQ // C_BLOCK
    q_row0 = N_CTX_TOK // C_BLOCK
    kv_row0 = N_CTX_TOK // DEC_SEQ
    k_col = C_Q_WIDTH // C_KV_WIDTH
    return pl.pallas_call(
        functools.partial(_lat_attn_c_kernel, sink0=j_layer * C_HEADS),
        out_shape=jax.ShapeDtypeStruct((N_TOK, D_MODEL), F32),
        grid=(DEC_BATCH, n_blocks),
        in_specs=[
            pl.BlockSpec(memory_space=pltpu.SMEM),
            pl.BlockSpec((C_BLOCK, C_Q_WIDTH), lambda b, j: (q_row0 + b * n_blocks + j, 0)),
            pl.BlockSpec((DEC_SEQ, C_KV_WIDTH), lambda b, j: (kv_row0 + b, k_col)),
            pl.BlockSpec((DEC_SEQ, C_KV_WIDTH), lambda b, j: (kv_row0 + b, k_col + 1)),
            pl.BlockSpec((None, None, PAST_LEN, C_KV_HEADS, HEAD_DIM),
                         lambda b, j: (b, j_layer, 0, 0, 0)),
            pl.BlockSpec((None, None, PAST_LEN, C_KV_HEADS, HEAD_DIM),
                         lambda b, j: (b, j_layer, 0, 0, 0)),
            pl.BlockSpec((C_BLOCK, C_Q_WIDTH), lambda b, j: (j, 0)),
            pl.BlockSpec((C_BLOCK, C_Q_WIDTH), lambda b, j: (j, 0)),
            pl.BlockSpec((DEC_SEQ, C_KV_WIDTH), lambda b, j: (0, 0)),
            pl.BlockSpec((DEC_SEQ, C_KV_WIDTH), lambda b, j: (0, 0)),
            pl.BlockSpec(memory_space=pl.ANY),
        ],
        out_specs=pl.BlockSpec((C_BLOCK, C_Q_WIDTH),
                               lambda b, j: (q_row0 + b * n_blocks + j, 0)),
        input_output_aliases={10: 0},
        scratch_shapes=[pltpu.VMEM((DEC_SEQ + 2 * C_BLOCK, C_KV_WIDTH), BF16),
                        pltpu.VMEM((DEC_SEQ + 2 * C_BLOCK, C_KV_WIDTH), BF16)],
        compiler_params=_params(2),
        name="lat_attn_c",
    )(sink.reshape(-1), proj, proj, proj, cache_k, cache_v, cos_t, sin_t, cos_t, sin_t, mix)


def _rope_tables():
    t = jnp.arange(DEC_SEQ)
    pos = jnp.stack([t // GRID_W, t % GRID_W], axis=-1).astype(F32)
    half = HEAD_DIM // 4
    inv = ROPE_BASE ** (-jnp.arange(half, dtype=F32) / half)
    ang = pos[:, :, None] * inv
    cos = jnp.cos(ang)
    sin = jnp.sin(ang)
    cos64 = jnp.stack([cos, cos], axis=2).reshape(DEC_SEQ, HEAD_DIM)
    sin64 = jnp.stack([-sin, sin], axis=2).reshape(DEC_SEQ, HEAD_DIM)
    return jnp.tile(cos64, (1, C_HEADS)), jnp.tile(sin64, (1, C_HEADS))


def _outproj_kernel(x_ref, mod_ref, mix_ref, w_ref, o_ref, wbf_ref):
    @pl.when(pl.program_id(0) == 0)
    def _():
        wbf_ref[...] = w_ref[...].astype(BF16)

    gate = mod_ref[:, 2 * D_MODEL:3 * D_MODEL]
    o_ref[...] = x_ref[...] + gate * _dot(mix_ref[...].astype(BF16), wbf_ref[...])


def _outproj(x, mod4, layer, mix, w_all, w_idx):
    tm = 512
    return pl.pallas_call(
        _outproj_kernel,
        out_shape=jax.ShapeDtypeStruct((N_TOK, D_MODEL), F32),
        grid=(N_TOK // tm,),
        in_specs=[
            pl.BlockSpec((tm, D_MODEL), lambda i: (i, 0)),
            pl.BlockSpec((None, None, 1, 6 * D_MODEL),
                         lambda i: (layer, _group_of_tile(i, tm), 0, 0)),
            pl.BlockSpec((tm, D_MODEL), lambda i: (i, 0)),
            pl.BlockSpec((None, D_MODEL, D_MODEL), lambda i: (w_idx, 0, 0),
                         pipeline_mode=pl.Buffered(1)),
        ],
        out_specs=pl.BlockSpec((tm, D_MODEL), lambda i: (i, 0)),
        scratch_shapes=[pltpu.VMEM((D_MODEL, D_MODEL), BF16)],
        compiler_params=_params(1),
        name="outproj",
    )(x, mod4, mix, w_all)


def _ffn_kernel(x_ref, g_ref, mod_ref, wg_ref, wu_ref, wd_ref, o_ref, h_ref, acc_ref):
    c = pl.program_id(1)

    @pl.when(c == 0)
    def _():
        h_ref[...] = _norm_mod(x_ref[...], g_ref[...], mod_ref, 3).astype(BF16)
        acc_ref[...] = jnp.zeros_like(acc_ref)

    h = h_ref[...]
    gate = _dot(h, wg_ref[...].astype(BF16))
    up = _dot(h, wu_ref[...].astype(BF16))
    act = gate / (1.0 + jnp.exp(-gate)) * up
    acc_ref[...] += _dot(act.astype(BF16), wd_ref[...].astype(BF16))

    @pl.when(c == pl.num_programs(1) - 1)
    def _():
        o_ref[...] = x_ref[...] + mod_ref[:, 5 * D_MODEL:6 * D_MODEL] * acc_ref[...]


def _ffn(x, g_all, mod4, layer, w_gate_up, w_down):
    tm = 1024
    th = 256
    n_chunks = FFN_HIDDEN // th
    return pl.pallas_call(
        _ffn_kernel,
        out_shape=jax.ShapeDtypeStruct((N_TOK, D_MODEL), F32),
        grid=(N_TOK // tm, n_chunks),
        in_specs=[
            pl.BlockSpec((tm, D_MODEL), lambda i, c: (i, 0)),
            pl.BlockSpec((None, 1, D_MODEL), lambda i, c: (layer, 0, 0)),
            pl.BlockSpec((None, None, 1, 6 * D_MODEL),
                         lambda i, c: (layer, _group_of_tile(i, tm), 0, 0)),
            pl.BlockSpec((None, D_MODEL, th), lambda i, c: (layer, 0, c)),
            pl.BlockSpec((None, D_MODEL, th), lambda i, c: (layer, 0, n_chunks + c)),
            pl.BlockSpec((None, th, D_MODEL), lambda i, c: (layer, c, 0)),
        ],
        out_specs=pl.BlockSpec((tm, D_MODEL), lambda i, c: (i, 0)),
        scratch_shapes=[pltpu.VMEM((tm, D_MODEL), BF16), pltpu.VMEM((tm, D_MODEL), F32)],
        compiler_params=_params(2),
        name="ffn",
    )(x, g_all.reshape(DEPTH, 1, D_MODEL), mod4, w_gate_up, w_gate_up, w_down)


def _final_norm_kernel(x_ref, g_ref, ctx_ref, lat_ref, *, n_ctx_tiles):
    x = x_ref[...]
    var = jnp.mean(x * x, axis=-1, keepdims=True)
    y = x * lax.rsqrt(var + EPS) * g_ref[...]
    is_ctx = pl.program_id(0) < n_ctx_tiles

    @pl.when(is_ctx)
    def _():
        ctx_ref[...] = y

    @pl.when(jnp.logical_not(is_ctx))
    def _():
        lat_ref[...] = y


def _final_norm(x, g):
    tm = 1024
    n_ctx_tiles = N_CTX_TOK // tm
    return pl.pallas_call(
        functools.partial(_final_norm_kernel, n_ctx_tiles=n_ctx_tiles),
        out_shape=(jax.ShapeDtypeStruct((N_CTX_TOK, D_MODEL), F32),
                   jax.ShapeDtypeStruct((N_LAT_TOK, D_MODEL), F32)),
        grid=(N_TOK // tm,),
        in_specs=[pl.BlockSpec((tm, D_MODEL), lambda i: (i, 0)),
                  pl.BlockSpec((1, D_MODEL), lambda i: (0, 0))],
        out_specs=(pl.BlockSpec((tm, D_MODEL), lambda i: (jnp.minimum(i, n_ctx_tiles - 1), 0)),
                   pl.BlockSpec((tm, D_MODEL), lambda i: (jnp.maximum(i - n_ctx_tiles, 0), 0))),
        compiler_params=_params(1),
        name="final_norm",
    )(x, g.reshape(1, D_MODEL))


def kernel(x_prompt, x_sample, cache_a_k, cache_a_v, cache_c_k, cache_c_v, c, c_ctx, w_mod, b_mod, norm_mix, norm_ffn, w_in_ab, rpb_a, w_pool, pool_scale, w_out_ab, w_in_c, sink_c, w_out_c, w_gate_up, w_down, norm_final):
    xs = (x_prompt.reshape(N_CTX_TOK, D_MODEL), x_sample.reshape(N_LAT_TOK, D_MODEL))
    cond8 = jnp.concatenate(
        [c_ctx[None], c, jnp.zeros((N_GROUPS_PAD - 1 - DEC_BATCH, D_MODEL), F32)], axis=0)
    mod4 = _modulation(cond8, w_mod, b_mod).reshape(DEPTH, N_GROUPS_PAD, 1, 6 * D_MODEL)

    n_ab = cache_a_k.shape[1]
    n_c = cache_c_k.shape[1]
    cos_t, sin_t = _rope_tables()
    bias_tiles = _na_bias_tiles(rpb_a)

    new_a = []
    new_c = []
    for l in range(DEPTH):
        if l % 2 == 0:
            i = l // 2
            proj, *rest = _inproj(xs, norm_mix, mod4, l, w_in_ab, i, A_HEADS,
                                  A_WIDTH, 2 * A_WIDTH, n_ab, new_a)
            new_a = rest[:2]
            if len(xs) == 2:
                xs = (rest[2],)
            mix = _ctx_attn_a(proj)
            mix = _lat_attn_a(proj, cache_a_k, cache_a_v, bias_tiles, i, mix)
            mix = _pool(proj, w_pool, pool_scale, i, SEQ, BATCH, 0, mix)
            mix = _pool(proj, w_pool, pool_scale, i, DEC_SEQ, DEC_BATCH,
                        N_CTX_TOK // DEC_SEQ, mix)
            x = _outproj(xs[0], mod4, l, mix, w_out_ab, i)
        else:
            j = l // 2
            proj, *new_c = _inproj(xs, norm_mix, mod4, l, w_in_c, j, C_KV_HEADS,
                                   C_Q_WIDTH, C_Q_WIDTH + C_KV_WIDTH, n_c, new_c)
            mix = _ctx_attn_c(proj, sink_c, j)
            mix = _lat_attn_c(proj, cache_c_k, cache_c_v, sink_c, cos_t, sin_t, j, mix)
            x = _outproj(xs[0], mod4, l, mix, w_out_c, j)
        xs = (_ffn(x, norm_ffn, mod4, l, w_gate_up, w_down),)

    x = xs[0]

    y_ctx, y_lat = _final_norm(x, norm_final)
    return (y_ctx.reshape(BATCH, SEQ, D_MODEL), y_lat.reshape(DEC_BATCH, DEC_SEQ, D_MODEL),
            new_a[0], new_a[1], new_c[0], new_c[1])
```

```python
import functools

import jax
import jax.numpy as jnp
from jax import lax
from jax.experimental import pallas as pl
from jax.experimental.pallas import tpu as pltpu

D_MODEL = 1024
BATCH = 16
SEQ = 256
DEPTH = 4
DEC_BATCH = 2
DEC_SEQ = 1024
PAST_LEN = 512
GRID_W = 64
HEAD_DIM = 64
A_WIDTH = 512
A_HEADS = 8
B_WIDTH = 512
POOL_WINDOWS = (2, 4, 8, 16)
B_GROUP_DIM = 128
NA_ROWS = 8
NA_COLS = 16
C_HEADS = 16
C_KV_HEADS = 4
C_GROUP = C_HEADS // C_KV_HEADS
C_Q_WIDTH = 1024
C_KV_WIDTH = 256
C_BLOCK = 128
FFN_HIDDEN = 2816
ROPE_BASE = 10000.0
EPS = 1e-6
NEG_INF = -1e30

N_CTX_TOK = BATCH * SEQ
N_LAT_TOK = DEC_BATCH * DEC_SEQ
N_TOK = N_CTX_TOK + N_LAT_TOK
GRID_ROWS = DEC_SEQ // GRID_W
N_GROUPS_PAD = 8

VMEM_LIMIT = 56 * 1024 * 1024

F32 = jnp.float32
BF16 = jnp.bfloat16


def _params(n_axes):
    return pltpu.CompilerParams(dimension_semantics=("arbitrary",) * n_axes,
                                vmem_limit_bytes=VMEM_LIMIT)


def _group_of_tile(i, tm):
    row0 = i * tm
    return jnp.where(row0 < N_CTX_TOK, 0, 1 + (row0 - N_CTX_TOK) // DEC_SEQ)


def _dot_nt(a, b):
    return lax.dot_general(a, b, (((1,), (1,)), ((), ())), preferred_element_type=F32)


def _dot(a, b):
    return jnp.dot(a, b, preferred_element_type=F32)


def _mod_kernel(cond_ref, w_ref, b_ref, o_ref):
    c = cond_ref[...]
    s = c / (1.0 + jnp.exp(-c))
    o_ref[...] = _dot(s.astype(BF16), w_ref[...].astype(BF16)) + b_ref[...]


def _modulation(cond8, w_mod, b_mod):
    tn = 1536
    return pl.pallas_call(
        _mod_kernel,
        out_shape=jax.ShapeDtypeStruct((DEPTH, N_GROUPS_PAD, 6 * D_MODEL), F32),
        grid=(DEPTH, 6 * D_MODEL // tn),
        in_specs=[
            pl.BlockSpec((N_GROUPS_PAD, D_MODEL), lambda l, j: (0, 0)),
            pl.BlockSpec((None, D_MODEL, tn), lambda l, j: (l, 0, j)),
            pl.BlockSpec((None, 1, tn), lambda l, j: (l, 0, j)),
        ],
        out_specs=pl.BlockSpec((None, N_GROUPS_PAD, tn), lambda l, j: (l, 0, j)),
        compiler_params=_params(2),
        name="modulation",
    )(cond8, w_mod, b_mod.reshape(DEPTH, 1, 6 * D_MODEL))


def _norm_mod(x, g, mod_ref, shift_idx):
    var = jnp.mean(x * x, axis=-1, keepdims=True)
    y = x * lax.rsqrt(var + EPS) * g
    shift = mod_ref[:, shift_idx * D_MODEL:(shift_idx + 1) * D_MODEL]
    scale = mod_ref[:, (shift_idx + 1) * D_MODEL:(shift_idx + 2) * D_MODEL]
    return y * (1.0 + scale) + shift


def _inproj_kernel(*refs, tm, n_heads, k_col, v_col, n_prev, split_x):
    n_x = 2 if split_x else 1
    g_ref, mod_ref, w_ref = refs[n_x:n_x + 3]
    outs = refs[n_x + 3 + n_prev:]
    o_ref, ck_ref, cv_ref = outs[:3]
    wbf_ref = outs[-1]
    i = pl.program_id(0)
    is_ctx = i < N_CTX_TOK // tm

    @pl.when(i == 0)
    def _():
        wbf_ref[...] = w_ref[...].astype(BF16)

    if split_x:
        x = jnp.where(is_ctx, refs[0][...], refs[1][...])
        outs[3][...] = x
    else:
        x = refs[0][...]
    h = _norm_mod(x, g_ref[...], mod_ref, 0)
    o_ref[...] = _dot(h.astype(BF16), wbf_ref[...])

    @pl.when(is_ctx)
    def _():
        rows = SEQ * n_heads
        for col, c_ref in ((k_col, ck_ref), (v_col, cv_ref)):
            flat = c_ref.reshape(tm // SEQ * rows, HEAD_DIM)
            for r in range(tm // SEQ):
                for hd in range(n_heads):
                    c0 = col + hd * HEAD_DIM
                    flat[pl.ds(r * rows + hd, SEQ, stride=n_heads), :] = (
                        o_ref[r * SEQ:(r + 1) * SEQ, c0:c0 + HEAD_DIM])


def _inproj(xs, g_all, mod4, layer, w_all, w_idx, n_heads, k_col, v_col, n_slots, prev_caches):
    tm = 512
    n_out = w_all.shape[2]
    n_ctx_tiles = N_CTX_TOK // tm
    split_x = len(xs) == 2
    cache_shape = jax.ShapeDtypeStruct((BATCH, n_slots, SEQ, n_heads, HEAD_DIM), F32)
    cache_spec = pl.BlockSpec(
        (tm // SEQ, None, SEQ, n_heads, HEAD_DIM),
        lambda i: (jnp.minimum(i, n_ctx_tiles - 1), w_idx, 0, 0, 0))
    row_spec = pl.BlockSpec((tm, D_MODEL), lambda i: (i, 0))
    if split_x:
        x_specs = [
            pl.BlockSpec((tm, D_MODEL), lambda i: (jnp.minimum(i, n_ctx_tiles - 1), 0)),
            pl.BlockSpec((tm, D_MODEL), lambda i: (jnp.maximum(i - n_ctx_tiles, 0), 0)),
        ]
    else:
        x_specs = [row_spec]
    out_shape = [jax.ShapeDtypeStruct((N_TOK, n_out), F32), cache_shape, cache_shape]
    out_specs = [pl.BlockSpec((tm, n_out), lambda i: (i, 0)), cache_spec, cache_spec]
    if split_x:
        out_shape.append(jax.ShapeDtypeStruct((N_TOK, D_MODEL), F32))
        out_specs.append(row_spec)
    n_prev = len(prev_caches)
    n_x = len(xs)
    return pl.pallas_call(
        functools.partial(_inproj_kernel, tm=tm, n_heads=n_heads, k_col=k_col, v_col=v_col,
                          n_prev=n_prev, split_x=split_x),
        out_shape=out_shape,
        grid=(N_TOK // tm,),
        in_specs=x_specs + [
            pl.BlockSpec((None, 1, D_MODEL), lambda i: (layer, 0, 0)),
            pl.BlockSpec((None, None, 1, 6 * D_MODEL),
                         lambda i: (layer, _group_of_tile(i, tm), 0, 0)),
            pl.BlockSpec((None, D_MODEL, n_out), lambda i: (w_idx, 0, 0),
                         pipeline_mode=pl.Buffered(1)),
        ] + [pl.BlockSpec(memory_space=pl.ANY)] * n_prev,
        out_specs=out_specs,
        scratch_shapes=[pltpu.VMEM((D_MODEL, n_out), BF16)],
        input_output_aliases={n_x + 3 + k: 1 + k for k in range(n_prev)},
        compiler_params=_params(1),
        name="inproj",
    )(*xs, g_all.reshape(DEPTH, 1, D_MODEL), mod4, w_all, *prev_caches)


def _ctx_attn_a_kernel(p_ref, o_ref):
    for h in range(A_HEADS):
        c0 = h * HEAD_DIM
        q = (p_ref[:, c0:c0 + HEAD_DIM] * HEAD_DIM ** -0.5).astype(BF16)
        k = p_ref[:, A_WIDTH + c0:A_WIDTH + c0 + HEAD_DIM].astype(BF16)
        v = p_ref[:, 2 * A_WIDTH + c0:2 * A_WIDTH + c0 + HEAD_DIM].astype(BF16)
        s = _dot_nt(q, k)
        m = jnp.max(s, axis=-1, keepdims=True)
        p = jnp.exp(s - m)
        l = jnp.sum(p, axis=-1, keepdims=True)
        o_ref[:, c0:c0 + HEAD_DIM] = _dot(p.astype(BF16), v) / l


def _ctx_attn_a(proj):
    return pl.pallas_call(
        _ctx_attn_a_kernel,
        out_shape=jax.ShapeDtypeStruct((N_TOK, D_MODEL), F32),
        grid=(BATCH,),
        in_specs=[pl.BlockSpec((SEQ, 4 * A_WIDTH), lambda b: (b, 0))],
        out_specs=pl.BlockSpec((SEQ, A_WIDTH), lambda b: (b, 0)),
        compiler_params=_params(1),
        name="ctx_attn_a",
    )(proj)


N_DROW = 2 * NA_ROWS - 1
N_DCOL = 2 * NA_COLS - 1
N_BIAS_TILES = 16
BIAS_TILE_LEFT_PAD = 14
BIAS_TILE_RIGHT_PAD = 15
MID_DROW = NA_ROWS - 1 - NA_ROWS // 2


def _na_bias_kernel(rpb_ref, o_ref):
    base = (pl.program_id(0) * A_HEADS + pl.program_id(1)) * (N_DROW * N_DCOL)
    qi = lax.broadcasted_iota(jnp.int32, (GRID_W, 2 * GRID_W), 0)
    lane = lax.broadcasted_iota(jnp.int32, (GRID_W, 2 * GRID_W), 1)
    right = lane >= GRID_W
    kc = jnp.where(right, lane - GRID_W, lane)
    rel = kc - qi + (NA_COLS - 1)
    qstart = jnp.clip(qi - NA_COLS // 2, 0, GRID_W - NA_COLS)
    valid = (kc >= qstart) & (kc < qstart + NA_COLS)

    def tile(dr_left, dr_right):
        acc = jnp.zeros((GRID_W, 2 * GRID_W), F32)
        for d in range(N_DCOL):
            if dr_left is None:
                val = rpb_ref[base + dr_right * N_DCOL + d]
            elif dr_right is None:
                val = rpb_ref[base + dr_left * N_DCOL + d]
            else:
                val = jnp.where(right, rpb_ref[base + dr_right * N_DCOL + d],
                                rpb_ref[base + dr_left * N_DCOL + d])
            acc = jnp.where(rel == d, val, acc)
        ok = valid
        if dr_left is None:
            ok = ok & right
        if dr_right is None:
            ok = ok & jnp.logical_not(right)
        return jnp.where(ok, acc, NEG_INF)

    for t in range(N_DROW - 1):
        o_ref[t] = tile(t, t + 1)
    o_ref[BIAS_TILE_LEFT_PAD] = tile(None, MID_DROW)
    o_ref[BIAS_TILE_RIGHT_PAD] = tile(MID_DROW + NA_ROWS - 1, None)


def _na_bias_tiles(rpb_a):
    n_layers = rpb_a.shape[0]
    return pl.pallas_call(
        _na_bias_kernel,
        out_shape=jax.ShapeDtypeStruct(
            (n_layers, A_HEADS, N_BIAS_TILES, GRID_W, 2 * GRID_W), F32),
        grid=(n_layers, A_HEADS),
        in_specs=[pl.BlockSpec(memory_space=pltpu.SMEM)],
        out_specs=pl.BlockSpec((None, None, N_BIAS_TILES, GRID_W, 2 * GRID_W),
                               lambda l, h: (l, h, 0, 0, 0)),
        compiler_params=_params(2),
        name="na_bias",
    )(rpb_a.reshape(-1))


def _na_window(r):
    start = min(max(r - NA_ROWS // 2, 0), GRID_ROWS - NA_ROWS)
    first_drow = start - r + NA_ROWS - 1
    if start % 2 == 0:
        return start, [first_drow + 2 * p for p in range(NA_ROWS // 2)]
    assert first_drow == MID_DROW
    inner = [first_drow + 1 + 2 * p for p in range(NA_ROWS // 2 - 1)]
    return start - 1, [BIAS_TILE_LEFT_PAD] + inner + [BIAS_TILE_RIGHT_PAD]


def _lat_attn_a_kernel(q_ref, k_ref, v_ref, kc_ref, vc_ref, bias_ref, mix_ref, o_ref,
                       s_loc, s_ctx, p_loc, p_ctx):
    del mix_ref
    hp = pl.program_id(1)

    @pl.when((pl.program_id(0) == 0) & (hp == 0))
    def _():
        p_loc[...] = jnp.zeros_like(p_loc)

    kc_flat = kc_ref.reshape(PAST_LEN * A_HEADS, HEAD_DIM)
    vc_flat = vc_ref.reshape(PAST_LEN * A_HEADS, HEAD_DIM)
    for hh in range(2):
        c0 = hh * HEAD_DIM
        head = hp * 2 + hh
        q = (q_ref[:, c0:c0 + HEAD_DIM] * HEAD_DIM ** -0.5).astype(BF16)
        k = k_ref[:, c0:c0 + HEAD_DIM].astype(BF16)
        v = v_ref[:, c0:c0 + HEAD_DIM].astype(BF16)
        kc = kc_flat[pl.ds(head, PAST_LEN, stride=A_HEADS), :].astype(BF16)
        vc = vc_flat[pl.ds(head, PAST_LEN, stride=A_HEADS), :].astype(BF16)
        s_loc[...] = _dot_nt(q, k)
        s_ctx[...] = _dot_nt(q, kc)
        inv_l = []
        for r in range(GRID_ROWS):
            rows = slice(r * GRID_W, (r + 1) * GRID_W)
            first_row, tiles = _na_window(r)
            cols = slice(first_row * GRID_W, (first_row + 2 * len(tiles)) * GRID_W)
            bias = jnp.concatenate([bias_ref[hh, t] for t in tiles], axis=1)
            sc = s_ctx[rows, :]
            sl = s_loc[rows, cols] + bias
            m = jnp.maximum(jnp.max(sc, axis=-1, keepdims=True),
                            jnp.max(sl, axis=-1, keepdims=True))
            pc = jnp.exp(sc - m)
            pw = jnp.exp(sl - m)
            inv_l.append(1.0 / (jnp.sum(pc, axis=-1, keepdims=True)
                                + jnp.sum(pw, axis=-1, keepdims=True)))
            p_ctx[rows, :] = pc.astype(BF16)
            p_loc[rows, cols] = pw.astype(BF16)
        o = _dot(p_ctx[...], vc) + _dot(p_loc[...], v)
        o_ref[:, c0:c0 + HEAD_DIM] = o * jnp.concatenate(inv_l, axis=0)


def _lat_attn_a(proj, cache_k, cache_v, bias_tiles, i_layer, mix):
    row0 = N_CTX_TOK // DEC_SEQ
    pair = 2 * HEAD_DIM
    k_col0 = A_WIDTH // pair
    v_col0 = 2 * A_WIDTH // pair
    return pl.pallas_call(
        _lat_attn_a_kernel,
        out_shape=jax.ShapeDtypeStruct((N_TOK, D_MODEL), F32),
        grid=(DEC_BATCH, A_HEADS // 2),
        in_specs=[
            pl.BlockSpec((DEC_SEQ, pair), lambda b, hp: (row0 + b, hp)),
            pl.BlockSpec((DEC_SEQ, pair), lambda b, hp: (row0 + b, k_col0 + hp)),
            pl.BlockSpec((DEC_SEQ, pair), lambda b, hp: (row0 + b, v_col0 + hp)),
            pl.BlockSpec((None, None, PAST_LEN, A_HEADS, HEAD_DIM),
                         lambda b, hp: (b, i_layer, 0, 0, 0)),
            pl.BlockSpec((None, None, PAST_LEN, A_HEADS, HEAD_DIM),
                         lambda b, hp: (b, i_layer, 0, 0, 0)),
            pl.BlockSpec((None, 2, N_BIAS_TILES, GRID_W, 2 * GRID_W),
                         lambda b, hp: (i_layer, hp, 0, 0, 0)),
            pl.BlockSpec(memory_space=pl.ANY),
        ],
        out_specs=pl.BlockSpec((DEC_SEQ, pair), lambda b, hp: (row0 + b, hp)),
        scratch_shapes=[pltpu.VMEM((DEC_SEQ, DEC_SEQ), F32),
                        pltpu.VMEM((DEC_SEQ, PAST_LEN), F32),
                        pltpu.VMEM((DEC_SEQ, DEC_SEQ), BF16),
                        pltpu.VMEM((DEC_SEQ, PAST_LEN), BF16)],
        input_output_aliases={6: 0},
        compiler_params=_params(2),
        name="lat_attn_a",
    )(proj, proj, proj, cache_k, cache_v, bias_tiles, mix)


def _pool_kernel(u_ref, wp_ref, ps_ref, mix_ref, o_ref, *, n):
    del mix_ref
    t = lax.broadcasted_iota(jnp.int32, (n, n), 0)
    j = lax.broadcasted_iota(jnp.int32, (n, n), 1)
    tc = lax.broadcasted_iota(jnp.int32, (n, 1), 0)
    for g, w in enumerate(POOL_WINDOWS):
        c0 = g * B_GROUP_DIM
        lo = jnp.maximum(t - w // 2, 0)
        hi = jnp.minimum(t - w // 2 + w, n)
        band = jnp.where((j >= lo) & (j < hi), 1.0, 0.0).astype(BF16)
        count = (jnp.minimum(tc - w // 2 + w, n) - jnp.maximum(tc - w // 2, 0)).astype(F32)
        u = u_ref[:, c0:c0 + B_GROUP_DIM]
        u_hi = u.astype(BF16)
        u_lo = (u - u_hi.astype(F32)).astype(BF16)
        total = _dot(band, u_hi) + _dot(band, u_lo)
        pooled = total / count - u
        y = _dot(pooled.astype(BF16), wp_ref[g].astype(BF16))
        o_ref[:, c0:c0 + B_GROUP_DIM] = y * ps_ref[:, c0:c0 + B_GROUP_DIM]


def _pool(proj, w_pool, pool_scale, i_layer, n, n_seq, row_block0, mix):
    n_groups = len(POOL_WINDOWS)
    return pl.pallas_call(
        functools.partial(_pool_kernel, n=n),
        out_shape=jax.ShapeDtypeStruct((N_TOK, D_MODEL), F32),
        grid=(n_seq,),
        in_specs=[
            pl.BlockSpec((n, B_WIDTH), lambda b: (row_block0 + b, 3)),
            pl.BlockSpec((None, n_groups, B_GROUP_DIM, B_GROUP_DIM),
                         lambda b: (i_layer, 0, 0, 0)),
            pl.BlockSpec((None, 1, B_WIDTH), lambda b: (i_layer, 0, 0)),
            pl.BlockSpec(memory_space=pl.ANY),
        ],
        out_specs=pl.BlockSpec((n, B_WIDTH), lambda b: (row_block0 + b, 1)),
        input_output_aliases={3: 0},
        compiler_params=_params(1),
        name="pool_mixer",
    )(proj, w_pool, pool_scale.reshape(-1, 1, B_WIDTH), mix)


def _sink_softmax_pv(scores, sinks, values):
    m = sinks
    for s in scores:
        m = jnp.maximum(m, jnp.max(s, axis=-1, keepdims=True))
    l = jnp.exp(sinks - m)
    o = None
    for s, v in zip(scores, values):
        p = jnp.exp(s - m)
        l = l + jnp.sum(p, axis=-1, keepdims=True)
        pv = _dot(p.astype(BF16), v)
        o = pv if o is None else o + pv
    return o / l


def _sink_column(sink_ref, head0, rows_per_head):
    rows = lax.broadcasted_iota(jnp.int32, (C_GROUP * rows_per_head, 1), 0)
    col = jnp.full((C_GROUP * rows_per_head, 1), sink_ref[head0], F32)
    for g in range(1, C_GROUP):
        col = jnp.where(rows >= g * rows_per_head, sink_ref[head0 + g], col)
    return col


def _ctx_attn_c_kernel(sink_ref, p_ref, o_ref, *, sink0):
    for kk in range(C_KV_HEADS):
        heads = [kk * C_GROUP + g for g in range(C_GROUP)]
        q = jnp.concatenate(
            [p_ref[:, h * HEAD_DIM:(h + 1) * HEAD_DIM] for h in heads], axis=0)
        q = (q * HEAD_DIM ** -0.5).astype(BF16)
        k0 = C_Q_WIDTH + kk * HEAD_DIM
        v0 = C_Q_WIDTH + C_KV_WIDTH + kk * HEAD_DIM
        k = p_ref[:, k0:k0 + HEAD_DIM].astype(BF16)
        v = p_ref[:, v0:v0 + HEAD_DIM].astype(BF16)
        o = _sink_softmax_pv([_dot_nt(q, k)],
                             _sink_column(sink_ref, sink0 + kk * C_GROUP, SEQ), [v])
        for g, h in enumerate(heads):
            o_ref[:, h * HEAD_DIM:(h + 1) * HEAD_DIM] = o[g * SEQ:(g + 1) * SEQ]


def _ctx_attn_c(proj, sink_all, j_layer):
    return pl.pallas_call(
        functools.partial(_ctx_attn_c_kernel, sink0=j_layer * C_HEADS),
        out_shape=jax.ShapeDtypeStruct((N_TOK, D_MODEL), F32),
        grid=(BATCH,),
        in_specs=[
            pl.BlockSpec(memory_space=pltpu.SMEM),
            pl.BlockSpec((SEQ, C_Q_WIDTH + 2 * C_KV_WIDTH), lambda b: (b, 0)),
        ],
        out_specs=pl.BlockSpec((SEQ, C_Q_WIDTH), lambda b: (b, 0)),
        compiler_params=_params(1),
        name="ctx_attn_c",
    )(sink_all.reshape(-1), proj)


def _rope(x, cos, sin_signed):
    n = x.shape[-1]
    lane = lax.broadcasted_iota(jnp.int32, x.shape, x.ndim - 1)
    first = (lane % 32) < 16
    partner = jnp.where(first, pltpu.roll(x, n - 16, axis=x.ndim - 1),
                        pltpu.roll(x, 16, axis=x.ndim - 1))
    return x * cos + partner * sin_signed


def _lat_attn_c_kernel(sink_ref, q_ref, k_ref, v_ref, kc_ref, vc_ref, cosq_ref, sinq_ref,
                       cosk_ref, sink_rot_ref, mix_ref, o_ref, kb_ref, vb_ref, *, sink0):
    del mix_ref
    j = pl.program_id(1)
    kc_flat = kc_ref.reshape(PAST_LEN * C_KV_HEADS, HEAD_DIM)
    vc_flat = vc_ref.reshape(PAST_LEN * C_KV_HEADS, HEAD_DIM)
    n_blocks = DEC_SEQ // C_BLOCK

    @pl.when(j == 0)
    def _():
        zeros = jnp.zeros((C_BLOCK, C_KV_WIDTH), BF16)
        kb_ref[0:C_BLOCK] = zeros
        vb_ref[0:C_BLOCK] = zeros
        kb_ref[C_BLOCK + DEC_SEQ:2 * C_BLOCK + DEC_SEQ] = zeros
        vb_ref[C_BLOCK + DEC_SEQ:2 * C_BLOCK + DEC_SEQ] = zeros
        kb_ref[C_BLOCK:C_BLOCK + DEC_SEQ] = _rope(
            k_ref[...], cosk_ref[...], sink_rot_ref[...]).astype(BF16)
        vb_ref[C_BLOCK:C_BLOCK + DEC_SEQ] = v_ref[...].astype(BF16)

    q = (_rope(q_ref[...], cosq_ref[...], sinq_ref[...]) * HEAD_DIM ** -0.5).astype(BF16)

    rows = C_GROUP * C_BLOCK
    qi = lax.broadcasted_iota(jnp.int32, (rows, 3 * C_BLOCK), 0) % C_BLOCK
    jj = lax.broadcasted_iota(jnp.int32, (rows, 3 * C_BLOCK), 1)
    kpos = (j - 1) * C_BLOCK + jj
    valid = (jj >= qi) & (jj <= qi + 2 * C_BLOCK) & (kpos >= 0) & (kpos < DEC_SEQ)
    band0 = pl.multiple_of(j * C_BLOCK, C_BLOCK)

    for kk in range(C_KV_HEADS):
        heads = [kk * C_GROUP + g for g in range(C_GROUP)]
        qs = jnp.concatenate([q[:, h * HEAD_DIM:(h + 1) * HEAD_DIM] for h in heads], axis=0)
        c0 = kk * HEAD_DIM
        k_ctx = kc_flat[pl.ds(kk, PAST_LEN, stride=C_KV_HEADS), :].astype(BF16)
        v_ctx = vc_flat[pl.ds(kk, PAST_LEN, stride=C_KV_HEADS), :].astype(BF16)
        k_band = kb_ref[pl.ds(band0, 3 * C_BLOCK), c0:c0 + HEAD_DIM]
        v_band = vb_ref[pl.ds(band0, 3 * C_BLOCK), c0:c0 + HEAD_DIM]
        s_ctx = _dot_nt(qs, k_ctx)
        s_band = jnp.where(valid, _dot_nt(qs, k_band), NEG_INF)
        o = _sink_softmax_pv([s_ctx, s_band],
                             _sink_column(sink_ref, sink0 + kk * C_GROUP, C_BLOCK),
                             [v_ctx, v_band])
        for g, h in enumerate(heads):
            o_ref[:, h * HEAD_DIM:(h + 1) * HEAD_DIM] = o[g * C_BLOCK:(g + 1) * C_BLOCK]


def _lat_attn_c(proj, cache_k, cache_v, sink, cos_t, sin_t, j_layer, mix):
    n_blocks = DEC_SEQ // C_BLOCK
    q_row0 = N_CTX_TOK // C_BLOCK
    kv_row0 = N_CTX_TOK // DEC_SEQ
    k_col = C_Q_WIDTH // C_KV_WIDTH
    return pl.pallas_call(
        functools.partial(_lat_attn_c_kernel, sink0=j_layer * C_HEADS),
        out_shape=jax.ShapeDtypeStruct((N_TOK, D_MODEL), F32),
        grid=(DEC_BATCH, n_blocks),
        in_specs=[
            pl.BlockSpec(memory_space=pltpu.SMEM),
            pl.BlockSpec((C_BLOCK, C_Q_WIDTH), lambda b, j: (q_row0 + b * n_blocks + j, 0)),
            pl.BlockSpec((DEC_SEQ, C_KV_WIDTH), lambda b, j: (kv_row0 + b, k_col)),
            pl.BlockSpec((DEC_SEQ, C_KV_WIDTH), lambda b, j: (kv_row0 + b, k_col + 1)),
            pl.BlockSpec((None, None, PAST_LEN, C_KV_HEADS, HEAD_DIM),
                         lambda b, j: (b, j_layer, 0, 0, 0)),
            pl.BlockSpec((None, None, PAST_LEN, C_KV_HEADS, HEAD_DIM),
                         lambda b, j: (b, j_layer, 0, 0, 0)),
            pl.BlockSpec((C_BLOCK, C_Q_WIDTH), lambda b, j: (j, 0)),
            pl.BlockSpec((C_BLOCK, C_Q_WIDTH), lambda b, j: (j, 0)),
            pl.BlockSpec((DEC_SEQ, C_KV_WIDTH), lambda b, j: (0, 0)),
            pl.BlockSpec((DEC_SEQ, C_KV_WIDTH), lambda b, j: (0, 0)),
            pl.BlockSpec(memory_space=pl.ANY),
        ],
        out_specs=pl.BlockSpec((C_BLOCK, C_Q_WIDTH),
                               lambda b, j: (q_row0 + b * n_blocks + j, 0)),
        input_output_aliases={10: 0},
        scratch_shapes=[pltpu.VMEM((DEC_SEQ + 2 * C_BLOCK, C_KV_WIDTH), BF16),
                        pltpu.VMEM((DEC_SEQ + 2 * C_BLOCK, C_KV_WIDTH), BF16)],
        compiler_params=_params(2),
        name="lat_attn_c",
    )(sink.reshape(-1), proj, proj, proj, cache_k, cache_v, cos_t, sin_t, cos_t, sin_t, mix)


def _rope_tables():
    t = jnp.arange(DEC_SEQ)
    pos = jnp.stack([t // GRID_W, t % GRID_W], axis=-1).astype(F32)
    half = HEAD_DIM // 4
    inv = ROPE_BASE ** (-jnp.arange(half, dtype=F32) / half)
    ang = pos[:, :, None] * inv
    cos = jnp.cos(ang)
    sin = jnp.sin(ang)
    cos64 = jnp.stack([cos, cos], axis=2).reshape(DEC_SEQ, HEAD_DIM)
    sin64 = jnp.stack([-sin, sin], axis=2).reshape(DEC_SEQ, HEAD_DIM)
    return jnp.tile(cos64, (1, C_HEADS)), jnp.tile(sin64, (1, C_HEADS))


FFN_CHUNK = 256
N_FFN_CHUNKS = FFN_HIDDEN // FFN_CHUNK
N_WO_PIECES = D_MODEL // FFN_CHUNK


def _load_layer_weights(wo_hbm, wgu_hbm, wd_hbm, wo_s, wg_s, wu_s, wd_s, stage_col, stage_row,
                        sem_col, sem_row, *, layer, wo_idx):
    def col_copy(which, c, slot):
        src = wgu_hbm.at[layer, :, pl.ds(which * FFN_HIDDEN + c * FFN_CHUNK, FFN_CHUNK)]
        return pltpu.make_async_copy(src, stage_col.at[which, slot], sem_col.at[which, slot])

    def row_copy(p, slot):
        if isinstance(p, int) and p >= N_FFN_CHUNKS:
            src = wo_hbm.at[wo_idx, pl.ds((p - N_FFN_CHUNKS) * FFN_CHUNK, FFN_CHUNK), :]
        else:
            src = wd_hbm.at[layer, pl.ds(p * FFN_CHUNK, FFN_CHUNK), :]
        return pltpu.make_async_copy(src, stage_row.at[slot], sem_row.at[slot])

    for which in range(2):
        col_copy(which, 0, 0).start()
    row_copy(0, 0).start()

    def chunk(c, carry):
        slot = c % 2

        @pl.when(c + 1 < N_FFN_CHUNKS)
        def _():
            for which in range(2):
                col_copy(which, c + 1, 1 - slot).start()
            row_copy(c + 1, 1 - slot).start()

        @pl.when(c + 1 == N_FFN_CHUNKS)
        def _():
            row_copy(N_FFN_CHUNKS, 1 - slot).start()

        col_copy(0, c, slot).wait()
        wg_s[c] = stage_col[0, slot].astype(BF16)
        col_copy(1, c, slot).wait()
        wu_s[c] = stage_col[1, slot].astype(BF16)
        row_copy(c, slot).wait()
        wd_s[pl.ds(pl.multiple_of(c * FFN_CHUNK, FFN_CHUNK), FFN_CHUNK), :] = (
            stage_row[slot].astype(BF16))
        return carry

    lax.fori_loop(0, N_FFN_CHUNKS, chunk, 0)

    for k in range(N_WO_PIECES):
        p = N_FFN_CHUNKS + k
        slot = p % 2
        if k + 1 < N_WO_PIECES:
            row_copy(p + 1, 1 - slot).start()
        row_copy(p, slot).wait()
        wo_s[k * FFN_CHUNK:(k + 1) * FFN_CHUNK, :] = stage_row[slot].astype(BF16)


def _post_mixer_kernel(x_ref, mix_ref, g_ref, mod_ref, wo_hbm, wgu_hbm, wd_hbm, o_ref,
                       wo_s, wg_s, wu_s, wd_s, act_s, stage_col, stage_row, sem_col, sem_row,
                       *, layer, wo_idx):
    @pl.when(pl.program_id(0) == 0)
    def _():
        _load_layer_weights(wo_hbm, wgu_hbm, wd_hbm, wo_s, wg_s, wu_s, wd_s, stage_col,
                            stage_row, sem_col, sem_row, layer=layer, wo_idx=wo_idx)

    gate1 = mod_ref[:, 2 * D_MODEL:3 * D_MODEL]
    x1 = x_ref[...] + gate1 * _dot(mix_ref[...].astype(BF16), wo_s[...])
    h = _norm_mod(x1, g_ref[...], mod_ref, 3).astype(BF16)
    for c in range(N_FFN_CHUNKS):
        gate = _dot(h, wg_s[c])
        up = _dot(h, wu_s[c])
        act = gate / (1.0 + jnp.exp(-gate)) * up
        act_s[:, c * FFN_CHUNK:(c + 1) * FFN_CHUNK] = act.astype(BF16)
    gate2 = mod_ref[:, 5 * D_MODEL:6 * D_MODEL]
    o_ref[...] = x1 + gate2 * _dot(act_s[...], wd_s[...])


def _post_mixer(x, mix, g_all, mod4, layer, w_out_all, wo_idx, w_gate_up, w_down):
    tm = 512
    row_spec = pl.BlockSpec((tm, D_MODEL), lambda i: (i, 0))
    hbm = pl.BlockSpec(memory_space=pl.ANY)
    return pl.pallas_call(
        functools.partial(_post_mixer_kernel, layer=layer, wo_idx=wo_idx),
        out_shape=jax.ShapeDtypeStruct((N_TOK, D_MODEL), F32),
        grid=(N_TOK // tm,),
        in_specs=[
            row_spec,
            row_spec,
            pl.BlockSpec((None, 1, D_MODEL), lambda i: (layer, 0, 0)),
            pl.BlockSpec((None, None, 1, 6 * D_MODEL),
                         lambda i: (layer, _group_of_tile(i, tm), 0, 0)),
            hbm, hbm, hbm,
        ],
        out_specs=row_spec,
        scratch_shapes=[
            pltpu.VMEM((D_MODEL, D_MODEL), BF16),
            pltpu.VMEM((N_FFN_CHUNKS, D_MODEL, FFN_CHUNK), BF16),
            pltpu.VMEM((N_FFN_CHUNKS, D_MODEL, FFN_CHUNK), BF16),
            pltpu.VMEM((FFN_HIDDEN, D_MODEL), BF16),
            pltpu.VMEM((tm, FFN_HIDDEN), BF16),
            pltpu.VMEM((2, 2, D_MODEL, FFN_CHUNK), F32),
            pltpu.VMEM((2, FFN_CHUNK, D_MODEL), F32),
            pltpu.SemaphoreType.DMA((2, 2)),
            pltpu.SemaphoreType.DMA((2,)),
        ],
        compiler_params=_params(1),
        name="post_mixer",
    )(x, mix, g_all.reshape(DEPTH, 1, D_MODEL), mod4, w_out_all, w_gate_up, w_down)


def _final_norm_kernel(x_ref, g_ref, ctx_ref, lat_ref, *, n_ctx_tiles):
    x = x_ref[...]
    var = jnp.mean(x * x, axis=-1, keepdims=True)
    y = x * lax.rsqrt(var + EPS) * g_ref[...]
    is_ctx = pl.program_id(0) < n_ctx_tiles

    @pl.when(is_ctx)
    def _():
        ctx_ref[...] = y

    @pl.when(jnp.logical_not(is_ctx))
    def _():
        lat_ref[...] = y


def _final_norm(x, g):
    tm = 1024
    n_ctx_tiles = N_CTX_TOK // tm
    return pl.pallas_call(
        functools.partial(_final_norm_kernel, n_ctx_tiles=n_ctx_tiles),
        out_shape=(jax.ShapeDtypeStruct((N_CTX_TOK, D_MODEL), F32),
                   jax.ShapeDtypeStruct((N_LAT_TOK, D_MODEL), F32)),
        grid=(N_TOK // tm,),
        in_specs=[pl.BlockSpec((tm, D_MODEL), lambda i: (i, 0)),
                  pl.BlockSpec((1, D_MODEL), lambda i: (0, 0))],
        out_specs=(pl.BlockSpec((tm, D_MODEL), lambda i: (jnp.minimum(i, n_ctx_tiles - 1), 0)),
                   pl.BlockSpec((tm, D_MODEL), lambda i: (jnp.maximum(i - n_ctx_tiles, 0), 0))),
        compiler_params=_params(1),
        name="final_norm",
    )(x, g.reshape(1, D_MODEL))


def kernel(x_prompt, x_sample, cache_a_k, cache_a_v, cache_c_k, cache_c_v, c, c_ctx, w_mod, b_mod, norm_mix, norm_ffn, w_in_ab, rpb_a, w_pool, pool_scale, w_out_ab, w_in_c, sink_c, w_out_c, w_gate_up, w_down, norm_final):
    xs = (x_prompt.reshape(N_CTX_TOK, D_MODEL), x_sample.reshape(N_LAT_TOK, D_MODEL))
    cond8 = jnp.concatenate(
        [c_ctx[None], c, jnp.zeros((N_GROUPS_PAD - 1 - DEC_BATCH, D_MODEL), F32)], axis=0)
    mod4 = _modulation(cond8, w_mod, b_mod).reshape(DEPTH, N_GROUPS_PAD, 1, 6 * D_MODEL)

    n_ab = cache_a_k.shape[1]
    n_c = cache_c_k.shape[1]
    cos_t, sin_t = _rope_tables()
    bias_tiles = _na_bias_tiles(rpb_a)

    new_a = []
    new_c = []
    for l in range(DEPTH):
        if l % 2 == 0:
            i = l // 2
            proj, *rest = _inproj(xs, norm_mix, mod4, l, w_in_ab, i, A_HEADS,
                                  A_WIDTH, 2 * A_WIDTH, n_ab, new_a)
            new_a = rest[:2]
            if len(xs) == 2:
                xs = (rest[2],)
            mix = _ctx_attn_a(proj)
            mix = _lat_attn_a(proj, cache_a_k, cache_a_v, bias_tiles, i, mix)
            mix = _pool(proj, w_pool, pool_scale, i, SEQ, BATCH, 0, mix)
            mix = _pool(proj, w_pool, pool_scale, i, DEC_SEQ, DEC_BATCH,
                        N_CTX_TOK // DEC_SEQ, mix)
            w_out, wo_idx = w_out_ab, i
        else:
            j = l // 2
            proj, *new_c = _inproj(xs, norm_mix, mod4, l, w_in_c, j, C_KV_HEADS,
                                   C_Q_WIDTH, C_Q_WIDTH + C_KV_WIDTH, n_c, new_c)
            mix = _ctx_attn_c(proj, sink_c, j)
            mix = _lat_attn_c(proj, cache_c_k, cache_c_v, sink_c, cos_t, sin_t, j, mix)
            w_out, wo_idx = w_out_c, j
        xs = (_post_mixer(xs[0], mix, norm_ffn, mod4, l, w_out, wo_idx, w_gate_up, w_down),)

    x = xs[0]

    y_ctx, y_lat = _final_norm(x, norm_final)
    return (y_ctx.reshape(BATCH, SEQ, D_MODEL), y_lat.reshape(DEC_BATCH, DEC_SEQ, D_MODEL),
            new_a[0], new_a[1], new_c[0], new_c[1])
```

```python
import functools

import jax
import jax.numpy as jnp
from jax import lax
from jax.experimental import pallas as pl
from jax.experimental.pallas import tpu as pltpu

D_MODEL = 1024
BATCH = 16
SEQ = 256
DEPTH = 4
DEC_BATCH = 2
DEC_SEQ = 1024
PAST_LEN = 512
GRID_W = 64
HEAD_DIM = 64
A_WIDTH = 512
A_HEADS = 8
B_WIDTH = 512
POOL_WINDOWS = (2, 4, 8, 16)
B_GROUP_DIM = 128
NA_ROWS = 8
NA_COLS = 16
C_HEADS = 16
C_KV_HEADS = 4
C_GROUP = C_HEADS // C_KV_HEADS
C_Q_WIDTH = 1024
C_KV_WIDTH = 256
C_BLOCK = 128
FFN_HIDDEN = 2816
ROPE_BASE = 10000.0
EPS = 1e-6
NEG_INF = -1e30

N_CTX_TOK = BATCH * SEQ
N_LAT_TOK = DEC_BATCH * DEC_SEQ
N_TOK = N_CTX_TOK + N_LAT_TOK
GRID_ROWS = DEC_SEQ // GRID_W
N_GROUPS_PAD = 8

VMEM_LIMIT = 56 * 1024 * 1024

F32 = jnp.float32
BF16 = jnp.bfloat16


def _params(n_axes):
    return pltpu.CompilerParams(dimension_semantics=("arbitrary",) * n_axes,
                                vmem_limit_bytes=VMEM_LIMIT)


def _group_of_tile(i, tm):
    row0 = i * tm
    return jnp.where(row0 < N_CTX_TOK, 0, 1 + (row0 - N_CTX_TOK) // DEC_SEQ)


def _dot_nt(a, b):
    return lax.dot_general(a, b, (((1,), (1,)), ((), ())), preferred_element_type=F32)


def _dot(a, b):
    return jnp.dot(a, b, preferred_element_type=F32)


def _dot_tn(a, b):
    return lax.dot_general(a, b, (((0,), (0,)), ((), ())), preferred_element_type=F32)


def _softmax_pv_t(scores_t, sink_row, values):
    m = sink_row
    for s in scores_t:
        ms = jnp.max(s, axis=0, keepdims=True)
        m = ms if m is None else jnp.maximum(m, ms)
    l = None if sink_row is None else jnp.exp(sink_row - m)
    o = None
    for s, v in zip(scores_t, values):
        p = jnp.exp(s - m)
        ls = jnp.sum(p, axis=0, keepdims=True)
        l = ls if l is None else l + ls
        pv = _dot_tn(v, p.astype(BF16))
        o = pv if o is None else o + pv
    return o * (1.0 / l)


def _mod_kernel(cond_ref, w_ref, b_ref, o_ref):
    c = cond_ref[...]
    s = c / (1.0 + jnp.exp(-c))
    o_ref[...] = _dot(s.astype(BF16), w_ref[...].astype(BF16)) + b_ref[...]


def _modulation(cond8, w_mod, b_mod):
    tn = 1536
    return pl.pallas_call(
        _mod_kernel,
        out_shape=jax.ShapeDtypeStruct((DEPTH, N_GROUPS_PAD, 6 * D_MODEL), F32),
        grid=(DEPTH, 6 * D_MODEL // tn),
        in_specs=[
            pl.BlockSpec((N_GROUPS_PAD, D_MODEL), lambda l, j: (0, 0)),
            pl.BlockSpec((None, D_MODEL, tn), lambda l, j: (l, 0, j)),
            pl.BlockSpec((None, 1, tn), lambda l, j: (l, 0, j)),
        ],
        out_specs=pl.BlockSpec((None, N_GROUPS_PAD, tn), lambda l, j: (l, 0, j)),
        compiler_params=_params(2),
        name="modulation",
    )(cond8, w_mod, b_mod.reshape(DEPTH, 1, 6 * D_MODEL))


def _norm_mod(x, g, mod_ref, shift_idx):
    var = jnp.mean(x * x, axis=-1, keepdims=True)
    y = x * lax.rsqrt(var + EPS) * g
    shift = mod_ref[:, shift_idx * D_MODEL:(shift_idx + 1) * D_MODEL]
    scale = mod_ref[:, (shift_idx + 1) * D_MODEL:(shift_idx + 2) * D_MODEL]
    return y * (1.0 + scale) + shift


def _inproj_kernel(*refs, tm, n_heads, k_col, v_col, n_prev, split_x):
    n_x = 2 if split_x else 1
    g_ref, mod_ref, w_ref = refs[n_x:n_x + 3]
    outs = refs[n_x + 3 + n_prev:]
    o_ref, ck_ref, cv_ref = outs[:3]
    wbf_ref = outs[-1]
    i = pl.program_id(0)
    is_ctx = i < N_CTX_TOK // tm

    @pl.when(i == 0)
    def _():
        wbf_ref[...] = w_ref[...].astype(BF16)

    if split_x:
        x = jnp.where(is_ctx, refs[0][...], refs[1][...])
        outs[3][...] = x
    else:
        x = refs[0][...]
    h = _norm_mod(x, g_ref[...], mod_ref, 0)
    o_ref[...] = _dot(h.astype(BF16), wbf_ref[...])

    @pl.when(is_ctx)
    def _():
        rows = SEQ * n_heads
        for col, c_ref in ((k_col, ck_ref), (v_col, cv_ref)):
            flat = c_ref.reshape(tm // SEQ * rows, HEAD_DIM)
            for r in range(tm // SEQ):
                for hd in range(n_heads):
                    c0 = col + hd * HEAD_DIM
                    flat[pl.ds(r * rows + hd, SEQ, stride=n_heads), :] = (
                        o_ref[r * SEQ:(r + 1) * SEQ, c0:c0 + HEAD_DIM])


def _inproj(xs, g_all, mod4, layer, w_all, w_idx, n_heads, k_col, v_col, n_slots, prev_caches):
    tm = 512
    n_out = w_all.shape[2]
    n_ctx_tiles = N_CTX_TOK // tm
    split_x = len(xs) == 2
    cache_shape = jax.ShapeDtypeStruct((BATCH, n_slots, SEQ, n_heads, HEAD_DIM), F32)
    cache_spec = pl.BlockSpec(
        (tm // SEQ, None, SEQ, n_heads, HEAD_DIM),
        lambda i: (jnp.minimum(i, n_ctx_tiles - 1), w_idx, 0, 0, 0))
    row_spec = pl.BlockSpec((tm, D_MODEL), lambda i: (i, 0))
    if split_x:
        x_specs = [
            pl.BlockSpec((tm, D_MODEL), lambda i: (jnp.minimum(i, n_ctx_tiles - 1), 0)),
            pl.BlockSpec((tm, D_MODEL), lambda i: (jnp.maximum(i - n_ctx_tiles, 0), 0)),
        ]
    else:
        x_specs = [row_spec]
    out_shape = [jax.ShapeDtypeStruct((N_TOK, n_out), F32), cache_shape, cache_shape]
    out_specs = [pl.BlockSpec((tm, n_out), lambda i: (i, 0)), cache_spec, cache_spec]
    if split_x:
        out_shape.append(jax.ShapeDtypeStruct((N_TOK, D_MODEL), F32))
        out_specs.append(row_spec)
    n_prev = len(prev_caches)
    n_x = len(xs)
    return pl.pallas_call(
        functools.partial(_inproj_kernel, tm=tm, n_heads=n_heads, k_col=k_col, v_col=v_col,
                          n_prev=n_prev, split_x=split_x),
        out_shape=out_shape,
        grid=(N_TOK // tm,),
        in_specs=x_specs + [
            pl.BlockSpec((None, 1, D_MODEL), lambda i: (layer, 0, 0)),
            pl.BlockSpec((None, None, 1, 6 * D_MODEL),
                         lambda i: (layer, _group_of_tile(i, tm), 0, 0)),
            pl.BlockSpec((None, D_MODEL, n_out), lambda i: (w_idx, 0, 0),
                         pipeline_mode=pl.Buffered(1)),
        ] + [pl.BlockSpec(memory_space=pl.ANY)] * n_prev,
        out_specs=out_specs,
        scratch_shapes=[pltpu.VMEM((D_MODEL, n_out), BF16)],
        input_output_aliases={n_x + 3 + k: 1 + k for k in range(n_prev)},
        compiler_params=_params(1),
        name="inproj",
    )(*xs, g_all.reshape(DEPTH, 1, D_MODEL), mod4, w_all, *prev_caches)


def _ctx_attn_a_kernel(p_ref, o_ref):
    for h in range(A_HEADS):
        c0 = h * HEAD_DIM
        q = (p_ref[:, c0:c0 + HEAD_DIM] * HEAD_DIM ** -0.5).astype(BF16)
        k = p_ref[:, A_WIDTH + c0:A_WIDTH + c0 + HEAD_DIM].astype(BF16)
        v = p_ref[:, 2 * A_WIDTH + c0:2 * A_WIDTH + c0 + HEAD_DIM].astype(BF16)
        s = _dot_nt(q, k)
        m = jnp.max(s, axis=-1, keepdims=True)
        p = jnp.exp(s - m)
        l = jnp.sum(p, axis=-1, keepdims=True)
        o_ref[:, c0:c0 + HEAD_DIM] = _dot(p.astype(BF16), v) / l


def _ctx_attn_a(proj):
    return pl.pallas_call(
        _ctx_attn_a_kernel,
        out_shape=jax.ShapeDtypeStruct((N_TOK, D_MODEL), F32),
        grid=(BATCH,),
        in_specs=[pl.BlockSpec((SEQ, 4 * A_WIDTH), lambda b: (b, 0))],
        out_specs=pl.BlockSpec((SEQ, A_WIDTH), lambda b: (b, 0)),
        compiler_params=_params(1),
        name="ctx_attn_a",
    )(proj)


N_DROW = 2 * NA_ROWS - 1
N_DCOL = 2 * NA_COLS - 1
N_BIAS_TILES = 16
BIAS_TILE_LEFT_PAD = 14
BIAS_TILE_RIGHT_PAD = 15
MID_DROW = NA_ROWS - 1 - NA_ROWS // 2


def _na_bias_kernel(rpb_ref, o_ref):
    base = (pl.program_id(0) * A_HEADS + pl.program_id(1)) * (N_DROW * N_DCOL)
    qi = lax.broadcasted_iota(jnp.int32, (GRID_W, 2 * GRID_W), 0)
    lane = lax.broadcasted_iota(jnp.int32, (GRID_W, 2 * GRID_W), 1)
    right = lane >= GRID_W
    kc = jnp.where(right, lane - GRID_W, lane)
    rel = kc - qi + (NA_COLS - 1)
    qstart = jnp.clip(qi - NA_COLS // 2, 0, GRID_W - NA_COLS)
    valid = (kc >= qstart) & (kc < qstart + NA_COLS)

    rows = [jnp.zeros((GRID_W, 2 * GRID_W), F32)] * N_DROW
    for d in range(N_DCOL):
        hit = rel == d
        rows = [jnp.where(hit, rpb_ref[base + dr * N_DCOL + d], rows[dr])
                for dr in range(N_DROW)]

    for t in range(N_DROW - 1):
        o_ref[t] = jnp.where(valid, jnp.where(right, rows[t + 1], rows[t]), NEG_INF)
    o_ref[BIAS_TILE_LEFT_PAD] = jnp.where(valid & right, rows[MID_DROW], NEG_INF)
    o_ref[BIAS_TILE_RIGHT_PAD] = jnp.where(valid & jnp.logical_not(right),
                                           rows[MID_DROW + NA_ROWS - 1], NEG_INF)


def _na_bias_tiles(rpb_a):
    n_layers = rpb_a.shape[0]
    return pl.pallas_call(
        _na_bias_kernel,
        out_shape=jax.ShapeDtypeStruct(
            (n_layers, A_HEADS, N_BIAS_TILES, GRID_W, 2 * GRID_W), F32),
        grid=(n_layers, A_HEADS),
        in_specs=[pl.BlockSpec(memory_space=pltpu.SMEM)],
        out_specs=pl.BlockSpec((None, None, N_BIAS_TILES, GRID_W, 2 * GRID_W),
                               lambda l, h: (l, h, 0, 0, 0)),
        compiler_params=_params(2),
        name="na_bias",
    )(rpb_a.reshape(-1))


def _na_window(r):
    start = min(max(r - NA_ROWS // 2, 0), GRID_ROWS - NA_ROWS)
    first_drow = start - r + NA_ROWS - 1
    if start % 2 == 0:
        return start, [first_drow + 2 * p for p in range(NA_ROWS // 2)]
    assert first_drow == MID_DROW
    inner = [first_drow + 1 + 2 * p for p in range(NA_ROWS // 2 - 1)]
    return start - 1, [BIAS_TILE_LEFT_PAD] + inner + [BIAS_TILE_RIGHT_PAD]


def _lat_attn_a_kernel(q_ref, k_ref, v_ref, kc_ref, vc_ref, bias_ref, mix_ref, o_ref,
                       s_loc, s_ctx, p_loc, p_ctx):
    del mix_ref
    hp = pl.program_id(1)

    @pl.when((pl.program_id(0) == 0) & (hp == 0))
    def _():
        p_loc[...] = jnp.zeros_like(p_loc)

    for hh in range(2):
        c0 = hh * HEAD_DIM
        q = (q_ref[:, c0:c0 + HEAD_DIM] * HEAD_DIM ** -0.5).astype(BF16)
        k = k_ref[:, c0:c0 + HEAD_DIM].astype(BF16)
        v = v_ref[:, c0:c0 + HEAD_DIM].astype(BF16)
        kc_t = kc_ref[hh].astype(BF16)
        vc_t = vc_ref[hh].astype(BF16)
        s_loc[...] = _dot_nt(q, k)
        s_ctx[...] = _dot(q, kc_t)
        inv_l = []
        for r in range(GRID_ROWS):
            rows = slice(r * GRID_W, (r + 1) * GRID_W)
            first_row, tiles = _na_window(r)
            cols = slice(first_row * GRID_W, (first_row + 2 * len(tiles)) * GRID_W)
            bias = jnp.concatenate([bias_ref[hh, t] for t in tiles], axis=1)
            sc = s_ctx[rows, :]
            sl = s_loc[rows, cols] + bias
            m = jnp.maximum(jnp.max(sc, axis=-1, keepdims=True),
                            jnp.max(sl, axis=-1, keepdims=True))
            pc = jnp.exp(sc - m)
            pw = jnp.exp(sl - m)
            inv_l.append(1.0 / (jnp.sum(pc, axis=-1, keepdims=True)
                                + jnp.sum(pw, axis=-1, keepdims=True)))
            p_ctx[rows, :] = pc.astype(BF16)
            p_loc[rows, cols] = pw.astype(BF16)
        o = _dot_nt(p_ctx[...], vc_t) + _dot(p_loc[...], v)
        o_ref[:, c0:c0 + HEAD_DIM] = o * jnp.concatenate(inv_l, axis=0)


def _lat_attn_a(proj, cache_k, cache_v, bias_tiles, i_layer, mix):
    row0 = N_CTX_TOK // DEC_SEQ
    pair = 2 * HEAD_DIM
    k_col0 = A_WIDTH // pair
    v_col0 = 2 * A_WIDTH // pair
    return pl.pallas_call(
        _lat_attn_a_kernel,
        out_shape=jax.ShapeDtypeStruct((N_TOK, D_MODEL), F32),
        grid=(DEC_BATCH, A_HEADS // 2),
        in_specs=[
            pl.BlockSpec((DEC_SEQ, pair), lambda b, hp: (row0 + b, hp)),
            pl.BlockSpec((DEC_SEQ, pair), lambda b, hp: (row0 + b, k_col0 + hp)),
            pl.BlockSpec((DEC_SEQ, pair), lambda b, hp: (row0 + b, v_col0 + hp)),
            pl.BlockSpec((None, None, 2, HEAD_DIM, PAST_LEN),
                         lambda b, hp: (b, i_layer, hp, 0, 0)),
            pl.BlockSpec((None, None, 2, HEAD_DIM, PAST_LEN),
                         lambda b, hp: (b, i_layer, hp, 0, 0)),
            pl.BlockSpec((None, 2, N_BIAS_TILES, GRID_W, 2 * GRID_W),
                         lambda b, hp: (i_layer, hp, 0, 0, 0)),
            pl.BlockSpec(memory_space=pl.ANY),
        ],
        out_specs=pl.BlockSpec((DEC_SEQ, pair), lambda b, hp: (row0 + b, hp)),
        scratch_shapes=[pltpu.VMEM((DEC_SEQ, DEC_SEQ), F32),
                        pltpu.VMEM((DEC_SEQ, PAST_LEN), F32),
                        pltpu.VMEM((DEC_SEQ, DEC_SEQ), BF16),
                        pltpu.VMEM((DEC_SEQ, PAST_LEN), BF16)],
        input_output_aliases={6: 0},
        compiler_params=_params(2),
        name="lat_attn_a",
    )(proj, proj, proj, cache_k, cache_v, bias_tiles, mix)


def _pool_kernel(u_ref, wp_ref, ps_ref, mix_ref, o_ref, *, n):
    del mix_ref
    t = lax.broadcasted_iota(jnp.int32, (n, n), 0)
    j = lax.broadcasted_iota(jnp.int32, (n, n), 1)
    tc = lax.broadcasted_iota(jnp.int32, (n, 1), 0)
    for g, w in enumerate(POOL_WINDOWS):
        c0 = g * B_GROUP_DIM
        lo = jnp.maximum(t - w // 2, 0)
        hi = jnp.minimum(t - w // 2 + w, n)
        band = jnp.where((j >= lo) & (j < hi), 1.0, 0.0).astype(BF16)
        count = (jnp.minimum(tc - w // 2 + w, n) - jnp.maximum(tc - w // 2, 0)).astype(F32)
        u = u_ref[:, c0:c0 + B_GROUP_DIM]
        u_hi = u.astype(BF16)
        u_lo = (u - u_hi.astype(F32)).astype(BF16)
        total = _dot(band, u_hi) + _dot(band, u_lo)
        pooled = total / count - u
        y = _dot(pooled.astype(BF16), wp_ref[g].astype(BF16))
        o_ref[:, c0:c0 + B_GROUP_DIM] = y * ps_ref[:, c0:c0 + B_GROUP_DIM]


def _pool(proj, w_pool, pool_scale, i_layer, n, n_seq, row_block0, mix):
    n_groups = len(POOL_WINDOWS)
    return pl.pallas_call(
        functools.partial(_pool_kernel, n=n),
        out_shape=jax.ShapeDtypeStruct((N_TOK, D_MODEL), F32),
        grid=(n_seq,),
        in_specs=[
            pl.BlockSpec((n, B_WIDTH), lambda b: (row_block0 + b, 3)),
            pl.BlockSpec((None, n_groups, B_GROUP_DIM, B_GROUP_DIM),
                         lambda b: (i_layer, 0, 0, 0)),
            pl.BlockSpec((None, 1, B_WIDTH), lambda b: (i_layer, 0, 0)),
            pl.BlockSpec(memory_space=pl.ANY),
        ],
        out_specs=pl.BlockSpec((n, B_WIDTH), lambda b: (row_block0 + b, 1)),
        input_output_aliases={3: 0},
        compiler_params=_params(1),
        name="pool_mixer",
    )(proj, w_pool, pool_scale.reshape(-1, 1, B_WIDTH), mix)


def _sink_row(sink_ref, head0, queries_per_head):
    lane = lax.broadcasted_iota(jnp.int32, (1, C_GROUP * queries_per_head), 1)
    row = jnp.full((1, C_GROUP * queries_per_head), sink_ref[head0], F32)
    for g in range(1, C_GROUP):
        row = jnp.where(lane >= g * queries_per_head, sink_ref[head0 + g], row)
    return row


def _ctx_attn_c_kernel(sink_ref, p_ref, o_ref, *, sink0):
    for kk in range(C_KV_HEADS):
        heads = [kk * C_GROUP + g for g in range(C_GROUP)]
        q = jnp.concatenate(
            [p_ref[:, h * HEAD_DIM:(h + 1) * HEAD_DIM] for h in heads], axis=0)
        q = (q * HEAD_DIM ** -0.5).astype(BF16)
        k0 = C_Q_WIDTH + kk * HEAD_DIM
        v0 = C_Q_WIDTH + C_KV_WIDTH + kk * HEAD_DIM
        k = p_ref[:, k0:k0 + HEAD_DIM].astype(BF16)
        v = p_ref[:, v0:v0 + HEAD_DIM].astype(BF16)
        o_t = _softmax_pv_t([_dot_nt(k, q)], _sink_row(sink_ref, sink0 + kk * C_GROUP, SEQ),
                            [v])
        for g, h in enumerate(heads):
            o_ref[:, h * HEAD_DIM:(h + 1) * HEAD_DIM] = o_t[:, g * SEQ:(g + 1) * SEQ].T


def _ctx_attn_c(proj, sink_all, j_layer):
    return pl.pallas_call(
        functools.partial(_ctx_attn_c_kernel, sink0=j_layer * C_HEADS),
        out_shape=jax.ShapeDtypeStruct((N_TOK, D_MODEL), F32),
        grid=(BATCH,),
        in_specs=[
            pl.BlockSpec(memory_space=pltpu.SMEM),
            pl.BlockSpec((SEQ, C_Q_WIDTH + 2 * C_KV_WIDTH), lambda b: (b, 0)),
        ],
        out_specs=pl.BlockSpec((SEQ, C_Q_WIDTH), lambda b: (b, 0)),
        compiler_params=_params(1),
        name="ctx_attn_c",
    )(sink_all.reshape(-1), proj)


def _rope(x, cos, sin_signed):
    n = x.shape[-1]
    lane = lax.broadcasted_iota(jnp.int32, x.shape, x.ndim - 1)
    first = (lane % 32) < 16
    partner = jnp.where(first, pltpu.roll(x, n - 16, axis=x.ndim - 1),
                        pltpu.roll(x, 16, axis=x.ndim - 1))
    return x * cos + partner * sin_signed


def _lat_attn_c_kernel(sink_ref, q_ref, k_ref, v_ref, kc_ref, vc_ref, cosq_ref, sinq_ref,
                       cosk_ref, sink_rot_ref, mix_ref, o_ref, kb_ref, vb_ref, kctx_ref, vctx_ref,
                       *, sink0):
    del mix_ref
    j = pl.program_id(1)

    @pl.when(j == 0)
    def _():
        for kk in range(C_KV_HEADS):
            c0 = kk * HEAD_DIM
            kctx_ref[:, c0:c0 + HEAD_DIM] = kc_ref[kk].T.astype(BF16)
            vctx_ref[:, c0:c0 + HEAD_DIM] = vc_ref[kk].T.astype(BF16)
        zeros = jnp.zeros((C_BLOCK, C_KV_WIDTH), BF16)
        kb_ref[0:C_BLOCK] = zeros
        vb_ref[0:C_BLOCK] = zeros
        kb_ref[C_BLOCK + DEC_SEQ:2 * C_BLOCK + DEC_SEQ] = zeros
        vb_ref[C_BLOCK + DEC_SEQ:2 * C_BLOCK + DEC_SEQ] = zeros
        kb_ref[C_BLOCK:C_BLOCK + DEC_SEQ] = _rope(
            k_ref[...], cosk_ref[...], sink_rot_ref[...]).astype(BF16)
        vb_ref[C_BLOCK:C_BLOCK + DEC_SEQ] = v_ref[...].astype(BF16)

    q = (_rope(q_ref[...], cosq_ref[...], sinq_ref[...]) * HEAD_DIM ** -0.5).astype(BF16)

    n_q = C_GROUP * C_BLOCK
    qi = lax.broadcasted_iota(jnp.int32, (3 * C_BLOCK, n_q), 1) % C_BLOCK
    jj = lax.broadcasted_iota(jnp.int32, (3 * C_BLOCK, n_q), 0)
    kpos = (j - 1) * C_BLOCK + jj
    valid = (jj >= qi) & (jj <= qi + 2 * C_BLOCK) & (kpos >= 0) & (kpos < DEC_SEQ)
    band0 = pl.multiple_of(j * C_BLOCK, C_BLOCK)

    for kk in range(C_KV_HEADS):
        heads = [kk * C_GROUP + g for g in range(C_GROUP)]
        qs = jnp.concatenate([q[:, h * HEAD_DIM:(h + 1) * HEAD_DIM] for h in heads], axis=0)
        c0 = kk * HEAD_DIM
        k_ctx = kctx_ref[:, c0:c0 + HEAD_DIM]
        v_ctx = vctx_ref[:, c0:c0 + HEAD_DIM]
        k_band = kb_ref[pl.ds(band0, 3 * C_BLOCK), c0:c0 + HEAD_DIM]
        v_band = vb_ref[pl.ds(band0, 3 * C_BLOCK), c0:c0 + HEAD_DIM]
        s_ctx = _dot_nt(k_ctx, qs)
        s_band = jnp.where(valid, _dot_nt(k_band, qs), NEG_INF)
        o_t = _softmax_pv_t([s_ctx, s_band],
                            _sink_row(sink_ref, sink0 + kk * C_GROUP, C_BLOCK),
                            [v_ctx, v_band])
        for g, h in enumerate(heads):
            o_ref[:, h * HEAD_DIM:(h + 1) * HEAD_DIM] = (
                o_t[:, g * C_BLOCK:(g + 1) * C_BLOCK].T)


def _lat_attn_c(proj, cache_k, cache_v, sink, cos_t, sin_t, j_layer, mix):
    n_blocks = DEC_SEQ // C_BLOCK
    q_row0 = N_CTX_TOK // C_BLOCK
    kv_row0 = N_CTX_TOK // DEC_SEQ
    k_col = C_Q_WIDTH // C_KV_WIDTH
    return pl.pallas_call(
        functools.partial(_lat_attn_c_kernel, sink0=j_layer * C_HEADS),
        out_shape=jax.ShapeDtypeStruct((N_TOK, D_MODEL), F32),
        grid=(DEC_BATCH, n_blocks),
        in_specs=[
            pl.BlockSpec(memory_space=pltpu.SMEM),
            pl.BlockSpec((C_BLOCK, C_Q_WIDTH), lambda b, j: (q_row0 + b * n_blocks + j, 0)),
            pl.BlockSpec((DEC_SEQ, C_KV_WIDTH), lambda b, j: (kv_row0 + b, k_col)),
            pl.BlockSpec((DEC_SEQ, C_KV_WIDTH), lambda b, j: (kv_row0 + b, k_col + 1)),
            pl.BlockSpec((None, None, C_KV_HEADS, HEAD_DIM, PAST_LEN),
                         lambda b, j: (b, j_layer, 0, 0, 0)),
            pl.BlockSpec((None, None, C_KV_HEADS, HEAD_DIM, PAST_LEN),
                         lambda b, j: (b, j_layer, 0, 0, 0)),
            pl.BlockSpec((C_BLOCK, C_Q_WIDTH), lambda b, j: (j, 0)),
            pl.BlockSpec((C_BLOCK, C_Q_WIDTH), lambda b, j: (j, 0)),
            pl.BlockSpec((DEC_SEQ, C_KV_WIDTH), lambda b, j: (0, 0)),
            pl.BlockSpec((DEC_SEQ, C_KV_WIDTH), lambda b, j: (0, 0)),
            pl.BlockSpec(memory_space=pl.ANY),
        ],
        out_specs=pl.BlockSpec((C_BLOCK, C_Q_WIDTH),
                               lambda b, j: (q_row0 + b * n_blocks + j, 0)),
        input_output_aliases={10: 0},
        scratch_shapes=[pltpu.VMEM((DEC_SEQ + 2 * C_BLOCK, C_KV_WIDTH), BF16),
                        pltpu.VMEM((DEC_SEQ + 2 * C_BLOCK, C_KV_WIDTH), BF16),
                        pltpu.VMEM((PAST_LEN, C_KV_WIDTH), BF16),
                        pltpu.VMEM((PAST_LEN, C_KV_WIDTH), BF16)],
        compiler_params=_params(2),
        name="lat_attn_c",
    )(sink.reshape(-1), proj, proj, proj, cache_k, cache_v, cos_t, sin_t, cos_t, sin_t, mix)


def _rope_tables():
    t = jnp.arange(DEC_SEQ)
    pos = jnp.stack([t // GRID_W, t % GRID_W], axis=-1).astype(F32)
    half = HEAD_DIM // 4
    inv = ROPE_BASE ** (-jnp.arange(half, dtype=F32) / half)
    ang = pos[:, :, None] * inv
    cos = jnp.cos(ang)
    sin = jnp.sin(ang)
    cos64 = jnp.stack([cos, cos], axis=2).reshape(DEC_SEQ, HEAD_DIM)
    sin64 = jnp.stack([-sin, sin], axis=2).reshape(DEC_SEQ, HEAD_DIM)
    return jnp.tile(cos64, (1, C_HEADS)), jnp.tile(sin64, (1, C_HEADS))


FFN_CHUNK = 256
N_FFN_CHUNKS = FFN_HIDDEN // FFN_CHUNK
N_WO_PIECES = D_MODEL // FFN_CHUNK


def _load_layer_weights(wo_hbm, wgu_hbm, wd_hbm, wo_s, wg_s, wu_s, wd_s, stage_col, stage_row,
                        sem_col, sem_row, *, layer, wo_idx):
    def col_copy(which, c, slot):
        src = wgu_hbm.at[layer, :, pl.ds(which * FFN_HIDDEN + c * FFN_CHUNK, FFN_CHUNK)]
        return pltpu.make_async_copy(src, stage_col.at[which, slot], sem_col.at[which, slot])

    def row_copy(p, slot):
        if isinstance(p, int) and p >= N_FFN_CHUNKS:
            src = wo_hbm.at[wo_idx, pl.ds((p - N_FFN_CHUNKS) * FFN_CHUNK, FFN_CHUNK), :]
        else:
            src = wd_hbm.at[layer, pl.ds(p * FFN_CHUNK, FFN_CHUNK), :]
        return pltpu.make_async_copy(src, stage_row.at[slot], sem_row.at[slot])

    for which in range(2):
        col_copy(which, 0, 0).start()
    row_copy(0, 0).start()

    def chunk(c, carry):
        slot = c % 2

        @pl.when(c + 1 < N_FFN_CHUNKS)
        def _():
            for which in range(2):
                col_copy(which, c + 1, 1 - slot).start()
            row_copy(c + 1, 1 - slot).start()

        @pl.when(c + 1 == N_FFN_CHUNKS)
        def _():
            row_copy(N_FFN_CHUNKS, 1 - slot).start()

        col_copy(0, c, slot).wait()
        wg_s[c] = stage_col[0, slot].astype(BF16)
        col_copy(1, c, slot).wait()
        wu_s[c] = stage_col[1, slot].astype(BF16)
        row_copy(c, slot).wait()
        wd_s[pl.ds(pl.multiple_of(c * FFN_CHUNK, FFN_CHUNK), FFN_CHUNK), :] = (
            stage_row[slot].astype(BF16))
        return carry

    lax.fori_loop(0, N_FFN_CHUNKS, chunk, 0)

    for k in range(N_WO_PIECES):
        p = N_FFN_CHUNKS + k
        slot = p % 2
        if k + 1 < N_WO_PIECES:
            row_copy(p + 1, 1 - slot).start()
        row_copy(p, slot).wait()
        wo_s[k * FFN_CHUNK:(k + 1) * FFN_CHUNK, :] = stage_row[slot].astype(BF16)


def _post_mixer_kernel(*refs, layer, wo_idx, tm, final):
    x_ref, mix_ref, g_ref, mod_ref = refs[:4]
    n_in = 5 if final else 4
    wo_hbm, wgu_hbm, wd_hbm = refs[n_in:n_in + 3]
    n_out = 2 if final else 1
    out_refs = refs[n_in + 3:n_in + 3 + n_out]
    (wo_s, wg_s, wu_s, wd_s, act_s, stage_col, stage_row, sem_col,
     sem_row) = refs[n_in + 3 + n_out:]

    @pl.when(pl.program_id(0) == 0)
    def _():
        _load_layer_weights(wo_hbm, wgu_hbm, wd_hbm, wo_s, wg_s, wu_s, wd_s, stage_col,
                            stage_row, sem_col, sem_row, layer=layer, wo_idx=wo_idx)

    gate1 = mod_ref[:, 2 * D_MODEL:3 * D_MODEL]
    x1 = x_ref[...] + gate1 * _dot(mix_ref[...].astype(BF16), wo_s[...])
    h = _norm_mod(x1, g_ref[...], mod_ref, 3).astype(BF16)
    for c in range(N_FFN_CHUNKS):
        gate = _dot(h, wg_s[c])
        up = _dot(h, wu_s[c])
        act = gate / (1.0 + jnp.exp(-gate)) * up
        act_s[:, c * FFN_CHUNK:(c + 1) * FFN_CHUNK] = act.astype(BF16)
    gate2 = mod_ref[:, 5 * D_MODEL:6 * D_MODEL]
    x2 = x1 + gate2 * _dot(act_s[...], wd_s[...])
    if not final:
        out_refs[0][...] = x2
        return
    var = jnp.mean(x2 * x2, axis=-1, keepdims=True)
    y = x2 * lax.rsqrt(var + EPS) * refs[4][...]
    is_ctx = pl.program_id(0) < N_CTX_TOK // tm

    @pl.when(is_ctx)
    def _():
        out_refs[0][...] = y

    @pl.when(jnp.logical_not(is_ctx))
    def _():
        out_refs[1][...] = y


def _post_mixer(x, mix, g_all, mod4, layer, w_out_all, wo_idx, w_gate_up, w_down,
                norm_final=None):
    tm = 512
    n_ctx_tiles = N_CTX_TOK // tm
    final = norm_final is not None
    row_spec = pl.BlockSpec((tm, D_MODEL), lambda i: (i, 0))
    hbm = pl.BlockSpec(memory_space=pl.ANY)
    if final:
        extra_in = [norm_final.reshape(1, D_MODEL)]
        extra_specs = [pl.BlockSpec((1, D_MODEL), lambda i: (0, 0))]
        out_shape = (jax.ShapeDtypeStruct((N_CTX_TOK, D_MODEL), F32),
                     jax.ShapeDtypeStruct((N_LAT_TOK, D_MODEL), F32))
        out_specs = (
            pl.BlockSpec((tm, D_MODEL), lambda i: (jnp.minimum(i, n_ctx_tiles - 1), 0)),
            pl.BlockSpec((tm, D_MODEL), lambda i: (jnp.maximum(i - n_ctx_tiles, 0), 0)))
    else:
        extra_in, extra_specs = [], []
        out_shape = jax.ShapeDtypeStruct((N_TOK, D_MODEL), F32)
        out_specs = row_spec
    return pl.pallas_call(
        functools.partial(_post_mixer_kernel, layer=layer, wo_idx=wo_idx, tm=tm, final=final),
        out_shape=out_shape,
        grid=(N_TOK // tm,),
        in_specs=[
            row_spec,
            row_spec,
            pl.BlockSpec((None, 1, D_MODEL), lambda i: (layer, 0, 0)),
            pl.BlockSpec((None, None, 1, 6 * D_MODEL),
                         lambda i: (layer, _group_of_tile(i, tm), 0, 0)),
        ] + extra_specs + [hbm, hbm, hbm],
        out_specs=out_specs,
        scratch_shapes=[
            pltpu.VMEM((D_MODEL, D_MODEL), BF16),
            pltpu.VMEM((N_FFN_CHUNKS, D_MODEL, FFN_CHUNK), BF16),
            pltpu.VMEM((N_FFN_CHUNKS, D_MODEL, FFN_CHUNK), BF16),
            pltpu.VMEM((FFN_HIDDEN, D_MODEL), BF16),
            pltpu.VMEM((tm, FFN_HIDDEN), BF16),
            pltpu.VMEM((2, 2, D_MODEL, FFN_CHUNK), F32),
            pltpu.VMEM((2, FFN_CHUNK, D_MODEL), F32),
            pltpu.SemaphoreType.DMA((2, 2)),
            pltpu.SemaphoreType.DMA((2,)),
        ],
        compiler_params=_params(1),
        name="post_mixer",
    )(x, mix, g_all.reshape(DEPTH, 1, D_MODEL), mod4, *extra_in, w_out_all, w_gate_up, w_down)


def kernel(x_prompt, x_sample, cache_a_k, cache_a_v, cache_c_k, cache_c_v, c, c_ctx, w_mod, b_mod, norm_mix, norm_ffn, w_in_ab, rpb_a, w_pool, pool_scale, w_out_ab, w_in_c, sink_c, w_out_c, w_gate_up, w_down, norm_final):
    xs = (x_prompt.reshape(N_CTX_TOK, D_MODEL), x_sample.reshape(N_LAT_TOK, D_MODEL))
    cond8 = jnp.concatenate(
        [c_ctx[None], c, jnp.zeros((N_GROUPS_PAD - 1 - DEC_BATCH, D_MODEL), F32)], axis=0)
    mod4 = _modulation(cond8, w_mod, b_mod).reshape(DEPTH, N_GROUPS_PAD, 1, 6 * D_MODEL)

    n_ab = cache_a_k.shape[1]
    n_c = cache_c_k.shape[1]
    cache_a_k, cache_a_v, cache_c_k, cache_c_v = (
        jnp.transpose(t, (0, 1, 3, 4, 2)) for t in (cache_a_k, cache_a_v, cache_c_k, cache_c_v))
    cos_t, sin_t = _rope_tables()
    bias_tiles = _na_bias_tiles(rpb_a)

    new_a = []
    new_c = []
    for l in range(DEPTH):
        if l % 2 == 0:
            i = l // 2
            proj, *rest = _inproj(xs, norm_mix, mod4, l, w_in_ab, i, A_HEADS,
                                  A_WIDTH, 2 * A_WIDTH, n_ab, new_a)
            new_a = rest[:2]
            if len(xs) == 2:
                xs = (rest[2],)
            mix = _ctx_attn_a(proj)
            mix = _lat_attn_a(proj, cache_a_k, cache_a_v, bias_tiles, i, mix)
            mix = _pool(proj, w_pool, pool_scale, i, SEQ, BATCH, 0, mix)
            mix = _pool(proj, w_pool, pool_scale, i, DEC_SEQ, DEC_BATCH,
                        N_CTX_TOK // DEC_SEQ, mix)
            w_out, wo_idx = w_out_ab, i
        else:
            j = l // 2
            proj, *new_c = _inproj(xs, norm_mix, mod4, l, w_in_c, j, C_KV_HEADS,
                                   C_Q_WIDTH, C_Q_WIDTH + C_KV_WIDTH, n_c, new_c)
            mix = _ctx_attn_c(proj, sink_c, j)
            mix = _lat_attn_c(proj, cache_c_k, cache_c_v, sink_c, cos_t, sin_t, j, mix)
            w_out, wo_idx = w_out_c, j
        if l + 1 < DEPTH:
            xs = (_post_mixer(xs[0], mix, norm_ffn, mod4, l, w_out, wo_idx, w_gate_up, w_down),)
        else:
            y_ctx, y_lat = _post_mixer(xs[0], mix, norm_ffn, mod4, l, w_out, wo_idx, w_gate_up,
                                       w_down, norm_final)

    return (y_ctx.reshape(BATCH, SEQ, D_MODEL), y_lat.reshape(DEC_BATCH, DEC_SEQ, D_MODEL),
            new_a[0], new_a[1], new_c[0], new_c[1])
```

```python
import functools

import jax
import jax.numpy as jnp
from jax import lax
from jax.experimental import pallas as pl
from jax.experimental.pallas import tpu as pltpu

D_MODEL = 1024
BATCH = 16
SEQ = 256
DEPTH = 4
DEC_BATCH = 2
DEC_SEQ = 1024
PAST_LEN = 512
GRID_W = 64
HEAD_DIM = 64
A_WIDTH = 512
A_HEADS = 8
B_WIDTH = 512
POOL_WINDOWS = (2, 4, 8, 16)
B_GROUP_DIM = 128
NA_ROWS = 8
NA_COLS = 16
C_HEADS = 16
C_KV_HEADS = 4
C_GROUP = C_HEADS // C_KV_HEADS
C_Q_WIDTH = 1024
C_KV_WIDTH = 256
C_BLOCK = 128
FFN_HIDDEN = 2816
ROPE_BASE = 10000.0
EPS = 1e-6
NEG_INF = -1e30

N_CTX_TOK = BATCH * SEQ
N_LAT_TOK = DEC_BATCH * DEC_SEQ
N_TOK = N_CTX_TOK + N_LAT_TOK
GRID_ROWS = DEC_SEQ // GRID_W
N_GROUPS_PAD = 8

VMEM_LIMIT = 56 * 1024 * 1024

F32 = jnp.float32
BF16 = jnp.bfloat16


def _params(n_axes):
    return pltpu.CompilerParams(dimension_semantics=("arbitrary",) * n_axes,
                                vmem_limit_bytes=VMEM_LIMIT)


def _group_of_tile(i, tm):
    row0 = i * tm
    return jnp.where(row0 < N_CTX_TOK, 0, 1 + (row0 - N_CTX_TOK) // DEC_SEQ)


def _dot_nt(a, b):
    return lax.dot_general(a, b, (((1,), (1,)), ((), ())), preferred_element_type=F32)


def _dot(a, b):
    return jnp.dot(a, b, preferred_element_type=F32)


def _dot_tn(a, b):
    return lax.dot_general(a, b, (((0,), (0,)), ((), ())), preferred_element_type=F32)


def _softmax_pv_t(scores_t, sink_row, values):
    m = sink_row
    for s in scores_t:
        ms = jnp.max(s, axis=0, keepdims=True)
        m = ms if m is None else jnp.maximum(m, ms)
    l = None if sink_row is None else jnp.exp(sink_row - m)
    o = None
    for s, v in zip(scores_t, values):
        p = jnp.exp(s - m)
        ls = jnp.sum(p, axis=0, keepdims=True)
        l = ls if l is None else l + ls
        pv = _dot_tn(v, p.astype(BF16))
        o = pv if o is None else o + pv
    return o * (1.0 / l)


def _mod_kernel(cond_ref, w_ref, b_ref, o_ref):
    c = cond_ref[...]
    s = c / (1.0 + jnp.exp(-c))
    o_ref[...] = _dot(s.astype(BF16), w_ref[...].astype(BF16)) + b_ref[...]


def _modulation(cond8, w_mod, b_mod):
    tn = 1536
    return pl.pallas_call(
        _mod_kernel,
        out_shape=jax.ShapeDtypeStruct((DEPTH, N_GROUPS_PAD, 6 * D_MODEL), F32),
        grid=(DEPTH, 6 * D_MODEL // tn),
        in_specs=[
            pl.BlockSpec((N_GROUPS_PAD, D_MODEL), lambda l, j: (0, 0)),
            pl.BlockSpec((None, D_MODEL, tn), lambda l, j: (l, 0, j)),
            pl.BlockSpec((None, 1, tn), lambda l, j: (l, 0, j)),
        ],
        out_specs=pl.BlockSpec((None, N_GROUPS_PAD, tn), lambda l, j: (l, 0, j)),
        compiler_params=_params(2),
        name="modulation",
    )(cond8, w_mod, b_mod.reshape(DEPTH, 1, 6 * D_MODEL))


def _norm_mod(x, g, mod_ref, shift_idx):
    var = jnp.mean(x * x, axis=-1, keepdims=True)
    y = x * lax.rsqrt(var + EPS) * g
    shift = mod_ref[:, shift_idx * D_MODEL:(shift_idx + 1) * D_MODEL]
    scale = mod_ref[:, (shift_idx + 1) * D_MODEL:(shift_idx + 2) * D_MODEL]
    return y * (1.0 + scale) + shift


def _inproj_kernel(*refs, tm, n_heads, k_col, v_col, n_prev, split_x):
    n_x = 2 if split_x else 1
    g_ref, mod_ref, w_ref = refs[n_x:n_x + 3]
    outs = refs[n_x + 3 + n_prev:]
    o_ref, ck_ref, cv_ref = outs[:3]
    wbf_ref = outs[-1]
    i = pl.program_id(0)
    is_ctx = i < N_CTX_TOK // tm

    @pl.when(i == 0)
    def _():
        wbf_ref[...] = w_ref[...].astype(BF16)

    if split_x:
        x = jnp.where(is_ctx, refs[0][...], refs[1][...])
        outs[3][...] = x
    else:
        x = refs[0][...]
    h = _norm_mod(x, g_ref[...], mod_ref, 0)
    o_ref[...] = _dot(h.astype(BF16), wbf_ref[...])

    @pl.when(is_ctx)
    def _():
        rows = SEQ * n_heads
        for col, c_ref in ((k_col, ck_ref), (v_col, cv_ref)):
            flat = c_ref.reshape(tm // SEQ * rows, HEAD_DIM)
            for r in range(tm // SEQ):
                for hd in range(n_heads):
                    c0 = col + hd * HEAD_DIM
                    flat[pl.ds(r * rows + hd, SEQ, stride=n_heads), :] = (
                        o_ref[r * SEQ:(r + 1) * SEQ, c0:c0 + HEAD_DIM])


def _inproj(xs, g_all, mod4, layer, w_all, w_idx, n_heads, k_col, v_col, n_slots, prev_caches):
    tm = 512
    n_out = w_all.shape[2]
    n_ctx_tiles = N_CTX_TOK // tm
    split_x = len(xs) == 2
    cache_shape = jax.ShapeDtypeStruct((BATCH, n_slots, SEQ, n_heads, HEAD_DIM), F32)
    cache_spec = pl.BlockSpec(
        (tm // SEQ, None, SEQ, n_heads, HEAD_DIM),
        lambda i: (jnp.minimum(i, n_ctx_tiles - 1), w_idx, 0, 0, 0))
    row_spec = pl.BlockSpec((tm, D_MODEL), lambda i: (i, 0))
    if split_x:
        x_specs = [
            pl.BlockSpec((tm, D_MODEL), lambda i: (jnp.minimum(i, n_ctx_tiles - 1), 0)),
            pl.BlockSpec((tm, D_MODEL), lambda i: (jnp.maximum(i - n_ctx_tiles, 0), 0)),
        ]
    else:
        x_specs = [row_spec]
    out_shape = [jax.ShapeDtypeStruct((N_TOK, n_out), F32), cache_shape, cache_shape]
    out_specs = [pl.BlockSpec((tm, n_out), lambda i: (i, 0)), cache_spec, cache_spec]
    if split_x:
        out_shape.append(jax.ShapeDtypeStruct((N_TOK, D_MODEL), F32))
        out_specs.append(row_spec)
    n_prev = len(prev_caches)
    n_x = len(xs)
    return pl.pallas_call(
        functools.partial(_inproj_kernel, tm=tm, n_heads=n_heads, k_col=k_col, v_col=v_col,
                          n_prev=n_prev, split_x=split_x),
        out_shape=out_shape,
        grid=(N_TOK // tm,),
        in_specs=x_specs + [
            pl.BlockSpec((None, 1, D_MODEL), lambda i: (layer, 0, 0)),
            pl.BlockSpec((None, None, 1, 6 * D_MODEL),
                         lambda i: (layer, _group_of_tile(i, tm), 0, 0)),
            pl.BlockSpec((None, D_MODEL, n_out), lambda i: (w_idx, 0, 0),
                         pipeline_mode=pl.Buffered(1)),
        ] + [pl.BlockSpec(memory_space=pl.ANY)] * n_prev,
        out_specs=out_specs,
        scratch_shapes=[pltpu.VMEM((D_MODEL, n_out), BF16)],
        input_output_aliases={n_x + 3 + k: 1 + k for k in range(n_prev)},
        compiler_params=_params(1),
        name="inproj",
    )(*xs, g_all.reshape(DEPTH, 1, D_MODEL), mod4, w_all, *prev_caches)


def _ctx_attn_a_kernel(p_ref, o_ref):
    for h in range(A_HEADS):
        c0 = h * HEAD_DIM
        q = (p_ref[:, c0:c0 + HEAD_DIM] * HEAD_DIM ** -0.5).astype(BF16)
        k = p_ref[:, A_WIDTH + c0:A_WIDTH + c0 + HEAD_DIM].astype(BF16)
        v = p_ref[:, 2 * A_WIDTH + c0:2 * A_WIDTH + c0 + HEAD_DIM].astype(BF16)
        s = _dot_nt(q, k)
        m = jnp.max(s, axis=-1, keepdims=True)
        p = jnp.exp(s - m)
        l = jnp.sum(p, axis=-1, keepdims=True)
        o_ref[:, c0:c0 + HEAD_DIM] = _dot(p.astype(BF16), v) / l


def _ctx_attn_a(proj):
    return pl.pallas_call(
        _ctx_attn_a_kernel,
        out_shape=jax.ShapeDtypeStruct((N_TOK, D_MODEL), F32),
        grid=(BATCH,),
        in_specs=[pl.BlockSpec((SEQ, 4 * A_WIDTH), lambda b: (b, 0))],
        out_specs=pl.BlockSpec((SEQ, A_WIDTH), lambda b: (b, 0)),
        compiler_params=_params(1),
        name="ctx_attn_a",
    )(proj)


N_DROW = 2 * NA_ROWS - 1
N_DCOL = 2 * NA_COLS - 1
N_BIAS_TILES = 16
BIAS_TILE_LEFT_PAD = 14
BIAS_TILE_RIGHT_PAD = 15
MID_DROW = NA_ROWS - 1 - NA_ROWS // 2


def _na_bias_kernel(rpb_ref, o_ref):
    base = (pl.program_id(0) * A_HEADS + pl.program_id(1)) * (N_DROW * N_DCOL)
    qi = lax.broadcasted_iota(jnp.int32, (GRID_W, 2 * GRID_W), 0)
    lane = lax.broadcasted_iota(jnp.int32, (GRID_W, 2 * GRID_W), 1)
    right = lane >= GRID_W
    kc = jnp.where(right, lane - GRID_W, lane)
    rel = kc - qi + (NA_COLS - 1)
    qstart = jnp.clip(qi - NA_COLS // 2, 0, GRID_W - NA_COLS)
    valid = (kc >= qstart) & (kc < qstart + NA_COLS)

    rows = [jnp.zeros((GRID_W, 2 * GRID_W), F32)] * N_DROW
    for d in range(N_DCOL):
        hit = rel == d
        rows = [jnp.where(hit, rpb_ref[base + dr * N_DCOL + d], rows[dr])
                for dr in range(N_DROW)]

    for t in range(N_DROW - 1):
        o_ref[t] = jnp.where(valid, jnp.where(right, rows[t + 1], rows[t]), NEG_INF)
    o_ref[BIAS_TILE_LEFT_PAD] = jnp.where(valid & right, rows[MID_DROW], NEG_INF)
    o_ref[BIAS_TILE_RIGHT_PAD] = jnp.where(valid & jnp.logical_not(right),
                                           rows[MID_DROW + NA_ROWS - 1], NEG_INF)


def _na_bias_tiles(rpb_a):
    n_layers = rpb_a.shape[0]
    return pl.pallas_call(
        _na_bias_kernel,
        out_shape=jax.ShapeDtypeStruct(
            (n_layers, A_HEADS, N_BIAS_TILES, GRID_W, 2 * GRID_W), F32),
        grid=(n_layers, A_HEADS),
        in_specs=[pl.BlockSpec(memory_space=pltpu.SMEM)],
        out_specs=pl.BlockSpec((None, None, N_BIAS_TILES, GRID_W, 2 * GRID_W),
                               lambda l, h: (l, h, 0, 0, 0)),
        compiler_params=_params(2),
        name="na_bias",
    )(rpb_a.reshape(-1))


def _na_window(r):
    start = min(max(r - NA_ROWS // 2, 0), GRID_ROWS - NA_ROWS)
    first_drow = start - r + NA_ROWS - 1
    if start % 2 == 0:
        return start, [first_drow + 2 * p for p in range(NA_ROWS // 2)]
    assert first_drow == MID_DROW
    inner = [first_drow + 1 + 2 * p for p in range(NA_ROWS // 2 - 1)]
    return start - 1, [BIAS_TILE_LEFT_PAD] + inner + [BIAS_TILE_RIGHT_PAD]


def _lat_attn_a_kernel(q_ref, k_ref, v_ref, kc_ref, vc_ref, bias_ref, mix_ref, o_ref,
                       s_loc, s_ctx, p_loc, p_ctx):
    del mix_ref
    hp = pl.program_id(1)

    @pl.when((pl.program_id(0) == 0) & (hp == 0))
    def _():
        p_loc[...] = jnp.zeros_like(p_loc)

    for hh in range(2):
        c0 = hh * HEAD_DIM
        q = (q_ref[:, c0:c0 + HEAD_DIM] * HEAD_DIM ** -0.5).astype(BF16)
        k = k_ref[:, c0:c0 + HEAD_DIM].astype(BF16)
        v = v_ref[:, c0:c0 + HEAD_DIM].astype(BF16)
        kc_t = kc_ref[hh].astype(BF16)
        vc_t = vc_ref[hh].astype(BF16)
        s_loc[...] = _dot_nt(q, k)
        s_ctx[...] = _dot(q, kc_t)
        inv_l = []
        for r in range(GRID_ROWS):
            rows = slice(r * GRID_W, (r + 1) * GRID_W)
            first_row, tiles = _na_window(r)
            cols = slice(first_row * GRID_W, (first_row + 2 * len(tiles)) * GRID_W)
            bias = jnp.concatenate([bias_ref[hh, t] for t in tiles], axis=1)
            sc = s_ctx[rows, :]
            sl = s_loc[rows, cols] + bias
            m = jnp.maximum(jnp.max(sc, axis=-1, keepdims=True),
                            jnp.max(sl, axis=-1, keepdims=True))
            pc = jnp.exp(sc - m)
            pw = jnp.exp(sl - m)
            inv_l.append(1.0 / (jnp.sum(pc, axis=-1, keepdims=True)
                                + jnp.sum(pw, axis=-1, keepdims=True)))
            p_ctx[rows, :] = pc.astype(BF16)
            p_loc[rows, cols] = pw.astype(BF16)
        o = _dot_nt(p_ctx[...], vc_t) + _dot(p_loc[...], v)
        o_ref[:, c0:c0 + HEAD_DIM] = o * jnp.concatenate(inv_l, axis=0)


def _lat_attn_a(proj, cache_k, cache_v, bias_tiles, i_layer, mix):
    row0 = N_CTX_TOK // DEC_SEQ
    pair = 2 * HEAD_DIM
    k_col0 = A_WIDTH // pair
    v_col0 = 2 * A_WIDTH // pair
    return pl.pallas_call(
        _lat_attn_a_kernel,
        out_shape=jax.ShapeDtypeStruct((N_TOK, D_MODEL), F32),
        grid=(DEC_BATCH, A_HEADS // 2),
        in_specs=[
            pl.BlockSpec((DEC_SEQ, pair), lambda b, hp: (row0 + b, hp)),
            pl.BlockSpec((DEC_SEQ, pair), lambda b, hp: (row0 + b, k_col0 + hp)),
            pl.BlockSpec((DEC_SEQ, pair), lambda b, hp: (row0 + b, v_col0 + hp)),
            pl.BlockSpec((None, None, 2, HEAD_DIM, PAST_LEN),
                         lambda b, hp: (b, i_layer, hp, 0, 0)),
            pl.BlockSpec((None, None, 2, HEAD_DIM, PAST_LEN),
                         lambda b, hp: (b, i_layer, hp, 0, 0)),
            pl.BlockSpec((None, 2, N_BIAS_TILES, GRID_W, 2 * GRID_W),
                         lambda b, hp: (i_layer, hp, 0, 0, 0)),
            pl.BlockSpec(memory_space=pl.ANY),
        ],
        out_specs=pl.BlockSpec((DEC_SEQ, pair), lambda b, hp: (row0 + b, hp)),
        scratch_shapes=[pltpu.VMEM((DEC_SEQ, DEC_SEQ), F32),
                        pltpu.VMEM((DEC_SEQ, PAST_LEN), F32),
                        pltpu.VMEM((DEC_SEQ, DEC_SEQ), BF16),
                        pltpu.VMEM((DEC_SEQ, PAST_LEN), BF16)],
        input_output_aliases={6: 0},
        compiler_params=_params(2),
        name="lat_attn_a",
    )(proj, proj, proj, cache_k, cache_v, bias_tiles, mix)


def _pool_kernel(u_ref, wp_ref, ps_ref, mix_ref, o_ref, *, n):
    del mix_ref
    t = lax.broadcasted_iota(jnp.int32, (n, B_GROUP_DIM), 0)

    def earlier(x, k):
        return jnp.where(t >= k, pltpu.roll(x, k, axis=0), 0.0)

    def later(x, k):
        return jnp.where(t < n - k, pltpu.roll(x, n - k, axis=0), 0.0)

    for g, w in enumerate(POOL_WINDOWS):
        c0 = g * B_GROUP_DIM
        half = w // 2
        u = u_ref[:, c0:c0 + B_GROUP_DIM]
        before, after = u, u
        k = 1
        while k < half:
            before = before + earlier(before, k)
            after = after + later(after, k)
            k *= 2
        total = earlier(before, 1) + after
        count = (jnp.minimum(t + half, n) - jnp.maximum(t - half, 0)).astype(F32)
        pooled = total / count - u
        y = _dot(pooled.astype(BF16), wp_ref[g].astype(BF16))
        o_ref[:, c0:c0 + B_GROUP_DIM] = y * ps_ref[:, c0:c0 + B_GROUP_DIM]


def _pool(proj, w_pool, pool_scale, i_layer, n, n_seq, row_block0, mix):
    n_groups = len(POOL_WINDOWS)
    return pl.pallas_call(
        functools.partial(_pool_kernel, n=n),
        out_shape=jax.ShapeDtypeStruct((N_TOK, D_MODEL), F32),
        grid=(n_seq,),
        in_specs=[
            pl.BlockSpec((n, B_WIDTH), lambda b: (row_block0 + b, 3)),
            pl.BlockSpec((None, n_groups, B_GROUP_DIM, B_GROUP_DIM),
                         lambda b: (i_layer, 0, 0, 0)),
            pl.BlockSpec((None, 1, B_WIDTH), lambda b: (i_layer, 0, 0)),
            pl.BlockSpec(memory_space=pl.ANY),
        ],
        out_specs=pl.BlockSpec((n, B_WIDTH), lambda b: (row_block0 + b, 1)),
        input_output_aliases={3: 0},
        compiler_params=_params(1),
        name="pool_mixer",
    )(proj, w_pool, pool_scale.reshape(-1, 1, B_WIDTH), mix)


def _sink_row(sink_ref, head0, queries_per_head):
    lane = lax.broadcasted_iota(jnp.int32, (1, C_GROUP * queries_per_head), 1)
    row = jnp.full((1, C_GROUP * queries_per_head), sink_ref[head0], F32)
    for g in range(1, C_GROUP):
        row = jnp.where(lane >= g * queries_per_head, sink_ref[head0 + g], row)
    return row


def _ctx_attn_c_kernel(sink_ref, p_ref, o_ref, *, sink0):
    for kk in range(C_KV_HEADS):
        heads = [kk * C_GROUP + g for g in range(C_GROUP)]
        q = jnp.concatenate(
            [p_ref[:, h * HEAD_DIM:(h + 1) * HEAD_DIM] for h in heads], axis=0)
        q = (q * HEAD_DIM ** -0.5).astype(BF16)
        k0 = C_Q_WIDTH + kk * HEAD_DIM
        v0 = C_Q_WIDTH + C_KV_WIDTH + kk * HEAD_DIM
        k = p_ref[:, k0:k0 + HEAD_DIM].astype(BF16)
        v = p_ref[:, v0:v0 + HEAD_DIM].astype(BF16)
        o_t = _softmax_pv_t([_dot_nt(k, q)], _sink_row(sink_ref, sink0 + kk * C_GROUP, SEQ),
                            [v])
        for g, h in enumerate(heads):
            o_ref[:, h * HEAD_DIM:(h + 1) * HEAD_DIM] = o_t[:, g * SEQ:(g + 1) * SEQ].T


def _ctx_attn_c(proj, sink_all, j_layer):
    return pl.pallas_call(
        functools.partial(_ctx_attn_c_kernel, sink0=j_layer * C_HEADS),
        out_shape=jax.ShapeDtypeStruct((N_TOK, D_MODEL), F32),
        grid=(BATCH,),
        in_specs=[
            pl.BlockSpec(memory_space=pltpu.SMEM),
            pl.BlockSpec((SEQ, C_Q_WIDTH + 2 * C_KV_WIDTH), lambda b: (b, 0)),
        ],
        out_specs=pl.BlockSpec((SEQ, C_Q_WIDTH), lambda b: (b, 0)),
        compiler_params=_params(1),
        name="ctx_attn_c",
    )(sink_all.reshape(-1), proj)


def _rope(x, cos, sin_signed):
    n = x.shape[-1]
    lane = lax.broadcasted_iota(jnp.int32, x.shape, x.ndim - 1)
    first = (lane % 32) < 16
    partner = jnp.where(first, pltpu.roll(x, n - 16, axis=x.ndim - 1),
                        pltpu.roll(x, 16, axis=x.ndim - 1))
    return x * cos + partner * sin_signed


def _lat_attn_c_kernel(sink_ref, q_ref, k_ref, v_ref, kc_ref, vc_ref, cosq_ref, sinq_ref,
                       cosk_ref, sink_rot_ref, mix_ref, o_ref, kb_ref, vb_ref, kctx_ref, vctx_ref,
                       *, sink0):
    del mix_ref
    j = pl.program_id(1)

    @pl.when(j == 0)
    def _():
        for kk in range(C_KV_HEADS):
            c0 = kk * HEAD_DIM
            kctx_ref[:, c0:c0 + HEAD_DIM] = kc_ref[kk].T.astype(BF16)
            vctx_ref[:, c0:c0 + HEAD_DIM] = vc_ref[kk].T.astype(BF16)
        zeros = jnp.zeros((C_BLOCK, C_KV_WIDTH), BF16)
        kb_ref[0:C_BLOCK] = zeros
        vb_ref[0:C_BLOCK] = zeros
        kb_ref[C_BLOCK + DEC_SEQ:2 * C_BLOCK + DEC_SEQ] = zeros
        vb_ref[C_BLOCK + DEC_SEQ:2 * C_BLOCK + DEC_SEQ] = zeros
        kb_ref[C_BLOCK:C_BLOCK + DEC_SEQ] = _rope(
            k_ref[...], cosk_ref[...], sink_rot_ref[...]).astype(BF16)
        vb_ref[C_BLOCK:C_BLOCK + DEC_SEQ] = v_ref[...].astype(BF16)

    q = (_rope(q_ref[...], cosq_ref[...], sinq_ref[...]) * HEAD_DIM ** -0.5).astype(BF16)

    n_q = C_GROUP * C_BLOCK
    qi = lax.broadcasted_iota(jnp.int32, (3 * C_BLOCK, n_q), 1) % C_BLOCK
    jj = lax.broadcasted_iota(jnp.int32, (3 * C_BLOCK, n_q), 0)
    kpos = (j - 1) * C_BLOCK + jj
    valid = (jj >= qi) & (jj <= qi + 2 * C_BLOCK) & (kpos >= 0) & (kpos < DEC_SEQ)
    band0 = pl.multiple_of(j * C_BLOCK, C_BLOCK)

    for kk in range(C_KV_HEADS):
        heads = [kk * C_GROUP + g for g in range(C_GROUP)]
        qs = jnp.concatenate([q[:, h * HEAD_DIM:(h + 1) * HEAD_DIM] for h in heads], axis=0)
        c0 = kk * HEAD_DIM
        k_ctx = kctx_ref[:, c0:c0 + HEAD_DIM]
        v_ctx = vctx_ref[:, c0:c0 + HEAD_DIM]
        k_band = kb_ref[pl.ds(band0, 3 * C_BLOCK), c0:c0 + HEAD_DIM]
        v_band = vb_ref[pl.ds(band0, 3 * C_BLOCK), c0:c0 + HEAD_DIM]
        s_ctx = _dot_nt(k_ctx, qs)
        s_band = jnp.where(valid, _dot_nt(k_band, qs), NEG_INF)
        o_t = _softmax_pv_t([s_ctx, s_band],
                            _sink_row(sink_ref, sink0 + kk * C_GROUP, C_BLOCK),
                            [v_ctx, v_band])
        for g, h in enumerate(heads):
            o_ref[:, h * HEAD_DIM:(h + 1) * HEAD_DIM] = (
                o_t[:, g * C_BLOCK:(g + 1) * C_BLOCK].T)


def _lat_attn_c(proj, cache_k, cache_v, sink, cos_t, sin_t, j_layer, mix):
    n_blocks = DEC_SEQ // C_BLOCK
    q_row0 = N_CTX_TOK // C_BLOCK
    kv_row0 = N_CTX_TOK // DEC_SEQ
    k_col = C_Q_WIDTH // C_KV_WIDTH
    return pl.pallas_call(
        functools.partial(_lat_attn_c_kernel, sink0=j_layer * C_HEADS),
        out_shape=jax.ShapeDtypeStruct((N_TOK, D_MODEL), F32),
        grid=(DEC_BATCH, n_blocks),
        in_specs=[
            pl.BlockSpec(memory_space=pltpu.SMEM),
            pl.BlockSpec((C_BLOCK, C_Q_WIDTH), lambda b, j: (q_row0 + b * n_blocks + j, 0)),
            pl.BlockSpec((DEC_SEQ, C_KV_WIDTH), lambda b, j: (kv_row0 + b, k_col)),
            pl.BlockSpec((DEC_SEQ, C_KV_WIDTH), lambda b, j: (kv_row0 + b, k_col + 1)),
            pl.BlockSpec((None, None, C_KV_HEADS, HEAD_DIM, PAST_LEN),
                         lambda b, j: (b, j_layer, 0, 0, 0)),
            pl.BlockSpec((None, None, C_KV_HEADS, HEAD_DIM, PAST_LEN),
                         lambda b, j: (b, j_layer, 0, 0, 0)),
            pl.BlockSpec((C_BLOCK, C_Q_WIDTH), lambda b, j: (j, 0)),
            pl.BlockSpec((C_BLOCK, C_Q_WIDTH), lambda b, j: (j, 0)),
            pl.BlockSpec((DEC_SEQ, C_KV_WIDTH), lambda b, j: (0, 0)),
            pl.BlockSpec((DEC_SEQ, C_KV_WIDTH), lambda b, j: (0, 0)),
            pl.BlockSpec(memory_space=pl.ANY),
        ],
        out_specs=pl.BlockSpec((C_BLOCK, C_Q_WIDTH),
                               lambda b, j: (q_row0 + b * n_blocks + j, 0)),
        input_output_aliases={10: 0},
        scratch_shapes=[pltpu.VMEM((DEC_SEQ + 2 * C_BLOCK, C_KV_WIDTH), BF16),
                        pltpu.VMEM((DEC_SEQ + 2 * C_BLOCK, C_KV_WIDTH), BF16),
                        pltpu.VMEM((PAST_LEN, C_KV_WIDTH), BF16),
                        pltpu.VMEM((PAST_LEN, C_KV_WIDTH), BF16)],
        compiler_params=_params(2),
        name="lat_attn_c",
    )(sink.reshape(-1), proj, proj, proj, cache_k, cache_v, cos_t, sin_t, cos_t, sin_t, mix)


def _rope_tables():
    t = jnp.arange(DEC_SEQ)
    pos = jnp.stack([t // GRID_W, t % GRID_W], axis=-1).astype(F32)
    half = HEAD_DIM // 4
    inv = ROPE_BASE ** (-jnp.arange(half, dtype=F32) / half)
    ang = pos[:, :, None] * inv
    cos = jnp.cos(ang)
    sin = jnp.sin(ang)
    cos64 = jnp.stack([cos, cos], axis=2).reshape(DEC_SEQ, HEAD_DIM)
    sin64 = jnp.stack([-sin, sin], axis=2).reshape(DEC_SEQ, HEAD_DIM)
    return jnp.tile(cos64, (1, C_HEADS)), jnp.tile(sin64, (1, C_HEADS))


FFN_CHUNK = 256
N_FFN_CHUNKS = FFN_HIDDEN // FFN_CHUNK
N_WO_PIECES = D_MODEL // FFN_CHUNK


def _post_mixer_kernel(*refs, layer, wo_idx, tm, final):
    x_ref, mix_ref, g_ref, mod_ref = refs[:4]
    n_in = 5 if final else 4
    wo_hbm, wgu_hbm, wd_hbm = refs[n_in:n_in + 3]
    n_out = 2 if final else 1
    out_refs = refs[n_in + 3:n_in + 3 + n_out]
    (wo_s, wg_s, wu_s, wd_s, act_s, stage_col, stage_row, sem_col,
     sem_row) = refs[n_in + 3 + n_out:]
    n_row_pieces = N_WO_PIECES + N_FFN_CHUNKS

    def col_copy(which, c):
        src = wgu_hbm.at[layer, :, pl.ds(which * FFN_HIDDEN + c * FFN_CHUNK, FFN_CHUNK)]
        return pltpu.make_async_copy(src, stage_col.at[which, c % 2], sem_col.at[which, c % 2])

    def row_copy(p):
        if p < N_WO_PIECES:
            src = wo_hbm.at[wo_idx, pl.ds(p * FFN_CHUNK, FFN_CHUNK), :]
        else:
            src = wd_hbm.at[layer, pl.ds((p - N_WO_PIECES) * FFN_CHUNK, FFN_CHUNK), :]
        return pltpu.make_async_copy(src, stage_row.at[p % 2], sem_row.at[p % 2])

    def take_row_piece(p, dst_ref, row0):
        if p + 1 < n_row_pieces:
            row_copy(p + 1).start()
        row_copy(p).wait()
        dst_ref[row0:row0 + FFN_CHUNK, :] = stage_row[p % 2].astype(BF16)

    def tile(load_weights):
        if load_weights:
            row_copy(0).start()
            for which in range(2):
                col_copy(which, 0).start()
            for p in range(N_WO_PIECES):
                take_row_piece(p, wo_s, p * FFN_CHUNK)
        gate1 = mod_ref[:, 2 * D_MODEL:3 * D_MODEL]
        x1 = x_ref[...] + gate1 * _dot(mix_ref[...].astype(BF16), wo_s[...])
        h = _norm_mod(x1, g_ref[...], mod_ref, 3).astype(BF16)
        for c in range(N_FFN_CHUNKS):
            if load_weights:
                for which, dst in ((0, wg_s), (1, wu_s)):
                    if c + 1 < N_FFN_CHUNKS:
                        col_copy(which, c + 1).start()
                    col_copy(which, c).wait()
                    dst[c] = stage_col[which, c % 2].astype(BF16)
            gate = _dot(h, wg_s[c])
            up = _dot(h, wu_s[c])
            act = gate / (1.0 + jnp.exp(-gate)) * up
            act_s[:, c * FFN_CHUNK:(c + 1) * FFN_CHUNK] = act.astype(BF16)
            if load_weights:
                take_row_piece(N_WO_PIECES + c, wd_s, c * FFN_CHUNK)
        gate2 = mod_ref[:, 5 * D_MODEL:6 * D_MODEL]
        return x1 + gate2 * _dot(act_s[...], wd_s[...])

    def emit(x2):
        if not final:
            out_refs[0][...] = x2
            return
        var = jnp.mean(x2 * x2, axis=-1, keepdims=True)
        y = x2 * lax.rsqrt(var + EPS) * refs[4][...]
        is_ctx = pl.program_id(0) < N_CTX_TOK // tm

        @pl.when(is_ctx)
        def _():
            out_refs[0][...] = y

        @pl.when(jnp.logical_not(is_ctx))
        def _():
            out_refs[1][...] = y

    first = pl.program_id(0) == 0

    @pl.when(first)
    def _():
        emit(tile(True))

    @pl.when(jnp.logical_not(first))
    def _():
        emit(tile(False))


def _post_mixer(x, mix, g_all, mod4, layer, w_out_all, wo_idx, w_gate_up, w_down,
                norm_final=None):
    tm = 512
    n_ctx_tiles = N_CTX_TOK // tm
    final = norm_final is not None
    row_spec = pl.BlockSpec((tm, D_MODEL), lambda i: (i, 0))
    hbm = pl.BlockSpec(memory_space=pl.ANY)
    if final:
        extra_in = [norm_final.reshape(1, D_MODEL)]
        extra_specs = [pl.BlockSpec((1, D_MODEL), lambda i: (0, 0))]
        out_shape = (jax.ShapeDtypeStruct((N_CTX_TOK, D_MODEL), F32),
                     jax.ShapeDtypeStruct((N_LAT_TOK, D_MODEL), F32))
        out_specs = (
            pl.BlockSpec((tm, D_MODEL), lambda i: (jnp.minimum(i, n_ctx_tiles - 1), 0)),
            pl.BlockSpec((tm, D_MODEL), lambda i: (jnp.maximum(i - n_ctx_tiles, 0), 0)))
    else:
        extra_in, extra_specs = [], []
        out_shape = jax.ShapeDtypeStruct((N_TOK, D_MODEL), F32)
        out_specs = row_spec
    return pl.pallas_call(
        functools.partial(_post_mixer_kernel, layer=layer, wo_idx=wo_idx, tm=tm, final=final),
        out_shape=out_shape,
        grid=(N_TOK // tm,),
        in_specs=[
            row_spec,
            row_spec,
            pl.BlockSpec((None, 1, D_MODEL), lambda i: (layer, 0, 0)),
            pl.BlockSpec((None, None, 1, 6 * D_MODEL),
                         lambda i: (layer, _group_of_tile(i, tm), 0, 0)),
        ] + extra_specs + [hbm, hbm, hbm],
        out_specs=out_specs,
        scratch_shapes=[
            pltpu.VMEM((D_MODEL, D_MODEL), BF16),
            pltpu.VMEM((N_FFN_CHUNKS, D_MODEL, FFN_CHUNK), BF16),
            pltpu.VMEM((N_FFN_CHUNKS, D_MODEL, FFN_CHUNK), BF16),
            pltpu.VMEM((FFN_HIDDEN, D_MODEL), BF16),
            pltpu.VMEM((tm, FFN_HIDDEN), BF16),
            pltpu.VMEM((2, 2, D_MODEL, FFN_CHUNK), F32),
            pltpu.VMEM((2, FFN_CHUNK, D_MODEL), F32),
            pltpu.SemaphoreType.DMA((2, 2)),
            pltpu.SemaphoreType.DMA((2,)),
        ],
        compiler_params=_params(1),
        name="post_mixer",
    )(x, mix, g_all.reshape(DEPTH, 1, D_MODEL), mod4, *extra_in, w_out_all, w_gate_up, w_down)


def kernel(x_prompt, x_sample, cache_a_k, cache_a_v, cache_c_k, cache_c_v, c, c_ctx, w_mod, b_mod, norm_mix, norm_ffn, w_in_ab, rpb_a, w_pool, pool_scale, w_out_ab, w_in_c, sink_c, w_out_c, w_gate_up, w_down, norm_final):
    xs = (x_prompt.reshape(N_CTX_TOK, D_MODEL), x_sample.reshape(N_LAT_TOK, D_MODEL))
    cond8 = jnp.concatenate(
        [c_ctx[None], c, jnp.zeros((N_GROUPS_PAD - 1 - DEC_BATCH, D_MODEL), F32)], axis=0)
    mod4 = _modulation(cond8, w_mod, b_mod).reshape(DEPTH, N_GROUPS_PAD, 1, 6 * D_MODEL)

    n_ab = cache_a_k.shape[1]
    n_c = cache_c_k.shape[1]
    cache_a_k, cache_a_v, cache_c_k, cache_c_v = (
        jnp.transpose(t, (0, 1, 3, 4, 2)) for t in (cache_a_k, cache_a_v, cache_c_k, cache_c_v))
    cos_t, sin_t = _rope_tables()
    bias_tiles = _na_bias_tiles(rpb_a)

    new_a = []
    new_c = []
    for l in range(DEPTH):
        if l % 2 == 0:
            i = l // 2
            proj, *rest = _inproj(xs, norm_mix, mod4, l, w_in_ab, i, A_HEADS,
                                  A_WIDTH, 2 * A_WIDTH, n_ab, new_a)
            new_a = rest[:2]
            if len(xs) == 2:
                xs = (rest[2],)
            mix = _ctx_attn_a(proj)
            mix = _lat_attn_a(proj, cache_a_k, cache_a_v, bias_tiles, i, mix)
            mix = _pool(proj, w_pool, pool_scale, i, SEQ, BATCH, 0, mix)
            mix = _pool(proj, w_pool, pool_scale, i, DEC_SEQ, DEC_BATCH,
                        N_CTX_TOK // DEC_SEQ, mix)
            w_out, wo_idx = w_out_ab, i
        else:
            j = l // 2
            proj, *new_c = _inproj(xs, norm_mix, mod4, l, w_in_c, j, C_KV_HEADS,
                                   C_Q_WIDTH, C_Q_WIDTH + C_KV_WIDTH, n_c, new_c)
            mix = _ctx_attn_c(proj, sink_c, j)
            mix = _lat_attn_c(proj, cache_c_k, cache_c_v, sink_c, cos_t, sin_t, j, mix)
            w_out, wo_idx = w_out_c, j
        if l + 1 < DEPTH:
            xs = (_post_mixer(xs[0], mix, norm_ffn, mod4, l, w_out, wo_idx, w_gate_up, w_down),)
        else:
            y_ctx, y_lat = _post_mixer(xs[0], mix, norm_ffn, mod4, l, w_out, wo_idx, w_gate_up,
                                       w_down, norm_final)

    return (y_ctx.reshape(BATCH, SEQ, D_MODEL), y_lat.reshape(DEC_BATCH, DEC_SEQ, D_MODEL),
            new_a[0], new_a[1], new_c[0], new_c[1])
```

```python
import functools

import jax
import jax.numpy as jnp
import numpy as np
from jax import lax
from jax.experimental import pallas as pl
from jax.experimental.pallas import tpu as pltpu

D_MODEL = 1024
BATCH = 16
SEQ = 256
DEPTH = 4
DEC_BATCH = 2
DEC_SEQ = 1024
PAST_LEN = 512
GRID_W = 64
HEAD_DIM = 64
A_WIDTH = 512
A_HEADS = 8
B_WIDTH = 512
POOL_WINDOWS = (2, 4, 8, 16)
B_GROUP_DIM = 128
NA_ROWS = 8
NA_COLS = 16
C_HEADS = 16
C_KV_HEADS = 4
C_GROUP = C_HEADS // C_KV_HEADS
C_Q_WIDTH = 1024
C_KV_WIDTH = 256
C_BLOCK = 128
FFN_HIDDEN = 2816
ROPE_BASE = 10000.0
EPS = 1e-6
NEG_INF = -1e30

N_CTX_TOK = BATCH * SEQ
N_LAT_TOK = DEC_BATCH * DEC_SEQ
N_TOK = N_CTX_TOK + N_LAT_TOK
GRID_ROWS = DEC_SEQ // GRID_W
N_GROUPS_PAD = 8

VMEM_LIMIT = 56 * 1024 * 1024

F32 = jnp.float32
BF16 = jnp.bfloat16


def _params(n_axes):
    return pltpu.CompilerParams(dimension_semantics=("arbitrary",) * n_axes,
                                vmem_limit_bytes=VMEM_LIMIT)


def _group_of_tile(i, tm):
    row0 = i * tm
    return jnp.where(row0 < N_CTX_TOK, 0, 1 + (row0 - N_CTX_TOK) // DEC_SEQ)


def _dot_nt(a, b):
    return lax.dot_general(a, b, (((1,), (1,)), ((), ())), preferred_element_type=F32)


def _dot(a, b):
    return jnp.dot(a, b, preferred_element_type=F32)


def _dot_tn(a, b):
    return lax.dot_general(a, b, (((0,), (0,)), ((), ())), preferred_element_type=F32)


def _softmax_pv_t(scores_t, sink_row, values):
    m = sink_row
    for s in scores_t:
        ms = jnp.max(s, axis=0, keepdims=True)
        m = ms if m is None else jnp.maximum(m, ms)
    l = None if sink_row is None else jnp.exp(sink_row - m)
    o = None
    for s, v in zip(scores_t, values):
        p = jnp.exp(s - m)
        ls = jnp.sum(p, axis=0, keepdims=True)
        l = ls if l is None else l + ls
        pv = _dot_tn(v, p.astype(BF16))
        o = pv if o is None else o + pv
    return o * (1.0 / l)


def _mod_kernel(cond_ref, w_ref, b_ref, o_ref):
    c = cond_ref[...]
    s = c / (1.0 + jnp.exp(-c))
    o_ref[...] = _dot(s.astype(BF16), w_ref[...].astype(BF16)) + b_ref[...]


def _modulation(cond8, w_mod, b_mod):
    tn = 1536
    return pl.pallas_call(
        _mod_kernel,
        out_shape=jax.ShapeDtypeStruct((DEPTH, N_GROUPS_PAD, 6 * D_MODEL), F32),
        grid=(DEPTH, 6 * D_MODEL // tn),
        in_specs=[
            pl.BlockSpec((N_GROUPS_PAD, D_MODEL), lambda l, j: (0, 0)),
            pl.BlockSpec((None, D_MODEL, tn), lambda l, j: (l, 0, j)),
            pl.BlockSpec((None, 1, tn), lambda l, j: (l, 0, j)),
        ],
        out_specs=pl.BlockSpec((None, N_GROUPS_PAD, tn), lambda l, j: (l, 0, j)),
        compiler_params=_params(2),
        name="modulation",
    )(cond8, w_mod, b_mod.reshape(DEPTH, 1, 6 * D_MODEL))


def _norm_mod(x, g, mod_ref, shift_idx):
    var = jnp.mean(x * x, axis=-1, keepdims=True)
    y = x * lax.rsqrt(var + EPS) * g
    shift = mod_ref[:, shift_idx * D_MODEL:(shift_idx + 1) * D_MODEL]
    scale = mod_ref[:, (shift_idx + 1) * D_MODEL:(shift_idx + 2) * D_MODEL]
    return y * (1.0 + scale) + shift


QKV_WIDTH = 3 * A_WIDTH
Q_SCALE = HEAD_DIM ** -0.5


def _rope(x, cos, sin_signed):
    n = x.shape[-1]
    lane = lax.broadcasted_iota(jnp.int32, x.shape, x.ndim - 1)
    first = (lane % 32) < 16
    partner = jnp.where(first, pltpu.roll(x, n - 16, axis=x.ndim - 1),
                        pltpu.roll(x, 16, axis=x.ndim - 1))
    return x * cos + partner * sin_signed


def _inproj_kernel(*refs, tm, n_heads, q_width, n_prev, split_x, rope, has_u):
    n_x = 2 if split_x else 1
    g_ref, mod_ref, w_ref = refs[n_x:n_x + 3]
    n_in = n_x + 3 + (2 if rope else 0) + n_prev
    qkv_ref, ck_ref, cv_ref = refs[n_in:n_in + 3]
    extra_out = refs[n_in + 3:-2]
    wbf_ref, res_ref = refs[-2:]
    i = pl.program_id(0)
    is_ctx = i < N_CTX_TOK // tm
    kv_width = (QKV_WIDTH - q_width) // 2
    k_col, v_col = q_width, q_width + kv_width

    @pl.when(i == 0)
    def _():
        wbf_ref[...] = w_ref[...].astype(BF16)

    if split_x:
        x = jnp.where(is_ctx, refs[0][...], refs[1][...])
        extra_out[-1][...] = x
    else:
        x = refs[0][...]
    h = _norm_mod(x, g_ref[...], mod_ref, 0)
    res_ref[...] = _dot(h.astype(BF16), wbf_ref[...])

    qkv_ref[:, v_col:] = res_ref[:, v_col:QKV_WIDTH].astype(BF16)
    if has_u:
        extra_out[0][...] = res_ref[:, QKV_WIDTH:]

    def plain_qk():
        qkv_ref[:, :q_width] = (res_ref[:, :q_width] * Q_SCALE).astype(BF16)
        qkv_ref[:, k_col:v_col] = res_ref[:, k_col:v_col].astype(BF16)

    if rope:
        cos_ref, sin_ref = refs[n_x + 3:n_x + 5]

        @pl.when(jnp.logical_not(is_ctx))
        def _():
            q = _rope(res_ref[:, :q_width], cos_ref[...], sin_ref[...])
            qkv_ref[:, :q_width] = (q * Q_SCALE).astype(BF16)
            k = _rope(res_ref[:, k_col:v_col], cos_ref[:, :kv_width], sin_ref[:, :kv_width])
            qkv_ref[:, k_col:v_col] = k.astype(BF16)

        pl.when(is_ctx)(plain_qk)
    else:
        plain_qk()

    @pl.when(is_ctx)
    def _():
        rows = SEQ * n_heads
        for col, c_ref in ((k_col, ck_ref), (v_col, cv_ref)):
            flat = c_ref.reshape(tm // SEQ * rows, HEAD_DIM)
            for r in range(tm // SEQ):
                for hd in range(n_heads):
                    c0 = col + hd * HEAD_DIM
                    flat[pl.ds(r * rows + hd, SEQ, stride=n_heads), :] = (
                        res_ref[r * SEQ:(r + 1) * SEQ, c0:c0 + HEAD_DIM])


def _inproj(xs, g_all, mod4, layer, w_all, w_idx, n_heads, q_width, n_slots, prev_caches,
            rope_tables=None):
    tm = 512
    n_out = w_all.shape[2]
    n_ctx_tiles = N_CTX_TOK // tm
    split_x = len(xs) == 2
    rope = rope_tables is not None
    has_u = n_out > QKV_WIDTH
    cache_shape = jax.ShapeDtypeStruct((BATCH, n_slots, SEQ, n_heads, HEAD_DIM), F32)
    cache_spec = pl.BlockSpec(
        (tm // SEQ, None, SEQ, n_heads, HEAD_DIM),
        lambda i: (jnp.minimum(i, n_ctx_tiles - 1), w_idx, 0, 0, 0))
    row_spec = pl.BlockSpec((tm, D_MODEL), lambda i: (i, 0))
    if split_x:
        x_specs = [
            pl.BlockSpec((tm, D_MODEL), lambda i: (jnp.minimum(i, n_ctx_tiles - 1), 0)),
            pl.BlockSpec((tm, D_MODEL), lambda i: (jnp.maximum(i - n_ctx_tiles, 0), 0)),
        ]
    else:
        x_specs = [row_spec]
    out_shape = [jax.ShapeDtypeStruct((N_TOK, QKV_WIDTH), BF16), cache_shape, cache_shape]
    out_specs = [pl.BlockSpec((tm, QKV_WIDTH), lambda i: (i, 0)), cache_spec, cache_spec]
    if has_u:
        out_shape.append(jax.ShapeDtypeStruct((N_TOK, n_out - QKV_WIDTH), F32))
        out_specs.append(pl.BlockSpec((tm, n_out - QKV_WIDTH), lambda i: (i, 0)))
    if split_x:
        out_shape.append(jax.ShapeDtypeStruct((N_TOK, D_MODEL), F32))
        out_specs.append(row_spec)
    rope_in, rope_specs = [], []
    if rope:
        tiles_per_seq = DEC_SEQ // tm
        rope_spec = pl.BlockSpec(
            (tm, D_MODEL), lambda i: (jnp.maximum(i - n_ctx_tiles, 0) % tiles_per_seq, 0))
        rope_in, rope_specs = list(rope_tables), [rope_spec, rope_spec]
    n_prev = len(prev_caches)
    n_before = len(xs) + 3 + len(rope_in)
    return pl.pallas_call(
        functools.partial(_inproj_kernel, tm=tm, n_heads=n_heads, q_width=q_width,
                          n_prev=n_prev, split_x=split_x, rope=rope, has_u=has_u),
        out_shape=out_shape,
        grid=(N_TOK // tm,),
        in_specs=x_specs + [
            pl.BlockSpec((None, 1, D_MODEL), lambda i: (layer, 0, 0)),
            pl.BlockSpec((None, None, 1, 6 * D_MODEL),
                         lambda i: (layer, _group_of_tile(i, tm), 0, 0)),
            pl.BlockSpec((None, D_MODEL, n_out), lambda i: (w_idx, 0, 0),
                         pipeline_mode=pl.Buffered(1)),
        ] + rope_specs + [pl.BlockSpec(memory_space=pl.ANY)] * n_prev,
        out_specs=out_specs,
        scratch_shapes=[pltpu.VMEM((D_MODEL, n_out), BF16), pltpu.VMEM((tm, n_out), F32)],
        input_output_aliases={n_before + k: 1 + k for k in range(n_prev)},
        compiler_params=_params(1),
        name="inproj",
    )(*xs, g_all.reshape(DEPTH, 1, D_MODEL), mod4, w_all, *rope_in, *prev_caches)


def _ctx_attn_a_kernel(p_ref, o_ref):
    for h in range(A_HEADS):
        c0 = h * HEAD_DIM
        q = p_ref[:, c0:c0 + HEAD_DIM]
        k = p_ref[:, A_WIDTH + c0:A_WIDTH + c0 + HEAD_DIM]
        v = p_ref[:, 2 * A_WIDTH + c0:2 * A_WIDTH + c0 + HEAD_DIM]
        s = _dot_nt(q, k)
        m = jnp.max(s, axis=-1, keepdims=True)
        p = jnp.exp(s - m)
        l = jnp.sum(p, axis=-1, keepdims=True)
        o_ref[:, c0:c0 + HEAD_DIM] = _dot(p.astype(BF16), v) / l


def _ctx_attn_a(qkv):
    return pl.pallas_call(
        _ctx_attn_a_kernel,
        out_shape=jax.ShapeDtypeStruct((N_TOK, D_MODEL), F32),
        grid=(BATCH,),
        in_specs=[pl.BlockSpec((SEQ, QKV_WIDTH), lambda b: (b, 0))],
        out_specs=pl.BlockSpec((SEQ, A_WIDTH), lambda b: (b, 0)),
        compiler_params=_params(1),
        name="ctx_attn_a",
    )(qkv)


N_DROW = 2 * NA_ROWS - 1
N_DCOL = 2 * NA_COLS - 1
N_BIAS_TILES = 16
BIAS_TILE_LEFT_PAD = 14
BIAS_TILE_RIGHT_PAD = 15
MID_DROW = NA_ROWS - 1 - NA_ROWS // 2


def _na_bias_kernel(rpb_ref, o_ref):
    base = (pl.program_id(0) * A_HEADS + pl.program_id(1)) * (N_DROW * N_DCOL)
    qi = lax.broadcasted_iota(jnp.int32, (GRID_W, 2 * GRID_W), 0)
    lane = lax.broadcasted_iota(jnp.int32, (GRID_W, 2 * GRID_W), 1)
    right = lane >= GRID_W
    kc = jnp.where(right, lane - GRID_W, lane)
    rel = kc - qi + (NA_COLS - 1)
    qstart = jnp.clip(qi - NA_COLS // 2, 0, GRID_W - NA_COLS)
    valid = (kc >= qstart) & (kc < qstart + NA_COLS)

    rows = [jnp.zeros((GRID_W, 2 * GRID_W), F32)] * N_DROW
    for d in range(N_DCOL):
        hit = rel == d
        rows = [jnp.where(hit, rpb_ref[base + dr * N_DCOL + d], rows[dr])
                for dr in range(N_DROW)]

    for t in range(N_DROW - 1):
        o_ref[t] = jnp.where(valid, jnp.where(right, rows[t + 1], rows[t]), NEG_INF)
    o_ref[BIAS_TILE_LEFT_PAD] = jnp.where(valid & right, rows[MID_DROW], NEG_INF)
    o_ref[BIAS_TILE_RIGHT_PAD] = jnp.where(valid & jnp.logical_not(right),
                                           rows[MID_DROW + NA_ROWS - 1], NEG_INF)


def _na_bias_tiles(rpb_a):
    n_layers = rpb_a.shape[0]
    return pl.pallas_call(
        _na_bias_kernel,
        out_shape=jax.ShapeDtypeStruct(
            (n_layers, A_HEADS, N_BIAS_TILES, GRID_W, 2 * GRID_W), F32),
        grid=(n_layers, A_HEADS),
        in_specs=[pl.BlockSpec(memory_space=pltpu.SMEM)],
        out_specs=pl.BlockSpec((None, None, N_BIAS_TILES, GRID_W, 2 * GRID_W),
                               lambda l, h: (l, h, 0, 0, 0)),
        compiler_params=_params(2),
        name="na_bias",
    )(rpb_a.reshape(-1))


def _na_window(r):
    start = min(max(r - NA_ROWS // 2, 0), GRID_ROWS - NA_ROWS)
    first_drow = start - r + NA_ROWS - 1
    if start % 2 == 0:
        return start, [first_drow + 2 * p for p in range(NA_ROWS // 2)]
    assert first_drow == MID_DROW
    inner = [first_drow + 1 + 2 * p for p in range(NA_ROWS // 2 - 1)]
    return start - 1, [BIAS_TILE_LEFT_PAD] + inner + [BIAS_TILE_RIGHT_PAD]


def _lat_attn_a_kernel(q_ref, k_ref, v_ref, kc_ref, vc_ref, bias_ref, mix_ref, o_ref,
                       s_loc, s_ctx, p_loc, p_ctx):
    del mix_ref
    hp = pl.program_id(1)

    @pl.when((pl.program_id(0) == 0) & (hp == 0))
    def _():
        p_loc[...] = jnp.zeros_like(p_loc)

    for hh in range(2):
        c0 = hh * HEAD_DIM
        q = q_ref[:, c0:c0 + HEAD_DIM]
        k = k_ref[:, c0:c0 + HEAD_DIM]
        v = v_ref[:, c0:c0 + HEAD_DIM]
        kc_t = kc_ref[hh].astype(BF16)
        vc_t = vc_ref[hh].astype(BF16)
        s_loc[...] = _dot_nt(q, k)
        s_ctx[...] = _dot(q, kc_t)
        inv_l = []
        for r in range(GRID_ROWS):
            rows = slice(r * GRID_W, (r + 1) * GRID_W)
            first_row, tiles = _na_window(r)
            cols = slice(first_row * GRID_W, (first_row + 2 * len(tiles)) * GRID_W)
            bias = jnp.concatenate([bias_ref[hh, t] for t in tiles], axis=1)
            sc = s_ctx[rows, :]
            sl = s_loc[rows, cols] + bias
            m = jnp.maximum(jnp.max(sc, axis=-1, keepdims=True),
                            jnp.max(sl, axis=-1, keepdims=True))
            pc = jnp.exp(sc - m)
            pw = jnp.exp(sl - m)
            inv_l.append(1.0 / (jnp.sum(pc, axis=-1, keepdims=True)
                                + jnp.sum(pw, axis=-1, keepdims=True)))
            p_ctx[rows, :] = pc.astype(BF16)
            p_loc[rows, cols] = pw.astype(BF16)
        o = _dot_nt(p_ctx[...], vc_t) + _dot(p_loc[...], v)
        o_ref[:, c0:c0 + HEAD_DIM] = o * jnp.concatenate(inv_l, axis=0)


def _lat_attn_a(qkv, cache_k, cache_v, bias_tiles, i_layer, mix):
    row0 = N_CTX_TOK // DEC_SEQ
    pair = 2 * HEAD_DIM
    k_col0 = A_WIDTH // pair
    v_col0 = 2 * A_WIDTH // pair
    return pl.pallas_call(
        _lat_attn_a_kernel,
        out_shape=jax.ShapeDtypeStruct((N_TOK, D_MODEL), F32),
        grid=(DEC_BATCH, A_HEADS // 2),
        in_specs=[
            pl.BlockSpec((DEC_SEQ, pair), lambda b, hp: (row0 + b, hp)),
            pl.BlockSpec((DEC_SEQ, pair), lambda b, hp: (row0 + b, k_col0 + hp)),
            pl.BlockSpec((DEC_SEQ, pair), lambda b, hp: (row0 + b, v_col0 + hp)),
            pl.BlockSpec((None, None, 2, HEAD_DIM, PAST_LEN),
                         lambda b, hp: (b, i_layer, hp, 0, 0)),
            pl.BlockSpec((None, None, 2, HEAD_DIM, PAST_LEN),
                         lambda b, hp: (b, i_layer, hp, 0, 0)),
            pl.BlockSpec((None, 2, N_BIAS_TILES, GRID_W, 2 * GRID_W),
                         lambda b, hp: (i_layer, hp, 0, 0, 0)),
            pl.BlockSpec(memory_space=pl.ANY),
        ],
        out_specs=pl.BlockSpec((DEC_SEQ, pair), lambda b, hp: (row0 + b, hp)),
        scratch_shapes=[pltpu.VMEM((DEC_SEQ, DEC_SEQ), F32),
                        pltpu.VMEM((DEC_SEQ, PAST_LEN), F32),
                        pltpu.VMEM((DEC_SEQ, DEC_SEQ), BF16),
                        pltpu.VMEM((DEC_SEQ, PAST_LEN), BF16)],
        input_output_aliases={6: 0},
        compiler_params=_params(2),
        name="lat_attn_a",
    )(qkv, qkv, qkv, cache_k, cache_v, bias_tiles, mix)


def _pool_kernel(u_ref, wp_ref, ps_ref, mix_ref, o_ref, *, rows):
    del mix_ref
    n = jnp.where(pl.program_id(0) < N_CTX_TOK // rows, SEQ, DEC_SEQ)
    t = lax.broadcasted_iota(jnp.int32, (rows, B_GROUP_DIM), 0) & (n - 1)

    def earlier(x, k):
        return jnp.where(t >= k, pltpu.roll(x, k, axis=0), 0.0)

    def later(x, k):
        return jnp.where(t < n - k, pltpu.roll(x, rows - k, axis=0), 0.0)

    for g, w in enumerate(POOL_WINDOWS):
        c0 = g * B_GROUP_DIM
        half = w // 2
        u = u_ref[:, c0:c0 + B_GROUP_DIM]
        before, after = u, u
        k = 1
        while k < half:
            before = before + earlier(before, k)
            after = after + later(after, k)
            k *= 2
        total = earlier(before, 1) + after
        count = (jnp.minimum(t + half, n) - jnp.maximum(t - half, 0)).astype(F32)
        pooled = total / count - u
        y = _dot(pooled.astype(BF16), wp_ref[g].astype(BF16))
        o_ref[:, c0:c0 + B_GROUP_DIM] = y * ps_ref[:, c0:c0 + B_GROUP_DIM]


def _pool(u, w_pool, pool_scale, i_layer, mix):
    n_groups = len(POOL_WINDOWS)
    rows = DEC_SEQ
    return pl.pallas_call(
        functools.partial(_pool_kernel, rows=rows),
        out_shape=jax.ShapeDtypeStruct((N_TOK, D_MODEL), F32),
        grid=(N_TOK // rows,),
        in_specs=[
            pl.BlockSpec((rows, B_WIDTH), lambda b: (b, 0)),
            pl.BlockSpec((None, n_groups, B_GROUP_DIM, B_GROUP_DIM),
                         lambda b: (i_layer, 0, 0, 0)),
            pl.BlockSpec((None, 1, B_WIDTH), lambda b: (i_layer, 0, 0)),
            pl.BlockSpec(memory_space=pl.ANY),
        ],
        out_specs=pl.BlockSpec((rows, B_WIDTH), lambda b: (b, 1)),
        input_output_aliases={3: 0},
        compiler_params=_params(1),
        name="pool_mixer",
    )(u, w_pool, pool_scale.reshape(-1, 1, B_WIDTH), mix)


def _sink_row(sink_ref, head0, queries_per_head):
    lane = lax.broadcasted_iota(jnp.int32, (1, C_GROUP * queries_per_head), 1)
    row = jnp.full((1, C_GROUP * queries_per_head), sink_ref[head0], F32)
    for g in range(1, C_GROUP):
        row = jnp.where(lane >= g * queries_per_head, sink_ref[head0 + g], row)
    return row


def _ctx_attn_c_kernel(sink_ref, p_ref, o_ref, *, sink0):
    for kk in range(C_KV_HEADS):
        heads = [kk * C_GROUP + g for g in range(C_GROUP)]
        q = jnp.concatenate(
            [p_ref[:, h * HEAD_DIM:(h + 1) * HEAD_DIM] for h in heads], axis=0)
        k0 = C_Q_WIDTH + kk * HEAD_DIM
        v0 = C_Q_WIDTH + C_KV_WIDTH + kk * HEAD_DIM
        k = p_ref[:, k0:k0 + HEAD_DIM]
        v = p_ref[:, v0:v0 + HEAD_DIM]
        o_t = _softmax_pv_t([_dot_nt(k, q)], _sink_row(sink_ref, sink0 + kk * C_GROUP, SEQ),
                            [v])
        for g, h in enumerate(heads):
            o_ref[:, h * HEAD_DIM:(h + 1) * HEAD_DIM] = o_t[:, g * SEQ:(g + 1) * SEQ].T


def _ctx_attn_c(qkv, sink_all, j_layer):
    return pl.pallas_call(
        functools.partial(_ctx_attn_c_kernel, sink0=j_layer * C_HEADS),
        out_shape=jax.ShapeDtypeStruct((N_TOK, D_MODEL), F32),
        grid=(BATCH,),
        in_specs=[
            pl.BlockSpec(memory_space=pltpu.SMEM),
            pl.BlockSpec((SEQ, C_Q_WIDTH + 2 * C_KV_WIDTH), lambda b: (b, 0)),
        ],
        out_specs=pl.BlockSpec((SEQ, C_Q_WIDTH), lambda b: (b, 0)),
        compiler_params=_params(1),
        name="ctx_attn_c",
    )(sink_all.reshape(-1), qkv)


def _lat_attn_c_kernel(sink_ref, q_ref, k_ref, v_ref, kc_ref, vc_ref, mix_ref, o_ref,
                       kctx_ref, vctx_ref, *, sink0):
    del mix_ref
    j = pl.program_id(1)
    n_blocks = DEC_SEQ // C_BLOCK

    @pl.when(j == 0)
    def _():
        for kk in range(C_KV_HEADS):
            c0 = kk * HEAD_DIM
            kctx_ref[:, c0:c0 + HEAD_DIM] = kc_ref[kk].T.astype(BF16)
            vctx_ref[:, c0:c0 + HEAD_DIM] = vc_ref[kk].T.astype(BF16)

    n_q = C_GROUP * C_BLOCK
    qi = lax.broadcasted_iota(jnp.int32, (C_BLOCK, n_q), 1) % C_BLOCK
    jl = lax.broadcasted_iota(jnp.int32, (C_BLOCK, n_q), 0)
    valid_prev = (jl >= qi) & (j > 0)
    valid_next = (jl <= qi) & (j < n_blocks - 1)
    rows_prev = pl.ds(pl.multiple_of(jnp.maximum(j - 1, 0) * C_BLOCK, C_BLOCK), C_BLOCK)
    rows_cur = pl.ds(pl.multiple_of(j * C_BLOCK, C_BLOCK), C_BLOCK)
    rows_next = pl.ds(pl.multiple_of(jnp.minimum(j + 1, n_blocks - 1) * C_BLOCK, C_BLOCK),
                      C_BLOCK)

    for kk in range(C_KV_HEADS):
        heads = [kk * C_GROUP + g for g in range(C_GROUP)]
        qs = jnp.concatenate(
            [q_ref[:, h * HEAD_DIM:(h + 1) * HEAD_DIM] for h in heads], axis=0)
        c0 = kk * HEAD_DIM
        cols = slice(c0, c0 + HEAD_DIM)
        k_ctx = kctx_ref[:, cols]
        v_ctx = vctx_ref[:, cols]
        k_band = jnp.concatenate(
            [k_ref[rows_prev, cols], k_ref[rows_cur, cols], k_ref[rows_next, cols]], axis=0)
        v_band = jnp.concatenate(
            [v_ref[rows_prev, cols], v_ref[rows_cur, cols], v_ref[rows_next, cols]], axis=0)
        s_ctx = _dot_nt(k_ctx, qs)
        s_raw = _dot_nt(k_band, qs)
        s_band = jnp.concatenate(
            [jnp.where(valid_prev, s_raw[:C_BLOCK], NEG_INF),
             s_raw[C_BLOCK:2 * C_BLOCK],
             jnp.where(valid_next, s_raw[2 * C_BLOCK:], NEG_INF)], axis=0)
        o_t = _softmax_pv_t([s_ctx, s_band],
                            _sink_row(sink_ref, sink0 + kk * C_GROUP, C_BLOCK),
                            [v_ctx, v_band])
        for g, h in enumerate(heads):
            o_ref[:, h * HEAD_DIM:(h + 1) * HEAD_DIM] = (
                o_t[:, g * C_BLOCK:(g + 1) * C_BLOCK].T)


def _lat_attn_c(qkv, cache_k, cache_v, sink, j_layer, mix):
    n_blocks = DEC_SEQ // C_BLOCK
    q_row0 = N_CTX_TOK // C_BLOCK
    kv_row0 = N_CTX_TOK // DEC_SEQ
    k_col = C_Q_WIDTH // C_KV_WIDTH
    return pl.pallas_call(
        functools.partial(_lat_attn_c_kernel, sink0=j_layer * C_HEADS),
        out_shape=jax.ShapeDtypeStruct((N_TOK, D_MODEL), F32),
        grid=(DEC_BATCH, n_blocks),
        in_specs=[
            pl.BlockSpec(memory_space=pltpu.SMEM),
            pl.BlockSpec((C_BLOCK, C_Q_WIDTH), lambda b, j: (q_row0 + b * n_blocks + j, 0)),
            pl.BlockSpec((DEC_SEQ, C_KV_WIDTH), lambda b, j: (kv_row0 + b, k_col)),
            pl.BlockSpec((DEC_SEQ, C_KV_WIDTH), lambda b, j: (kv_row0 + b, k_col + 1)),
            pl.BlockSpec((None, None, C_KV_HEADS, HEAD_DIM, PAST_LEN),
                         lambda b, j: (b, j_layer, 0, 0, 0)),
            pl.BlockSpec((None, None, C_KV_HEADS, HEAD_DIM, PAST_LEN),
                         lambda b, j: (b, j_layer, 0, 0, 0)),
            pl.BlockSpec(memory_space=pl.ANY),
        ],
        out_specs=pl.BlockSpec((C_BLOCK, C_Q_WIDTH),
                               lambda b, j: (q_row0 + b * n_blocks + j, 0)),
        input_output_aliases={6: 0},
        scratch_shapes=[pltpu.VMEM((PAST_LEN, C_KV_WIDTH), BF16),
                        pltpu.VMEM((PAST_LEN, C_KV_WIDTH), BF16)],
        compiler_params=_params(2),
        name="lat_attn_c",
    )(sink.reshape(-1), qkv, qkv, qkv, cache_k, cache_v, mix)


def _rope_tables():
    t = np.arange(DEC_SEQ)
    pos = np.stack([t // GRID_W, t % GRID_W], axis=-1).astype(np.float64)
    half = HEAD_DIM // 4
    inv = ROPE_BASE ** (-np.arange(half, dtype=np.float64) / half)
    ang = pos[:, :, None] * inv
    cos = np.cos(ang)
    sin = np.sin(ang)
    cos64 = np.stack([cos, cos], axis=2).reshape(DEC_SEQ, HEAD_DIM)
    sin64 = np.stack([-sin, sin], axis=2).reshape(DEC_SEQ, HEAD_DIM)
    return (jnp.asarray(np.tile(cos64, (1, C_HEADS)), F32),
            jnp.asarray(np.tile(sin64, (1, C_HEADS)), F32))


FFN_CHUNK = 256
N_FFN_CHUNKS = FFN_HIDDEN // FFN_CHUNK
N_WO_PIECES = D_MODEL // FFN_CHUNK


def _post_mixer_kernel(*refs, layer, wo_idx, tm, final):
    x_ref, mix_ref, g_ref, mod_ref = refs[:4]
    n_in = 5 if final else 4
    wo_hbm, wgu_hbm, wd_hbm = refs[n_in:n_in + 3]
    n_out = 2 if final else 1
    out_refs = refs[n_in + 3:n_in + 3 + n_out]
    (wo_s, wg_s, wu_s, wd_s, act_s, stage_col, stage_row, sem_col,
     sem_row) = refs[n_in + 3 + n_out:]
    n_row_pieces = N_WO_PIECES + N_FFN_CHUNKS

    def col_copy(which, c):
        src = wgu_hbm.at[layer, :, pl.ds(which * FFN_HIDDEN + c * FFN_CHUNK, FFN_CHUNK)]
        return pltpu.make_async_copy(src, stage_col.at[which, c % 2], sem_col.at[which, c % 2])

    def row_copy(p):
        if p < N_WO_PIECES:
            src = wo_hbm.at[wo_idx, pl.ds(p * FFN_CHUNK, FFN_CHUNK), :]
        else:
            src = wd_hbm.at[layer, pl.ds((p - N_WO_PIECES) * FFN_CHUNK, FFN_CHUNK), :]
        return pltpu.make_async_copy(src, stage_row.at[p % 2], sem_row.at[p % 2])

    def take_row_piece(p, dst_ref, row0):
        if p + 1 < n_row_pieces:
            row_copy(p + 1).start()
        row_copy(p).wait()
        dst_ref[row0:row0 + FFN_CHUNK, :] = stage_row[p % 2].astype(BF16)

    def tile(load_weights):
        if load_weights:
            row_copy(0).start()
            for which in range(2):
                col_copy(which, 0).start()
            for p in range(N_WO_PIECES):
                take_row_piece(p, wo_s, p * FFN_CHUNK)
        gate1 = mod_ref[:, 2 * D_MODEL:3 * D_MODEL]
        x1 = x_ref[...] + gate1 * _dot(mix_ref[...].astype(BF16), wo_s[...])
        h = _norm_mod(x1, g_ref[...], mod_ref, 3).astype(BF16)
        for c in range(N_FFN_CHUNKS):
            if load_weights:
                for which, dst in ((0, wg_s), (1, wu_s)):
                    if c + 1 < N_FFN_CHUNKS:
                        col_copy(which, c + 1).start()
                    col_copy(which, c).wait()
                    dst[c] = stage_col[which, c % 2].astype(BF16)
            gate = _dot(h, wg_s[c])
            up = _dot(h, wu_s[c])
            act = gate / (1.0 + jnp.exp(-gate)) * up
            act_s[:, c * FFN_CHUNK:(c + 1) * FFN_CHUNK] = act.astype(BF16)
            if load_weights:
                take_row_piece(N_WO_PIECES + c, wd_s, c * FFN_CHUNK)
        gate2 = mod_ref[:, 5 * D_MODEL:6 * D_MODEL]
        return x1 + gate2 * _dot(act_s[...], wd_s[...])

    def emit(x2):
        if not final:
            out_refs[0][...] = x2
            return
        var = jnp.mean(x2 * x2, axis=-1, keepdims=True)
        y = x2 * lax.rsqrt(var + EPS) * refs[4][...]
        is_ctx = pl.program_id(0) < N_CTX_TOK // tm

        @pl.when(is_ctx)
        def _():
            out_refs[0][...] = y

        @pl.when(jnp.logical_not(is_ctx))
        def _():
            out_refs[1][...] = y

    first = pl.program_id(0) == 0

    @pl.when(first)
    def _():
        emit(tile(True))

    @pl.when(jnp.logical_not(first))
    def _():
        emit(tile(False))


def _post_mixer(x, mix, g_all, mod4, layer, w_out_all, wo_idx, w_gate_up, w_down,
                norm_final=None):
    tm = 512
    n_ctx_tiles = N_CTX_TOK // tm
    final = norm_final is not None
    row_spec = pl.BlockSpec((tm, D_MODEL), lambda i: (i, 0))
    hbm = pl.BlockSpec(memory_space=pl.ANY)
    if final:
        extra_in = [norm_final.reshape(1, D_MODEL)]
        extra_specs = [pl.BlockSpec((1, D_MODEL), lambda i: (0, 0))]
        out_shape = (jax.ShapeDtypeStruct((N_CTX_TOK, D_MODEL), F32),
                     jax.ShapeDtypeStruct((N_LAT_TOK, D_MODEL), F32))
        out_specs = (
            pl.BlockSpec((tm, D_MODEL), lambda i: (jnp.minimum(i, n_ctx_tiles - 1), 0)),
            pl.BlockSpec((tm, D_MODEL), lambda i: (jnp.maximum(i - n_ctx_tiles, 0), 0)))
    else:
        extra_in, extra_specs = [], []
        out_shape = jax.ShapeDtypeStruct((N_TOK, D_MODEL), F32)
        out_specs = row_spec
    return pl.pallas_call(
        functools.partial(_post_mixer_kernel, layer=layer, wo_idx=wo_idx, tm=tm, final=final),
        out_shape=out_shape,
        grid=(N_TOK // tm,),
        in_specs=[
            row_spec,
            row_spec,
            pl.BlockSpec((None, 1, D_MODEL), lambda i: (layer, 0, 0)),
            pl.BlockSpec((None, None, 1, 6 * D_MODEL),
                         lambda i: (layer, _group_of_tile(i, tm), 0, 0)),
        ] + extra_specs + [hbm, hbm, hbm],
        out_specs=out_specs,
        scratch_shapes=[
            pltpu.VMEM((D_MODEL, D_MODEL), BF16),
            pltpu.VMEM((N_FFN_CHUNKS, D_MODEL, FFN_CHUNK), BF16),
            pltpu.VMEM((N_FFN_CHUNKS, D_MODEL, FFN_CHUNK), BF16),
            pltpu.VMEM((FFN_HIDDEN, D_MODEL), BF16),
            pltpu.VMEM((tm, FFN_HIDDEN), BF16),
            pltpu.VMEM((2, 2, D_MODEL, FFN_CHUNK), F32),
            pltpu.VMEM((2, FFN_CHUNK, D_MODEL), F32),
            pltpu.SemaphoreType.DMA((2, 2)),
            pltpu.SemaphoreType.DMA((2,)),
        ],
        compiler_params=_params(1),
        name="post_mixer",
    )(x, mix, g_all.reshape(DEPTH, 1, D_MODEL), mod4, *extra_in, w_out_all, w_gate_up, w_down)


def kernel(x_prompt, x_sample, cache_a_k, cache_a_v, cache_c_k, cache_c_v, c, c_ctx, w_mod, b_mod, norm_mix, norm_ffn, w_in_ab, rpb_a, w_pool, pool_scale, w_out_ab, w_in_c, sink_c, w_out_c, w_gate_up, w_down, norm_final):
    xs = (x_prompt.reshape(N_CTX_TOK, D_MODEL), x_sample.reshape(N_LAT_TOK, D_MODEL))
    cond8 = jnp.concatenate(
        [c_ctx[None], c, jnp.zeros((N_GROUPS_PAD - 1 - DEC_BATCH, D_MODEL), F32)], axis=0)
    mod4 = _modulation(cond8, w_mod, b_mod).reshape(DEPTH, N_GROUPS_PAD, 1, 6 * D_MODEL)

    n_ab = cache_a_k.shape[1]
    n_c = cache_c_k.shape[1]
    cache_a_k, cache_a_v, cache_c_k, cache_c_v = (
        jnp.transpose(t, (0, 1, 3, 4, 2)) for t in (cache_a_k, cache_a_v, cache_c_k, cache_c_v))
    rope_tables = _rope_tables()
    bias_tiles = _na_bias_tiles(rpb_a)

    new_a = []
    new_c = []
    for l in range(DEPTH):
        if l % 2 == 0:
            i = l // 2
            qkv, *rest = _inproj(xs, norm_mix, mod4, l, w_in_ab, i, A_HEADS, A_WIDTH, n_ab,
                                 new_a)
            new_a, u = rest[:2], rest[2]
            if len(xs) == 2:
                xs = (rest[3],)
            mix = _ctx_attn_a(qkv)
            mix = _lat_attn_a(qkv, cache_a_k, cache_a_v, bias_tiles, i, mix)
            mix = _pool(u, w_pool, pool_scale, i, mix)
            w_out, wo_idx = w_out_ab, i
        else:
            j = l // 2
            qkv, *new_c = _inproj(xs, norm_mix, mod4, l, w_in_c, j, C_KV_HEADS, C_Q_WIDTH, n_c,
                                  new_c, rope_tables)
            mix = _ctx_attn_c(qkv, sink_c, j)
            mix = _lat_attn_c(qkv, cache_c_k, cache_c_v, sink_c, j, mix)
            w_out, wo_idx = w_out_c, j
        if l + 1 < DEPTH:
            xs = (_post_mixer(xs[0], mix, norm_ffn, mod4, l, w_out, wo_idx, w_gate_up, w_down),)
        else:
            y_ctx, y_lat = _post_mixer(xs[0], mix, norm_ffn, mod4, l, w_out, wo_idx, w_gate_up,
                                       w_down, norm_final)

    return (y_ctx.reshape(BATCH, SEQ, D_MODEL), y_lat.reshape(DEC_BATCH, DEC_SEQ, D_MODEL),
            new_a[0], new_a[1], new_c[0], new_c[1])
```

```python
import functools

import jax
import jax.numpy as jnp
import numpy as np
from jax import lax
from jax.experimental import pallas as pl
from jax.experimental.pallas import tpu as pltpu

D_MODEL = 1024
BATCH = 16
SEQ = 256
DEPTH = 4
DEC_BATCH = 2
DEC_SEQ = 1024
PAST_LEN = 512
GRID_W = 64
HEAD_DIM = 64
A_WIDTH = 512
A_HEADS = 8
B_WIDTH = 512
POOL_WINDOWS = (2, 4, 8, 16)
B_GROUP_DIM = 128
NA_ROWS = 8
NA_COLS = 16
C_HEADS = 16
C_KV_HEADS = 4
C_GROUP = C_HEADS // C_KV_HEADS
C_Q_WIDTH = 1024
C_KV_WIDTH = 256
C_BLOCK = 128
FFN_HIDDEN = 2816
ROPE_BASE = 10000.0
EPS = 1e-6
NEG_INF = -1e30

N_CTX_TOK = BATCH * SEQ
N_LAT_TOK = DEC_BATCH * DEC_SEQ
N_TOK = N_CTX_TOK + N_LAT_TOK
GRID_ROWS = DEC_SEQ // GRID_W
N_GROUPS_PAD = 8

VMEM_LIMIT = 56 * 1024 * 1024

F32 = jnp.float32
BF16 = jnp.bfloat16


def _params(n_axes):
    return pltpu.CompilerParams(dimension_semantics=("arbitrary",) * n_axes,
                                vmem_limit_bytes=VMEM_LIMIT)


def _group_of_tile(i, tm):
    row0 = i * tm
    return jnp.where(row0 < N_CTX_TOK, 0, 1 + (row0 - N_CTX_TOK) // DEC_SEQ)


def _dot_nt(a, b):
    return lax.dot_general(a, b, (((1,), (1,)), ((), ())), preferred_element_type=F32)


def _dot(a, b):
    return jnp.dot(a, b, preferred_element_type=F32)


def _dot_tn(a, b):
    return lax.dot_general(a, b, (((0,), (0,)), ((), ())), preferred_element_type=F32)


def _softmax_pv_t(scores_t, sink_row, values):
    m = sink_row
    for s in scores_t:
        ms = jnp.max(s, axis=0, keepdims=True)
        m = ms if m is None else jnp.maximum(m, ms)
    l = None if sink_row is None else jnp.exp(sink_row - m)
    o = None
    for s, v in zip(scores_t, values):
        p = jnp.exp(s - m)
        ls = jnp.sum(p, axis=0, keepdims=True)
        l = ls if l is None else l + ls
        pv = _dot_tn(v, p.astype(BF16))
        o = pv if o is None else o + pv
    return o * (1.0 / l)


def _mod_bias_kernel(rpb_ref, cond_ref, w_ref, b_ref, o_ref, bias_ref, *, n_col_blocks):
    c = cond_ref[...]
    s = c / (1.0 + jnp.exp(-c))
    o_ref[...] = _dot(s.astype(BF16), w_ref[...].astype(BF16)) + b_ref[...]
    step = pl.program_id(0) * n_col_blocks + pl.program_id(1)
    _write_bias_tiles(rpb_ref, step * (N_DROW * N_DCOL), bias_ref)


def _modulation_and_bias(cond8, w_mod, b_mod, rpb_a):
    n_col_blocks = 4
    tn = 6 * D_MODEL // n_col_blocks
    n_bias_layers = rpb_a.shape[0]
    assert n_bias_layers * A_HEADS == DEPTH * n_col_blocks

    def bias_index(l, j):
        step = l * n_col_blocks + j
        return (step // A_HEADS, step % A_HEADS, 0, 0, 0)

    return pl.pallas_call(
        functools.partial(_mod_bias_kernel, n_col_blocks=n_col_blocks),
        out_shape=(
            jax.ShapeDtypeStruct((DEPTH, N_GROUPS_PAD, 6 * D_MODEL), F32),
            jax.ShapeDtypeStruct(
                (n_bias_layers, A_HEADS, N_BIAS_TILES, GRID_W, 2 * GRID_W), F32)),
        grid=(DEPTH, n_col_blocks),
        in_specs=[
            pl.BlockSpec(memory_space=pltpu.SMEM),
            pl.BlockSpec((N_GROUPS_PAD, D_MODEL), lambda l, j: (0, 0)),
            pl.BlockSpec((None, D_MODEL, tn), lambda l, j: (l, 0, j)),
            pl.BlockSpec((None, 1, tn), lambda l, j: (l, 0, j)),
        ],
        out_specs=(
            pl.BlockSpec((None, N_GROUPS_PAD, tn), lambda l, j: (l, 0, j)),
            pl.BlockSpec((None, None, N_BIAS_TILES, GRID_W, 2 * GRID_W), bias_index)),
        compiler_params=_params(2),
        name="modulation_bias",
    )(rpb_a.reshape(-1), cond8, w_mod, b_mod.reshape(DEPTH, 1, 6 * D_MODEL))


def _norm_mod(x, g, mod_ref, shift_idx):
    var = jnp.mean(x * x, axis=-1, keepdims=True)
    y = x * lax.rsqrt(var + EPS) * g
    shift = mod_ref[:, shift_idx * D_MODEL:(shift_idx + 1) * D_MODEL]
    scale = mod_ref[:, (shift_idx + 1) * D_MODEL:(shift_idx + 2) * D_MODEL]
    return y * (1.0 + scale) + shift


QKV_WIDTH = 3 * A_WIDTH
Q_SCALE = HEAD_DIM ** -0.5


def _rope(x, cos, sin_signed):
    n = x.shape[-1]
    lane = lax.broadcasted_iota(jnp.int32, x.shape, x.ndim - 1)
    first = (lane % 32) < 16
    partner = jnp.where(first, pltpu.roll(x, n - 16, axis=x.ndim - 1),
                        pltpu.roll(x, 16, axis=x.ndim - 1))
    return x * cos + partner * sin_signed


def _inproj_kernel(*refs, tm, n_heads, q_width, n_prev, split_x, rope, has_u):
    n_x = 2 if split_x else 1
    g_ref, mod_ref, w_ref = refs[n_x:n_x + 3]
    n_in = n_x + 3 + (2 if rope else 0) + n_prev
    qkv_ref, ck_ref, cv_ref = refs[n_in:n_in + 3]
    extra_out = refs[n_in + 3:-3]
    wbf_ref, h_ref, res_ref = refs[-3:]
    i = pl.program_id(0)
    is_ctx = i < N_CTX_TOK // tm
    kv_width = (QKV_WIDTH - q_width) // 2
    k_col, v_col = q_width, q_width + kv_width
    n_out = res_ref.shape[1]
    chunk = 4 * HEAD_DIM

    @pl.when(i == 0)
    def _():
        wbf_ref[...] = w_ref[...].astype(BF16)

    def tile(ctx):
        if split_x:
            x = (refs[0] if ctx else refs[1])[...]
            extra_out[-1][...] = x
        else:
            x = refs[0][...]
        h_ref[...] = _norm_mod(x, g_ref[...], mod_ref, 0).astype(BF16)
        for c0 in range(0, n_out, chunk):
            cols = slice(c0, c0 + chunk)
            res_ref[:, cols] = _dot(h_ref[...], wbf_ref[:, cols])
            if c0 >= QKV_WIDTH:
                extra_out[0][:, c0 - QKV_WIDTH:c0 - QKV_WIDTH + chunk] = res_ref[:, cols]
                continue
            r = res_ref[:, cols]
            if rope and not ctx and c0 < v_col:
                cos_ref, sin_ref = refs[n_x + 3:n_x + 5]
                r = _rope(r, cos_ref[:, :chunk], sin_ref[:, :chunk])
            if c0 < q_width:
                r = r * Q_SCALE
            qkv_ref[:, cols] = r.astype(BF16)
            if ctx and c0 >= k_col:
                c_ref, col0 = (ck_ref, k_col) if c0 < v_col else (cv_ref, v_col)
                rows = SEQ * n_heads
                flat = c_ref.reshape(tm // SEQ * rows, HEAD_DIM)
                for req in range(tm // SEQ):
                    for hc in range(chunk // HEAD_DIM):
                        hd = (c0 - col0) // HEAD_DIM + hc
                        flat[pl.ds(req * rows + hd, SEQ, stride=n_heads), :] = res_ref[
                            req * SEQ:(req + 1) * SEQ,
                            c0 + hc * HEAD_DIM:c0 + (hc + 1) * HEAD_DIM]

    pl.when(is_ctx)(lambda: tile(True))
    pl.when(jnp.logical_not(is_ctx))(lambda: tile(False))


def _inproj(xs, g_all, mod4, layer, w_all, w_idx, n_heads, q_width, n_slots, prev_caches,
            rope_tables=None):
    tm = 512
    n_out = w_all.shape[2]
    n_ctx_tiles = N_CTX_TOK // tm
    split_x = len(xs) == 2
    rope = rope_tables is not None
    has_u = n_out > QKV_WIDTH
    cache_shape = jax.ShapeDtypeStruct((BATCH, n_slots, SEQ, n_heads, HEAD_DIM), F32)
    cache_spec = pl.BlockSpec(
        (tm // SEQ, None, SEQ, n_heads, HEAD_DIM),
        lambda i: (jnp.minimum(i, n_ctx_tiles - 1), w_idx, 0, 0, 0))
    row_spec = pl.BlockSpec((tm, D_MODEL), lambda i: (i, 0))
    if split_x:
        x_specs = [
            pl.BlockSpec((tm, D_MODEL), lambda i: (jnp.minimum(i, n_ctx_tiles - 1), 0)),
            pl.BlockSpec((tm, D_MODEL), lambda i: (jnp.maximum(i - n_ctx_tiles, 0), 0)),
        ]
    else:
        x_specs = [row_spec]
    out_shape = [jax.ShapeDtypeStruct((N_TOK, QKV_WIDTH), BF16), cache_shape, cache_shape]
    out_specs = [pl.BlockSpec((tm, QKV_WIDTH), lambda i: (i, 0)), cache_spec, cache_spec]
    if has_u:
        out_shape.append(jax.ShapeDtypeStruct((N_TOK, n_out - QKV_WIDTH), F32))
        out_specs.append(pl.BlockSpec((tm, n_out - QKV_WIDTH), lambda i: (i, 0)))
    if split_x:
        out_shape.append(jax.ShapeDtypeStruct((N_TOK, D_MODEL), F32))
        out_specs.append(row_spec)
    rope_in, rope_specs = [], []
    if rope:
        tiles_per_seq = DEC_SEQ // tm
        rope_spec = pl.BlockSpec(
            (tm, D_MODEL), lambda i: (jnp.maximum(i - n_ctx_tiles, 0) % tiles_per_seq, 0))
        rope_in, rope_specs = list(rope_tables), [rope_spec, rope_spec]
    n_prev = len(prev_caches)
    n_before = len(xs) + 3 + len(rope_in)
    return pl.pallas_call(
        functools.partial(_inproj_kernel, tm=tm, n_heads=n_heads, q_width=q_width,
                          n_prev=n_prev, split_x=split_x, rope=rope, has_u=has_u),
        out_shape=out_shape,
        grid=(N_TOK // tm,),
        in_specs=x_specs + [
            pl.BlockSpec((None, 1, D_MODEL), lambda i: (layer, 0, 0)),
            pl.BlockSpec((None, None, 1, 6 * D_MODEL),
                         lambda i: (layer, _group_of_tile(i, tm), 0, 0)),
            pl.BlockSpec((None, D_MODEL, n_out), lambda i: (w_idx, 0, 0),
                         pipeline_mode=pl.Buffered(1)),
        ] + rope_specs + [pl.BlockSpec(memory_space=pl.ANY)] * n_prev,
        out_specs=out_specs,
        scratch_shapes=[pltpu.VMEM((D_MODEL, n_out), BF16), pltpu.VMEM((tm, D_MODEL), BF16),
                        pltpu.VMEM((tm, n_out), F32)],
        input_output_aliases={n_before + k: 1 + k for k in range(n_prev)},
        compiler_params=_params(1),
        name="inproj",
    )(*xs, g_all.reshape(DEPTH, 1, D_MODEL), mod4, w_all, *rope_in, *prev_caches)


def _ctx_attn_a_kernel(p_ref, o_ref):
    for h in range(A_HEADS):
        c0 = h * HEAD_DIM
        q = p_ref[:, c0:c0 + HEAD_DIM]
        k = p_ref[:, A_WIDTH + c0:A_WIDTH + c0 + HEAD_DIM]
        v = p_ref[:, 2 * A_WIDTH + c0:2 * A_WIDTH + c0 + HEAD_DIM]
        s = _dot_nt(q, k)
        m = jnp.max(s, axis=-1, keepdims=True)
        p = jnp.exp(s - m)
        l = jnp.sum(p, axis=-1, keepdims=True)
        o_ref[:, c0:c0 + HEAD_DIM] = _dot(p.astype(BF16), v) / l


def _ctx_attn_a(qkv):
    return pl.pallas_call(
        _ctx_attn_a_kernel,
        out_shape=jax.ShapeDtypeStruct((N_TOK, D_MODEL), F32),
        grid=(BATCH,),
        in_specs=[pl.BlockSpec((SEQ, QKV_WIDTH), lambda b: (b, 0))],
        out_specs=pl.BlockSpec((SEQ, A_WIDTH), lambda b: (b, 0)),
        compiler_params=_params(1),
        name="ctx_attn_a",
    )(qkv)


N_DROW = 2 * NA_ROWS - 1
N_DCOL = 2 * NA_COLS - 1
N_BIAS_TILES = 16
BIAS_TILE_LEFT_PAD = 14
BIAS_TILE_RIGHT_PAD = 15
MID_DROW = NA_ROWS - 1 - NA_ROWS // 2


def _write_bias_tiles(rpb_ref, base, o_ref):
    qi = lax.broadcasted_iota(jnp.int32, (GRID_W, 2 * GRID_W), 0)
    lane = lax.broadcasted_iota(jnp.int32, (GRID_W, 2 * GRID_W), 1)
    right = lane >= GRID_W
    kc = jnp.where(right, lane - GRID_W, lane)
    rel = kc - qi + (NA_COLS - 1)
    qstart = jnp.clip(qi - NA_COLS // 2, 0, GRID_W - NA_COLS)
    valid = (kc >= qstart) & (kc < qstart + NA_COLS)

    rows = [jnp.zeros((GRID_W, 2 * GRID_W), F32)] * N_DROW
    for d in range(N_DCOL):
        hit = rel == d
        rows = [jnp.where(hit, rpb_ref[base + dr * N_DCOL + d], rows[dr])
                for dr in range(N_DROW)]

    for t in range(N_DROW - 1):
        o_ref[t] = jnp.where(valid, jnp.where(right, rows[t + 1], rows[t]), NEG_INF)
    o_ref[BIAS_TILE_LEFT_PAD] = jnp.where(valid & right, rows[MID_DROW], NEG_INF)
    o_ref[BIAS_TILE_RIGHT_PAD] = jnp.where(valid & jnp.logical_not(right),
                                           rows[MID_DROW + NA_ROWS - 1], NEG_INF)


def _na_window(r):
    start = min(max(r - NA_ROWS // 2, 0), GRID_ROWS - NA_ROWS)
    first_drow = start - r + NA_ROWS - 1
    if start % 2 == 0:
        return start, [first_drow + 2 * p for p in range(NA_ROWS // 2)]
    assert first_drow == MID_DROW
    inner = [first_drow + 1 + 2 * p for p in range(NA_ROWS // 2 - 1)]
    return start - 1, [BIAS_TILE_LEFT_PAD] + inner + [BIAS_TILE_RIGHT_PAD]


def _lat_attn_a_kernel(q_ref, k_ref, v_ref, kc_ref, vc_ref, bias_ref, mix_ref, o_ref,
                       s_loc, s_ctx, p_loc, p_ctx):
    del mix_ref
    hp = pl.program_id(1)

    @pl.when((pl.program_id(0) == 0) & (hp == 0))
    def _():
        p_loc[...] = jnp.zeros_like(p_loc)

    for hh in range(2):
        c0 = hh * HEAD_DIM
        q = q_ref[:, c0:c0 + HEAD_DIM]
        k = k_ref[:, c0:c0 + HEAD_DIM]
        v = v_ref[:, c0:c0 + HEAD_DIM]
        kc_t = kc_ref[hh].astype(BF16)
        vc_t = vc_ref[hh].astype(BF16)
        s_loc[...] = _dot_nt(q, k)
        s_ctx[...] = _dot(q, kc_t)
        inv_l = []
        for r in range(GRID_ROWS):
            rows = slice(r * GRID_W, (r + 1) * GRID_W)
            first_row, tiles = _na_window(r)
            cols = slice(first_row * GRID_W, (first_row + 2 * len(tiles)) * GRID_W)
            bias = jnp.concatenate([bias_ref[hh, t] for t in tiles], axis=1)
            sc = s_ctx[rows, :]
            sl = s_loc[rows, cols] + bias
            m = jnp.maximum(jnp.max(sc, axis=-1, keepdims=True),
                            jnp.max(sl, axis=-1, keepdims=True))
            pc = jnp.exp(sc - m)
            pw = jnp.exp(sl - m)
            inv_l.append(1.0 / (jnp.sum(pc, axis=-1, keepdims=True)
                                + jnp.sum(pw, axis=-1, keepdims=True)))
            p_ctx[rows, :] = pc.astype(BF16)
            p_loc[rows, cols] = pw.astype(BF16)
        o = _dot_nt(p_ctx[...], vc_t) + _dot(p_loc[...], v)
        o_ref[:, c0:c0 + HEAD_DIM] = o * jnp.concatenate(inv_l, axis=0)


def _lat_attn_a(qkv, cache_k, cache_v, bias_tiles, i_layer, mix):
    row0 = N_CTX_TOK // DEC_SEQ
    pair = 2 * HEAD_DIM
    k_col0 = A_WIDTH // pair
    v_col0 = 2 * A_WIDTH // pair
    return pl.pallas_call(
        _lat_attn_a_kernel,
        out_shape=jax.ShapeDtypeStruct((N_TOK, D_MODEL), F32),
        grid=(DEC_BATCH, A_HEADS // 2),
        in_specs=[
            pl.BlockSpec((DEC_SEQ, pair), lambda b, hp: (row0 + b, hp)),
            pl.BlockSpec((DEC_SEQ, pair), lambda b, hp: (row0 + b, k_col0 + hp)),
            pl.BlockSpec((DEC_SEQ, pair), lambda b, hp: (row0 + b, v_col0 + hp)),
            pl.BlockSpec((None, None, 2, HEAD_DIM, PAST_LEN),
                         lambda b, hp: (b, i_layer, hp, 0, 0)),
            pl.BlockSpec((None, None, 2, HEAD_DIM, PAST_LEN),
                         lambda b, hp: (b, i_layer, hp, 0, 0)),
            pl.BlockSpec((None, 2, N_BIAS_TILES, GRID_W, 2 * GRID_W),
                         lambda b, hp: (i_layer, hp, 0, 0, 0)),
            pl.BlockSpec(memory_space=pl.ANY),
        ],
        out_specs=pl.BlockSpec((DEC_SEQ, pair), lambda b, hp: (row0 + b, hp)),
        scratch_shapes=[pltpu.VMEM((DEC_SEQ, DEC_SEQ), F32),
                        pltpu.VMEM((DEC_SEQ, PAST_LEN), F32),
                        pltpu.VMEM((DEC_SEQ, DEC_SEQ), BF16),
                        pltpu.VMEM((DEC_SEQ, PAST_LEN), BF16)],
        input_output_aliases={6: 0},
        compiler_params=_params(2),
        name="lat_attn_a",
    )(qkv, qkv, qkv, cache_k, cache_v, bias_tiles, mix)


def _pool_kernel(u_ref, wp_ref, ps_ref, mix_ref, o_ref, *, rows):
    del mix_ref
    n = jnp.where(pl.program_id(0) < N_CTX_TOK // rows, SEQ, DEC_SEQ)
    t = lax.broadcasted_iota(jnp.int32, (rows, B_GROUP_DIM), 0) & (n - 1)

    def earlier(x, k):
        return jnp.where(t >= k, pltpu.roll(x, k, axis=0), 0.0)

    def later(x, k):
        return jnp.where(t < n - k, pltpu.roll(x, rows - k, axis=0), 0.0)

    for g, w in enumerate(POOL_WINDOWS):
        c0 = g * B_GROUP_DIM
        half = w // 2
        u = u_ref[:, c0:c0 + B_GROUP_DIM]
        before, after = u, u
        k = 1
        while k < half:
            before = before + earlier(before, k)
            after = after + later(after, k)
            k *= 2
        total = earlier(before, 1) + after
        count = (jnp.minimum(t + half, n) - jnp.maximum(t - half, 0)).astype(F32)
        pooled = total / count - u
        y = _dot(pooled.astype(BF16), wp_ref[g].astype(BF16))
        o_ref[:, c0:c0 + B_GROUP_DIM] = y * ps_ref[:, c0:c0 + B_GROUP_DIM]


def _pool(u, w_pool, pool_scale, i_layer, mix):
    n_groups = len(POOL_WINDOWS)
    rows = DEC_SEQ
    return pl.pallas_call(
        functools.partial(_pool_kernel, rows=rows),
        out_shape=jax.ShapeDtypeStruct((N_TOK, D_MODEL), F32),
        grid=(N_TOK // rows,),
        in_specs=[
            pl.BlockSpec((rows, B_WIDTH), lambda b: (b, 0)),
            pl.BlockSpec((None, n_groups, B_GROUP_DIM, B_GROUP_DIM),
                         lambda b: (i_layer, 0, 0, 0)),
            pl.BlockSpec((None, 1, B_WIDTH), lambda b: (i_layer, 0, 0)),
            pl.BlockSpec(memory_space=pl.ANY),
        ],
        out_specs=pl.BlockSpec((rows, B_WIDTH), lambda b: (b, 1)),
        input_output_aliases={3: 0},
        compiler_params=_params(1),
        name="pool_mixer",
    )(u, w_pool, pool_scale.reshape(-1, 1, B_WIDTH), mix)


def _sink_row(sink_ref, head0, queries_per_head):
    lane = lax.broadcasted_iota(jnp.int32, (1, C_GROUP * queries_per_head), 1)
    row = jnp.full((1, C_GROUP * queries_per_head), sink_ref[head0], F32)
    for g in range(1, C_GROUP):
        row = jnp.where(lane >= g * queries_per_head, sink_ref[head0 + g], row)
    return row


def _ctx_attn_c_kernel(sink_ref, p_ref, o_ref, *, sink0):
    for kk in range(C_KV_HEADS):
        heads = [kk * C_GROUP + g for g in range(C_GROUP)]
        q = jnp.concatenate(
            [p_ref[:, h * HEAD_DIM:(h + 1) * HEAD_DIM] for h in heads], axis=0)
        k0 = C_Q_WIDTH + kk * HEAD_DIM
        v0 = C_Q_WIDTH + C_KV_WIDTH + kk * HEAD_DIM
        k = p_ref[:, k0:k0 + HEAD_DIM]
        v = p_ref[:, v0:v0 + HEAD_DIM]
        o_t = _softmax_pv_t([_dot_nt(k, q)], _sink_row(sink_ref, sink0 + kk * C_GROUP, SEQ),
                            [v])
        for g, h in enumerate(heads):
            o_ref[:, h * HEAD_DIM:(h + 1) * HEAD_DIM] = o_t[:, g * SEQ:(g + 1) * SEQ].T


def _ctx_attn_c(qkv, sink_all, j_layer):
    return pl.pallas_call(
        functools.partial(_ctx_attn_c_kernel, sink0=j_layer * C_HEADS),
        out_shape=jax.ShapeDtypeStruct((N_TOK, D_MODEL), F32),
        grid=(BATCH,),
        in_specs=[
            pl.BlockSpec(memory_space=pltpu.SMEM),
            pl.BlockSpec((SEQ, C_Q_WIDTH + 2 * C_KV_WIDTH), lambda b: (b, 0)),
        ],
        out_specs=pl.BlockSpec((SEQ, C_Q_WIDTH), lambda b: (b, 0)),
        compiler_params=_params(1),
        name="ctx_attn_c",
    )(sink_all.reshape(-1), qkv)


def _lat_attn_c_kernel(sink_ref, q_ref, k_ref, v_ref, kc_ref, vc_ref, mix_ref, o_ref,
                       kctx_ref, vctx_ref, *, sink0):
    del mix_ref
    j = pl.program_id(1)
    n_blocks = DEC_SEQ // C_BLOCK

    @pl.when(j == 0)
    def _():
        for kk in range(C_KV_HEADS):
            c0 = kk * HEAD_DIM
            kctx_ref[:, c0:c0 + HEAD_DIM] = kc_ref[kk].T.astype(BF16)
            vctx_ref[:, c0:c0 + HEAD_DIM] = vc_ref[kk].T.astype(BF16)

    n_q = C_GROUP * C_BLOCK
    qi = lax.broadcasted_iota(jnp.int32, (C_BLOCK, n_q), 1) % C_BLOCK
    jl = lax.broadcasted_iota(jnp.int32, (C_BLOCK, n_q), 0)
    valid_prev = (jl >= qi) & (j > 0)
    valid_next = (jl <= qi) & (j < n_blocks - 1)
    rows_prev = pl.ds(pl.multiple_of(jnp.maximum(j - 1, 0) * C_BLOCK, C_BLOCK), C_BLOCK)
    rows_cur = pl.ds(pl.multiple_of(j * C_BLOCK, C_BLOCK), C_BLOCK)
    rows_next = pl.ds(pl.multiple_of(jnp.minimum(j + 1, n_blocks - 1) * C_BLOCK, C_BLOCK),
                      C_BLOCK)

    for kk in range(C_KV_HEADS):
        heads = [kk * C_GROUP + g for g in range(C_GROUP)]
        qs = jnp.concatenate(
            [q_ref[:, h * HEAD_DIM:(h + 1) * HEAD_DIM] for h in heads], axis=0)
        c0 = kk * HEAD_DIM
        cols = slice(c0, c0 + HEAD_DIM)
        k_ctx = kctx_ref[:, cols]
        v_ctx = vctx_ref[:, cols]
        k_band = jnp.concatenate(
            [k_ref[rows_prev, cols], k_ref[rows_cur, cols], k_ref[rows_next, cols]], axis=0)
        v_band = jnp.concatenate(
            [v_ref[rows_prev, cols], v_ref[rows_cur, cols], v_ref[rows_next, cols]], axis=0)
        s_ctx = _dot_nt(k_ctx, qs)
        s_raw = _dot_nt(k_band, qs)
        s_band = jnp.concatenate(
            [jnp.where(valid_prev, s_raw[:C_BLOCK], NEG_INF),
             s_raw[C_BLOCK:2 * C_BLOCK],
             jnp.where(valid_next, s_raw[2 * C_BLOCK:], NEG_INF)], axis=0)
        o_t = _softmax_pv_t([s_ctx, s_band],
                            _sink_row(sink_ref, sink0 + kk * C_GROUP, C_BLOCK),
                            [v_ctx, v_band])
        for g, h in enumerate(heads):
            o_ref[:, h * HEAD_DIM:(h + 1) * HEAD_DIM] = (
                o_t[:, g * C_BLOCK:(g + 1) * C_BLOCK].T)


def _lat_attn_c(qkv, cache_k, cache_v, sink, j_layer, mix):
    n_blocks = DEC_SEQ // C_BLOCK
    q_row0 = N_CTX_TOK // C_BLOCK
    kv_row0 = N_CTX_TOK // DEC_SEQ
    k_col = C_Q_WIDTH // C_KV_WIDTH
    return pl.pallas_call(
        functools.partial(_lat_attn_c_kernel, sink0=j_layer * C_HEADS),
        out_shape=jax.ShapeDtypeStruct((N_TOK, D_MODEL), F32),
        grid=(DEC_BATCH, n_blocks),
        in_specs=[
            pl.BlockSpec(memory_space=pltpu.SMEM),
            pl.BlockSpec((C_BLOCK, C_Q_WIDTH), lambda b, j: (q_row0 + b * n_blocks + j, 0)),
            pl.BlockSpec((DEC_SEQ, C_KV_WIDTH), lambda b, j: (kv_row0 + b, k_col)),
            pl.BlockSpec((DEC_SEQ, C_KV_WIDTH), lambda b, j: (kv_row0 + b, k_col + 1)),
            pl.BlockSpec((None, None, C_KV_HEADS, HEAD_DIM, PAST_LEN),
                         lambda b, j: (b, j_layer, 0, 0, 0)),
            pl.BlockSpec((None, None, C_KV_HEADS, HEAD_DIM, PAST_LEN),
                         lambda b, j: (b, j_layer, 0, 0, 0)),
            pl.BlockSpec(memory_space=pl.ANY),
        ],
        out_specs=pl.BlockSpec((C_BLOCK, C_Q_WIDTH),
                               lambda b, j: (q_row0 + b * n_blocks + j, 0)),
        input_output_aliases={6: 0},
        scratch_shapes=[pltpu.VMEM((PAST_LEN, C_KV_WIDTH), BF16),
                        pltpu.VMEM((PAST_LEN, C_KV_WIDTH), BF16)],
        compiler_params=_params(2),
        name="lat_attn_c",
    )(sink.reshape(-1), qkv, qkv, qkv, cache_k, cache_v, mix)


def _rope_tables():
    t = np.arange(DEC_SEQ)
    pos = np.stack([t // GRID_W, t % GRID_W], axis=-1).astype(np.float64)
    half = HEAD_DIM // 4
    inv = ROPE_BASE ** (-np.arange(half, dtype=np.float64) / half)
    ang = pos[:, :, None] * inv
    cos = np.cos(ang)
    sin = np.sin(ang)
    cos64 = np.stack([cos, cos], axis=2).reshape(DEC_SEQ, HEAD_DIM)
    sin64 = np.stack([-sin, sin], axis=2).reshape(DEC_SEQ, HEAD_DIM)
    return (jnp.asarray(np.tile(cos64, (1, C_HEADS)), F32),
            jnp.asarray(np.tile(sin64, (1, C_HEADS)), F32))


FFN_CHUNK = 256
N_FFN_CHUNKS = FFN_HIDDEN // FFN_CHUNK
N_WO_PIECES = D_MODEL // FFN_CHUNK


def _post_mixer_kernel(*refs, layer, wo_idx, tm, final):
    x_ref, mix_ref, g_ref, mod_ref = refs[:4]
    n_in = 5 if final else 4
    wo_hbm, wgu_hbm, wd_hbm = refs[n_in:n_in + 3]
    n_out = 2 if final else 1
    out_refs = refs[n_in + 3:n_in + 3 + n_out]
    (wo_s, wg_s, wu_s, wd_s, act_s, stage_col, stage_row, sem_col,
     sem_row) = refs[n_in + 3 + n_out:]
    n_row_pieces = N_WO_PIECES + N_FFN_CHUNKS

    def col_copy(which, c):
        src = wgu_hbm.at[layer, :, pl.ds(which * FFN_HIDDEN + c * FFN_CHUNK, FFN_CHUNK)]
        return pltpu.make_async_copy(src, stage_col.at[which, c % 2], sem_col.at[which, c % 2])

    def row_copy(p):
        if p < N_WO_PIECES:
            src = wo_hbm.at[wo_idx, pl.ds(p * FFN_CHUNK, FFN_CHUNK), :]
        else:
            src = wd_hbm.at[layer, pl.ds((p - N_WO_PIECES) * FFN_CHUNK, FFN_CHUNK), :]
        return pltpu.make_async_copy(src, stage_row.at[p % 2], sem_row.at[p % 2])

    def take_row_piece(p, dst_ref, row0):
        if p + 1 < n_row_pieces:
            row_copy(p + 1).start()
        row_copy(p).wait()
        dst_ref[row0:row0 + FFN_CHUNK, :] = stage_row[p % 2].astype(BF16)

    def tile(load_weights):
        if load_weights:
            row_copy(0).start()
            for which in range(2):
                col_copy(which, 0).start()
            for p in range(N_WO_PIECES):
                take_row_piece(p, wo_s, p * FFN_CHUNK)
        gate1 = mod_ref[:, 2 * D_MODEL:3 * D_MODEL]
        x1 = x_ref[...] + gate1 * _dot(mix_ref[...].astype(BF16), wo_s[...])
        h = _norm_mod(x1, g_ref[...], mod_ref, 3).astype(BF16)
        for c in range(N_FFN_CHUNKS):
            if load_weights:
                for which, dst in ((0, wg_s), (1, wu_s)):
                    if c + 1 < N_FFN_CHUNKS:
                        col_copy(which, c + 1).start()
                    col_copy(which, c).wait()
                    dst[c] = stage_col[which, c % 2].astype(BF16)
            gate = _dot(h, wg_s[c])
            up = _dot(h, wu_s[c])
            act = gate / (1.0 + jnp.exp(-gate)) * up
            act_s[:, c * FFN_CHUNK:(c + 1) * FFN_CHUNK] = act.astype(BF16)
            if load_weights:
                take_row_piece(N_WO_PIECES + c, wd_s, c * FFN_CHUNK)
        gate2 = mod_ref[:, 5 * D_MODEL:6 * D_MODEL]
        return x1 + gate2 * _dot(act_s[...], wd_s[...])

    def emit(x2):
        if not final:
            out_refs[0][...] = x2
            return
        var = jnp.mean(x2 * x2, axis=-1, keepdims=True)
        y = x2 * lax.rsqrt(var + EPS) * refs[4][...]
        is_ctx = pl.program_id(0) < N_CTX_TOK // tm

        @pl.when(is_ctx)
        def _():
            out_refs[0][...] = y

        @pl.when(jnp.logical_not(is_ctx))
        def _():
            out_refs[1][...] = y

    first = pl.program_id(0) == 0

    @pl.when(first)
    def _():
        emit(tile(True))

    @pl.when(jnp.logical_not(first))
    def _():
        emit(tile(False))


def _post_mixer(x, mix, g_all, mod4, layer, w_out_all, wo_idx, w_gate_up, w_down,
                norm_final=None):
    tm = 512
    n_ctx_tiles = N_CTX_TOK // tm
    final = norm_final is not None
    row_spec = pl.BlockSpec((tm, D_MODEL), lambda i: (i, 0))
    hbm = pl.BlockSpec(memory_space=pl.ANY)
    if final:
        extra_in = [norm_final.reshape(1, D_MODEL)]
        extra_specs = [pl.BlockSpec((1, D_MODEL), lambda i: (0, 0))]
        out_shape = (jax.ShapeDtypeStruct((N_CTX_TOK, D_MODEL), F32),
                     jax.ShapeDtypeStruct((N_LAT_TOK, D_MODEL), F32))
        out_specs = (
            pl.BlockSpec((tm, D_MODEL), lambda i: (jnp.minimum(i, n_ctx_tiles - 1), 0)),
            pl.BlockSpec((tm, D_MODEL), lambda i: (jnp.maximum(i - n_ctx_tiles, 0), 0)))
    else:
        extra_in, extra_specs = [], []
        out_shape = jax.ShapeDtypeStruct((N_TOK, D_MODEL), F32)
        out_specs = row_spec
    return pl.pallas_call(
        functools.partial(_post_mixer_kernel, layer=layer, wo_idx=wo_idx, tm=tm, final=final),
        out_shape=out_shape,
        grid=(N_TOK // tm,),
        in_specs=[
            row_spec,
            row_spec,
            pl.BlockSpec((None, 1, D_MODEL), lambda i: (layer, 0, 0)),
            pl.BlockSpec((None, None, 1, 6 * D_MODEL),
                         lambda i: (layer, _group_of_tile(i, tm), 0, 0)),
        ] + extra_specs + [hbm, hbm, hbm],
        out_specs=out_specs,
        scratch_shapes=[
            pltpu.VMEM((D_MODEL, D_MODEL), BF16),
            pltpu.VMEM((N_FFN_CHUNKS, D_MODEL, FFN_CHUNK), BF16),
            pltpu.VMEM((N_FFN_CHUNKS, D_MODEL, FFN_CHUNK), BF16),
            pltpu.VMEM((FFN_HIDDEN, D_MODEL), BF16),
            pltpu.VMEM((tm, FFN_HIDDEN), BF16),
            pltpu.VMEM((2, 2, D_MODEL, FFN_CHUNK), F32),
            pltpu.VMEM((2, FFN_CHUNK, D_MODEL), F32),
            pltpu.SemaphoreType.DMA((2, 2)),
            pltpu.SemaphoreType.DMA((2,)),
        ],
        compiler_params=_params(1),
        name="post_mixer",
    )(x, mix, g_all.reshape(DEPTH, 1, D_MODEL), mod4, *extra_in, w_out_all, w_gate_up, w_down)


def kernel(x_prompt, x_sample, cache_a_k, cache_a_v, cache_c_k, cache_c_v, c, c_ctx, w_mod, b_mod, norm_mix, norm_ffn, w_in_ab, rpb_a, w_pool, pool_scale, w_out_ab, w_in_c, sink_c, w_out_c, w_gate_up, w_down, norm_final):
    xs = (x_prompt.reshape(N_CTX_TOK, D_MODEL), x_sample.reshape(N_LAT_TOK, D_MODEL))
    cond8 = jnp.concatenate(
        [c_ctx[None], c, jnp.zeros((N_GROUPS_PAD - 1 - DEC_BATCH, D_MODEL), F32)], axis=0)
    mod, bias_tiles = _modulation_and_bias(cond8, w_mod, b_mod, rpb_a)
    mod4 = mod.reshape(DEPTH, N_GROUPS_PAD, 1, 6 * D_MODEL)

    n_ab = cache_a_k.shape[1]
    n_c = cache_c_k.shape[1]
    cache_a_k, cache_a_v, cache_c_k, cache_c_v = (
        jnp.transpose(t, (0, 1, 3, 4, 2)) for t in (cache_a_k, cache_a_v, cache_c_k, cache_c_v))
    rope_tables = _rope_tables()

    new_a = []
    new_c = []
    for l in range(DEPTH):
        if l % 2 == 0:
            i = l // 2
            qkv, *rest = _inproj(xs, norm_mix, mod4, l, w_in_ab, i, A_HEADS, A_WIDTH, n_ab,
                                 new_a)
            new_a, u = rest[:2], rest[2]
            if len(xs) == 2:
                xs = (rest[3],)
            mix = _ctx_attn_a(qkv)
            mix = _lat_attn_a(qkv, cache_a_k, cache_a_v, bias_tiles, i, mix)
            mix = _pool(u, w_pool, pool_scale, i, mix)
            w_out, wo_idx = w_out_ab, i
        else:
            j = l // 2
            qkv, *new_c = _inproj(xs, norm_mix, mod4, l, w_in_c, j, C_KV_HEADS, C_Q_WIDTH, n_c,
                                  new_c, rope_tables)
            mix = _ctx_attn_c(qkv, sink_c, j)
            mix = _lat_attn_c(qkv, cache_c_k, cache_c_v, sink_c, j, mix)
            w_out, wo_idx = w_out_c, j
        if l + 1 < DEPTH:
            xs = (_post_mixer(xs[0], mix, norm_ffn, mod4, l, w_out, wo_idx, w_gate_up, w_down),)
        else:
            y_ctx, y_lat = _post_mixer(xs[0], mix, norm_ffn, mod4, l, w_out, wo_idx, w_gate_up,
                                       w_down, norm_final)

    return (y_ctx.reshape(BATCH, SEQ, D_MODEL), y_lat.reshape(DEC_BATCH, DEC_SEQ, D_MODEL),
            new_a[0], new_a[1], new_c[0], new_c[1])
```

```python
import functools

import jax
import jax.numpy as jnp
import numpy as np
from jax import lax
from jax.experimental import pallas as pl
from jax.experimental.pallas import tpu as pltpu

D_MODEL = 1024
BATCH = 16
SEQ = 256
DEPTH = 4
DEC_BATCH = 2
DEC_SEQ = 1024
PAST_LEN = 512
GRID_W = 64
HEAD_DIM = 64
A_WIDTH = 512
A_HEADS = 8
B_WIDTH = 512
POOL_WINDOWS = (2, 4, 8, 16)
B_GROUP_DIM = 128
NA_ROWS = 8
NA_COLS = 16
C_HEADS = 16
C_KV_HEADS = 4
C_GROUP = C_HEADS // C_KV_HEADS
C_Q_WIDTH = 1024
C_KV_WIDTH = 256
C_BLOCK = 128
FFN_HIDDEN = 2816
ROPE_BASE = 10000.0
EPS = 1e-6
NEG_INF = -1e30

N_CTX_TOK = BATCH * SEQ
N_LAT_TOK = DEC_BATCH * DEC_SEQ
N_TOK = N_CTX_TOK + N_LAT_TOK
GRID_ROWS = DEC_SEQ // GRID_W
N_GROUPS_PAD = 8

VMEM_LIMIT = 56 * 1024 * 1024

F32 = jnp.float32
BF16 = jnp.bfloat16


def _params(n_axes):
    return pltpu.CompilerParams(dimension_semantics=("arbitrary",) * n_axes,
                                vmem_limit_bytes=VMEM_LIMIT)


def _group_of_tile(i, tm):
    row0 = i * tm
    return jnp.where(row0 < N_CTX_TOK, 0, 1 + (row0 - N_CTX_TOK) // DEC_SEQ)


def _dot_nt(a, b):
    return lax.dot_general(a, b, (((1,), (1,)), ((), ())), preferred_element_type=F32)


def _dot(a, b):
    return jnp.dot(a, b, preferred_element_type=F32)


def _dot_tn(a, b):
    return lax.dot_general(a, b, (((0,), (0,)), ((), ())), preferred_element_type=F32)


def _softmax_pv_t(scores_t, sink_row, values):
    m = sink_row
    for s in scores_t:
        ms = jnp.max(s, axis=0, keepdims=True)
        m = ms if m is None else jnp.maximum(m, ms)
    l = None if sink_row is None else jnp.exp(sink_row - m)
    o = None
    for s, v in zip(scores_t, values):
        p = jnp.exp(s - m)
        ls = jnp.sum(p, axis=0, keepdims=True)
        l = ls if l is None else l + ls
        pv = _dot_tn(v, p.astype(BF16))
        o = pv if o is None else o + pv
    return o * (1.0 / l)


def _mod_bias_kernel(rpb_ref, cond_ref, w_ref, b_ref, o_ref, bias_ref, *, n_col_blocks):
    c = cond_ref[...]
    s = c / (1.0 + jnp.exp(-c))
    o_ref[...] = _dot(s.astype(BF16), w_ref[...].astype(BF16)) + b_ref[...]
    step = pl.program_id(0) * n_col_blocks + pl.program_id(1)
    _write_bias_tiles(rpb_ref, step * (N_DROW * N_DCOL), bias_ref)


def _modulation_and_bias(cond8, w_mod, b_mod, rpb_a):
    n_col_blocks = 4
    tn = 6 * D_MODEL // n_col_blocks
    n_bias_layers = rpb_a.shape[0]
    assert n_bias_layers * A_HEADS == DEPTH * n_col_blocks

    def bias_index(l, j):
        step = l * n_col_blocks + j
        return (step // A_HEADS, step % A_HEADS, 0, 0, 0)

    return pl.pallas_call(
        functools.partial(_mod_bias_kernel, n_col_blocks=n_col_blocks),
        out_shape=(
            jax.ShapeDtypeStruct((DEPTH, N_GROUPS_PAD, 6 * D_MODEL), F32),
            jax.ShapeDtypeStruct(
                (n_bias_layers, A_HEADS, N_BIAS_TILES, GRID_W, 2 * GRID_W), F32)),
        grid=(DEPTH, n_col_blocks),
        in_specs=[
            pl.BlockSpec(memory_space=pltpu.SMEM),
            pl.BlockSpec((N_GROUPS_PAD, D_MODEL), lambda l, j: (0, 0)),
            pl.BlockSpec((None, D_MODEL, tn), lambda l, j: (l, 0, j)),
            pl.BlockSpec((None, 1, tn), lambda l, j: (l, 0, j)),
        ],
        out_specs=(
            pl.BlockSpec((None, N_GROUPS_PAD, tn), lambda l, j: (l, 0, j)),
            pl.BlockSpec((None, None, N_BIAS_TILES, GRID_W, 2 * GRID_W), bias_index)),
        compiler_params=_params(2),
        name="modulation_bias",
    )(rpb_a.reshape(-1), cond8, w_mod, b_mod.reshape(DEPTH, 1, 6 * D_MODEL))


def _norm_mod(x, g, mod_ref, shift_idx):
    var = jnp.mean(x * x, axis=-1, keepdims=True)
    y = x * lax.rsqrt(var + EPS) * g
    shift = mod_ref[:, shift_idx * D_MODEL:(shift_idx + 1) * D_MODEL]
    scale = mod_ref[:, (shift_idx + 1) * D_MODEL:(shift_idx + 2) * D_MODEL]
    return y * (1.0 + scale) + shift


QKV_WIDTH = 3 * A_WIDTH
Q_SCALE = HEAD_DIM ** -0.5


def _rope(x, cos, sin_signed):
    n = x.shape[-1]
    lane = lax.broadcasted_iota(jnp.int32, x.shape, x.ndim - 1)
    first = (lane % 32) < 16
    partner = jnp.where(first, pltpu.roll(x, n - 16, axis=x.ndim - 1),
                        pltpu.roll(x, 16, axis=x.ndim - 1))
    return x * cos + partner * sin_signed


def _inproj_kernel(*refs, tm, n_heads, q_width, n_prev, split_x, rope, has_u):
    n_x = 2 if split_x else 1
    g_ref, mod_ref, w_ref = refs[n_x:n_x + 3]
    n_in = n_x + 3 + (2 if rope else 0) + n_prev
    qkv_ref, ck_ref, cv_ref = refs[n_in:n_in + 3]
    extra_out = refs[n_in + 3:-3]
    wbf_ref, h_ref, res_ref = refs[-3:]
    i = pl.program_id(0)
    is_ctx = i < N_CTX_TOK // tm
    kv_width = (QKV_WIDTH - q_width) // 2
    k_col, v_col = q_width, q_width + kv_width
    n_out = res_ref.shape[1]
    chunk = 4 * HEAD_DIM

    @pl.when(i == 0)
    def _():
        wbf_ref[...] = w_ref[...].astype(BF16)

    def tile(ctx):
        if split_x:
            x = (refs[0] if ctx else refs[1])[...]
            extra_out[-1][...] = x
        else:
            x = refs[0][...]
        h_ref[...] = _norm_mod(x, g_ref[...], mod_ref, 0).astype(BF16)
        for c0 in range(0, n_out, chunk):
            cols = slice(c0, c0 + chunk)
            res_ref[:, cols] = _dot(h_ref[...], wbf_ref[:, cols])
            if c0 >= QKV_WIDTH:
                extra_out[0][:, c0 - QKV_WIDTH:c0 - QKV_WIDTH + chunk] = res_ref[:, cols]
                continue
            r = res_ref[:, cols]
            if rope and not ctx and c0 < v_col:
                cos_ref, sin_ref = refs[n_x + 3:n_x + 5]
                r = _rope(r, cos_ref[:, :chunk], sin_ref[:, :chunk])
            if c0 < q_width:
                r = r * Q_SCALE
            qkv_ref[:, cols] = r.astype(BF16)
            if ctx and c0 >= k_col:
                c_ref, col0 = (ck_ref, k_col) if c0 < v_col else (cv_ref, v_col)
                for req in range(tm // SEQ):
                    for hc in range(chunk // HEAD_DIM):
                        hd = (c0 - col0) // HEAD_DIM + hc
                        c_ref[req, hd] = res_ref[
                            req * SEQ:(req + 1) * SEQ,
                            c0 + hc * HEAD_DIM:c0 + (hc + 1) * HEAD_DIM].T

    pl.when(is_ctx)(lambda: tile(True))
    pl.when(jnp.logical_not(is_ctx))(lambda: tile(False))


def _inproj(xs, g_all, mod4, layer, w_all, w_idx, n_heads, q_width, n_slots, prev_caches,
            rope_tables=None):
    tm = 512
    n_out = w_all.shape[2]
    n_ctx_tiles = N_CTX_TOK // tm
    split_x = len(xs) == 2
    rope = rope_tables is not None
    has_u = n_out > QKV_WIDTH
    cache_shape = jax.ShapeDtypeStruct((BATCH, n_slots, n_heads, HEAD_DIM, SEQ), F32)
    cache_spec = pl.BlockSpec(
        (tm // SEQ, None, n_heads, HEAD_DIM, SEQ),
        lambda i: (jnp.minimum(i, n_ctx_tiles - 1), w_idx, 0, 0, 0))
    row_spec = pl.BlockSpec((tm, D_MODEL), lambda i: (i, 0))
    if split_x:
        x_specs = [
            pl.BlockSpec((tm, D_MODEL), lambda i: (jnp.minimum(i, n_ctx_tiles - 1), 0)),
            pl.BlockSpec((tm, D_MODEL), lambda i: (jnp.maximum(i - n_ctx_tiles, 0), 0)),
        ]
    else:
        x_specs = [row_spec]
    out_shape = [jax.ShapeDtypeStruct((N_TOK, QKV_WIDTH), BF16), cache_shape, cache_shape]
    out_specs = [pl.BlockSpec((tm, QKV_WIDTH), lambda i: (i, 0)), cache_spec, cache_spec]
    if has_u:
        out_shape.append(jax.ShapeDtypeStruct((N_TOK, n_out - QKV_WIDTH), F32))
        out_specs.append(pl.BlockSpec((tm, n_out - QKV_WIDTH), lambda i: (i, 0)))
    if split_x:
        out_shape.append(jax.ShapeDtypeStruct((N_TOK, D_MODEL), F32))
        out_specs.append(row_spec)
    rope_in, rope_specs = [], []
    if rope:
        tiles_per_seq = DEC_SEQ // tm
        rope_spec = pl.BlockSpec(
            (tm, D_MODEL), lambda i: (jnp.maximum(i - n_ctx_tiles, 0) % tiles_per_seq, 0))
        rope_in, rope_specs = list(rope_tables), [rope_spec, rope_spec]
    n_prev = len(prev_caches)
    n_before = len(xs) + 3 + len(rope_in)
    return pl.pallas_call(
        functools.partial(_inproj_kernel, tm=tm, n_heads=n_heads, q_width=q_width,
                          n_prev=n_prev, split_x=split_x, rope=rope, has_u=has_u),
        out_shape=out_shape,
        grid=(N_TOK // tm,),
        in_specs=x_specs + [
            pl.BlockSpec((None, 1, D_MODEL), lambda i: (layer, 0, 0)),
            pl.BlockSpec((None, None, 1, 6 * D_MODEL),
                         lambda i: (layer, _group_of_tile(i, tm), 0, 0)),
            pl.BlockSpec((None, D_MODEL, n_out), lambda i: (w_idx, 0, 0),
                         pipeline_mode=pl.Buffered(1)),
        ] + rope_specs + [pl.BlockSpec(memory_space=pl.ANY)] * n_prev,
        out_specs=out_specs,
        scratch_shapes=[pltpu.VMEM((D_MODEL, n_out), BF16), pltpu.VMEM((tm, D_MODEL), BF16),
                        pltpu.VMEM((tm, n_out), F32)],
        input_output_aliases={n_before + k: 1 + k for k in range(n_prev)},
        compiler_params=_params(1),
        name="inproj",
    )(*xs, g_all.reshape(DEPTH, 1, D_MODEL), mod4, w_all, *rope_in, *prev_caches)


def _ctx_attn_a_kernel(p_ref, o_ref):
    for h in range(A_HEADS):
        c0 = h * HEAD_DIM
        q = p_ref[:, c0:c0 + HEAD_DIM]
        k = p_ref[:, A_WIDTH + c0:A_WIDTH + c0 + HEAD_DIM]
        v = p_ref[:, 2 * A_WIDTH + c0:2 * A_WIDTH + c0 + HEAD_DIM]
        s = _dot_nt(q, k)
        m = jnp.max(s, axis=-1, keepdims=True)
        p = jnp.exp(s - m)
        l = jnp.sum(p, axis=-1, keepdims=True)
        o_ref[:, c0:c0 + HEAD_DIM] = _dot(p.astype(BF16), v) / l


def _ctx_attn_a(qkv):
    return pl.pallas_call(
        _ctx_attn_a_kernel,
        out_shape=jax.ShapeDtypeStruct((N_TOK, D_MODEL), F32),
        grid=(BATCH,),
        in_specs=[pl.BlockSpec((SEQ, QKV_WIDTH), lambda b: (b, 0))],
        out_specs=pl.BlockSpec((SEQ, A_WIDTH), lambda b: (b, 0)),
        compiler_params=_params(1),
        name="ctx_attn_a",
    )(qkv)


N_DROW = 2 * NA_ROWS - 1
N_DCOL = 2 * NA_COLS - 1
N_BIAS_TILES = 16
BIAS_TILE_LEFT_PAD = 14
BIAS_TILE_RIGHT_PAD = 15
MID_DROW = NA_ROWS - 1 - NA_ROWS // 2


def _write_bias_tiles(rpb_ref, base, o_ref):
    qi = lax.broadcasted_iota(jnp.int32, (GRID_W, 2 * GRID_W), 0)
    lane = lax.broadcasted_iota(jnp.int32, (GRID_W, 2 * GRID_W), 1)
    right = lane >= GRID_W
    kc = jnp.where(right, lane - GRID_W, lane)
    rel = kc - qi + (NA_COLS - 1)
    qstart = jnp.clip(qi - NA_COLS // 2, 0, GRID_W - NA_COLS)
    valid = (kc >= qstart) & (kc < qstart + NA_COLS)

    rows = [jnp.zeros((GRID_W, 2 * GRID_W), F32)] * N_DROW
    for d in range(N_DCOL):
        hit = rel == d
        rows = [jnp.where(hit, rpb_ref[base + dr * N_DCOL + d], rows[dr])
                for dr in range(N_DROW)]

    for t in range(N_DROW - 1):
        o_ref[t] = jnp.where(valid, jnp.where(right, rows[t + 1], rows[t]), NEG_INF)
    o_ref[BIAS_TILE_LEFT_PAD] = jnp.where(valid & right, rows[MID_DROW], NEG_INF)
    o_ref[BIAS_TILE_RIGHT_PAD] = jnp.where(valid & jnp.logical_not(right),
                                           rows[MID_DROW + NA_ROWS - 1], NEG_INF)


def _na_window(r):
    start = min(max(r - NA_ROWS // 2, 0), GRID_ROWS - NA_ROWS)
    first_drow = start - r + NA_ROWS - 1
    if start % 2 == 0:
        return start, [first_drow + 2 * p for p in range(NA_ROWS // 2)]
    assert first_drow == MID_DROW
    inner = [first_drow + 1 + 2 * p for p in range(NA_ROWS // 2 - 1)]
    return start - 1, [BIAS_TILE_LEFT_PAD] + inner + [BIAS_TILE_RIGHT_PAD]


def _lat_attn_a_kernel(q_ref, k_ref, v_ref, kc_ref, vc_ref, bias_ref, mix_ref, o_ref,
                       s_loc, s_ctx, p_loc, p_ctx):
    del mix_ref
    hp = pl.program_id(1)

    @pl.when((pl.program_id(0) == 0) & (hp == 0))
    def _():
        p_loc[...] = jnp.zeros_like(p_loc)

    for hh in range(2):
        c0 = hh * HEAD_DIM
        q = q_ref[:, c0:c0 + HEAD_DIM]
        k = k_ref[:, c0:c0 + HEAD_DIM]
        v = v_ref[:, c0:c0 + HEAD_DIM]
        kc_t = kc_ref[hh].astype(BF16)
        vc_t = vc_ref[hh].astype(BF16)
        s_loc[...] = _dot_nt(q, k)
        s_ctx[...] = _dot(q, kc_t)
        inv_l = []
        for r in range(GRID_ROWS):
            rows = slice(r * GRID_W, (r + 1) * GRID_W)
            first_row, tiles = _na_window(r)
            cols = slice(first_row * GRID_W, (first_row + 2 * len(tiles)) * GRID_W)
            bias = jnp.concatenate([bias_ref[hh, t] for t in tiles], axis=1)
            sc = s_ctx[rows, :]
            sl = s_loc[rows, cols] + bias
            m = jnp.maximum(jnp.max(sc, axis=-1, keepdims=True),
                            jnp.max(sl, axis=-1, keepdims=True))
            pc = jnp.exp(sc - m)
            pw = jnp.exp(sl - m)
            inv_l.append(1.0 / (jnp.sum(pc, axis=-1, keepdims=True)
                                + jnp.sum(pw, axis=-1, keepdims=True)))
            p_ctx[rows, :] = pc.astype(BF16)
            p_loc[rows, cols] = pw.astype(BF16)
        o = _dot_nt(p_ctx[...], vc_t) + _dot(p_loc[...], v)
        o_ref[:, c0:c0 + HEAD_DIM] = o * jnp.concatenate(inv_l, axis=0)


def _lat_attn_a(qkv, cache_k, cache_v, bias_tiles, i_layer, mix):
    row0 = N_CTX_TOK // DEC_SEQ
    pair = 2 * HEAD_DIM
    k_col0 = A_WIDTH // pair
    v_col0 = 2 * A_WIDTH // pair
    return pl.pallas_call(
        _lat_attn_a_kernel,
        out_shape=jax.ShapeDtypeStruct((N_TOK, D_MODEL), F32),
        grid=(DEC_BATCH, A_HEADS // 2),
        in_specs=[
            pl.BlockSpec((DEC_SEQ, pair), lambda b, hp: (row0 + b, hp)),
            pl.BlockSpec((DEC_SEQ, pair), lambda b, hp: (row0 + b, k_col0 + hp)),
            pl.BlockSpec((DEC_SEQ, pair), lambda b, hp: (row0 + b, v_col0 + hp)),
            pl.BlockSpec((None, None, 2, HEAD_DIM, PAST_LEN),
                         lambda b, hp: (b, i_layer, hp, 0, 0)),
            pl.BlockSpec((None, None, 2, HEAD_DIM, PAST_LEN),
                         lambda b, hp: (b, i_layer, hp, 0, 0)),
            pl.BlockSpec((None, 2, N_BIAS_TILES, GRID_W, 2 * GRID_W),
                         lambda b, hp: (i_layer, hp, 0, 0, 0)),
            pl.BlockSpec(memory_space=pl.ANY),
        ],
        out_specs=pl.BlockSpec((DEC_SEQ, pair), lambda b, hp: (row0 + b, hp)),
        scratch_shapes=[pltpu.VMEM((DEC_SEQ, DEC_SEQ), F32),
                        pltpu.VMEM((DEC_SEQ, PAST_LEN), F32),
                        pltpu.VMEM((DEC_SEQ, DEC_SEQ), BF16),
                        pltpu.VMEM((DEC_SEQ, PAST_LEN), BF16)],
        input_output_aliases={6: 0},
        compiler_params=_params(2),
        name="lat_attn_a",
    )(qkv, qkv, qkv, cache_k, cache_v, bias_tiles, mix)


def _pool_kernel(u_ref, wp_ref, ps_ref, mix_ref, o_ref, *, rows):
    del mix_ref
    n = jnp.where(pl.program_id(0) < N_CTX_TOK // rows, SEQ, DEC_SEQ)
    t = lax.broadcasted_iota(jnp.int32, (rows, B_GROUP_DIM), 0) & (n - 1)

    def earlier(x, k):
        return jnp.where(t >= k, pltpu.roll(x, k, axis=0), 0.0)

    def later(x, k):
        return jnp.where(t < n - k, pltpu.roll(x, rows - k, axis=0), 0.0)

    for g, w in enumerate(POOL_WINDOWS):
        c0 = g * B_GROUP_DIM
        half = w // 2
        u = u_ref[:, c0:c0 + B_GROUP_DIM]
        before, after = u, u
        k = 1
        while k < half:
            before = before + earlier(before, k)
            after = after + later(after, k)
            k *= 2
        total = earlier(before, 1) + after
        count = (jnp.minimum(t + half, n) - jnp.maximum(t - half, 0)).astype(F32)
        pooled = total / count - u
        y = _dot(pooled.astype(BF16), wp_ref[g].astype(BF16))
        o_ref[:, c0:c0 + B_GROUP_DIM] = y * ps_ref[:, c0:c0 + B_GROUP_DIM]


def _pool(u, w_pool, pool_scale, i_layer, mix):
    n_groups = len(POOL_WINDOWS)
    rows = DEC_SEQ
    return pl.pallas_call(
        functools.partial(_pool_kernel, rows=rows),
        out_shape=jax.ShapeDtypeStruct((N_TOK, D_MODEL), F32),
        grid=(N_TOK // rows,),
        in_specs=[
            pl.BlockSpec((rows, B_WIDTH), lambda b: (b, 0)),
            pl.BlockSpec((None, n_groups, B_GROUP_DIM, B_GROUP_DIM),
                         lambda b: (i_layer, 0, 0, 0)),
            pl.BlockSpec((None, 1, B_WIDTH), lambda b: (i_layer, 0, 0)),
            pl.BlockSpec(memory_space=pl.ANY),
        ],
        out_specs=pl.BlockSpec((rows, B_WIDTH), lambda b: (b, 1)),
        input_output_aliases={3: 0},
        compiler_params=_params(1),
        name="pool_mixer",
    )(u, w_pool, pool_scale.reshape(-1, 1, B_WIDTH), mix)


def _sink_row(sink_ref, head0, queries_per_head):
    lane = lax.broadcasted_iota(jnp.int32, (1, C_GROUP * queries_per_head), 1)
    row = jnp.full((1, C_GROUP * queries_per_head), sink_ref[head0], F32)
    for g in range(1, C_GROUP):
        row = jnp.where(lane >= g * queries_per_head, sink_ref[head0 + g], row)
    return row


def _ctx_attn_c_kernel(sink_ref, p_ref, o_ref, *, sink0):
    for kk in range(C_KV_HEADS):
        heads = [kk * C_GROUP + g for g in range(C_GROUP)]
        q = jnp.concatenate(
            [p_ref[:, h * HEAD_DIM:(h + 1) * HEAD_DIM] for h in heads], axis=0)
        k0 = C_Q_WIDTH + kk * HEAD_DIM
        v0 = C_Q_WIDTH + C_KV_WIDTH + kk * HEAD_DIM
        k = p_ref[:, k0:k0 + HEAD_DIM]
        v = p_ref[:, v0:v0 + HEAD_DIM]
        o_t = _softmax_pv_t([_dot_nt(k, q)], _sink_row(sink_ref, sink0 + kk * C_GROUP, SEQ),
                            [v])
        for g, h in enumerate(heads):
            o_ref[:, h * HEAD_DIM:(h + 1) * HEAD_DIM] = o_t[:, g * SEQ:(g + 1) * SEQ].T


def _ctx_attn_c(qkv, sink_all, j_layer):
    return pl.pallas_call(
        functools.partial(_ctx_attn_c_kernel, sink0=j_layer * C_HEADS),
        out_shape=jax.ShapeDtypeStruct((N_TOK, D_MODEL), F32),
        grid=(BATCH,),
        in_specs=[
            pl.BlockSpec(memory_space=pltpu.SMEM),
            pl.BlockSpec((SEQ, C_Q_WIDTH + 2 * C_KV_WIDTH), lambda b: (b, 0)),
        ],
        out_specs=pl.BlockSpec((SEQ, C_Q_WIDTH), lambda b: (b, 0)),
        compiler_params=_params(1),
        name="ctx_attn_c",
    )(sink_all.reshape(-1), qkv)


def _lat_attn_c_kernel(sink_ref, q_ref, k_ref, v_ref, kc_ref, vc_ref, mix_ref, o_ref,
                       kctx_ref, vctx_ref, *, sink0):
    del mix_ref
    j = pl.program_id(1)
    n_blocks = DEC_SEQ // C_BLOCK

    @pl.when(j == 0)
    def _():
        for kk in range(C_KV_HEADS):
            c0 = kk * HEAD_DIM
            kctx_ref[:, c0:c0 + HEAD_DIM] = kc_ref[kk].T.astype(BF16)
            vctx_ref[:, c0:c0 + HEAD_DIM] = vc_ref[kk].T.astype(BF16)

    n_q = C_GROUP * C_BLOCK
    qi = lax.broadcasted_iota(jnp.int32, (C_BLOCK, n_q), 1) % C_BLOCK
    jl = lax.broadcasted_iota(jnp.int32, (C_BLOCK, n_q), 0)
    valid_prev = (jl >= qi) & (j > 0)
    valid_next = (jl <= qi) & (j < n_blocks - 1)
    rows_prev = pl.ds(pl.multiple_of(jnp.maximum(j - 1, 0) * C_BLOCK, C_BLOCK), C_BLOCK)
    rows_cur = pl.ds(pl.multiple_of(j * C_BLOCK, C_BLOCK), C_BLOCK)
    rows_next = pl.ds(pl.multiple_of(jnp.minimum(j + 1, n_blocks - 1) * C_BLOCK, C_BLOCK),
                      C_BLOCK)

    for kk in range(C_KV_HEADS):
        heads = [kk * C_GROUP + g for g in range(C_GROUP)]
        qs = jnp.concatenate(
            [q_ref[:, h * HEAD_DIM:(h + 1) * HEAD_DIM] for h in heads], axis=0)
        c0 = kk * HEAD_DIM
        cols = slice(c0, c0 + HEAD_DIM)
        k_ctx = kctx_ref[:, cols]
        v_ctx = vctx_ref[:, cols]
        k_band = jnp.concatenate(
            [k_ref[rows_prev, cols], k_ref[rows_cur, cols], k_ref[rows_next, cols]], axis=0)
        v_band = jnp.concatenate(
            [v_ref[rows_prev, cols], v_ref[rows_cur, cols], v_ref[rows_next, cols]], axis=0)
        s_ctx = _dot_nt(k_ctx, qs)
        s_raw = _dot_nt(k_band, qs)
        s_band = jnp.concatenate(
            [jnp.where(valid_prev, s_raw[:C_BLOCK], NEG_INF),
             s_raw[C_BLOCK:2 * C_BLOCK],
             jnp.where(valid_next, s_raw[2 * C_BLOCK:], NEG_INF)], axis=0)
        o_t = _softmax_pv_t([s_ctx, s_band],
                            _sink_row(sink_ref, sink0 + kk * C_GROUP, C_BLOCK),
                            [v_ctx, v_band])
        for g, h in enumerate(heads):
            o_ref[:, h * HEAD_DIM:(h + 1) * HEAD_DIM] = (
                o_t[:, g * C_BLOCK:(g + 1) * C_BLOCK].T)


def _lat_attn_c(qkv, cache_k, cache_v, sink, j_layer, mix):
    n_blocks = DEC_SEQ // C_BLOCK
    q_row0 = N_CTX_TOK // C_BLOCK
    kv_row0 = N_CTX_TOK // DEC_SEQ
    k_col = C_Q_WIDTH // C_KV_WIDTH
    return pl.pallas_call(
        functools.partial(_lat_attn_c_kernel, sink0=j_layer * C_HEADS),
        out_shape=jax.ShapeDtypeStruct((N_TOK, D_MODEL), F32),
        grid=(DEC_BATCH, n_blocks),
        in_specs=[
            pl.BlockSpec(memory_space=pltpu.SMEM),
            pl.BlockSpec((C_BLOCK, C_Q_WIDTH), lambda b, j: (q_row0 + b * n_blocks + j, 0)),
            pl.BlockSpec((DEC_SEQ, C_KV_WIDTH), lambda b, j: (kv_row0 + b, k_col)),
            pl.BlockSpec((DEC_SEQ, C_KV_WIDTH), lambda b, j: (kv_row0 + b, k_col + 1)),
            pl.BlockSpec((None, None, C_KV_HEADS, HEAD_DIM, PAST_LEN),
                         lambda b, j: (b, j_layer, 0, 0, 0)),
            pl.BlockSpec((None, None, C_KV_HEADS, HEAD_DIM, PAST_LEN),
                         lambda b, j: (b, j_layer, 0, 0, 0)),
            pl.BlockSpec(memory_space=pl.ANY),
        ],
        out_specs=pl.BlockSpec((C_BLOCK, C_Q_WIDTH),
                               lambda b, j: (q_row0 + b * n_blocks + j, 0)),
        input_output_aliases={6: 0},
        scratch_shapes=[pltpu.VMEM((PAST_LEN, C_KV_WIDTH), BF16),
                        pltpu.VMEM((PAST_LEN, C_KV_WIDTH), BF16)],
        compiler_params=_params(2),
        name="lat_attn_c",
    )(sink.reshape(-1), qkv, qkv, qkv, cache_k, cache_v, mix)


def _rope_tables():
    t = np.arange(DEC_SEQ)
    pos = np.stack([t // GRID_W, t % GRID_W], axis=-1).astype(np.float64)
    half = HEAD_DIM // 4
    inv = ROPE_BASE ** (-np.arange(half, dtype=np.float64) / half)
    ang = pos[:, :, None] * inv
    cos = np.cos(ang)
    sin = np.sin(ang)
    cos64 = np.stack([cos, cos], axis=2).reshape(DEC_SEQ, HEAD_DIM)
    sin64 = np.stack([-sin, sin], axis=2).reshape(DEC_SEQ, HEAD_DIM)
    return (jnp.asarray(np.tile(cos64, (1, C_HEADS)), F32),
            jnp.asarray(np.tile(sin64, (1, C_HEADS)), F32))


FFN_CHUNK = 256
N_FFN_CHUNKS = FFN_HIDDEN // FFN_CHUNK
N_WO_PIECES = D_MODEL // FFN_CHUNK


def _post_mixer_kernel(*refs, layer, wo_idx, tm, final):
    x_ref, mix_ref, g_ref, mod_ref = refs[:4]
    n_in = 5 if final else 4
    wo_hbm, wgu_hbm, wd_hbm = refs[n_in:n_in + 3]
    n_out = 2 if final else 1
    out_refs = refs[n_in + 3:n_in + 3 + n_out]
    (wo_s, wg_s, wu_s, wd_s, act_s, stage_col, stage_row, sem_col,
     sem_row) = refs[n_in + 3 + n_out:]
    n_row_pieces = N_WO_PIECES + N_FFN_CHUNKS

    def col_copy(which, c):
        src = wgu_hbm.at[layer, :, pl.ds(which * FFN_HIDDEN + c * FFN_CHUNK, FFN_CHUNK)]
        return pltpu.make_async_copy(src, stage_col.at[which, c % 2], sem_col.at[which, c % 2])

    def row_copy(p):
        if p < N_WO_PIECES:
            src = wo_hbm.at[wo_idx, pl.ds(p * FFN_CHUNK, FFN_CHUNK), :]
        else:
            src = wd_hbm.at[layer, pl.ds((p - N_WO_PIECES) * FFN_CHUNK, FFN_CHUNK), :]
        return pltpu.make_async_copy(src, stage_row.at[p % 2], sem_row.at[p % 2])

    def take_row_piece(p, dst_ref, row0):
        if p + 1 < n_row_pieces:
            row_copy(p + 1).start()
        row_copy(p).wait()
        dst_ref[row0:row0 + FFN_CHUNK, :] = stage_row[p % 2].astype(BF16)

    def tile(load_weights):
        if load_weights:
            row_copy(0).start()
            for which in range(2):
                col_copy(which, 0).start()
            for p in range(N_WO_PIECES):
                take_row_piece(p, wo_s, p * FFN_CHUNK)
        gate1 = mod_ref[:, 2 * D_MODEL:3 * D_MODEL]
        x1 = x_ref[...] + gate1 * _dot(mix_ref[...].astype(BF16), wo_s[...])
        h = _norm_mod(x1, g_ref[...], mod_ref, 3).astype(BF16)
        for c in range(N_FFN_CHUNKS):
            if load_weights:
                for which, dst in ((0, wg_s), (1, wu_s)):
                    if c + 1 < N_FFN_CHUNKS:
                        col_copy(which, c + 1).start()
                    col_copy(which, c).wait()
                    dst[c] = stage_col[which, c % 2].astype(BF16)
            gate = _dot(h, wg_s[c])
            up = _dot(h, wu_s[c])
            act = gate / (1.0 + jnp.exp(-gate)) * up
            act_s[:, c * FFN_CHUNK:(c + 1) * FFN_CHUNK] = act.astype(BF16)
            if load_weights:
                take_row_piece(N_WO_PIECES + c, wd_s, c * FFN_CHUNK)
        gate2 = mod_ref[:, 5 * D_MODEL:6 * D_MODEL]
        return x1 + gate2 * _dot(act_s[...], wd_s[...])

    def emit(x2):
        if not final:
            out_refs[0][...] = x2
            return
        var = jnp.mean(x2 * x2, axis=-1, keepdims=True)
        y = x2 * lax.rsqrt(var + EPS) * refs[4][...]
        is_ctx = pl.program_id(0) < N_CTX_TOK // tm

        @pl.when(is_ctx)
        def _():
            out_refs[0][...] = y

        @pl.when(jnp.logical_not(is_ctx))
        def _():
            out_refs[1][...] = y

    first = pl.program_id(0) == 0

    @pl.when(first)
    def _():
        emit(tile(True))

    @pl.when(jnp.logical_not(first))
    def _():
        emit(tile(False))


def _post_mixer(x, mix, g_all, mod4, layer, w_out_all, wo_idx, w_gate_up, w_down,
                norm_final=None):
    tm = 512
    n_ctx_tiles = N_CTX_TOK // tm
    final = norm_final is not None
    row_spec = pl.BlockSpec((tm, D_MODEL), lambda i: (i, 0))
    hbm = pl.BlockSpec(memory_space=pl.ANY)
    if final:
        extra_in = [norm_final.reshape(1, D_MODEL)]
        extra_specs = [pl.BlockSpec((1, D_MODEL), lambda i: (0, 0))]
        out_shape = (jax.ShapeDtypeStruct((N_CTX_TOK, D_MODEL), F32),
                     jax.ShapeDtypeStruct((N_LAT_TOK, D_MODEL), F32))
        out_specs = (
            pl.BlockSpec((tm, D_MODEL), lambda i: (jnp.minimum(i, n_ctx_tiles - 1), 0)),
            pl.BlockSpec((tm, D_MODEL), lambda i: (jnp.maximum(i - n_ctx_tiles, 0), 0)))
    else:
        extra_in, extra_specs = [], []
        out_shape = jax.ShapeDtypeStruct((N_TOK, D_MODEL), F32)
        out_specs = row_spec
    return pl.pallas_call(
        functools.partial(_post_mixer_kernel, layer=layer, wo_idx=wo_idx, tm=tm, final=final),
        out_shape=out_shape,
        grid=(N_TOK // tm,),
        in_specs=[
            row_spec,
            row_spec,
            pl.BlockSpec((None, 1, D_MODEL), lambda i: (layer, 0, 0)),
            pl.BlockSpec((None, None, 1, 6 * D_MODEL),
                         lambda i: (layer, _group_of_tile(i, tm), 0, 0)),
        ] + extra_specs + [hbm, hbm, hbm],
        out_specs=out_specs,
        scratch_shapes=[
            pltpu.VMEM((D_MODEL, D_MODEL), BF16),
            pltpu.VMEM((N_FFN_CHUNKS, D_MODEL, FFN_CHUNK), BF16),
            pltpu.VMEM((N_FFN_CHUNKS, D_MODEL, FFN_CHUNK), BF16),
            pltpu.VMEM((FFN_HIDDEN, D_MODEL), BF16),
            pltpu.VMEM((tm, FFN_HIDDEN), BF16),
            pltpu.VMEM((2, 2, D_MODEL, FFN_CHUNK), F32),
            pltpu.VMEM((2, FFN_CHUNK, D_MODEL), F32),
            pltpu.SemaphoreType.DMA((2, 2)),
            pltpu.SemaphoreType.DMA((2,)),
        ],
        compiler_params=_params(1),
        name="post_mixer",
    )(x, mix, g_all.reshape(DEPTH, 1, D_MODEL), mod4, *extra_in, w_out_all, w_gate_up, w_down)


def kernel(x_prompt, x_sample, cache_a_k, cache_a_v, cache_c_k, cache_c_v, c, c_ctx, w_mod, b_mod, norm_mix, norm_ffn, w_in_ab, rpb_a, w_pool, pool_scale, w_out_ab, w_in_c, sink_c, w_out_c, w_gate_up, w_down, norm_final):
    xs = (x_prompt.reshape(N_CTX_TOK, D_MODEL), x_sample.reshape(N_LAT_TOK, D_MODEL))
    cond8 = jnp.concatenate(
        [c_ctx[None], c, jnp.zeros((N_GROUPS_PAD - 1 - DEC_BATCH, D_MODEL), F32)], axis=0)
    mod, bias_tiles = _modulation_and_bias(cond8, w_mod, b_mod, rpb_a)
    mod4 = mod.reshape(DEPTH, N_GROUPS_PAD, 1, 6 * D_MODEL)

    n_ab = cache_a_k.shape[1]
    n_c = cache_c_k.shape[1]
    cache_a_k, cache_a_v, cache_c_k, cache_c_v = (
        jnp.transpose(t, (0, 1, 3, 4, 2)) for t in (cache_a_k, cache_a_v, cache_c_k, cache_c_v))
    rope_tables = _rope_tables()

    new_a = []
    new_c = []
    for l in range(DEPTH):
        if l % 2 == 0:
            i = l // 2
            qkv, *rest = _inproj(xs, norm_mix, mod4, l, w_in_ab, i, A_HEADS, A_WIDTH, n_ab,
                                 new_a)
            new_a, u = rest[:2], rest[2]
            if len(xs) == 2:
                xs = (rest[3],)
            mix = _ctx_attn_a(qkv)
            mix = _lat_attn_a(qkv, cache_a_k, cache_a_v, bias_tiles, i, mix)
            mix = _pool(u, w_pool, pool_scale, i, mix)
            w_out, wo_idx = w_out_ab, i
        else:
            j = l // 2
            qkv, *new_c = _inproj(xs, norm_mix, mod4, l, w_in_c, j, C_KV_HEADS, C_Q_WIDTH, n_c,
                                  new_c, rope_tables)
            mix = _ctx_attn_c(qkv, sink_c, j)
            mix = _lat_attn_c(qkv, cache_c_k, cache_c_v, sink_c, j, mix)
            w_out, wo_idx = w_out_c, j
        if l + 1 < DEPTH:
            xs = (_post_mixer(xs[0], mix, norm_ffn, mod4, l, w_out, wo_idx, w_gate_up, w_down),)
        else:
            y_ctx, y_lat = _post_mixer(xs[0], mix, norm_ffn, mod4, l, w_out, wo_idx, w_gate_up,
                                       w_down, norm_final)

    new_caches = [jnp.transpose(t, (0, 1, 4, 2, 3)) for t in (*new_a, *new_c)]
    return (y_ctx.reshape(BATCH, SEQ, D_MODEL), y_lat.reshape(DEC_BATCH, DEC_SEQ, D_MODEL),
            *new_caches)
```

```python
import functools

import jax
import jax.numpy as jnp
import numpy as np
from jax import lax
from jax.experimental import pallas as pl
from jax.experimental.pallas import tpu as pltpu

D_MODEL = 1024
BATCH = 16
SEQ = 256
DEPTH = 4
DEC_BATCH = 2
DEC_SEQ = 1024
PAST_LEN = 512
GRID_W = 64
HEAD_DIM = 64
A_WIDTH = 512
A_HEADS = 8
B_WIDTH = 512
POOL_WINDOWS = (2, 4, 8, 16)
B_GROUP_DIM = 128
NA_ROWS = 8
NA_COLS = 16
C_HEADS = 16
C_KV_HEADS = 4
C_GROUP = C_HEADS // C_KV_HEADS
C_Q_WIDTH = 1024
C_KV_WIDTH = 256
C_BLOCK = 128
FFN_HIDDEN = 2816
ROPE_BASE = 10000.0
EPS = 1e-6
NEG_INF = -1e30

N_CTX_TOK = BATCH * SEQ
N_LAT_TOK = DEC_BATCH * DEC_SEQ
N_TOK = N_CTX_TOK + N_LAT_TOK
GRID_ROWS = DEC_SEQ // GRID_W
N_GROUPS_PAD = 8

VMEM_LIMIT = 56 * 1024 * 1024

F32 = jnp.float32
BF16 = jnp.bfloat16


def _params(n_axes):
    return pltpu.CompilerParams(dimension_semantics=("arbitrary",) * n_axes,
                                vmem_limit_bytes=VMEM_LIMIT)


def _group_of_tile(i, tm):
    row0 = i * tm
    return jnp.where(row0 < N_CTX_TOK, 0, 1 + (row0 - N_CTX_TOK) // DEC_SEQ)


def _dot_nt(a, b):
    return lax.dot_general(a, b, (((1,), (1,)), ((), ())), preferred_element_type=F32)


def _dot(a, b):
    return jnp.dot(a, b, preferred_element_type=F32)


def _dot_tn(a, b):
    return lax.dot_general(a, b, (((0,), (0,)), ((), ())), preferred_element_type=F32)


def _softmax_pv_t(scores_t, sink_row, values):
    m = sink_row
    for s in scores_t:
        ms = jnp.max(s, axis=0, keepdims=True)
        m = ms if m is None else jnp.maximum(m, ms)
    l = None if sink_row is None else jnp.exp(sink_row - m)
    o = None
    for s, v in zip(scores_t, values):
        p = jnp.exp(s - m)
        ls = jnp.sum(p, axis=0, keepdims=True)
        l = ls if l is None else l + ls
        pv = _dot_tn(v, p.astype(BF16))
        o = pv if o is None else o + pv
    return o * (1.0 / l)


def _mod_bias_kernel(rpb_ref, cond_ref, w_ref, b_ref, o_ref, bias_ref, *, n_col_blocks):
    c = cond_ref[...]
    s = c / (1.0 + jnp.exp(-c))
    o_ref[...] = _dot(s.astype(BF16), w_ref[...].astype(BF16)) + b_ref[...]
    step = pl.program_id(0) * n_col_blocks + pl.program_id(1)
    _write_bias_tiles(rpb_ref, step * (N_DROW * N_DCOL), bias_ref)


def _modulation_and_bias(cond8, w_mod, b_mod, rpb_a):
    n_col_blocks = 4
    tn = 6 * D_MODEL // n_col_blocks
    n_bias_layers = rpb_a.shape[0]
    assert n_bias_layers * A_HEADS == DEPTH * n_col_blocks

    def bias_index(l, j):
        step = l * n_col_blocks + j
        return (step // A_HEADS, step % A_HEADS, 0, 0, 0)

    return pl.pallas_call(
        functools.partial(_mod_bias_kernel, n_col_blocks=n_col_blocks),
        out_shape=(
            jax.ShapeDtypeStruct((DEPTH, N_GROUPS_PAD, 6 * D_MODEL), F32),
            jax.ShapeDtypeStruct(
                (n_bias_layers, A_HEADS, N_BIAS_TILES, GRID_W, 2 * GRID_W), F32)),
        grid=(DEPTH, n_col_blocks),
        in_specs=[
            pl.BlockSpec(memory_space=pltpu.SMEM),
            pl.BlockSpec((N_GROUPS_PAD, D_MODEL), lambda l, j: (0, 0)),
            pl.BlockSpec((None, D_MODEL, tn), lambda l, j: (l, 0, j)),
            pl.BlockSpec((None, 1, tn), lambda l, j: (l, 0, j)),
        ],
        out_specs=(
            pl.BlockSpec((None, N_GROUPS_PAD, tn), lambda l, j: (l, 0, j)),
            pl.BlockSpec((None, None, N_BIAS_TILES, GRID_W, 2 * GRID_W), bias_index)),
        compiler_params=_params(2),
        name="modulation_bias",
    )(rpb_a.reshape(-1), cond8, w_mod, b_mod.reshape(DEPTH, 1, 6 * D_MODEL))


def _norm_mod(x, g, mod_ref, shift_idx):
    var = jnp.mean(x * x, axis=-1, keepdims=True)
    y = x * lax.rsqrt(var + EPS) * g
    shift = mod_ref[:, shift_idx * D_MODEL:(shift_idx + 1) * D_MODEL]
    scale = mod_ref[:, (shift_idx + 1) * D_MODEL:(shift_idx + 2) * D_MODEL]
    return y * (1.0 + scale) + shift


QKV_WIDTH = 3 * A_WIDTH
Q_SCALE = HEAD_DIM ** -0.5


def _rope(x, cos, sin_signed):
    n = x.shape[-1]
    lane = lax.broadcasted_iota(jnp.int32, x.shape, x.ndim - 1)
    first = (lane % 32) < 16
    partner = jnp.where(first, pltpu.roll(x, n - 16, axis=x.ndim - 1),
                        pltpu.roll(x, 16, axis=x.ndim - 1))
    return x * cos + partner * sin_signed


def _inproj_kernel(*refs, tm, n_heads, q_width, n_prev, split_x, rope, has_u):
    n_x = 2 if split_x else 1
    g_ref, mod_ref, w_ref = refs[n_x:n_x + 3]
    n_in = n_x + 3 + (2 if rope else 0) + n_prev
    qkv_ref, ck_ref, cv_ref = refs[n_in:n_in + 3]
    extra_out = refs[n_in + 3:-3]
    wbf_ref, h_ref, res_ref = refs[-3:]
    i = pl.program_id(0)
    is_ctx = i < N_CTX_TOK // tm
    kv_width = (QKV_WIDTH - q_width) // 2
    k_col, v_col = q_width, q_width + kv_width
    n_out = res_ref.shape[1]
    chunk = 4 * HEAD_DIM

    @pl.when(i == 0)
    def _():
        wbf_ref[...] = w_ref[...].astype(BF16)

    def tile(ctx):
        if split_x:
            x = (refs[0] if ctx else refs[1])[...]
            extra_out[-1][...] = x
        else:
            x = refs[0][...]
        h_ref[...] = _norm_mod(x, g_ref[...], mod_ref, 0).astype(BF16)
        for c0 in range(0, n_out, chunk):
            cols = slice(c0, c0 + chunk)
            res_ref[:, cols] = _dot(h_ref[...], wbf_ref[:, cols])
            if c0 >= QKV_WIDTH:
                extra_out[0][:, c0 - QKV_WIDTH:c0 - QKV_WIDTH + chunk] = res_ref[:, cols]
                continue
            r = res_ref[:, cols]
            if rope and not ctx and c0 < v_col:
                cos_ref, sin_ref = refs[n_x + 3:n_x + 5]
                r = _rope(r, cos_ref[:, :chunk], sin_ref[:, :chunk])
            if c0 < q_width:
                r = r * Q_SCALE
            qkv_ref[:, cols] = r.astype(BF16)
            if ctx and c0 >= k_col:
                c_ref, col0 = (ck_ref, k_col) if c0 < v_col else (cv_ref, v_col)
                for req in range(tm // SEQ):
                    for hc in range(chunk // HEAD_DIM):
                        hd = (c0 - col0) // HEAD_DIM + hc
                        c_ref[req, hd] = res_ref[
                            req * SEQ:(req + 1) * SEQ,
                            c0 + hc * HEAD_DIM:c0 + (hc + 1) * HEAD_DIM].T

    pl.when(is_ctx)(lambda: tile(True))
    pl.when(jnp.logical_not(is_ctx))(lambda: tile(False))


def _inproj(xs, g_all, mod4, layer, w_all, w_idx, n_heads, q_width, n_slots, prev_caches,
            rope_tables=None):
    tm = 512
    n_out = w_all.shape[2]
    n_ctx_tiles = N_CTX_TOK // tm
    split_x = len(xs) == 2
    rope = rope_tables is not None
    has_u = n_out > QKV_WIDTH
    cache_shape = jax.ShapeDtypeStruct((BATCH, n_slots, n_heads, HEAD_DIM, SEQ), F32)
    cache_spec = pl.BlockSpec(
        (tm // SEQ, None, n_heads, HEAD_DIM, SEQ),
        lambda i: (jnp.minimum(i, n_ctx_tiles - 1), w_idx, 0, 0, 0))
    row_spec = pl.BlockSpec((tm, D_MODEL), lambda i: (i, 0))
    if split_x:
        x_specs = [
            pl.BlockSpec((tm, D_MODEL), lambda i: (jnp.minimum(i, n_ctx_tiles - 1), 0)),
            pl.BlockSpec((tm, D_MODEL), lambda i: (jnp.maximum(i - n_ctx_tiles, 0), 0)),
        ]
    else:
        x_specs = [row_spec]
    out_shape = [jax.ShapeDtypeStruct((N_TOK, QKV_WIDTH), BF16), cache_shape, cache_shape]
    out_specs = [pl.BlockSpec((tm, QKV_WIDTH), lambda i: (i, 0)), cache_spec, cache_spec]
    if has_u:
        out_shape.append(jax.ShapeDtypeStruct((N_TOK, n_out - QKV_WIDTH), F32))
        out_specs.append(pl.BlockSpec((tm, n_out - QKV_WIDTH), lambda i: (i, 0)))
    if split_x:
        out_shape.append(jax.ShapeDtypeStruct((N_TOK, D_MODEL), F32))
        out_specs.append(row_spec)
    rope_in, rope_specs = [], []
    if rope:
        tiles_per_seq = DEC_SEQ // tm
        rope_spec = pl.BlockSpec(
            (tm, D_MODEL), lambda i: (jnp.maximum(i - n_ctx_tiles, 0) % tiles_per_seq, 0))
        rope_in, rope_specs = list(rope_tables), [rope_spec, rope_spec]
    n_prev = len(prev_caches)
    n_before = len(xs) + 3 + len(rope_in)
    return pl.pallas_call(
        functools.partial(_inproj_kernel, tm=tm, n_heads=n_heads, q_width=q_width,
                          n_prev=n_prev, split_x=split_x, rope=rope, has_u=has_u),
        out_shape=out_shape,
        grid=(N_TOK // tm,),
        in_specs=x_specs + [
            pl.BlockSpec((None, 1, D_MODEL), lambda i: (layer, 0, 0)),
            pl.BlockSpec((None, None, 1, 6 * D_MODEL),
                         lambda i: (layer, _group_of_tile(i, tm), 0, 0)),
            pl.BlockSpec((None, D_MODEL, n_out), lambda i: (w_idx, 0, 0),
                         pipeline_mode=pl.Buffered(1)),
        ] + rope_specs + [pl.BlockSpec(memory_space=pl.ANY)] * n_prev,
        out_specs=out_specs,
        scratch_shapes=[pltpu.VMEM((D_MODEL, n_out), BF16), pltpu.VMEM((tm, D_MODEL), BF16),
                        pltpu.VMEM((tm, n_out), F32)],
        input_output_aliases={n_before + k: 1 + k for k in range(n_prev)},
        compiler_params=_params(1),
        name="inproj",
    )(*xs, g_all.reshape(DEPTH, 1, D_MODEL), mod4, w_all, *rope_in, *prev_caches)


CTX_A_REQS_PER_STEP = 2


def _ctx_attn_a_kernel(p_ref, o_ref):
    for req in range(CTX_A_REQS_PER_STEP):
        rows = slice(req * SEQ, (req + 1) * SEQ)
        for h in range(A_HEADS):
            c0 = h * HEAD_DIM
            q = p_ref[rows, c0:c0 + HEAD_DIM]
            k = p_ref[rows, A_WIDTH + c0:A_WIDTH + c0 + HEAD_DIM]
            v = p_ref[rows, 2 * A_WIDTH + c0:2 * A_WIDTH + c0 + HEAD_DIM]
            s = _dot_nt(q, k)
            m = jnp.max(s, axis=-1, keepdims=True)
            p = jnp.exp(s - m)
            l = jnp.sum(p, axis=-1, keepdims=True)
            o_ref[rows, c0:c0 + HEAD_DIM] = _dot(p.astype(BF16), v) / l


def _ctx_attn_a(qkv):
    return pl.pallas_call(
        _ctx_attn_a_kernel,
        out_shape=jax.ShapeDtypeStruct((N_TOK, D_MODEL), F32),
        grid=(BATCH // CTX_A_REQS_PER_STEP,),
        in_specs=[pl.BlockSpec((CTX_A_REQS_PER_STEP * SEQ, QKV_WIDTH), lambda b: (b, 0))],
        out_specs=pl.BlockSpec((CTX_A_REQS_PER_STEP * SEQ, A_WIDTH), lambda b: (b, 0)),
        compiler_params=_params(1),
        name="ctx_attn_a",
    )(qkv)


N_DROW = 2 * NA_ROWS - 1
N_DCOL = 2 * NA_COLS - 1
N_BIAS_TILES = 16
BIAS_TILE_LEFT_PAD = 14
BIAS_TILE_RIGHT_PAD = 15
MID_DROW = NA_ROWS - 1 - NA_ROWS // 2


def _write_bias_tiles(rpb_ref, base, o_ref):
    qi = lax.broadcasted_iota(jnp.int32, (GRID_W, 2 * GRID_W), 0)
    lane = lax.broadcasted_iota(jnp.int32, (GRID_W, 2 * GRID_W), 1)
    right = lane >= GRID_W
    kc = jnp.where(right, lane - GRID_W, lane)
    rel = kc - qi + (NA_COLS - 1)
    qstart = jnp.clip(qi - NA_COLS // 2, 0, GRID_W - NA_COLS)
    valid = (kc >= qstart) & (kc < qstart + NA_COLS)

    rows = [jnp.zeros((GRID_W, 2 * GRID_W), F32)] * N_DROW
    for d in range(N_DCOL):
        hit = rel == d
        rows = [jnp.where(hit, rpb_ref[base + dr * N_DCOL + d], rows[dr])
                for dr in range(N_DROW)]

    for t in range(N_DROW - 1):
        o_ref[t] = jnp.where(valid, jnp.where(right, rows[t + 1], rows[t]), NEG_INF)
    o_ref[BIAS_TILE_LEFT_PAD] = jnp.where(valid & right, rows[MID_DROW], NEG_INF)
    o_ref[BIAS_TILE_RIGHT_PAD] = jnp.where(valid & jnp.logical_not(right),
                                           rows[MID_DROW + NA_ROWS - 1], NEG_INF)


def _na_window(r):
    start = min(max(r - NA_ROWS // 2, 0), GRID_ROWS - NA_ROWS)
    first_drow = start - r + NA_ROWS - 1
    if start % 2 == 0:
        return start, [first_drow + 2 * p for p in range(NA_ROWS // 2)]
    assert first_drow == MID_DROW
    inner = [first_drow + 1 + 2 * p for p in range(NA_ROWS // 2 - 1)]
    return start - 1, [BIAS_TILE_LEFT_PAD] + inner + [BIAS_TILE_RIGHT_PAD]


def _lat_attn_a_kernel(q_ref, k_ref, v_ref, kc_ref, vc_ref, bias_ref, mix_ref, o_ref,
                       s_loc, s_ctx, p_loc, p_ctx):
    del mix_ref
    hp = pl.program_id(1)

    @pl.when((pl.program_id(0) == 0) & (hp == 0))
    def _():
        p_loc[...] = jnp.zeros_like(p_loc)

    for hh in range(2):
        c0 = hh * HEAD_DIM
        q = q_ref[:, c0:c0 + HEAD_DIM]
        k = k_ref[:, c0:c0 + HEAD_DIM]
        v = v_ref[:, c0:c0 + HEAD_DIM]
        kc_t = kc_ref[hh].astype(BF16)
        vc_t = vc_ref[hh].astype(BF16)
        s_loc[...] = _dot_nt(q, k)
        s_ctx[...] = _dot(q, kc_t)
        inv_l = []
        for r in range(GRID_ROWS):
            rows = slice(r * GRID_W, (r + 1) * GRID_W)
            first_row, tiles = _na_window(r)
            cols = slice(first_row * GRID_W, (first_row + 2 * len(tiles)) * GRID_W)
            bias = jnp.concatenate([bias_ref[hh, t] for t in tiles], axis=1)
            sc = s_ctx[rows, :]
            sl = s_loc[rows, cols] + bias
            m = jnp.maximum(jnp.max(sc, axis=-1, keepdims=True),
                            jnp.max(sl, axis=-1, keepdims=True))
            pc = jnp.exp(sc - m)
            pw = jnp.exp(sl - m)
            inv_l.append(1.0 / (jnp.sum(pc, axis=-1, keepdims=True)
                                + jnp.sum(pw, axis=-1, keepdims=True)))
            p_ctx[rows, :] = pc.astype(BF16)
            p_loc[rows, cols] = pw.astype(BF16)
        o = _dot_nt(p_ctx[...], vc_t) + _dot(p_loc[...], v)
        o_ref[:, c0:c0 + HEAD_DIM] = o * jnp.concatenate(inv_l, axis=0)


def _lat_attn_a(qkv, cache_k, cache_v, bias_tiles, i_layer, mix):
    row0 = N_CTX_TOK // DEC_SEQ
    pair = 2 * HEAD_DIM
    k_col0 = A_WIDTH // pair
    v_col0 = 2 * A_WIDTH // pair
    return pl.pallas_call(
        _lat_attn_a_kernel,
        out_shape=jax.ShapeDtypeStruct((N_TOK, D_MODEL), F32),
        grid=(DEC_BATCH, A_HEADS // 2),
        in_specs=[
            pl.BlockSpec((DEC_SEQ, pair), lambda b, hp: (row0 + b, hp)),
            pl.BlockSpec((DEC_SEQ, pair), lambda b, hp: (row0 + b, k_col0 + hp)),
            pl.BlockSpec((DEC_SEQ, pair), lambda b, hp: (row0 + b, v_col0 + hp)),
            pl.BlockSpec((None, None, 2, HEAD_DIM, PAST_LEN),
                         lambda b, hp: (b, i_layer, hp, 0, 0)),
            pl.BlockSpec((None, None, 2, HEAD_DIM, PAST_LEN),
                         lambda b, hp: (b, i_layer, hp, 0, 0)),
            pl.BlockSpec((None, 2, N_BIAS_TILES, GRID_W, 2 * GRID_W),
                         lambda b, hp: (i_layer, hp, 0, 0, 0)),
            pl.BlockSpec(memory_space=pl.ANY),
        ],
        out_specs=pl.BlockSpec((DEC_SEQ, pair), lambda b, hp: (row0 + b, hp)),
        scratch_shapes=[pltpu.VMEM((DEC_SEQ, DEC_SEQ), F32),
                        pltpu.VMEM((DEC_SEQ, PAST_LEN), F32),
                        pltpu.VMEM((DEC_SEQ, DEC_SEQ), BF16),
                        pltpu.VMEM((DEC_SEQ, PAST_LEN), BF16)],
        input_output_aliases={6: 0},
        compiler_params=_params(2),
        name="lat_attn_a",
    )(qkv, qkv, qkv, cache_k, cache_v, bias_tiles, mix)


def _pool_kernel(u_ref, wp_ref, ps_ref, mix_ref, o_ref, *, rows):
    del mix_ref
    n = jnp.where(pl.program_id(0) < N_CTX_TOK // rows, SEQ, DEC_SEQ)
    t = lax.broadcasted_iota(jnp.int32, (rows, B_GROUP_DIM), 0) & (n - 1)

    def earlier(x, k):
        return jnp.where(t >= k, pltpu.roll(x, k, axis=0), 0.0)

    def later(x, k):
        return jnp.where(t < n - k, pltpu.roll(x, rows - k, axis=0), 0.0)

    for g, w in enumerate(POOL_WINDOWS):
        c0 = g * B_GROUP_DIM
        half = w // 2
        u = u_ref[:, c0:c0 + B_GROUP_DIM]
        before, after = u, u
        k = 1
        while k < half:
            before = before + earlier(before, k)
            after = after + later(after, k)
            k *= 2
        total = earlier(before, 1) + after
        count = (jnp.minimum(t + half, n) - jnp.maximum(t - half, 0)).astype(F32)
        pooled = total / count - u
        y = _dot(pooled.astype(BF16), wp_ref[g].astype(BF16))
        o_ref[:, c0:c0 + B_GROUP_DIM] = y * ps_ref[:, c0:c0 + B_GROUP_DIM]


def _pool(u, w_pool, pool_scale, i_layer, mix):
    n_groups = len(POOL_WINDOWS)
    rows = DEC_SEQ
    return pl.pallas_call(
        functools.partial(_pool_kernel, rows=rows),
        out_shape=jax.ShapeDtypeStruct((N_TOK, D_MODEL), F32),
        grid=(N_TOK // rows,),
        in_specs=[
            pl.BlockSpec((rows, B_WIDTH), lambda b: (b, 0)),
            pl.BlockSpec((None, n_groups, B_GROUP_DIM, B_GROUP_DIM),
                         lambda b: (i_layer, 0, 0, 0)),
            pl.BlockSpec((None, 1, B_WIDTH), lambda b: (i_layer, 0, 0)),
            pl.BlockSpec(memory_space=pl.ANY),
        ],
        out_specs=pl.BlockSpec((rows, B_WIDTH), lambda b: (b, 1)),
        input_output_aliases={3: 0},
        compiler_params=_params(1),
        name="pool_mixer",
    )(u, w_pool, pool_scale.reshape(-1, 1, B_WIDTH), mix)


def _sink_row(sink_ref, head0, queries_per_head):
    lane = lax.broadcasted_iota(jnp.int32, (1, C_GROUP * queries_per_head), 1)
    row = jnp.full((1, C_GROUP * queries_per_head), sink_ref[head0], F32)
    for g in range(1, C_GROUP):
        row = jnp.where(lane >= g * queries_per_head, sink_ref[head0 + g], row)
    return row


CTX_C_REQS_PER_STEP = 2


def _ctx_attn_c_kernel(sink_ref, p_ref, o_ref, *, sink0):
    for req in range(CTX_C_REQS_PER_STEP):
        rows = slice(req * SEQ, (req + 1) * SEQ)
        outs_t = []
        for kk in range(C_KV_HEADS):
            heads = [kk * C_GROUP + g for g in range(C_GROUP)]
            q = jnp.concatenate(
                [p_ref[rows, h * HEAD_DIM:(h + 1) * HEAD_DIM] for h in heads], axis=0)
            k0 = C_Q_WIDTH + kk * HEAD_DIM
            v0 = C_Q_WIDTH + C_KV_WIDTH + kk * HEAD_DIM
            k = p_ref[rows, k0:k0 + HEAD_DIM]
            v = p_ref[rows, v0:v0 + HEAD_DIM]
            o_t = _softmax_pv_t([_dot_nt(k, q)],
                                _sink_row(sink_ref, sink0 + kk * C_GROUP, SEQ),
                                [v])
            outs_t += [o_t[:, g * SEQ:(g + 1) * SEQ] for g in range(C_GROUP)]
        o_ref[rows, :] = jnp.concatenate(outs_t, axis=0).T


def _ctx_attn_c(qkv, sink_all, j_layer):
    return pl.pallas_call(
        functools.partial(_ctx_attn_c_kernel, sink0=j_layer * C_HEADS),
        out_shape=jax.ShapeDtypeStruct((N_TOK, D_MODEL), F32),
        grid=(BATCH // CTX_C_REQS_PER_STEP,),
        in_specs=[
            pl.BlockSpec(memory_space=pltpu.SMEM),
            pl.BlockSpec((CTX_C_REQS_PER_STEP * SEQ, QKV_WIDTH), lambda b: (b, 0)),
        ],
        out_specs=pl.BlockSpec((CTX_C_REQS_PER_STEP * SEQ, C_Q_WIDTH), lambda b: (b, 0)),
        compiler_params=_params(1),
        name="ctx_attn_c",
    )(sink_all.reshape(-1), qkv)


def _lat_attn_c_kernel(sink_ref, q_ref, k_ref, v_ref, kc_ref, vc_ref, mix_ref, o_ref,
                       kctx_ref, vctx_ref, *, sink0):
    del mix_ref
    j = pl.program_id(1)
    n_blocks = DEC_SEQ // C_BLOCK

    @pl.when(j == 0)
    def _():
        for kk in range(C_KV_HEADS):
            c0 = kk * HEAD_DIM
            kctx_ref[:, c0:c0 + HEAD_DIM] = kc_ref[kk].T.astype(BF16)
            vctx_ref[:, c0:c0 + HEAD_DIM] = vc_ref[kk].T.astype(BF16)

    n_q = C_GROUP * C_BLOCK
    qi = lax.broadcasted_iota(jnp.int32, (C_BLOCK, n_q), 1) % C_BLOCK
    jl = lax.broadcasted_iota(jnp.int32, (C_BLOCK, n_q), 0)
    valid_prev = (jl >= qi) & (j > 0)
    valid_next = (jl <= qi) & (j < n_blocks - 1)
    rows_prev = pl.ds(pl.multiple_of(jnp.maximum(j - 1, 0) * C_BLOCK, C_BLOCK), C_BLOCK)
    rows_cur = pl.ds(pl.multiple_of(j * C_BLOCK, C_BLOCK), C_BLOCK)
    rows_next = pl.ds(pl.multiple_of(jnp.minimum(j + 1, n_blocks - 1) * C_BLOCK, C_BLOCK),
                      C_BLOCK)

    outs_t = []
    for kk in range(C_KV_HEADS):
        heads = [kk * C_GROUP + g for g in range(C_GROUP)]
        qs = jnp.concatenate(
            [q_ref[:, h * HEAD_DIM:(h + 1) * HEAD_DIM] for h in heads], axis=0)
        c0 = kk * HEAD_DIM
        cols = slice(c0, c0 + HEAD_DIM)
        k_ctx = kctx_ref[:, cols]
        v_ctx = vctx_ref[:, cols]
        k_band = jnp.concatenate(
            [k_ref[rows_prev, cols], k_ref[rows_cur, cols], k_ref[rows_next, cols]], axis=0)
        v_band = jnp.concatenate(
            [v_ref[rows_prev, cols], v_ref[rows_cur, cols], v_ref[rows_next, cols]], axis=0)
        s_ctx = _dot_nt(k_ctx, qs)
        s_raw = _dot_nt(k_band, qs)
        s_band = jnp.concatenate(
            [jnp.where(valid_prev, s_raw[:C_BLOCK], NEG_INF),
             s_raw[C_BLOCK:2 * C_BLOCK],
             jnp.where(valid_next, s_raw[2 * C_BLOCK:], NEG_INF)], axis=0)
        o_t = _softmax_pv_t([s_ctx, s_band],
                            _sink_row(sink_ref, sink0 + kk * C_GROUP, C_BLOCK),
                            [v_ctx, v_band])
        outs_t += [o_t[:, g * C_BLOCK:(g + 1) * C_BLOCK] for g in range(C_GROUP)]
    o_ref[...] = jnp.concatenate(outs_t, axis=0).T


def _lat_attn_c(qkv, cache_k, cache_v, sink, j_layer, mix):
    n_blocks = DEC_SEQ // C_BLOCK
    q_row0 = N_CTX_TOK // C_BLOCK
    kv_row0 = N_CTX_TOK // DEC_SEQ
    k_col = C_Q_WIDTH // C_KV_WIDTH
    return pl.pallas_call(
        functools.partial(_lat_attn_c_kernel, sink0=j_layer * C_HEADS),
        out_shape=jax.ShapeDtypeStruct((N_TOK, D_MODEL), F32),
        grid=(DEC_BATCH, n_blocks),
        in_specs=[
            pl.BlockSpec(memory_space=pltpu.SMEM),
            pl.BlockSpec((C_BLOCK, C_Q_WIDTH), lambda b, j: (q_row0 + b * n_blocks + j, 0)),
            pl.BlockSpec((DEC_SEQ, C_KV_WIDTH), lambda b, j: (kv_row0 + b, k_col)),
            pl.BlockSpec((DEC_SEQ, C_KV_WIDTH), lambda b, j: (kv_row0 + b, k_col + 1)),
            pl.BlockSpec((None, None, C_KV_HEADS, HEAD_DIM, PAST_LEN),
                         lambda b, j: (b, j_layer, 0, 0, 0)),
            pl.BlockSpec((None, None, C_KV_HEADS, HEAD_DIM, PAST_LEN),
                         lambda b, j: (b, j_layer, 0, 0, 0)),
            pl.BlockSpec(memory_space=pl.ANY),
        ],
        out_specs=pl.BlockSpec((C_BLOCK, C_Q_WIDTH),
                               lambda b, j: (q_row0 + b * n_blocks + j, 0)),
        input_output_aliases={6: 0},
        scratch_shapes=[pltpu.VMEM((PAST_LEN, C_KV_WIDTH), BF16),
                        pltpu.VMEM((PAST_LEN, C_KV_WIDTH), BF16)],
        compiler_params=_params(2),
        name="lat_attn_c",
    )(sink.reshape(-1), qkv, qkv, qkv, cache_k, cache_v, mix)


def _rope_tables():
    t = np.arange(DEC_SEQ)
    pos = np.stack([t // GRID_W, t % GRID_W], axis=-1).astype(np.float64)
    half = HEAD_DIM // 4
    inv = ROPE_BASE ** (-np.arange(half, dtype=np.float64) / half)
    ang = pos[:, :, None] * inv
    cos = np.cos(ang)
    sin = np.sin(ang)
    cos64 = np.stack([cos, cos], axis=2).reshape(DEC_SEQ, HEAD_DIM)
    sin64 = np.stack([-sin, sin], axis=2).reshape(DEC_SEQ, HEAD_DIM)
    return (jnp.asarray(np.tile(cos64, (1, C_HEADS)), F32),
            jnp.asarray(np.tile(sin64, (1, C_HEADS)), F32))


FFN_CHUNK = 256
N_FFN_CHUNKS = FFN_HIDDEN // FFN_CHUNK
N_WO_PIECES = D_MODEL // FFN_CHUNK


def _post_mixer_kernel(*refs, layer, wo_idx, tm, final):
    x_ref, mix_ref, g_ref, mod_ref = refs[:4]
    n_in = 5 if final else 4
    wo_hbm, wgu_hbm, wd_hbm = refs[n_in:n_in + 3]
    n_out = 2 if final else 1
    out_refs = refs[n_in + 3:n_in + 3 + n_out]
    (wo_s, wg_s, wu_s, wd_s, act_s, stage_col, stage_row, sem_col,
     sem_row) = refs[n_in + 3 + n_out:]
    n_row_pieces = N_WO_PIECES + N_FFN_CHUNKS

    def col_copy(which, c):
        src = wgu_hbm.at[layer, :, pl.ds(which * FFN_HIDDEN + c * FFN_CHUNK, FFN_CHUNK)]
        return pltpu.make_async_copy(src, stage_col.at[which, c % 2], sem_col.at[which, c % 2])

    def row_copy(p):
        if p < N_WO_PIECES:
            src = wo_hbm.at[wo_idx, pl.ds(p * FFN_CHUNK, FFN_CHUNK), :]
        else:
            src = wd_hbm.at[layer, pl.ds((p - N_WO_PIECES) * FFN_CHUNK, FFN_CHUNK), :]
        return pltpu.make_async_copy(src, stage_row.at[p % 2], sem_row.at[p % 2])

    def take_row_piece(p, dst_ref, row0):
        if p + 1 < n_row_pieces:
            row_copy(p + 1).start()
        row_copy(p).wait()
        dst_ref[row0:row0 + FFN_CHUNK, :] = stage_row[p % 2].astype(BF16)

    def tile(load_weights):
        if load_weights:
            row_copy(0).start()
            for which in range(2):
                col_copy(which, 0).start()
            for p in range(N_WO_PIECES):
                take_row_piece(p, wo_s, p * FFN_CHUNK)
        gate1 = mod_ref[:, 2 * D_MODEL:3 * D_MODEL]
        x1 = x_ref[...] + gate1 * _dot(mix_ref[...].astype(BF16), wo_s[...])
        h = _norm_mod(x1, g_ref[...], mod_ref, 3).astype(BF16)
        for c in range(N_FFN_CHUNKS):
            if load_weights:
                for which, dst in ((0, wg_s), (1, wu_s)):
                    if c + 1 < N_FFN_CHUNKS:
                        col_copy(which, c + 1).start()
                    col_copy(which, c).wait()
                    dst[c] = stage_col[which, c % 2].astype(BF16)
            gate = _dot(h, wg_s[c])
            up = _dot(h, wu_s[c])
            act = gate / (1.0 + jnp.exp(-gate)) * up
            act_s[:, c * FFN_CHUNK:(c + 1) * FFN_CHUNK] = act.astype(BF16)
            if load_weights:
                take_row_piece(N_WO_PIECES + c, wd_s, c * FFN_CHUNK)
        gate2 = mod_ref[:, 5 * D_MODEL:6 * D_MODEL]
        return x1 + gate2 * _dot(act_s[...], wd_s[...])

    def emit(x2):
        if not final:
            out_refs[0][...] = x2
            return
        var = jnp.mean(x2 * x2, axis=-1, keepdims=True)
        y = x2 * lax.rsqrt(var + EPS) * refs[4][...]
        is_ctx = pl.program_id(0) < N_CTX_TOK // tm

        @pl.when(is_ctx)
        def _():
            out_refs[0][...] = y

        @pl.when(jnp.logical_not(is_ctx))
        def _():
            out_refs[1][...] = y

    first = pl.program_id(0) == 0

    @pl.when(first)
    def _():
        emit(tile(True))

    @pl.when(jnp.logical_not(first))
    def _():
        emit(tile(False))


def _post_mixer(x, mix, g_all, mod4, layer, w_out_all, wo_idx, w_gate_up, w_down,
                norm_final=None):
    tm = 512
    n_ctx_tiles = N_CTX_TOK // tm
    final = norm_final is not None
    row_spec = pl.BlockSpec((tm, D_MODEL), lambda i: (i, 0))
    hbm = pl.BlockSpec(memory_space=pl.ANY)
    if final:
        extra_in = [norm_final.reshape(1, D_MODEL)]
        extra_specs = [pl.BlockSpec((1, D_MODEL), lambda i: (0, 0))]
        out_shape = (jax.ShapeDtypeStruct((N_CTX_TOK, D_MODEL), F32),
                     jax.ShapeDtypeStruct((N_LAT_TOK, D_MODEL), F32))
        out_specs = (
            pl.BlockSpec((tm, D_MODEL), lambda i: (jnp.minimum(i, n_ctx_tiles - 1), 0)),
            pl.BlockSpec((tm, D_MODEL), lambda i: (jnp.maximum(i - n_ctx_tiles, 0), 0)))
    else:
        extra_in, extra_specs = [], []
        out_shape = jax.ShapeDtypeStruct((N_TOK, D_MODEL), F32)
        out_specs = row_spec
    return pl.pallas_call(
        functools.partial(_post_mixer_kernel, layer=layer, wo_idx=wo_idx, tm=tm, final=final),
        out_shape=out_shape,
        grid=(N_TOK // tm,),
        in_specs=[
            row_spec,
            row_spec,
            pl.BlockSpec((None, 1, D_MODEL), lambda i: (layer, 0, 0)),
            pl.BlockSpec((None, None, 1, 6 * D_MODEL),
                         lambda i: (layer, _group_of_tile(i, tm), 0, 0)),
        ] + extra_specs + [hbm, hbm, hbm],
        out_specs=out_specs,
        scratch_shapes=[
            pltpu.VMEM((D_MODEL, D_MODEL), BF16),
            pltpu.VMEM((N_FFN_CHUNKS, D_MODEL, FFN_CHUNK), BF16),
            pltpu.VMEM((N_FFN_CHUNKS, D_MODEL, FFN_CHUNK), BF16),
            pltpu.VMEM((FFN_HIDDEN, D_MODEL), BF16),
            pltpu.VMEM((tm, FFN_HIDDEN), BF16),
            pltpu.VMEM((2, 2, D_MODEL, FFN_CHUNK), F32),
            pltpu.VMEM((2, FFN_CHUNK, D_MODEL), F32),
            pltpu.SemaphoreType.DMA((2, 2)),
            pltpu.SemaphoreType.DMA((2,)),
        ],
        compiler_params=_params(1),
        name="post_mixer",
    )(x, mix, g_all.reshape(DEPTH, 1, D_MODEL), mod4, *extra_in, w_out_all, w_gate_up, w_down)


def kernel(x_prompt, x_sample, cache_a_k, cache_a_v, cache_c_k, cache_c_v, c, c_ctx, w_mod, b_mod, norm_mix, norm_ffn, w_in_ab, rpb_a, w_pool, pool_scale, w_out_ab, w_in_c, sink_c, w_out_c, w_gate_up, w_down, norm_final):
    xs = (x_prompt.reshape(N_CTX_TOK, D_MODEL), x_sample.reshape(N_LAT_TOK, D_MODEL))
    cond8 = jnp.concatenate(
        [c_ctx[None], c, jnp.zeros((N_GROUPS_PAD - 1 - DEC_BATCH, D_MODEL), F32)], axis=0)
    mod, bias_tiles = _modulation_and_bias(cond8, w_mod, b_mod, rpb_a)
    mod4 = mod.reshape(DEPTH, N_GROUPS_PAD, 1, 6 * D_MODEL)

    n_ab = cache_a_k.shape[1]
    n_c = cache_c_k.shape[1]
    cache_a_k, cache_a_v, cache_c_k, cache_c_v = (
        jnp.transpose(t, (0, 1, 3, 4, 2)) for t in (cache_a_k, cache_a_v, cache_c_k, cache_c_v))
    rope_tables = _rope_tables()

    new_a = []
    new_c = []
    for l in range(DEPTH):
        if l % 2 == 0:
            i = l // 2
            qkv, *rest = _inproj(xs, norm_mix, mod4, l, w_in_ab, i, A_HEADS, A_WIDTH, n_ab,
                                 new_a)
            new_a, u = rest[:2], rest[2]
            if len(xs) == 2:
                xs = (rest[3],)
            mix = _ctx_attn_a(qkv)
            mix = _lat_attn_a(qkv, cache_a_k, cache_a_v, bias_tiles, i, mix)
            mix = _pool(u, w_pool, pool_scale, i, mix)
            w_out, wo_idx = w_out_ab, i
        else:
            j = l // 2
            qkv, *new_c = _inproj(xs, norm_mix, mod4, l, w_in_c, j, C_KV_HEADS, C_Q_WIDTH, n_c,
                                  new_c, rope_tables)
            mix = _ctx_attn_c(qkv, sink_c, j)
            mix = _lat_attn_c(qkv, cache_c_k, cache_c_v, sink_c, j, mix)
            w_out, wo_idx = w_out_c, j
        if l + 1 < DEPTH:
            xs = (_post_mixer(xs[0], mix, norm_ffn, mod4, l, w_out, wo_idx, w_gate_up, w_down),)
        else:
            y_ctx, y_lat = _post_mixer(xs[0], mix, norm_ffn, mod4, l, w_out, wo_idx, w_gate_up,
                                       w_down, norm_final)

    new_caches = [jnp.transpose(t, (0, 1, 4, 2, 3)) for t in (*new_a, *new_c)]
    return (y_ctx.reshape(BATCH, SEQ, D_MODEL), y_lat.reshape(DEC_BATCH, DEC_SEQ, D_MODEL),
            *new_caches)
```

```python
import functools

import jax
import jax.numpy as jnp
import numpy as np
from jax import lax
from jax.experimental import pallas as pl
from jax.experimental.pallas import tpu as pltpu

D_MODEL = 1024
BATCH = 16
SEQ = 256
DEPTH = 4
DEC_BATCH = 2
DEC_SEQ = 1024
PAST_LEN = 512
GRID_W = 64
HEAD_DIM = 64
A_WIDTH = 512
A_HEADS = 8
B_WIDTH = 512
POOL_WINDOWS = (2, 4, 8, 16)
B_GROUP_DIM = 128
NA_ROWS = 8
NA_COLS = 16
C_HEADS = 16
C_KV_HEADS = 4
C_GROUP = C_HEADS // C_KV_HEADS
C_Q_WIDTH = 1024
C_KV_WIDTH = 256
C_BLOCK = 128
FFN_HIDDEN = 2816
ROPE_BASE = 10000.0
EPS = 1e-6
NEG_INF = -1e30

N_CTX_TOK = BATCH * SEQ
N_LAT_TOK = DEC_BATCH * DEC_SEQ
N_TOK = N_CTX_TOK + N_LAT_TOK
GRID_ROWS = DEC_SEQ // GRID_W
N_GROUPS_PAD = 8

VMEM_LIMIT = 56 * 1024 * 1024

F32 = jnp.float32
BF16 = jnp.bfloat16


def _params(n_axes):
    return pltpu.CompilerParams(dimension_semantics=("arbitrary",) * n_axes,
                                vmem_limit_bytes=VMEM_LIMIT)


def _group_of_tile(i, tm):
    row0 = i * tm
    return jnp.where(row0 < N_CTX_TOK, 0, 1 + (row0 - N_CTX_TOK) // DEC_SEQ)


def _dot_nt(a, b):
    return lax.dot_general(a, b, (((1,), (1,)), ((), ())), preferred_element_type=F32)


def _dot(a, b):
    return jnp.dot(a, b, preferred_element_type=F32)


def _dot_tn(a, b):
    return lax.dot_general(a, b, (((0,), (0,)), ((), ())), preferred_element_type=F32)


def _softmax_pv_t(scores_t, sink_row, values):
    m = sink_row
    for s in scores_t:
        ms = jnp.max(s, axis=0, keepdims=True)
        m = ms if m is None else jnp.maximum(m, ms)
    l = None if sink_row is None else jnp.exp(sink_row - m)
    o = None
    for s, v in zip(scores_t, values):
        p = jnp.exp(s - m)
        ls = jnp.sum(p, axis=0, keepdims=True)
        l = ls if l is None else l + ls
        pv = _dot_tn(v, p.astype(BF16))
        o = pv if o is None else o + pv
    return o * (1.0 / l)


def _modulation_block(cond_ref, w_ref, b_ref):
    c = cond_ref[...]
    s = c / (1.0 + jnp.exp(-c))
    return _dot(s.astype(BF16), w_ref[...].astype(BF16)) + b_ref[...]


def _mod_bias_kernel(rpb_ref, cond_ref, w_ref, b_ref, o_ref, bias_ref, *, heads_per_step):
    o_ref[...] = _modulation_block(cond_ref, w_ref, b_ref)
    pair0 = pl.program_id(0) * heads_per_step
    for t in range(heads_per_step):
        _write_bias_tiles(rpb_ref, (pair0 + t) * (N_DROW * N_DCOL), bias_ref.at[t])


def _modulation0_and_bias(cond8, w_mod, b_mod3, rpb_a):
    n_col_blocks = 4
    tn = 6 * D_MODEL // n_col_blocks
    n_bias_layers = rpb_a.shape[0]
    heads_per_step = n_bias_layers * A_HEADS // n_col_blocks
    steps_per_layer = A_HEADS // heads_per_step
    return pl.pallas_call(
        functools.partial(_mod_bias_kernel, heads_per_step=heads_per_step),
        out_shape=(
            jax.ShapeDtypeStruct((N_GROUPS_PAD, 6 * D_MODEL), F32),
            jax.ShapeDtypeStruct(
                (n_bias_layers, A_HEADS, N_BIAS_TILES, GRID_W, 2 * GRID_W), F32)),
        grid=(n_col_blocks,),
        in_specs=[
            pl.BlockSpec(memory_space=pltpu.SMEM),
            pl.BlockSpec((N_GROUPS_PAD, D_MODEL), lambda j: (0, 0)),
            pl.BlockSpec((None, D_MODEL, tn), lambda j: (0, 0, j)),
            pl.BlockSpec((None, 1, tn), lambda j: (0, 0, j)),
        ],
        out_specs=(
            pl.BlockSpec((N_GROUPS_PAD, tn), lambda j: (0, j)),
            pl.BlockSpec((None, heads_per_step, N_BIAS_TILES, GRID_W, 2 * GRID_W),
                         lambda j: (j // steps_per_layer, j % steps_per_layer, 0, 0, 0))),
        compiler_params=_params(1),
        name="modulation_bias",
    )(rpb_a.reshape(-1), cond8, w_mod, b_mod3)


def _norm_mod(x, g, mod_ref, shift_idx):
    var = jnp.mean(x * x, axis=-1, keepdims=True)
    y = x * lax.rsqrt(var + EPS) * g
    shift = mod_ref[:, shift_idx * D_MODEL:(shift_idx + 1) * D_MODEL]
    scale = mod_ref[:, (shift_idx + 1) * D_MODEL:(shift_idx + 2) * D_MODEL]
    return y * (1.0 + scale) + shift


QKV_WIDTH = 3 * A_WIDTH
Q_SCALE = HEAD_DIM ** -0.5


def _rope(x, cos, sin_signed):
    n = x.shape[-1]
    lane = lax.broadcasted_iota(jnp.int32, x.shape, x.ndim - 1)
    first = (lane % 32) < 16
    partner = jnp.where(first, pltpu.roll(x, n - 16, axis=x.ndim - 1),
                        pltpu.roll(x, 16, axis=x.ndim - 1))
    return x * cos + partner * sin_signed


def _inproj_kernel(*refs, tm, n_heads, q_width, n_prev, split_x, rope, has_u):
    n_x = 2 if split_x else 1
    g_ref, mod_ref, w_ref = refs[n_x:n_x + 3]
    n_in = n_x + 3 + (2 if rope else 0) + n_prev
    qkv_ref, ck_ref, cv_ref = refs[n_in:n_in + 3]
    extra_out = refs[n_in + 3:-3]
    wbf_ref, h_ref, res_ref = refs[-3:]
    i = pl.program_id(0)
    is_ctx = i < N_CTX_TOK // tm
    kv_width = (QKV_WIDTH - q_width) // 2
    k_col, v_col = q_width, q_width + kv_width
    n_out = res_ref.shape[1]
    chunk = 4 * HEAD_DIM

    @pl.when(i == 0)
    def _():
        wbf_ref[...] = w_ref[...].astype(BF16)

    def tile(ctx):
        if split_x:
            x = (refs[0] if ctx else refs[1])[...]
            extra_out[-1][...] = x
        else:
            x = refs[0][...]
        h_ref[...] = _norm_mod(x, g_ref[...], mod_ref, 0).astype(BF16)
        for c0 in range(0, n_out, chunk):
            cols = slice(c0, c0 + chunk)
            res_ref[:, cols] = _dot(h_ref[...], wbf_ref[:, cols])
            if c0 >= QKV_WIDTH:
                extra_out[0][:, c0 - QKV_WIDTH:c0 - QKV_WIDTH + chunk] = res_ref[:, cols]
                continue
            r = res_ref[:, cols]
            if rope and not ctx and c0 < v_col:
                cos_ref, sin_ref = refs[n_x + 3:n_x + 5]
                r = _rope(r, cos_ref[:, :chunk], sin_ref[:, :chunk])
            if c0 < q_width:
                r = r * Q_SCALE
            qkv_ref[:, cols] = r.astype(BF16)
            if ctx and c0 >= k_col:
                c_ref, col0 = (ck_ref, k_col) if c0 < v_col else (cv_ref, v_col)
                for req in range(tm // SEQ):
                    for hc in range(chunk // HEAD_DIM):
                        hd = (c0 - col0) // HEAD_DIM + hc
                        c_ref[req, hd] = res_ref[
                            req * SEQ:(req + 1) * SEQ,
                            c0 + hc * HEAD_DIM:c0 + (hc + 1) * HEAD_DIM].T

    pl.when(is_ctx)(lambda: tile(True))
    pl.when(jnp.logical_not(is_ctx))(lambda: tile(False))


def _inproj(xs, g_all, mod_l, layer, w_all, w_idx, n_heads, q_width, n_slots, prev_caches,
            rope_tables=None):
    tm = 512
    n_out = w_all.shape[2]
    n_ctx_tiles = N_CTX_TOK // tm
    split_x = len(xs) == 2
    rope = rope_tables is not None
    has_u = n_out > QKV_WIDTH
    cache_shape = jax.ShapeDtypeStruct((BATCH, n_slots, n_heads, HEAD_DIM, SEQ), F32)
    cache_spec = pl.BlockSpec(
        (tm // SEQ, None, n_heads, HEAD_DIM, SEQ),
        lambda i: (jnp.minimum(i, n_ctx_tiles - 1), w_idx, 0, 0, 0))
    row_spec = pl.BlockSpec((tm, D_MODEL), lambda i: (i, 0))
    if split_x:
        x_specs = [
            pl.BlockSpec((tm, D_MODEL), lambda i: (jnp.minimum(i, n_ctx_tiles - 1), 0)),
            pl.BlockSpec((tm, D_MODEL), lambda i: (jnp.maximum(i - n_ctx_tiles, 0), 0)),
        ]
    else:
        x_specs = [row_spec]
    out_shape = [jax.ShapeDtypeStruct((N_TOK, QKV_WIDTH), BF16), cache_shape, cache_shape]
    out_specs = [pl.BlockSpec((tm, QKV_WIDTH), lambda i: (i, 0)), cache_spec, cache_spec]
    if has_u:
        out_shape.append(jax.ShapeDtypeStruct((N_TOK, n_out - QKV_WIDTH), F32))
        out_specs.append(pl.BlockSpec((tm, n_out - QKV_WIDTH), lambda i: (i, 0)))
    if split_x:
        out_shape.append(jax.ShapeDtypeStruct((N_TOK, D_MODEL), F32))
        out_specs.append(row_spec)
    rope_in, rope_specs = [], []
    if rope:
        tiles_per_seq = DEC_SEQ // tm
        rope_spec = pl.BlockSpec(
            (tm, D_MODEL), lambda i: (jnp.maximum(i - n_ctx_tiles, 0) % tiles_per_seq, 0))
        rope_in, rope_specs = list(rope_tables), [rope_spec, rope_spec]
    n_prev = len(prev_caches)
    n_before = len(xs) + 3 + len(rope_in)
    return pl.pallas_call(
        functools.partial(_inproj_kernel, tm=tm, n_heads=n_heads, q_width=q_width,
                          n_prev=n_prev, split_x=split_x, rope=rope, has_u=has_u),
        out_shape=out_shape,
        grid=(N_TOK // tm,),
        in_specs=x_specs + [
            pl.BlockSpec((None, 1, D_MODEL), lambda i: (layer, 0, 0)),
            pl.BlockSpec((None, 1, 6 * D_MODEL), lambda i: (_group_of_tile(i, tm), 0, 0)),
            pl.BlockSpec((None, D_MODEL, n_out), lambda i: (w_idx, 0, 0),
                         pipeline_mode=pl.Buffered(1)),
        ] + rope_specs + [pl.BlockSpec(memory_space=pl.ANY)] * n_prev,
        out_specs=out_specs,
        scratch_shapes=[pltpu.VMEM((D_MODEL, n_out), BF16), pltpu.VMEM((tm, D_MODEL), BF16),
                        pltpu.VMEM((tm, n_out), F32)],
        input_output_aliases={n_before + k: 1 + k for k in range(n_prev)},
        compiler_params=_params(1),
        name="inproj",
    )(*xs, g_all.reshape(DEPTH, 1, D_MODEL), mod_l, w_all, *rope_in, *prev_caches)


CTX_A_REQS_PER_STEP = 1


def _ctx_attn_a_kernel(p_ref, o_ref):
    for req in range(CTX_A_REQS_PER_STEP):
        rows = slice(req * SEQ, (req + 1) * SEQ)
        for h in range(A_HEADS):
            c0 = h * HEAD_DIM
            q = p_ref[rows, c0:c0 + HEAD_DIM]
            k = p_ref[rows, A_WIDTH + c0:A_WIDTH + c0 + HEAD_DIM]
            v = p_ref[rows, 2 * A_WIDTH + c0:2 * A_WIDTH + c0 + HEAD_DIM]
            s = _dot_nt(q, k)
            m = jnp.max(s, axis=-1, keepdims=True)
            p = jnp.exp(s - m)
            l = jnp.sum(p, axis=-1, keepdims=True)
            o_ref[rows, c0:c0 + HEAD_DIM] = _dot(p.astype(BF16), v) / l


def _ctx_attn_a(qkv):
    return pl.pallas_call(
        _ctx_attn_a_kernel,
        out_shape=jax.ShapeDtypeStruct((N_TOK, D_MODEL), F32),
        grid=(BATCH // CTX_A_REQS_PER_STEP,),
        in_specs=[pl.BlockSpec((CTX_A_REQS_PER_STEP * SEQ, QKV_WIDTH), lambda b: (b, 0))],
        out_specs=pl.BlockSpec((CTX_A_REQS_PER_STEP * SEQ, A_WIDTH), lambda b: (b, 0)),
        compiler_params=_params(1),
        name="ctx_attn_a",
    )(qkv)


N_DROW = 2 * NA_ROWS - 1
N_DCOL = 2 * NA_COLS - 1
N_BIAS_TILES = 16
BIAS_TILE_LEFT_PAD = 14
BIAS_TILE_RIGHT_PAD = 15
MID_DROW = NA_ROWS - 1 - NA_ROWS // 2


def _write_bias_tiles(rpb_ref, base, o_ref):
    qi = lax.broadcasted_iota(jnp.int32, (GRID_W, 2 * GRID_W), 0)
    lane = lax.broadcasted_iota(jnp.int32, (GRID_W, 2 * GRID_W), 1)
    right = lane >= GRID_W
    kc = jnp.where(right, lane - GRID_W, lane)
    rel = kc - qi + (NA_COLS - 1)
    qstart = jnp.clip(qi - NA_COLS // 2, 0, GRID_W - NA_COLS)
    valid = (kc >= qstart) & (kc < qstart + NA_COLS)

    rows = [jnp.zeros((GRID_W, 2 * GRID_W), F32)] * N_DROW
    for d in range(N_DCOL):
        hit = rel == d
        rows = [jnp.where(hit, rpb_ref[base + dr * N_DCOL + d], rows[dr])
                for dr in range(N_DROW)]

    for t in range(N_DROW - 1):
        o_ref[t] = jnp.where(valid, jnp.where(right, rows[t + 1], rows[t]), NEG_INF)
    o_ref[BIAS_TILE_LEFT_PAD] = jnp.where(valid & right, rows[MID_DROW], NEG_INF)
    o_ref[BIAS_TILE_RIGHT_PAD] = jnp.where(valid & jnp.logical_not(right),
                                           rows[MID_DROW + NA_ROWS - 1], NEG_INF)


def _na_window(r):
    start = min(max(r - NA_ROWS // 2, 0), GRID_ROWS - NA_ROWS)
    first_drow = start - r + NA_ROWS - 1
    if start % 2 == 0:
        return start, [first_drow + 2 * p for p in range(NA_ROWS // 2)]
    assert first_drow == MID_DROW
    inner = [first_drow + 1 + 2 * p for p in range(NA_ROWS // 2 - 1)]
    return start - 1, [BIAS_TILE_LEFT_PAD] + inner + [BIAS_TILE_RIGHT_PAD]


def _lat_attn_a_kernel(q_ref, k_ref, v_ref, kc_ref, vc_ref, bias_ref, mix_ref, o_ref,
                       s_loc, s_ctx, p_loc, p_ctx):
    del mix_ref
    hp = pl.program_id(1)

    @pl.when((pl.program_id(0) == 0) & (hp == 0))
    def _():
        p_loc[...] = jnp.zeros_like(p_loc)

    for hh in range(2):
        c0 = hh * HEAD_DIM
        q = q_ref[:, c0:c0 + HEAD_DIM]
        k = k_ref[:, c0:c0 + HEAD_DIM]
        v = v_ref[:, c0:c0 + HEAD_DIM]
        kc_t = kc_ref[hh].astype(BF16)
        vc_t = vc_ref[hh].astype(BF16)
        s_loc[...] = _dot_nt(q, k)
        s_ctx[...] = _dot(q, kc_t)
        inv_l = []
        for r in range(GRID_ROWS):
            rows = slice(r * GRID_W, (r + 1) * GRID_W)
            first_row, tiles = _na_window(r)
            cols = slice(first_row * GRID_W, (first_row + 2 * len(tiles)) * GRID_W)
            bias = jnp.concatenate([bias_ref[hh, t] for t in tiles], axis=1)
            sc = s_ctx[rows, :]
            sl = s_loc[rows, cols] + bias
            m = jnp.maximum(jnp.max(sc, axis=-1, keepdims=True),
                            jnp.max(sl, axis=-1, keepdims=True))
            pc = jnp.exp(sc - m)
            pw = jnp.exp(sl - m)
            inv_l.append(1.0 / (jnp.sum(pc, axis=-1, keepdims=True)
                                + jnp.sum(pw, axis=-1, keepdims=True)))
            p_ctx[rows, :] = pc.astype(BF16)
            p_loc[rows, cols] = pw.astype(BF16)
        o = _dot_nt(p_ctx[...], vc_t) + _dot(p_loc[...], v)
        o_ref[:, c0:c0 + HEAD_DIM] = o * jnp.concatenate(inv_l, axis=0)


def _lat_attn_a(qkv, cache_k, cache_v, bias_tiles, i_layer, mix):
    row0 = N_CTX_TOK // DEC_SEQ
    pair = 2 * HEAD_DIM
    k_col0 = A_WIDTH // pair
    v_col0 = 2 * A_WIDTH // pair
    return pl.pallas_call(
        _lat_attn_a_kernel,
        out_shape=jax.ShapeDtypeStruct((N_TOK, D_MODEL), F32),
        grid=(DEC_BATCH, A_HEADS // 2),
        in_specs=[
            pl.BlockSpec((DEC_SEQ, pair), lambda b, hp: (row0 + b, hp)),
            pl.BlockSpec((DEC_SEQ, pair), lambda b, hp: (row0 + b, k_col0 + hp)),
            pl.BlockSpec((DEC_SEQ, pair), lambda b, hp: (row0 + b, v_col0 + hp)),
            pl.BlockSpec((None, None, 2, HEAD_DIM, PAST_LEN),
                         lambda b, hp: (b, i_layer, hp, 0, 0)),
            pl.BlockSpec((None, None, 2, HEAD_DIM, PAST_LEN),
                         lambda b, hp: (b, i_layer, hp, 0, 0)),
            pl.BlockSpec((None, 2, N_BIAS_TILES, GRID_W, 2 * GRID_W),
                         lambda b, hp: (i_layer, hp, 0, 0, 0)),
            pl.BlockSpec(memory_space=pl.ANY),
        ],
        out_specs=pl.BlockSpec((DEC_SEQ, pair), lambda b, hp: (row0 + b, hp)),
        scratch_shapes=[pltpu.VMEM((DEC_SEQ, DEC_SEQ), F32),
                        pltpu.VMEM((DEC_SEQ, PAST_LEN), F32),
                        pltpu.VMEM((DEC_SEQ, DEC_SEQ), BF16),
                        pltpu.VMEM((DEC_SEQ, PAST_LEN), BF16)],
        input_output_aliases={6: 0},
        compiler_params=_params(2),
        name="lat_attn_a",
    )(qkv, qkv, qkv, cache_k, cache_v, bias_tiles, mix)


def _pool_kernel(u_ref, wp_ref, ps_ref, mix_ref, o_ref, *, rows):
    del mix_ref
    n = jnp.where(pl.program_id(0) < N_CTX_TOK // rows, SEQ, DEC_SEQ)
    t = lax.broadcasted_iota(jnp.int32, (rows, B_GROUP_DIM), 0) & (n - 1)

    def earlier(x, k):
        return jnp.where(t >= k, pltpu.roll(x, k, axis=0), 0.0)

    def later(x, k):
        return jnp.where(t < n - k, pltpu.roll(x, rows - k, axis=0), 0.0)

    for g, w in enumerate(POOL_WINDOWS):
        c0 = g * B_GROUP_DIM
        half = w // 2
        u = u_ref[:, c0:c0 + B_GROUP_DIM]
        before, after = u, u
        k = 1
        while k < half:
            before = before + earlier(before, k)
            after = after + later(after, k)
            k *= 2
        total = earlier(before, 1) + after
        count = (jnp.minimum(t + half, n) - jnp.maximum(t - half, 0)).astype(F32)
        pooled = total / count - u
        y = _dot(pooled.astype(BF16), wp_ref[g].astype(BF16))
        o_ref[:, c0:c0 + B_GROUP_DIM] = y * ps_ref[:, c0:c0 + B_GROUP_DIM]


def _pool(u, w_pool, pool_scale, i_layer, mix):
    n_groups = len(POOL_WINDOWS)
    rows = DEC_SEQ
    return pl.pallas_call(
        functools.partial(_pool_kernel, rows=rows),
        out_shape=jax.ShapeDtypeStruct((N_TOK, D_MODEL), F32),
        grid=(N_TOK // rows,),
        in_specs=[
            pl.BlockSpec((rows, B_WIDTH), lambda b: (b, 0)),
            pl.BlockSpec((None, n_groups, B_GROUP_DIM, B_GROUP_DIM),
                         lambda b: (i_layer, 0, 0, 0)),
            pl.BlockSpec((None, 1, B_WIDTH), lambda b: (i_layer, 0, 0)),
            pl.BlockSpec(memory_space=pl.ANY),
        ],
        out_specs=pl.BlockSpec((rows, B_WIDTH), lambda b: (b, 1)),
        input_output_aliases={3: 0},
        compiler_params=_params(1),
        name="pool_mixer",
    )(u, w_pool, pool_scale.reshape(-1, 1, B_WIDTH), mix)


def _sink_row(sink_ref, head0, queries_per_head):
    lane = lax.broadcasted_iota(jnp.int32, (1, C_GROUP * queries_per_head), 1)
    row = jnp.full((1, C_GROUP * queries_per_head), sink_ref[head0], F32)
    for g in range(1, C_GROUP):
        row = jnp.where(lane >= g * queries_per_head, sink_ref[head0 + g], row)
    return row


CTX_C_REQS_PER_STEP = 2


def _ctx_attn_c_kernel(sink_ref, p_ref, o_ref, *, sink0):
    for req in range(CTX_C_REQS_PER_STEP):
        rows = slice(req * SEQ, (req + 1) * SEQ)
        outs_t = []
        for kk in range(C_KV_HEADS):
            heads = [kk * C_GROUP + g for g in range(C_GROUP)]
            q = jnp.concatenate(
                [p_ref[rows, h * HEAD_DIM:(h + 1) * HEAD_DIM] for h in heads], axis=0)
            k0 = C_Q_WIDTH + kk * HEAD_DIM
            v0 = C_Q_WIDTH + C_KV_WIDTH + kk * HEAD_DIM
            k = p_ref[rows, k0:k0 + HEAD_DIM]
            v = p_ref[rows, v0:v0 + HEAD_DIM]
            o_t = _softmax_pv_t([_dot_nt(k, q)],
                                _sink_row(sink_ref, sink0 + kk * C_GROUP, SEQ),
                                [v])
            outs_t += [o_t[:, g * SEQ:(g + 1) * SEQ] for g in range(C_GROUP)]
        o_ref[rows, :] = jnp.concatenate(outs_t, axis=0).T


def _ctx_attn_c(qkv, sink_all, j_layer):
    return pl.pallas_call(
        functools.partial(_ctx_attn_c_kernel, sink0=j_layer * C_HEADS),
        out_shape=jax.ShapeDtypeStruct((N_TOK, D_MODEL), F32),
        grid=(BATCH // CTX_C_REQS_PER_STEP,),
        in_specs=[
            pl.BlockSpec(memory_space=pltpu.SMEM),
            pl.BlockSpec((CTX_C_REQS_PER_STEP * SEQ, QKV_WIDTH), lambda b: (b, 0)),
        ],
        out_specs=pl.BlockSpec((CTX_C_REQS_PER_STEP * SEQ, C_Q_WIDTH), lambda b: (b, 0)),
        compiler_params=_params(1),
        name="ctx_attn_c",
    )(sink_all.reshape(-1), qkv)


def _lat_attn_c_kernel(sink_ref, q_ref, k_ref, v_ref, kc_ref, vc_ref, mix_ref, o_ref,
                       kctx_ref, vctx_ref, *, sink0):
    del mix_ref
    j = pl.program_id(1)
    n_blocks = DEC_SEQ // C_BLOCK

    @pl.when(j == 0)
    def _():
        for kk in range(C_KV_HEADS):
            c0 = kk * HEAD_DIM
            kctx_ref[:, c0:c0 + HEAD_DIM] = kc_ref[kk].T.astype(BF16)
            vctx_ref[:, c0:c0 + HEAD_DIM] = vc_ref[kk].T.astype(BF16)

    n_q = C_GROUP * C_BLOCK
    qi = lax.broadcasted_iota(jnp.int32, (C_BLOCK, n_q), 1) % C_BLOCK
    jl = lax.broadcasted_iota(jnp.int32, (C_BLOCK, n_q), 0)
    valid_prev = (jl >= qi) & (j > 0)
    valid_next = (jl <= qi) & (j < n_blocks - 1)
    rows_prev = pl.ds(pl.multiple_of(jnp.maximum(j - 1, 0) * C_BLOCK, C_BLOCK), C_BLOCK)
    rows_cur = pl.ds(pl.multiple_of(j * C_BLOCK, C_BLOCK), C_BLOCK)
    rows_next = pl.ds(pl.multiple_of(jnp.minimum(j + 1, n_blocks - 1) * C_BLOCK, C_BLOCK),
                      C_BLOCK)

    outs_t = []
    for kk in range(C_KV_HEADS):
        heads = [kk * C_GROUP + g for g in range(C_GROUP)]
        qs = jnp.concatenate(
            [q_ref[:, h * HEAD_DIM:(h + 1) * HEAD_DIM] for h in heads], axis=0)
        c0 = kk * HEAD_DIM
        cols = slice(c0, c0 + HEAD_DIM)
        k_ctx = kctx_ref[:, cols]
        v_ctx = vctx_ref[:, cols]
        k_band = jnp.concatenate(
            [k_ref[rows_prev, cols], k_ref[rows_cur, cols], k_ref[rows_next, cols]], axis=0)
        v_band = jnp.concatenate(
            [v_ref[rows_prev, cols], v_ref[rows_cur, cols], v_ref[rows_next, cols]], axis=0)
        s_ctx = _dot_nt(k_ctx, qs)
        s_raw = _dot_nt(k_band, qs)
        s_band = jnp.concatenate(
            [jnp.where(valid_prev, s_raw[:C_BLOCK], NEG_INF),
             s_raw[C_BLOCK:2 * C_BLOCK],
             jnp.where(valid_next, s_raw[2 * C_BLOCK:], NEG_INF)], axis=0)
        o_t = _softmax_pv_t([s_ctx, s_band],
                            _sink_row(sink_ref, sink0 + kk * C_GROUP, C_BLOCK),
                            [v_ctx, v_band])
        outs_t += [o_t[:, g * C_BLOCK:(g + 1) * C_BLOCK] for g in range(C_GROUP)]
    o_ref[...] = jnp.concatenate(outs_t, axis=0).T


def _lat_attn_c(qkv, cache_k, cache_v, sink, j_layer, mix):
    n_blocks = DEC_SEQ // C_BLOCK
    q_row0 = N_CTX_TOK // C_BLOCK
    kv_row0 = N_CTX_TOK // DEC_SEQ
    k_col = C_Q_WIDTH // C_KV_WIDTH
    return pl.pallas_call(
        functools.partial(_lat_attn_c_kernel, sink0=j_layer * C_HEADS),
        out_shape=jax.ShapeDtypeStruct((N_TOK, D_MODEL), F32),
        grid=(DEC_BATCH, n_blocks),
        in_specs=[
            pl.BlockSpec(memory_space=pltpu.SMEM),
            pl.BlockSpec((C_BLOCK, C_Q_WIDTH), lambda b, j: (q_row0 + b * n_blocks + j, 0)),
            pl.BlockSpec((DEC_SEQ, C_KV_WIDTH), lambda b, j: (kv_row0 + b, k_col)),
            pl.BlockSpec((DEC_SEQ, C_KV_WIDTH), lambda b, j: (kv_row0 + b, k_col + 1)),
            pl.BlockSpec((None, None, C_KV_HEADS, HEAD_DIM, PAST_LEN),
                         lambda b, j: (b, j_layer, 0, 0, 0)),
            pl.BlockSpec((None, None, C_KV_HEADS, HEAD_DIM, PAST_LEN),
                         lambda b, j: (b, j_layer, 0, 0, 0)),
            pl.BlockSpec(memory_space=pl.ANY),
        ],
        out_specs=pl.BlockSpec((C_BLOCK, C_Q_WIDTH),
                               lambda b, j: (q_row0 + b * n_blocks + j, 0)),
        input_output_aliases={6: 0},
        scratch_shapes=[pltpu.VMEM((PAST_LEN, C_KV_WIDTH), BF16),
                        pltpu.VMEM((PAST_LEN, C_KV_WIDTH), BF16)],
        compiler_params=_params(2),
        name="lat_attn_c",
    )(sink.reshape(-1), qkv, qkv, qkv, cache_k, cache_v, mix)


def _rope_tables():
    t = np.arange(DEC_SEQ)
    pos = np.stack([t // GRID_W, t % GRID_W], axis=-1).astype(np.float64)
    half = HEAD_DIM // 4
    inv = ROPE_BASE ** (-np.arange(half, dtype=np.float64) / half)
    ang = pos[:, :, None] * inv
    cos = np.cos(ang)
    sin = np.sin(ang)
    cos64 = np.stack([cos, cos], axis=2).reshape(DEC_SEQ, HEAD_DIM)
    sin64 = np.stack([-sin, sin], axis=2).reshape(DEC_SEQ, HEAD_DIM)
    return (jnp.asarray(np.tile(cos64, (1, C_HEADS)), F32),
            jnp.asarray(np.tile(sin64, (1, C_HEADS)), F32))


FFN_CHUNK = 256
N_FFN_CHUNKS = FFN_HIDDEN // FFN_CHUNK
N_WO_PIECES = D_MODEL // FFN_CHUNK


def _post_mixer_kernel(*refs, layer, wo_idx, tm, final):
    x_ref, mix_ref, g_ref, mod_ref = refs[:4]
    n_in = 5 if final else 7
    wo_hbm, wgu_hbm, wd_hbm = refs[n_in:n_in + 3]
    n_out = 2
    out_refs = refs[n_in + 3:n_in + 3 + n_out]
    if not final:
        out_refs[1][...] = _modulation_block(*refs[4:7])
    (wo_s, wg_s, wu_s, wd_s, act_s, stage_col, stage_row, sem_col,
     sem_row) = refs[n_in + 3 + n_out:]
    n_row_pieces = N_WO_PIECES + N_FFN_CHUNKS

    def col_copy(which, c):
        src = wgu_hbm.at[layer, :, pl.ds(which * FFN_HIDDEN + c * FFN_CHUNK, FFN_CHUNK)]
        return pltpu.make_async_copy(src, stage_col.at[which, c % 2], sem_col.at[which, c % 2])

    def row_copy(p):
        if p < N_WO_PIECES:
            src = wo_hbm.at[wo_idx, pl.ds(p * FFN_CHUNK, FFN_CHUNK), :]
        else:
            src = wd_hbm.at[layer, pl.ds((p - N_WO_PIECES) * FFN_CHUNK, FFN_CHUNK), :]
        return pltpu.make_async_copy(src, stage_row.at[p % 2], sem_row.at[p % 2])

    def take_row_piece(p, dst_ref, row0):
        if p + 1 < n_row_pieces:
            row_copy(p + 1).start()
        row_copy(p).wait()
        dst_ref[row0:row0 + FFN_CHUNK, :] = stage_row[p % 2].astype(BF16)

    def tile(load_weights):
        if load_weights:
            row_copy(0).start()
            for which in range(2):
                col_copy(which, 0).start()
            for p in range(N_WO_PIECES):
                take_row_piece(p, wo_s, p * FFN_CHUNK)
        gate1 = mod_ref[:, 2 * D_MODEL:3 * D_MODEL]
        x1 = x_ref[...] + gate1 * _dot(mix_ref[...].astype(BF16), wo_s[...])
        h = _norm_mod(x1, g_ref[...], mod_ref, 3).astype(BF16)
        for c in range(N_FFN_CHUNKS):
            if load_weights:
                for which, dst in ((0, wg_s), (1, wu_s)):
                    if c + 1 < N_FFN_CHUNKS:
                        col_copy(which, c + 1).start()
                    col_copy(which, c).wait()
                    dst[c] = stage_col[which, c % 2].astype(BF16)
            gate = _dot(h, wg_s[c])
            up = _dot(h, wu_s[c])
            act = gate / (1.0 + jnp.exp(-gate)) * up
            act_s[:, c * FFN_CHUNK:(c + 1) * FFN_CHUNK] = act.astype(BF16)
            if load_weights:
                take_row_piece(N_WO_PIECES + c, wd_s, c * FFN_CHUNK)
        gate2 = mod_ref[:, 5 * D_MODEL:6 * D_MODEL]
        return x1 + gate2 * _dot(act_s[...], wd_s[...])

    def emit(x2):
        if not final:
            out_refs[0][...] = x2
            return
        var = jnp.mean(x2 * x2, axis=-1, keepdims=True)
        y = x2 * lax.rsqrt(var + EPS) * refs[4][...]
        is_ctx = pl.program_id(0) < N_CTX_TOK // tm

        @pl.when(is_ctx)
        def _():
            out_refs[0][...] = y

        @pl.when(jnp.logical_not(is_ctx))
        def _():
            out_refs[1][...] = y

    first = pl.program_id(0) == 0

    @pl.when(first)
    def _():
        emit(tile(True))

    @pl.when(jnp.logical_not(first))
    def _():
        emit(tile(False))


def _post_mixer(x, mix, g_all, mod_l, layer, w_out_all, wo_idx, w_gate_up, w_down,
                norm_final=None, next_mod_inputs=None):
    tm = 512
    n_tiles = N_TOK // tm
    n_ctx_tiles = N_CTX_TOK // tm
    final = norm_final is not None
    row_spec = pl.BlockSpec((tm, D_MODEL), lambda i: (i, 0))
    hbm = pl.BlockSpec(memory_space=pl.ANY)
    if final:
        extra_in = [norm_final.reshape(1, D_MODEL)]
        extra_specs = [pl.BlockSpec((1, D_MODEL), lambda i: (0, 0))]
        out_shape = (jax.ShapeDtypeStruct((N_CTX_TOK, D_MODEL), F32),
                     jax.ShapeDtypeStruct((N_LAT_TOK, D_MODEL), F32))
        out_specs = (
            pl.BlockSpec((tm, D_MODEL), lambda i: (jnp.minimum(i, n_ctx_tiles - 1), 0)),
            pl.BlockSpec((tm, D_MODEL), lambda i: (jnp.maximum(i - n_ctx_tiles, 0), 0)))
    else:
        tn = 6 * D_MODEL // n_tiles
        extra_in = list(next_mod_inputs)
        extra_specs = [
            pl.BlockSpec((N_GROUPS_PAD, D_MODEL), lambda i: (0, 0)),
            pl.BlockSpec((None, D_MODEL, tn), lambda i: (layer + 1, 0, i)),
            pl.BlockSpec((None, 1, tn), lambda i: (layer + 1, 0, i)),
        ]
        out_shape = (jax.ShapeDtypeStruct((N_TOK, D_MODEL), F32),
                     jax.ShapeDtypeStruct((N_GROUPS_PAD, 6 * D_MODEL), F32))
        out_specs = (row_spec, pl.BlockSpec((N_GROUPS_PAD, tn), lambda i: (0, i)))
    return pl.pallas_call(
        functools.partial(_post_mixer_kernel, layer=layer, wo_idx=wo_idx, tm=tm, final=final),
        out_shape=out_shape,
        grid=(N_TOK // tm,),
        in_specs=[
            row_spec,
            row_spec,
            pl.BlockSpec((None, 1, D_MODEL), lambda i: (layer, 0, 0)),
            pl.BlockSpec((None, 1, 6 * D_MODEL), lambda i: (_group_of_tile(i, tm), 0, 0)),
        ] + extra_specs + [hbm, hbm, hbm],
        out_specs=out_specs,
        scratch_shapes=[
            pltpu.VMEM((D_MODEL, D_MODEL), BF16),
            pltpu.VMEM((N_FFN_CHUNKS, D_MODEL, FFN_CHUNK), BF16),
            pltpu.VMEM((N_FFN_CHUNKS, D_MODEL, FFN_CHUNK), BF16),
            pltpu.VMEM((FFN_HIDDEN, D_MODEL), BF16),
            pltpu.VMEM((tm, FFN_HIDDEN), BF16),
            pltpu.VMEM((2, 2, D_MODEL, FFN_CHUNK), F32),
            pltpu.VMEM((2, FFN_CHUNK, D_MODEL), F32),
            pltpu.SemaphoreType.DMA((2, 2)),
            pltpu.SemaphoreType.DMA((2,)),
        ],
        compiler_params=_params(1),
        name="post_mixer",
    )(x, mix, g_all.reshape(DEPTH, 1, D_MODEL), mod_l, *extra_in, w_out_all, w_gate_up, w_down)


def kernel(x_prompt, x_sample, cache_a_k, cache_a_v, cache_c_k, cache_c_v, c, c_ctx, w_mod, b_mod, norm_mix, norm_ffn, w_in_ab, rpb_a, w_pool, pool_scale, w_out_ab, w_in_c, sink_c, w_out_c, w_gate_up, w_down, norm_final):
    xs = (x_prompt.reshape(N_CTX_TOK, D_MODEL), x_sample.reshape(N_LAT_TOK, D_MODEL))
    cond8 = jnp.concatenate(
        [c_ctx[None], c, jnp.zeros((N_GROUPS_PAD - 1 - DEC_BATCH, D_MODEL), F32)], axis=0)
    b_mod3 = b_mod.reshape(DEPTH, 1, 6 * D_MODEL)
    mod, bias_tiles = _modulation0_and_bias(cond8, w_mod, b_mod3, rpb_a)

    n_ab = cache_a_k.shape[1]
    n_c = cache_c_k.shape[1]
    cache_a_k, cache_a_v, cache_c_k, cache_c_v = (
        jnp.transpose(t, (0, 1, 3, 4, 2)) for t in (cache_a_k, cache_a_v, cache_c_k, cache_c_v))
    rope_tables = _rope_tables()

    new_a = []
    new_c = []
    for l in range(DEPTH):
        mod_l = mod.reshape(N_GROUPS_PAD, 1, 6 * D_MODEL)
        if l % 2 == 0:
            i = l // 2
            qkv, *rest = _inproj(xs, norm_mix, mod_l, l, w_in_ab, i, A_HEADS, A_WIDTH, n_ab,
                                 new_a)
            new_a, u = rest[:2], rest[2]
            if len(xs) == 2:
                xs = (rest[3],)
            mix = _ctx_attn_a(qkv)
            mix = _lat_attn_a(qkv, cache_a_k, cache_a_v, bias_tiles, i, mix)
            mix = _pool(u, w_pool, pool_scale, i, mix)
            w_out, wo_idx = w_out_ab, i
        else:
            j = l // 2
            qkv, *new_c = _inproj(xs, norm_mix, mod_l, l, w_in_c, j, C_KV_HEADS, C_Q_WIDTH, n_c,
                                  new_c, rope_tables)
            mix = _ctx_attn_c(qkv, sink_c, j)
            mix = _lat_attn_c(qkv, cache_c_k, cache_c_v, sink_c, j, mix)
            w_out, wo_idx = w_out_c, j
        if l + 1 < DEPTH:
            x, mod = _post_mixer(xs[0], mix, norm_ffn, mod_l, l, w_out, wo_idx, w_gate_up,
                                 w_down, next_mod_inputs=(cond8, w_mod, b_mod3))
            xs = (x,)
        else:
            y_ctx, y_lat = _post_mixer(xs[0], mix, norm_ffn, mod_l, l, w_out, wo_idx, w_gate_up,
                                       w_down, norm_final)

    new_caches = [jnp.transpose(t, (0, 1, 4, 2, 3)) for t in (*new_a, *new_c)]
    return (y_ctx.reshape(BATCH, SEQ, D_MODEL), y_lat.reshape(DEC_BATCH, DEC_SEQ, D_MODEL),
            *new_caches)
```

```python
import functools

import jax
import jax.numpy as jnp
import numpy as np
from jax import lax
from jax.experimental import pallas as pl
from jax.experimental.pallas import tpu as pltpu

D_MODEL = 1024
BATCH = 16
SEQ = 256
DEPTH = 4
DEC_BATCH = 2
DEC_SEQ = 1024
PAST_LEN = 512
GRID_W = 64
HEAD_DIM = 64
A_WIDTH = 512
A_HEADS = 8
B_WIDTH = 512
POOL_WINDOWS = (2, 4, 8, 16)
B_GROUP_DIM = 128
NA_ROWS = 8
NA_COLS = 16
C_HEADS = 16
C_KV_HEADS = 4
C_GROUP = C_HEADS // C_KV_HEADS
C_Q_WIDTH = 1024
C_KV_WIDTH = 256
C_BLOCK = 128
FFN_HIDDEN = 2816
ROPE_BASE = 10000.0
EPS = 1e-6
NEG_INF = -1e30

N_CTX_TOK = BATCH * SEQ
N_LAT_TOK = DEC_BATCH * DEC_SEQ
N_TOK = N_CTX_TOK + N_LAT_TOK
GRID_ROWS = DEC_SEQ // GRID_W
N_GROUPS_PAD = 8

VMEM_LIMIT = 56 * 1024 * 1024

F32 = jnp.float32
BF16 = jnp.bfloat16


def _params(n_axes):
    return pltpu.CompilerParams(dimension_semantics=("arbitrary",) * n_axes,
                                vmem_limit_bytes=VMEM_LIMIT)


def _group_of_tile(i, tm):
    row0 = i * tm
    return jnp.where(row0 < N_CTX_TOK, 0, 1 + (row0 - N_CTX_TOK) // DEC_SEQ)


def _dot_nt(a, b):
    return lax.dot_general(a, b, (((1,), (1,)), ((), ())), preferred_element_type=F32)


def _dot(a, b):
    return jnp.dot(a, b, preferred_element_type=F32)


def _dot_tn(a, b):
    return lax.dot_general(a, b, (((0,), (0,)), ((), ())), preferred_element_type=F32)


def _softmax_pv_t(scores_t, sink_row, values):
    m = sink_row
    for s in scores_t:
        ms = jnp.max(s, axis=0, keepdims=True)
        m = ms if m is None else jnp.maximum(m, ms)
    l = None if sink_row is None else jnp.exp(sink_row - m)
    o = None
    for s, v in zip(scores_t, values):
        p = jnp.exp(s - m)
        ls = jnp.sum(p, axis=0, keepdims=True)
        l = ls if l is None else l + ls
        pv = _dot_tn(v, p.astype(BF16))
        o = pv if o is None else o + pv
    return o * (1.0 / l)


def _modulation_block(cond_ref, w_ref, b_ref):
    c = cond_ref[...]
    s = c / (1.0 + jnp.exp(-c))
    return _dot(s.astype(BF16), w_ref[...].astype(BF16)) + b_ref[...]


def _mod_bias_kernel(rpb_ref, cond_ref, w_ref, b_ref, o_ref, bias_ref, *, heads_per_step):
    o_ref[...] = _modulation_block(cond_ref, w_ref, b_ref)
    pair0 = pl.program_id(0) * heads_per_step
    for t in range(heads_per_step):
        _write_bias_tiles(rpb_ref, (pair0 + t) * (N_DROW * N_DCOL), bias_ref.at[t])


def _modulation0_and_bias(cond8, w_mod, b_mod3, rpb_a):
    n_col_blocks = 4
    tn = 6 * D_MODEL // n_col_blocks
    n_bias_layers = rpb_a.shape[0]
    heads_per_step = n_bias_layers * A_HEADS // n_col_blocks
    steps_per_layer = A_HEADS // heads_per_step
    return pl.pallas_call(
        functools.partial(_mod_bias_kernel, heads_per_step=heads_per_step),
        out_shape=(
            jax.ShapeDtypeStruct((N_GROUPS_PAD, 6 * D_MODEL), F32),
            jax.ShapeDtypeStruct(
                (n_bias_layers, A_HEADS, N_BIAS_TILES, GRID_W, 2 * GRID_W), F32)),
        grid=(n_col_blocks,),
        in_specs=[
            pl.BlockSpec(memory_space=pltpu.SMEM),
            pl.BlockSpec((N_GROUPS_PAD, D_MODEL), lambda j: (0, 0)),
            pl.BlockSpec((None, D_MODEL, tn), lambda j: (0, 0, j)),
            pl.BlockSpec((None, 1, tn), lambda j: (0, 0, j)),
        ],
        out_specs=(
            pl.BlockSpec((N_GROUPS_PAD, tn), lambda j: (0, j)),
            pl.BlockSpec((None, heads_per_step, N_BIAS_TILES, GRID_W, 2 * GRID_W),
                         lambda j: (j // steps_per_layer, j % steps_per_layer, 0, 0, 0))),
        compiler_params=_params(1),
        name="modulation_bias",
    )(rpb_a.reshape(-1), cond8, w_mod, b_mod3)


def _norm_mod(x, g, mod_ref, shift_idx):
    var = jnp.mean(x * x, axis=-1, keepdims=True)
    y = x * lax.rsqrt(var + EPS) * g
    shift = mod_ref[:, shift_idx * D_MODEL:(shift_idx + 1) * D_MODEL]
    scale = mod_ref[:, (shift_idx + 1) * D_MODEL:(shift_idx + 2) * D_MODEL]
    return y * (1.0 + scale) + shift


QKV_WIDTH = 3 * A_WIDTH
Q_SCALE = HEAD_DIM ** -0.5


def _rope(x, cos, sin_signed):
    n = x.shape[-1]
    lane = lax.broadcasted_iota(jnp.int32, x.shape, x.ndim - 1)
    first = (lane % 32) < 16
    partner = jnp.where(first, pltpu.roll(x, n - 16, axis=x.ndim - 1),
                        pltpu.roll(x, 16, axis=x.ndim - 1))
    return x * cos + partner * sin_signed


def _inproj_kernel(*refs, tm, n_heads, q_width, n_prev, split_x, rope, has_u):
    n_x = 2 if split_x else 1
    g_ref, mod_ref, w_ref = refs[n_x:n_x + 3]
    n_in = n_x + 3 + (2 if rope else 0) + n_prev
    qkv_ref, ck_ref, cv_ref = refs[n_in:n_in + 3]
    extra_out = refs[n_in + 3:-3]
    wbf_ref, h_ref, res_ref = refs[-3:]
    i = pl.program_id(0)
    is_ctx = i < N_CTX_TOK // tm
    kv_width = (QKV_WIDTH - q_width) // 2
    k_col, v_col = q_width, q_width + kv_width
    n_out = res_ref.shape[1]
    chunk = 4 * HEAD_DIM

    @pl.when(i == 0)
    def _():
        wbf_ref[...] = w_ref[...].astype(BF16)

    def tile(ctx):
        if split_x:
            x = (refs[0] if ctx else refs[1])[...]
            extra_out[-1][...] = x
        else:
            x = refs[0][...]
        h_ref[...] = _norm_mod(x, g_ref[...], mod_ref, 0).astype(BF16)
        for c0 in range(0, n_out, chunk):
            cols = slice(c0, c0 + chunk)
            res_ref[:, cols] = _dot(h_ref[...], wbf_ref[:, cols])
            if c0 >= QKV_WIDTH:
                extra_out[0][:, c0 - QKV_WIDTH:c0 - QKV_WIDTH + chunk] = res_ref[:, cols]
                continue
            r = res_ref[:, cols]
            if rope and not ctx and c0 < v_col:
                cos_ref, sin_ref = refs[n_x + 3:n_x + 5]
                r = _rope(r, cos_ref[:, :chunk], sin_ref[:, :chunk])
            if c0 < q_width:
                r = r * Q_SCALE
            qkv_ref[:, cols] = r.astype(BF16)
            if ctx and c0 >= k_col:
                c_ref, col0 = (ck_ref, k_col) if c0 < v_col else (cv_ref, v_col)
                for req in range(tm // SEQ):
                    for hc in range(chunk // HEAD_DIM):
                        hd = (c0 - col0) // HEAD_DIM + hc
                        c_ref[req, hd] = res_ref[
                            req * SEQ:(req + 1) * SEQ,
                            c0 + hc * HEAD_DIM:c0 + (hc + 1) * HEAD_DIM].T

    pl.when(is_ctx)(lambda: tile(True))
    pl.when(jnp.logical_not(is_ctx))(lambda: tile(False))


def _inproj(xs, g_all, mod_l, layer, w_all, w_idx, n_heads, q_width, n_slots, prev_caches,
            rope_tables=None):
    tm = 512
    n_out = w_all.shape[2]
    n_ctx_tiles = N_CTX_TOK // tm
    split_x = len(xs) == 2
    rope = rope_tables is not None
    has_u = n_out > QKV_WIDTH
    cache_shape = jax.ShapeDtypeStruct((BATCH, n_slots, n_heads, HEAD_DIM, SEQ), F32)
    cache_spec = pl.BlockSpec(
        (tm // SEQ, None, n_heads, HEAD_DIM, SEQ),
        lambda i: (jnp.minimum(i, n_ctx_tiles - 1), w_idx, 0, 0, 0))
    row_spec = pl.BlockSpec((tm, D_MODEL), lambda i: (i, 0))
    if split_x:
        x_specs = [
            pl.BlockSpec((tm, D_MODEL), lambda i: (jnp.minimum(i, n_ctx_tiles - 1), 0)),
            pl.BlockSpec((tm, D_MODEL), lambda i: (jnp.maximum(i - n_ctx_tiles, 0), 0)),
        ]
    else:
        x_specs = [row_spec]
    out_shape = [jax.ShapeDtypeStruct((N_TOK, QKV_WIDTH), BF16), cache_shape, cache_shape]
    out_specs = [pl.BlockSpec((tm, QKV_WIDTH), lambda i: (i, 0)), cache_spec, cache_spec]
    if has_u:
        out_shape.append(jax.ShapeDtypeStruct((N_TOK, n_out - QKV_WIDTH), F32))
        out_specs.append(pl.BlockSpec((tm, n_out - QKV_WIDTH), lambda i: (i, 0)))
    if split_x:
        out_shape.append(jax.ShapeDtypeStruct((N_TOK, D_MODEL), F32))
        out_specs.append(row_spec)
    rope_in, rope_specs = [], []
    if rope:
        tiles_per_seq = DEC_SEQ // tm
        rope_spec = pl.BlockSpec(
            (tm, D_MODEL), lambda i: (jnp.maximum(i - n_ctx_tiles, 0) % tiles_per_seq, 0))
        rope_in, rope_specs = list(rope_tables), [rope_spec, rope_spec]
    n_prev = len(prev_caches)
    n_before = len(xs) + 3 + len(rope_in)
    return pl.pallas_call(
        functools.partial(_inproj_kernel, tm=tm, n_heads=n_heads, q_width=q_width,
                          n_prev=n_prev, split_x=split_x, rope=rope, has_u=has_u),
        out_shape=out_shape,
        grid=(N_TOK // tm,),
        in_specs=x_specs + [
            pl.BlockSpec((None, 1, D_MODEL), lambda i: (layer, 0, 0)),
            pl.BlockSpec((None, 1, 6 * D_MODEL), lambda i: (_group_of_tile(i, tm), 0, 0)),
            pl.BlockSpec((None, D_MODEL, n_out), lambda i: (w_idx, 0, 0),
                         pipeline_mode=pl.Buffered(1)),
        ] + rope_specs + [pl.BlockSpec(memory_space=pl.ANY)] * n_prev,
        out_specs=out_specs,
        scratch_shapes=[pltpu.VMEM((D_MODEL, n_out), BF16), pltpu.VMEM((tm, D_MODEL), BF16),
                        pltpu.VMEM((tm, n_out), F32)],
        input_output_aliases={n_before + k: 1 + k for k in range(n_prev)},
        compiler_params=_params(1),
        name="inproj",
    )(*xs, g_all.reshape(DEPTH, 1, D_MODEL), mod_l, w_all, *rope_in, *prev_caches)


CTX_A_REQS_PER_STEP = 1


def _ctx_attn_a_kernel(p_ref, o_ref):
    for req in range(CTX_A_REQS_PER_STEP):
        rows = slice(req * SEQ, (req + 1) * SEQ)
        for h in range(A_HEADS):
            c0 = h * HEAD_DIM
            q = p_ref[rows, c0:c0 + HEAD_DIM]
            k = p_ref[rows, A_WIDTH + c0:A_WIDTH + c0 + HEAD_DIM]
            v = p_ref[rows, 2 * A_WIDTH + c0:2 * A_WIDTH + c0 + HEAD_DIM]
            s = _dot_nt(q, k)
            m = jnp.max(s, axis=-1, keepdims=True)
            p = jnp.exp(s - m)
            l = jnp.sum(p, axis=-1, keepdims=True)
            o_ref[rows, c0:c0 + HEAD_DIM] = _dot(p.astype(BF16), v) / l


def _ctx_attn_a(qkv):
    return pl.pallas_call(
        _ctx_attn_a_kernel,
        out_shape=jax.ShapeDtypeStruct((N_TOK, D_MODEL), F32),
        grid=(BATCH // CTX_A_REQS_PER_STEP,),
        in_specs=[pl.BlockSpec((CTX_A_REQS_PER_STEP * SEQ, QKV_WIDTH), lambda b: (b, 0))],
        out_specs=pl.BlockSpec((CTX_A_REQS_PER_STEP * SEQ, A_WIDTH), lambda b: (b, 0)),
        compiler_params=_params(1),
        name="ctx_attn_a",
    )(qkv)


N_DROW = 2 * NA_ROWS - 1
N_DCOL = 2 * NA_COLS - 1
N_BIAS_TILES = 16
BIAS_TILE_LEFT_PAD = 14
BIAS_TILE_RIGHT_PAD = 15
MID_DROW = NA_ROWS - 1 - NA_ROWS // 2


def _write_bias_tiles(rpb_ref, base, o_ref):
    qi = lax.broadcasted_iota(jnp.int32, (GRID_W, 2 * GRID_W), 0)
    lane = lax.broadcasted_iota(jnp.int32, (GRID_W, 2 * GRID_W), 1)
    right = lane >= GRID_W
    kc = jnp.where(right, lane - GRID_W, lane)
    qstart = jnp.clip(qi - NA_COLS // 2, 0, GRID_W - NA_COLS)
    valid = (kc >= qstart) & (kc < qstart + NA_COLS)

    offs = lax.broadcasted_iota(jnp.int32, (1, 2 * GRID_W), 1) & (GRID_W - 1)
    shift = 2 * GRID_W - (NA_COLS - 1)
    rows = []
    for dr in range(N_DROW):
        vec = jnp.zeros((1, 2 * GRID_W), F32)
        for d in range(N_DCOL):
            vec = jnp.where(offs == d, rpb_ref[base + dr * N_DCOL + d], vec)
        rows.append(pltpu.roll(jnp.broadcast_to(vec, (GRID_W, 2 * GRID_W)), shift, axis=1,
                               stride=1, stride_axis=0))

    for t in range(N_DROW - 1):
        o_ref[t] = jnp.where(valid, jnp.where(right, rows[t + 1], rows[t]), NEG_INF)
    o_ref[BIAS_TILE_LEFT_PAD] = jnp.where(valid & right, rows[MID_DROW], NEG_INF)
    o_ref[BIAS_TILE_RIGHT_PAD] = jnp.where(valid & jnp.logical_not(right),
                                           rows[MID_DROW + NA_ROWS - 1], NEG_INF)


def _na_window(r):
    start = min(max(r - NA_ROWS // 2, 0), GRID_ROWS - NA_ROWS)
    first_drow = start - r + NA_ROWS - 1
    if start % 2 == 0:
        return start, [first_drow + 2 * p for p in range(NA_ROWS // 2)]
    assert first_drow == MID_DROW
    inner = [first_drow + 1 + 2 * p for p in range(NA_ROWS // 2 - 1)]
    return start - 1, [BIAS_TILE_LEFT_PAD] + inner + [BIAS_TILE_RIGHT_PAD]


def _lat_attn_a_kernel(q_ref, k_ref, v_ref, kc_ref, vc_ref, bias_ref, mix_ref, o_ref,
                       s_loc, s_ctx, p_loc, p_ctx):
    del mix_ref
    hp = pl.program_id(1)

    @pl.when((pl.program_id(0) == 0) & (hp == 0))
    def _():
        p_loc[...] = jnp.zeros_like(p_loc)

    for hh in range(2):
        c0 = hh * HEAD_DIM
        q = q_ref[:, c0:c0 + HEAD_DIM]
        k = k_ref[:, c0:c0 + HEAD_DIM]
        v = v_ref[:, c0:c0 + HEAD_DIM]
        kc_t = kc_ref[hh].astype(BF16)
        vc_t = vc_ref[hh].astype(BF16)
        s_loc[...] = _dot_nt(q, k)
        s_ctx[...] = _dot(q, kc_t)
        inv_l = []
        for r in range(GRID_ROWS):
            rows = slice(r * GRID_W, (r + 1) * GRID_W)
            first_row, tiles = _na_window(r)
            cols = slice(first_row * GRID_W, (first_row + 2 * len(tiles)) * GRID_W)
            bias = jnp.concatenate([bias_ref[hh, t] for t in tiles], axis=1)
            sc = s_ctx[rows, :]
            sl = s_loc[rows, cols] + bias
            m = jnp.maximum(jnp.max(sc, axis=-1, keepdims=True),
                            jnp.max(sl, axis=-1, keepdims=True))
            pc = jnp.exp(sc - m)
            pw = jnp.exp(sl - m)
            inv_l.append(1.0 / (jnp.sum(pc, axis=-1, keepdims=True)
                                + jnp.sum(pw, axis=-1, keepdims=True)))
            p_ctx[rows, :] = pc.astype(BF16)
            p_loc[rows, cols] = pw.astype(BF16)
        o = _dot_nt(p_ctx[...], vc_t) + _dot(p_loc[...], v)
        o_ref[:, c0:c0 + HEAD_DIM] = o * jnp.concatenate(inv_l, axis=0)


def _lat_attn_a(qkv, cache_k, cache_v, bias_tiles, i_layer, mix):
    row0 = N_CTX_TOK // DEC_SEQ
    pair = 2 * HEAD_DIM
    k_col0 = A_WIDTH // pair
    v_col0 = 2 * A_WIDTH // pair
    return pl.pallas_call(
        _lat_attn_a_kernel,
        out_shape=jax.ShapeDtypeStruct((N_TOK, D_MODEL), F32),
        grid=(DEC_BATCH, A_HEADS // 2),
        in_specs=[
            pl.BlockSpec((DEC_SEQ, pair), lambda b, hp: (row0 + b, hp)),
            pl.BlockSpec((DEC_SEQ, pair), lambda b, hp: (row0 + b, k_col0 + hp)),
            pl.BlockSpec((DEC_SEQ, pair), lambda b, hp: (row0 + b, v_col0 + hp)),
            pl.BlockSpec((None, None, 2, HEAD_DIM, PAST_LEN),
                         lambda b, hp: (b, i_layer, hp, 0, 0)),
            pl.BlockSpec((None, None, 2, HEAD_DIM, PAST_LEN),
                         lambda b, hp: (b, i_layer, hp, 0, 0)),
            pl.BlockSpec((None, 2, N_BIAS_TILES, GRID_W, 2 * GRID_W),
                         lambda b, hp: (i_layer, hp, 0, 0, 0)),
            pl.BlockSpec(memory_space=pl.ANY),
        ],
        out_specs=pl.BlockSpec((DEC_SEQ, pair), lambda b, hp: (row0 + b, hp)),
        scratch_shapes=[pltpu.VMEM((DEC_SEQ, DEC_SEQ), F32),
                        pltpu.VMEM((DEC_SEQ, PAST_LEN), F32),
                        pltpu.VMEM((DEC_SEQ, DEC_SEQ), BF16),
                        pltpu.VMEM((DEC_SEQ, PAST_LEN), BF16)],
        input_output_aliases={6: 0},
        compiler_params=_params(2),
        name="lat_attn_a",
    )(qkv, qkv, qkv, cache_k, cache_v, bias_tiles, mix)


def _pool_kernel(u_ref, wp_ref, ps_ref, mix_ref, o_ref, *, rows):
    del mix_ref
    n = jnp.where(pl.program_id(0) < N_CTX_TOK // rows, SEQ, DEC_SEQ)
    t = lax.broadcasted_iota(jnp.int32, (rows, B_GROUP_DIM), 0) & (n - 1)

    def earlier(x, k):
        return jnp.where(t >= k, pltpu.roll(x, k, axis=0), 0.0)

    def later(x, k):
        return jnp.where(t < n - k, pltpu.roll(x, rows - k, axis=0), 0.0)

    for g, w in enumerate(POOL_WINDOWS):
        c0 = g * B_GROUP_DIM
        half = w // 2
        u = u_ref[:, c0:c0 + B_GROUP_DIM]
        before, after = u, u
        k = 1
        while k < half:
            before = before + earlier(before, k)
            after = after + later(after, k)
            k *= 2
        total = earlier(before, 1) + after
        count = (jnp.minimum(t + half, n) - jnp.maximum(t - half, 0)).astype(F32)
        pooled = total / count - u
        y = _dot(pooled.astype(BF16), wp_ref[g].astype(BF16))
        o_ref[:, c0:c0 + B_GROUP_DIM] = y * ps_ref[:, c0:c0 + B_GROUP_DIM]


def _pool(u, w_pool, pool_scale, i_layer, mix):
    n_groups = len(POOL_WINDOWS)
    rows = DEC_SEQ
    return pl.pallas_call(
        functools.partial(_pool_kernel, rows=rows),
        out_shape=jax.ShapeDtypeStruct((N_TOK, D_MODEL), F32),
        grid=(N_TOK // rows,),
        in_specs=[
            pl.BlockSpec((rows, B_WIDTH), lambda b: (b, 0)),
            pl.BlockSpec((None, n_groups, B_GROUP_DIM, B_GROUP_DIM),
                         lambda b: (i_layer, 0, 0, 0)),
            pl.BlockSpec((None, 1, B_WIDTH), lambda b: (i_layer, 0, 0)),
            pl.BlockSpec(memory_space=pl.ANY),
        ],
        out_specs=pl.BlockSpec((rows, B_WIDTH), lambda b: (b, 1)),
        input_output_aliases={3: 0},
        compiler_params=_params(1),
        name="pool_mixer",
    )(u, w_pool, pool_scale.reshape(-1, 1, B_WIDTH), mix)


def _sink_row(sink_ref, head0, queries_per_head):
    lane = lax.broadcasted_iota(jnp.int32, (1, C_GROUP * queries_per_head), 1)
    row = jnp.full((1, C_GROUP * queries_per_head), sink_ref[head0], F32)
    for g in range(1, C_GROUP):
        row = jnp.where(lane >= g * queries_per_head, sink_ref[head0 + g], row)
    return row


CTX_C_REQS_PER_STEP = 2


def _ctx_attn_c_kernel(sink_ref, p_ref, o_ref, *, sink0):
    for req in range(CTX_C_REQS_PER_STEP):
        rows = slice(req * SEQ, (req + 1) * SEQ)
        outs_t = []
        for kk in range(C_KV_HEADS):
            heads = [kk * C_GROUP + g for g in range(C_GROUP)]
            q = jnp.concatenate(
                [p_ref[rows, h * HEAD_DIM:(h + 1) * HEAD_DIM] for h in heads], axis=0)
            k0 = C_Q_WIDTH + kk * HEAD_DIM
            v0 = C_Q_WIDTH + C_KV_WIDTH + kk * HEAD_DIM
            k = p_ref[rows, k0:k0 + HEAD_DIM]
            v = p_ref[rows, v0:v0 + HEAD_DIM]
            o_t = _softmax_pv_t([_dot_nt(k, q)],
                                _sink_row(sink_ref, sink0 + kk * C_GROUP, SEQ),
                                [v])
            outs_t += [o_t[:, g * SEQ:(g + 1) * SEQ] for g in range(C_GROUP)]
        o_ref[rows, :] = jnp.concatenate(outs_t, axis=0).T


def _ctx_attn_c(qkv, sink_all, j_layer):
    return pl.pallas_call(
        functools.partial(_ctx_attn_c_kernel, sink0=j_layer * C_HEADS),
        out_shape=jax.ShapeDtypeStruct((N_TOK, D_MODEL), F32),
        grid=(BATCH // CTX_C_REQS_PER_STEP,),
        in_specs=[
            pl.BlockSpec(memory_space=pltpu.SMEM),
            pl.BlockSpec((CTX_C_REQS_PER_STEP * SEQ, QKV_WIDTH), lambda b: (b, 0)),
        ],
        out_specs=pl.BlockSpec((CTX_C_REQS_PER_STEP * SEQ, C_Q_WIDTH), lambda b: (b, 0)),
        compiler_params=_params(1),
        name="ctx_attn_c",
    )(sink_all.reshape(-1), qkv)


def _lat_attn_c_kernel(sink_ref, q_ref, k_ref, v_ref, kc_ref, vc_ref, mix_ref, o_ref,
                       kctx_ref, vctx_ref, *, sink0):
    del mix_ref
    j = pl.program_id(1)
    n_blocks = DEC_SEQ // C_BLOCK

    @pl.when(j == 0)
    def _():
        for kk in range(C_KV_HEADS):
            c0 = kk * HEAD_DIM
            kctx_ref[:, c0:c0 + HEAD_DIM] = kc_ref[kk].T.astype(BF16)
            vctx_ref[:, c0:c0 + HEAD_DIM] = vc_ref[kk].T.astype(BF16)

    n_q = C_GROUP * C_BLOCK
    qi = lax.broadcasted_iota(jnp.int32, (C_BLOCK, n_q), 1) % C_BLOCK
    jl = lax.broadcasted_iota(jnp.int32, (C_BLOCK, n_q), 0)
    valid_prev = (jl >= qi) & (j > 0)
    valid_next = (jl <= qi) & (j < n_blocks - 1)
    rows_prev = pl.ds(pl.multiple_of(jnp.maximum(j - 1, 0) * C_BLOCK, C_BLOCK), C_BLOCK)
    rows_cur = pl.ds(pl.multiple_of(j * C_BLOCK, C_BLOCK), C_BLOCK)
    rows_next = pl.ds(pl.multiple_of(jnp.minimum(j + 1, n_blocks - 1) * C_BLOCK, C_BLOCK),
                      C_BLOCK)

    outs_t = []
    for kk in range(C_KV_HEADS):
        heads = [kk * C_GROUP + g for g in range(C_GROUP)]
        qs = jnp.concatenate(
            [q_ref[:, h * HEAD_DIM:(h + 1) * HEAD_DIM] for h in heads], axis=0)
        c0 = kk * HEAD_DIM
        cols = slice(c0, c0 + HEAD_DIM)
        k_ctx = kctx_ref[:, cols]
        v_ctx = vctx_ref[:, cols]
        k_band = jnp.concatenate(
            [k_ref[rows_prev, cols], k_ref[rows_cur, cols], k_ref[rows_next, cols]], axis=0)
        v_band = jnp.concatenate(
            [v_ref[rows_prev, cols], v_ref[rows_cur, cols], v_ref[rows_next, cols]], axis=0)
        s_ctx = _dot_nt(k_ctx, qs)
        s_raw = _dot_nt(k_band, qs)
        s_band = jnp.concatenate(
            [jnp.where(valid_prev, s_raw[:C_BLOCK], NEG_INF),
             s_raw[C_BLOCK:2 * C_BLOCK],
             jnp.where(valid_next, s_raw[2 * C_BLOCK:], NEG_INF)], axis=0)
        o_t = _softmax_pv_t([s_ctx, s_band],
                            _sink_row(sink_ref, sink0 + kk * C_GROUP, C_BLOCK),
                            [v_ctx, v_band])
        outs_t += [o_t[:, g * C_BLOCK:(g + 1) * C_BLOCK] for g in range(C_GROUP)]
    o_ref[...] = jnp.concatenate(outs_t, axis=0).T


def _lat_attn_c(qkv, cache_k, cache_v, sink, j_layer, mix):
    n_blocks = DEC_SEQ // C_BLOCK
    q_row0 = N_CTX_TOK // C_BLOCK
    kv_row0 = N_CTX_TOK // DEC_SEQ
    k_col = C_Q_WIDTH // C_KV_WIDTH
    return pl.pallas_call(
        functools.partial(_lat_attn_c_kernel, sink0=j_layer * C_HEADS),
        out_shape=jax.ShapeDtypeStruct((N_TOK, D_MODEL), F32),
        grid=(DEC_BATCH, n_blocks),
        in_specs=[
            pl.BlockSpec(memory_space=pltpu.SMEM),
            pl.BlockSpec((C_BLOCK, C_Q_WIDTH), lambda b, j: (q_row0 + b * n_blocks + j, 0)),
            pl.BlockSpec((DEC_SEQ, C_KV_WIDTH), lambda b, j: (kv_row0 + b, k_col)),
            pl.BlockSpec((DEC_SEQ, C_KV_WIDTH), lambda b, j: (kv_row0 + b, k_col + 1)),
            pl.BlockSpec((None, None, C_KV_HEADS, HEAD_DIM, PAST_LEN),
                         lambda b, j: (b, j_layer, 0, 0, 0)),
            pl.BlockSpec((None, None, C_KV_HEADS, HEAD_DIM, PAST_LEN),
                         lambda b, j: (b, j_layer, 0, 0, 0)),
            pl.BlockSpec(memory_space=pl.ANY),
        ],
        out_specs=pl.BlockSpec((C_BLOCK, C_Q_WIDTH),
                               lambda b, j: (q_row0 + b * n_blocks + j, 0)),
        input_output_aliases={6: 0},
        scratch_shapes=[pltpu.VMEM((PAST_LEN, C_KV_WIDTH), BF16),
                        pltpu.VMEM((PAST_LEN, C_KV_WIDTH), BF16)],
        compiler_params=_params(2),
        name="lat_attn_c",
    )(sink.reshape(-1), qkv, qkv, qkv, cache_k, cache_v, mix)


def _rope_tables():
    t = np.arange(DEC_SEQ)
    pos = np.stack([t // GRID_W, t % GRID_W], axis=-1).astype(np.float64)
    half = HEAD_DIM // 4
    inv = ROPE_BASE ** (-np.arange(half, dtype=np.float64) / half)
    ang = pos[:, :, None] * inv
    cos = np.cos(ang)
    sin = np.sin(ang)
    cos64 = np.stack([cos, cos], axis=2).reshape(DEC_SEQ, HEAD_DIM)
    sin64 = np.stack([-sin, sin], axis=2).reshape(DEC_SEQ, HEAD_DIM)
    return (jnp.asarray(np.tile(cos64, (1, C_HEADS)), F32),
            jnp.asarray(np.tile(sin64, (1, C_HEADS)), F32))


FFN_CHUNK = 256
N_FFN_CHUNKS = FFN_HIDDEN // FFN_CHUNK
N_WO_PIECES = D_MODEL // FFN_CHUNK


def _post_mixer_kernel(*refs, layer, wo_idx, tm, final):
    x_ref, mix_ref, g_ref, mod_ref = refs[:4]
    n_in = 5 if final else 7
    wo_hbm, wgu_hbm, wd_hbm = refs[n_in:n_in + 3]
    n_out = 2
    out_refs = refs[n_in + 3:n_in + 3 + n_out]
    (wo_s, wg_s, wu_s, wd_s, act_s, stage_col, stage_row, sem_col,
     sem_row) = refs[n_in + 3 + n_out:]
    n_row_pieces = N_WO_PIECES + N_FFN_CHUNKS

    def col_copy(which, c):
        src = wgu_hbm.at[layer, :, pl.ds(which * FFN_HIDDEN + c * FFN_CHUNK, FFN_CHUNK)]
        return pltpu.make_async_copy(src, stage_col.at[which, c % 2], sem_col.at[which, c % 2])

    def row_copy(p):
        if p < N_WO_PIECES:
            src = wo_hbm.at[wo_idx, pl.ds(p * FFN_CHUNK, FFN_CHUNK), :]
        else:
            src = wd_hbm.at[layer, pl.ds((p - N_WO_PIECES) * FFN_CHUNK, FFN_CHUNK), :]
        return pltpu.make_async_copy(src, stage_row.at[p % 2], sem_row.at[p % 2])

    def take_row_piece(p, dst_ref, row0):
        if p + 1 < n_row_pieces:
            row_copy(p + 1).start()
        row_copy(p).wait()
        dst_ref[row0:row0 + FFN_CHUNK, :] = stage_row[p % 2].astype(BF16)

    def tile(load_weights):
        if load_weights:
            row_copy(0).start()
            for which in range(2):
                col_copy(which, 0).start()
            for p in range(N_WO_PIECES):
                take_row_piece(p, wo_s, p * FFN_CHUNK)
        if not final:
            out_refs[1][...] = _modulation_block(*refs[4:7])
        gate1 = mod_ref[:, 2 * D_MODEL:3 * D_MODEL]
        x1 = x_ref[...] + gate1 * _dot(mix_ref[...].astype(BF16), wo_s[...])
        h = _norm_mod(x1, g_ref[...], mod_ref, 3).astype(BF16)
        for c in range(N_FFN_CHUNKS):
            if load_weights:
                for which, dst in ((0, wg_s), (1, wu_s)):
                    if c + 1 < N_FFN_CHUNKS:
                        col_copy(which, c + 1).start()
                    col_copy(which, c).wait()
                    dst[c] = stage_col[which, c % 2].astype(BF16)
            gate = _dot(h, wg_s[c])
            up = _dot(h, wu_s[c])
            act = gate / (1.0 + jnp.exp(-gate)) * up
            act_s[:, c * FFN_CHUNK:(c + 1) * FFN_CHUNK] = act.astype(BF16)
            if load_weights:
                take_row_piece(N_WO_PIECES + c, wd_s, c * FFN_CHUNK)
        gate2 = mod_ref[:, 5 * D_MODEL:6 * D_MODEL]
        return x1 + gate2 * _dot(act_s[...], wd_s[...])

    def emit(x2):
        if not final:
            out_refs[0][...] = x2
            return
        var = jnp.mean(x2 * x2, axis=-1, keepdims=True)
        y = x2 * lax.rsqrt(var + EPS) * refs[4][...]
        is_ctx = pl.program_id(0) < N_CTX_TOK // tm

        @pl.when(is_ctx)
        def _():
            out_refs[0][...] = y

        @pl.when(jnp.logical_not(is_ctx))
        def _():
            out_refs[1][...] = y

    first = pl.program_id(0) == 0

    @pl.when(first)
    def _():
        emit(tile(True))

    @pl.when(jnp.logical_not(first))
    def _():
        emit(tile(False))


def _post_mixer(x, mix, g_all, mod_l, layer, w_out_all, wo_idx, w_gate_up, w_down,
                norm_final=None, next_mod_inputs=None):
    tm = 512
    n_tiles = N_TOK // tm
    n_ctx_tiles = N_CTX_TOK // tm
    final = norm_final is not None
    row_spec = pl.BlockSpec((tm, D_MODEL), lambda i: (i, 0))
    hbm = pl.BlockSpec(memory_space=pl.ANY)
    if final:
        extra_in = [norm_final.reshape(1, D_MODEL)]
        extra_specs = [pl.BlockSpec((1, D_MODEL), lambda i: (0, 0))]
        out_shape = (jax.ShapeDtypeStruct((N_CTX_TOK, D_MODEL), F32),
                     jax.ShapeDtypeStruct((N_LAT_TOK, D_MODEL), F32))
        out_specs = (
            pl.BlockSpec((tm, D_MODEL), lambda i: (jnp.minimum(i, n_ctx_tiles - 1), 0)),
            pl.BlockSpec((tm, D_MODEL), lambda i: (jnp.maximum(i - n_ctx_tiles, 0), 0)))
    else:
        tn = 6 * D_MODEL // n_tiles
        extra_in = list(next_mod_inputs)
        extra_specs = [
            pl.BlockSpec((N_GROUPS_PAD, D_MODEL), lambda i: (0, 0)),
            pl.BlockSpec((None, D_MODEL, tn), lambda i: (layer + 1, 0, i)),
            pl.BlockSpec((None, 1, tn), lambda i: (layer + 1, 0, i)),
        ]
        out_shape = (jax.ShapeDtypeStruct((N_TOK, D_MODEL), F32),
                     jax.ShapeDtypeStruct((N_GROUPS_PAD, 6 * D_MODEL), F32))
        out_specs = (row_spec, pl.BlockSpec((N_GROUPS_PAD, tn), lambda i: (0, i)))
    return pl.pallas_call(
        functools.partial(_post_mixer_kernel, layer=layer, wo_idx=wo_idx, tm=tm, final=final),
        out_shape=out_shape,
        grid=(N_TOK // tm,),
        in_specs=[
            row_spec,
            row_spec,
            pl.BlockSpec((None, 1, D_MODEL), lambda i: (layer, 0, 0)),
            pl.BlockSpec((None, 1, 6 * D_MODEL), lambda i: (_group_of_tile(i, tm), 0, 0)),
        ] + extra_specs + [hbm, hbm, hbm],
        out_specs=out_specs,
        scratch_shapes=[
            pltpu.VMEM((D_MODEL, D_MODEL), BF16),
            pltpu.VMEM((N_FFN_CHUNKS, D_MODEL, FFN_CHUNK), BF16),
            pltpu.VMEM((N_FFN_CHUNKS, D_MODEL, FFN_CHUNK), BF16),
            pltpu.VMEM((FFN_HIDDEN, D_MODEL), BF16),
            pltpu.VMEM((tm, FFN_HIDDEN), BF16),
            pltpu.VMEM((2, 2, D_MODEL, FFN_CHUNK), F32),
            pltpu.VMEM((2, FFN_CHUNK, D_MODEL), F32),
            pltpu.SemaphoreType.DMA((2, 2)),
            pltpu.SemaphoreType.DMA((2,)),
        ],
        compiler_params=_params(1),
        name="post_mixer",
    )(x, mix, g_all.reshape(DEPTH, 1, D_MODEL), mod_l, *extra_in, w_out_all, w_gate_up, w_down)


def kernel(x_prompt, x_sample, cache_a_k, cache_a_v, cache_c_k, cache_c_v, c, c_ctx, w_mod, b_mod, norm_mix, norm_ffn, w_in_ab, rpb_a, w_pool, pool_scale, w_out_ab, w_in_c, sink_c, w_out_c, w_gate_up, w_down, norm_final):
    xs = (x_prompt.reshape(N_CTX_TOK, D_MODEL), x_sample.reshape(N_LAT_TOK, D_MODEL))
    cond8 = jnp.concatenate(
        [c_ctx[None], c, jnp.zeros((N_GROUPS_PAD - 1 - DEC_BATCH, D_MODEL), F32)], axis=0)
    b_mod3 = b_mod.reshape(DEPTH, 1, 6 * D_MODEL)
    mod, bias_tiles = _modulation0_and_bias(cond8, w_mod, b_mod3, rpb_a)

    n_ab = cache_a_k.shape[1]
    n_c = cache_c_k.shape[1]
    cache_a_k, cache_a_v, cache_c_k, cache_c_v = (
        jnp.transpose(t, (0, 1, 3, 4, 2)) for t in (cache_a_k, cache_a_v, cache_c_k, cache_c_v))
    rope_tables = _rope_tables()

    new_a = []
    new_c = []
    for l in range(DEPTH):
        mod_l = mod.reshape(N_GROUPS_PAD, 1, 6 * D_MODEL)
        if l % 2 == 0:
            i = l // 2
            qkv, *rest = _inproj(xs, norm_mix, mod_l, l, w_in_ab, i, A_HEADS, A_WIDTH, n_ab,
                                 new_a)
            new_a, u = rest[:2], rest[2]
            if len(xs) == 2:
                xs = (rest[3],)
            mix = _ctx_attn_a(qkv)
            mix = _lat_attn_a(qkv, cache_a_k, cache_a_v, bias_tiles, i, mix)
            mix = _pool(u, w_pool, pool_scale, i, mix)
            w_out, wo_idx = w_out_ab, i
        else:
            j = l // 2
            qkv, *new_c = _inproj(xs, norm_mix, mod_l, l, w_in_c, j, C_KV_HEADS, C_Q_WIDTH, n_c,
                                  new_c, rope_tables)
            mix = _ctx_attn_c(qkv, sink_c, j)
            mix = _lat_attn_c(qkv, cache_c_k, cache_c_v, sink_c, j, mix)
            w_out, wo_idx = w_out_c, j
        if l + 1 < DEPTH:
            x, mod = _post_mixer(xs[0], mix, norm_ffn, mod_l, l, w_out, wo_idx, w_gate_up,
                                 w_down, next_mod_inputs=(cond8, w_mod, b_mod3))
            xs = (x,)
        else:
            y_ctx, y_lat = _post_mixer(xs[0], mix, norm_ffn, mod_l, l, w_out, wo_idx, w_gate_up,
                                       w_down, norm_final)

    new_caches = [jnp.transpose(t, (0, 1, 4, 2, 3)) for t in (*new_a, *new_c)]
    return (y_ctx.reshape(BATCH, SEQ, D_MODEL), y_lat.reshape(DEC_BATCH, DEC_SEQ, D_MODEL),
            *new_caches)
```

```python
import functools

import jax
import jax.numpy as jnp
import numpy as np
from jax import lax
from jax.experimental import pallas as pl
from jax.experimental.pallas import tpu as pltpu

D_MODEL = 1024
BATCH = 16
SEQ = 256
DEPTH = 4
DEC_BATCH = 2
DEC_SEQ = 1024
PAST_LEN = 512
GRID_W = 64
HEAD_DIM = 64
A_WIDTH = 512
A_HEADS = 8
B_WIDTH = 512
POOL_WINDOWS = (2, 4, 8, 16)
B_GROUP_DIM = 128
NA_ROWS = 8
NA_COLS = 16
C_HEADS = 16
C_KV_HEADS = 4
C_GROUP = C_HEADS // C_KV_HEADS
C_Q_WIDTH = 1024
C_KV_WIDTH = 256
C_BLOCK = 128
FFN_HIDDEN = 2816
ROPE_BASE = 10000.0
EPS = 1e-6
NEG_INF = -1e30

N_CTX_TOK = BATCH * SEQ
N_LAT_TOK = DEC_BATCH * DEC_SEQ
N_TOK = N_CTX_TOK + N_LAT_TOK
GRID_ROWS = DEC_SEQ // GRID_W
N_GROUPS_PAD = 8

VMEM_LIMIT = 56 * 1024 * 1024

F32 = jnp.float32
BF16 = jnp.bfloat16


def _params(n_axes):
    return pltpu.CompilerParams(dimension_semantics=("arbitrary",) * n_axes,
                                vmem_limit_bytes=VMEM_LIMIT)


def _group_of_tile(i, tm):
    row0 = i * tm
    return jnp.where(row0 < N_CTX_TOK, 0, 1 + (row0 - N_CTX_TOK) // DEC_SEQ)


def _dot_nt(a, b):
    return lax.dot_general(a, b, (((1,), (1,)), ((), ())), preferred_element_type=F32)


def _dot(a, b):
    return jnp.dot(a, b, preferred_element_type=F32)


def _dot_tn(a, b):
    return lax.dot_general(a, b, (((0,), (0,)), ((), ())), preferred_element_type=F32)


def _softmax_pv_t(score_fns, sink_rows, values, s_refs, p_refs):
    n_groups, n_blocks = len(score_fns), len(s_refs)
    for g in range(n_groups):
        for b in range(n_blocks):
            s_refs[b][g] = score_fns[g][b]()
    inv_l = []
    for g in range(n_groups):
        m = sink_rows[g]
        for b in range(n_blocks):
            m = jnp.maximum(m, jnp.max(s_refs[b][g], axis=0, keepdims=True))
        l = jnp.exp(sink_rows[g] - m)
        for b in range(n_blocks):
            p = jnp.exp(s_refs[b][g] - m)
            l = l + jnp.sum(p, axis=0, keepdims=True)
            p_refs[b][g] = p.astype(BF16)
        inv_l.append(1.0 / l)
    outs = []
    for g in range(n_groups):
        o = None
        for b in range(n_blocks):
            pv = _dot_tn(values[g][b], p_refs[b][g])
            o = pv if o is None else o + pv
        outs.append(o * inv_l[g])
    return outs


def _modulation_block(cond_ref, w_ref, b_ref):
    c = cond_ref[...]
    s = c / (1.0 + jnp.exp(-c))
    return _dot(s.astype(BF16), w_ref[...].astype(BF16)) + b_ref[...]


def _mod_bias_kernel(rpb_ref, cond_ref, w_ref, b_ref, o_ref, bias_ref, *, heads_per_step):
    o_ref[...] = _modulation_block(cond_ref, w_ref, b_ref)
    pair0 = pl.program_id(0) * heads_per_step
    for t in range(heads_per_step):
        _write_bias_tiles(rpb_ref, (pair0 + t) * (N_DROW * N_DCOL), bias_ref.at[t])


def _modulation0_and_bias(cond8, w_mod, b_mod3, rpb_a):
    n_col_blocks = 4
    tn = 6 * D_MODEL // n_col_blocks
    n_bias_layers = rpb_a.shape[0]
    heads_per_step = n_bias_layers * A_HEADS // n_col_blocks
    steps_per_layer = A_HEADS // heads_per_step
    return pl.pallas_call(
        functools.partial(_mod_bias_kernel, heads_per_step=heads_per_step),
        out_shape=(
            jax.ShapeDtypeStruct((N_GROUPS_PAD, 6 * D_MODEL), F32),
            jax.ShapeDtypeStruct(
                (n_bias_layers, A_HEADS, N_BIAS_TILES, GRID_W, 2 * GRID_W), F32)),
        grid=(n_col_blocks,),
        in_specs=[
            pl.BlockSpec(memory_space=pltpu.SMEM),
            pl.BlockSpec((N_GROUPS_PAD, D_MODEL), lambda j: (0, 0)),
            pl.BlockSpec((None, D_MODEL, tn), lambda j: (0, 0, j)),
            pl.BlockSpec((None, 1, tn), lambda j: (0, 0, j)),
        ],
        out_specs=(
            pl.BlockSpec((N_GROUPS_PAD, tn), lambda j: (0, j)),
            pl.BlockSpec((None, heads_per_step, N_BIAS_TILES, GRID_W, 2 * GRID_W),
                         lambda j: (j // steps_per_layer, j % steps_per_layer, 0, 0, 0))),
        compiler_params=_params(1),
        name="modulation_bias",
    )(rpb_a.reshape(-1), cond8, w_mod, b_mod3)


def _norm_mod(x, g, mod_ref, shift_idx):
    var = jnp.mean(x * x, axis=-1, keepdims=True)
    y = x * lax.rsqrt(var + EPS) * g
    shift = mod_ref[:, shift_idx * D_MODEL:(shift_idx + 1) * D_MODEL]
    scale = mod_ref[:, (shift_idx + 1) * D_MODEL:(shift_idx + 2) * D_MODEL]
    return y * (1.0 + scale) + shift


QKV_WIDTH = 3 * A_WIDTH
Q_SCALE = HEAD_DIM ** -0.5
PROJ_CHUNK = 4 * HEAD_DIM
ROPE_TABLE_WIDTH = PROJ_CHUNK


def _rope(x, cos, sin_signed):
    n = x.shape[-1]
    lane = lax.broadcasted_iota(jnp.int32, x.shape, x.ndim - 1)
    first = (lane % 32) < 16
    partner = jnp.where(first, pltpu.roll(x, n - 16, axis=x.ndim - 1),
                        pltpu.roll(x, 16, axis=x.ndim - 1))
    return x * cos + partner * sin_signed


def _inproj_kernel(*refs, tm, n_heads, q_width, n_prev, split_x, rope, has_u):
    n_x = 2 if split_x else 1
    g_ref, mod_ref, w_ref = refs[n_x:n_x + 3]
    n_in = n_x + 3 + (2 if rope else 0) + n_prev
    qkv_ref, ck_ref, cv_ref = refs[n_in:n_in + 3]
    extra_out = refs[n_in + 3:-3]
    wbf_ref, h_ref, res_ref = refs[-3:]
    i = pl.program_id(0)
    is_ctx = i < N_CTX_TOK // tm
    kv_width = (QKV_WIDTH - q_width) // 2
    k_col, v_col = q_width, q_width + kv_width
    n_out = res_ref.shape[1]
    chunk = PROJ_CHUNK

    @pl.when(i == 0)
    def _():
        wbf_ref[...] = w_ref[...].astype(BF16)

    def tile(ctx):
        if split_x:
            x = (refs[0] if ctx else refs[1])[...]
            extra_out[-1][...] = x
        else:
            x = refs[0][...]
        h_ref[...] = _norm_mod(x, g_ref[...], mod_ref, 0).astype(BF16)
        for c0 in range(0, n_out, chunk):
            cols = slice(c0, c0 + chunk)
            res_ref[:, cols] = _dot(h_ref[...], wbf_ref[:, cols])
            if c0 >= QKV_WIDTH:
                extra_out[0][:, c0 - QKV_WIDTH:c0 - QKV_WIDTH + chunk] = res_ref[:, cols]
                continue
            r = res_ref[:, cols]
            if rope and not ctx and c0 < v_col:
                cos_ref, sin_ref = refs[n_x + 3:n_x + 5]
                r = _rope(r, cos_ref[...], sin_ref[...])
            if c0 < q_width:
                r = r * Q_SCALE
            qkv_ref[:, cols] = r.astype(BF16)
            if ctx and c0 >= k_col:
                c_ref, col0 = (ck_ref, k_col) if c0 < v_col else (cv_ref, v_col)
                for req in range(tm // SEQ):
                    for hc in range(chunk // HEAD_DIM):
                        hd = (c0 - col0) // HEAD_DIM + hc
                        c_ref[req, hd] = res_ref[
                            req * SEQ:(req + 1) * SEQ,
                            c0 + hc * HEAD_DIM:c0 + (hc + 1) * HEAD_DIM].T

    pl.when(is_ctx)(lambda: tile(True))
    pl.when(jnp.logical_not(is_ctx))(lambda: tile(False))


def _inproj(xs, g_all, mod_l, layer, w_all, w_idx, n_heads, q_width, n_slots, prev_caches,
            rope_tables=None):
    tm = 512
    n_out = w_all.shape[2]
    n_ctx_tiles = N_CTX_TOK // tm
    split_x = len(xs) == 2
    rope = rope_tables is not None
    has_u = n_out > QKV_WIDTH
    cache_shape = jax.ShapeDtypeStruct((BATCH, n_slots, n_heads, HEAD_DIM, SEQ), F32)
    cache_spec = pl.BlockSpec(
        (tm // SEQ, None, n_heads, HEAD_DIM, SEQ),
        lambda i: (jnp.minimum(i, n_ctx_tiles - 1), w_idx, 0, 0, 0))
    row_spec = pl.BlockSpec((tm, D_MODEL), lambda i: (i, 0))
    if split_x:
        x_specs = [
            pl.BlockSpec((tm, D_MODEL), lambda i: (jnp.minimum(i, n_ctx_tiles - 1), 0)),
            pl.BlockSpec((tm, D_MODEL), lambda i: (jnp.maximum(i - n_ctx_tiles, 0), 0)),
        ]
    else:
        x_specs = [row_spec]
    out_shape = [jax.ShapeDtypeStruct((N_TOK, QKV_WIDTH), BF16), cache_shape, cache_shape]
    out_specs = [pl.BlockSpec((tm, QKV_WIDTH), lambda i: (i, 0)), cache_spec, cache_spec]
    if has_u:
        out_shape.append(jax.ShapeDtypeStruct((N_TOK, n_out - QKV_WIDTH), F32))
        out_specs.append(pl.BlockSpec((tm, n_out - QKV_WIDTH), lambda i: (i, 0)))
    if split_x:
        out_shape.append(jax.ShapeDtypeStruct((N_TOK, D_MODEL), F32))
        out_specs.append(row_spec)
    rope_in, rope_specs = [], []
    if rope:
        tiles_per_seq = DEC_SEQ // tm
        rope_spec = pl.BlockSpec(
            (tm, ROPE_TABLE_WIDTH),
            lambda i: (jnp.maximum(i - n_ctx_tiles, 0) % tiles_per_seq, 0))
        rope_in, rope_specs = list(rope_tables), [rope_spec, rope_spec]
    n_prev = len(prev_caches)
    n_before = len(xs) + 3 + len(rope_in)
    return pl.pallas_call(
        functools.partial(_inproj_kernel, tm=tm, n_heads=n_heads, q_width=q_width,
                          n_prev=n_prev, split_x=split_x, rope=rope, has_u=has_u),
        out_shape=out_shape,
        grid=(N_TOK // tm,),
        in_specs=x_specs + [
            pl.BlockSpec((None, 1, D_MODEL), lambda i: (layer, 0, 0)),
            pl.BlockSpec((None, 1, 6 * D_MODEL), lambda i: (_group_of_tile(i, tm), 0, 0)),
            pl.BlockSpec((None, D_MODEL, n_out), lambda i: (w_idx, 0, 0),
                         pipeline_mode=pl.Buffered(1)),
        ] + rope_specs + [pl.BlockSpec(memory_space=pl.ANY)] * n_prev,
        out_specs=out_specs,
        scratch_shapes=[pltpu.VMEM((D_MODEL, n_out), BF16), pltpu.VMEM((tm, D_MODEL), BF16),
                        pltpu.VMEM((tm, n_out), F32)],
        input_output_aliases={n_before + k: 1 + k for k in range(n_prev)},
        compiler_params=_params(1),
        name="inproj",
    )(*xs, g_all.reshape(DEPTH, 1, D_MODEL), mod_l, w_all, *rope_in, *prev_caches)


def _ctx_attn_a_kernel(p_ref, o_ref, s_ref, p_scr):
    for h in range(A_HEADS):
        c0 = h * HEAD_DIM
        s_ref[h] = _dot_nt(p_ref[:, c0:c0 + HEAD_DIM],
                           p_ref[:, A_WIDTH + c0:A_WIDTH + c0 + HEAD_DIM])
    inv_l = []
    for h in range(A_HEADS):
        s = s_ref[h]
        p = jnp.exp(s - jnp.max(s, axis=-1, keepdims=True))
        inv_l.append(1.0 / jnp.sum(p, axis=-1, keepdims=True))
        p_scr[h] = p.astype(BF16)
    for h in range(A_HEADS):
        c0 = h * HEAD_DIM
        v = p_ref[:, 2 * A_WIDTH + c0:2 * A_WIDTH + c0 + HEAD_DIM]
        o_ref[:, c0:c0 + HEAD_DIM] = _dot(p_scr[h], v) * inv_l[h]


def _ctx_attn_a(qkv):
    return pl.pallas_call(
        _ctx_attn_a_kernel,
        out_shape=jax.ShapeDtypeStruct((N_TOK, D_MODEL), F32),
        grid=(BATCH,),
        in_specs=[pl.BlockSpec((SEQ, QKV_WIDTH), lambda b: (b, 0))],
        out_specs=pl.BlockSpec((SEQ, A_WIDTH), lambda b: (b, 0)),
        scratch_shapes=[pltpu.VMEM((A_HEADS, SEQ, SEQ), F32),
                        pltpu.VMEM((A_HEADS, SEQ, SEQ), BF16)],
        compiler_params=_params(1),
        name="ctx_attn_a",
    )(qkv)


N_DROW = 2 * NA_ROWS - 1
N_DCOL = 2 * NA_COLS - 1
N_BIAS_TILES = 16
BIAS_TILE_LEFT_PAD = 14
BIAS_TILE_RIGHT_PAD = 15
MID_DROW = NA_ROWS - 1 - NA_ROWS // 2


def _write_bias_tiles(rpb_ref, base, o_ref):
    qi = lax.broadcasted_iota(jnp.int32, (GRID_W, 2 * GRID_W), 0)
    lane = lax.broadcasted_iota(jnp.int32, (GRID_W, 2 * GRID_W), 1)
    right = lane >= GRID_W
    kc = jnp.where(right, lane - GRID_W, lane)
    qstart = jnp.clip(qi - NA_COLS // 2, 0, GRID_W - NA_COLS)
    valid = (kc >= qstart) & (kc < qstart + NA_COLS)

    offs = lax.broadcasted_iota(jnp.int32, (1, 2 * GRID_W), 1) & (GRID_W - 1)
    shift = 2 * GRID_W - (NA_COLS - 1)
    rows = []
    for dr in range(N_DROW):
        vec = jnp.zeros((1, 2 * GRID_W), F32)
        for d in range(N_DCOL):
            vec = jnp.where(offs == d, rpb_ref[base + dr * N_DCOL + d], vec)
        rows.append(pltpu.roll(jnp.broadcast_to(vec, (GRID_W, 2 * GRID_W)), shift, axis=1,
                               stride=1, stride_axis=0))

    for t in range(N_DROW - 1):
        o_ref[t] = jnp.where(valid, jnp.where(right, rows[t + 1], rows[t]), NEG_INF)
    o_ref[BIAS_TILE_LEFT_PAD] = jnp.where(valid & right, rows[MID_DROW], NEG_INF)
    o_ref[BIAS_TILE_RIGHT_PAD] = jnp.where(valid & jnp.logical_not(right),
                                           rows[MID_DROW + NA_ROWS - 1], NEG_INF)


def _na_window(r):
    start = min(max(r - NA_ROWS // 2, 0), GRID_ROWS - NA_ROWS)
    first_drow = start - r + NA_ROWS - 1
    if start % 2 == 0:
        return start, [first_drow + 2 * p for p in range(NA_ROWS // 2)]
    assert first_drow == MID_DROW
    inner = [first_drow + 1 + 2 * p for p in range(NA_ROWS // 2 - 1)]
    return start - 1, [BIAS_TILE_LEFT_PAD] + inner + [BIAS_TILE_RIGHT_PAD]


def _lat_attn_a_kernel(q_ref, k_ref, v_ref, kc_ref, vc_ref, bias_ref, mix_ref, o_ref,
                       s_loc, s_ctx, p_loc, p_ctx):
    del mix_ref
    hp = pl.program_id(1)

    @pl.when((pl.program_id(0) == 0) & (hp == 0))
    def _():
        p_loc[...] = jnp.zeros_like(p_loc)

    for hh in range(2):
        c0 = hh * HEAD_DIM
        q = q_ref[:, c0:c0 + HEAD_DIM]
        s_loc[hh] = _dot_nt(q, k_ref[:, c0:c0 + HEAD_DIM])
        s_ctx[hh] = _dot(q, kc_ref[hh].astype(BF16))
    inv_l = [[], []]
    for hh in range(2):
        for r in range(GRID_ROWS):
            rows = slice(r * GRID_W, (r + 1) * GRID_W)
            first_row, tiles = _na_window(r)
            cols = slice(first_row * GRID_W, (first_row + 2 * len(tiles)) * GRID_W)
            bias = jnp.concatenate([bias_ref[hh, t] for t in tiles], axis=1)
            sc = s_ctx[hh, rows, :]
            sl = s_loc[hh, rows, cols] + bias
            m = jnp.maximum(jnp.max(sc, axis=-1, keepdims=True),
                            jnp.max(sl, axis=-1, keepdims=True))
            pc = jnp.exp(sc - m)
            pw = jnp.exp(sl - m)
            inv_l[hh].append(1.0 / (jnp.sum(pc, axis=-1, keepdims=True)
                                    + jnp.sum(pw, axis=-1, keepdims=True)))
            p_ctx[hh, rows, :] = pc.astype(BF16)
            p_loc[hh, rows, cols] = pw.astype(BF16)
    for hh in range(2):
        c0 = hh * HEAD_DIM
        o = (_dot_nt(p_ctx[hh], vc_ref[hh].astype(BF16))
             + _dot(p_loc[hh], v_ref[:, c0:c0 + HEAD_DIM]))
        o_ref[:, c0:c0 + HEAD_DIM] = o * jnp.concatenate(inv_l[hh], axis=0)


def _lat_attn_a(qkv, cache_k, cache_v, bias_tiles, i_layer, mix):
    row0 = N_CTX_TOK // DEC_SEQ
    pair = 2 * HEAD_DIM
    k_col0 = A_WIDTH // pair
    v_col0 = 2 * A_WIDTH // pair
    return pl.pallas_call(
        _lat_attn_a_kernel,
        out_shape=jax.ShapeDtypeStruct((N_TOK, D_MODEL), F32),
        grid=(DEC_BATCH, A_HEADS // 2),
        in_specs=[
            pl.BlockSpec((DEC_SEQ, pair), lambda b, hp: (row0 + b, hp)),
            pl.BlockSpec((DEC_SEQ, pair), lambda b, hp: (row0 + b, k_col0 + hp)),
            pl.BlockSpec((DEC_SEQ, pair), lambda b, hp: (row0 + b, v_col0 + hp)),
            pl.BlockSpec((None, None, 2, HEAD_DIM, PAST_LEN),
                         lambda b, hp: (b, i_layer, hp, 0, 0)),
            pl.BlockSpec((None, None, 2, HEAD_DIM, PAST_LEN),
                         lambda b, hp: (b, i_layer, hp, 0, 0)),
            pl.BlockSpec((None, 2, N_BIAS_TILES, GRID_W, 2 * GRID_W),
                         lambda b, hp: (i_layer, hp, 0, 0, 0)),
            pl.BlockSpec(memory_space=pl.ANY),
        ],
        out_specs=pl.BlockSpec((DEC_SEQ, pair), lambda b, hp: (row0 + b, hp)),
        scratch_shapes=[pltpu.VMEM((2, DEC_SEQ, DEC_SEQ), F32),
                        pltpu.VMEM((2, DEC_SEQ, PAST_LEN), F32),
                        pltpu.VMEM((2, DEC_SEQ, DEC_SEQ), BF16),
                        pltpu.VMEM((2, DEC_SEQ, PAST_LEN), BF16)],
        input_output_aliases={6: 0},
        compiler_params=_params(2),
        name="lat_attn_a",
    )(qkv, qkv, qkv, cache_k, cache_v, bias_tiles, mix)


def _pool_kernel(u_ref, wp_ref, ps_ref, mix_ref, o_ref, *, rows):
    del mix_ref
    n = jnp.where(pl.program_id(0) < N_CTX_TOK // rows, SEQ, DEC_SEQ)
    t = lax.broadcasted_iota(jnp.int32, (rows, B_GROUP_DIM), 0) & (n - 1)

    def earlier(x, k):
        return jnp.where(t >= k, pltpu.roll(x, k, axis=0), 0.0)

    def later(x, k):
        return jnp.where(t < n - k, pltpu.roll(x, rows - k, axis=0), 0.0)

    for g, w in enumerate(POOL_WINDOWS):
        c0 = g * B_GROUP_DIM
        half = w // 2
        u = u_ref[:, c0:c0 + B_GROUP_DIM]
        before, after = u, u
        k = 1
        while k < half:
            before = before + earlier(before, k)
            after = after + later(after, k)
            k *= 2
        total = earlier(before, 1) + after
        count = (jnp.minimum(t + half, n) - jnp.maximum(t - half, 0)).astype(F32)
        pooled = total / count - u
        y = _dot(pooled.astype(BF16), wp_ref[g].astype(BF16))
        o_ref[:, c0:c0 + B_GROUP_DIM] = y * ps_ref[:, c0:c0 + B_GROUP_DIM]


def _pool(u, w_pool, pool_scale, i_layer, mix):
    n_groups = len(POOL_WINDOWS)
    rows = DEC_SEQ
    return pl.pallas_call(
        functools.partial(_pool_kernel, rows=rows),
        out_shape=jax.ShapeDtypeStruct((N_TOK, D_MODEL), F32),
        grid=(N_TOK // rows,),
        in_specs=[
            pl.BlockSpec((rows, B_WIDTH), lambda b: (b, 0)),
            pl.BlockSpec((None, n_groups, B_GROUP_DIM, B_GROUP_DIM),
                         lambda b: (i_layer, 0, 0, 0)),
            pl.BlockSpec((None, 1, B_WIDTH), lambda b: (i_layer, 0, 0)),
            pl.BlockSpec(memory_space=pl.ANY),
        ],
        out_specs=pl.BlockSpec((rows, B_WIDTH), lambda b: (b, 1)),
        input_output_aliases={3: 0},
        compiler_params=_params(1),
        name="pool_mixer",
    )(u, w_pool, pool_scale.reshape(-1, 1, B_WIDTH), mix)


def _sink_row(sink_ref, head0, queries_per_head):
    lane = lax.broadcasted_iota(jnp.int32, (1, C_GROUP * queries_per_head), 1)
    row = jnp.full((1, C_GROUP * queries_per_head), sink_ref[head0], F32)
    for g in range(1, C_GROUP):
        row = jnp.where(lane >= g * queries_per_head, sink_ref[head0 + g], row)
    return row


CTX_C_REQS_PER_STEP = 2


def _ctx_attn_c_kernel(sink_ref, p_ref, o_ref, s_ref, p_scr, *, sink0):
    for req in range(CTX_C_REQS_PER_STEP):
        rows = slice(req * SEQ, (req + 1) * SEQ)

        def scores(kk):
            q = jnp.concatenate(
                [p_ref[rows, h * HEAD_DIM:(h + 1) * HEAD_DIM]
                 for h in range(kk * C_GROUP, (kk + 1) * C_GROUP)], axis=0)
            k0 = C_Q_WIDTH + kk * HEAD_DIM
            return _dot_nt(p_ref[rows, k0:k0 + HEAD_DIM], q)

        v0 = C_Q_WIDTH + C_KV_WIDTH
        outs = _softmax_pv_t(
            [[functools.partial(scores, kk)] for kk in range(C_KV_HEADS)],
            [_sink_row(sink_ref, sink0 + kk * C_GROUP, SEQ) for kk in range(C_KV_HEADS)],
            [[p_ref[rows, v0 + kk * HEAD_DIM:v0 + (kk + 1) * HEAD_DIM]]
             for kk in range(C_KV_HEADS)],
            [s_ref], [p_scr])
        outs_t = [o_t[:, g * SEQ:(g + 1) * SEQ] for o_t in outs for g in range(C_GROUP)]
        o_ref[rows, :] = jnp.concatenate(outs_t, axis=0).T


def _ctx_attn_c(qkv, sink_all, j_layer):
    return pl.pallas_call(
        functools.partial(_ctx_attn_c_kernel, sink0=j_layer * C_HEADS),
        out_shape=jax.ShapeDtypeStruct((N_TOK, D_MODEL), F32),
        grid=(BATCH // CTX_C_REQS_PER_STEP,),
        in_specs=[
            pl.BlockSpec(memory_space=pltpu.SMEM),
            pl.BlockSpec((CTX_C_REQS_PER_STEP * SEQ, QKV_WIDTH), lambda b: (b, 0)),
        ],
        out_specs=pl.BlockSpec((CTX_C_REQS_PER_STEP * SEQ, C_Q_WIDTH), lambda b: (b, 0)),
        scratch_shapes=[pltpu.VMEM((C_KV_HEADS, SEQ, C_GROUP * SEQ), F32),
                        pltpu.VMEM((C_KV_HEADS, SEQ, C_GROUP * SEQ), BF16)],
        compiler_params=_params(1),
        name="ctx_attn_c",
    )(sink_all.reshape(-1), qkv)


def _lat_attn_c_kernel(sink_ref, q_ref, k_ref, v_ref, kc_ref, vc_ref, mix_ref, o_ref,
                       kctx_ref, vctx_ref, sc_ref, sb_ref, pc_ref, pb_ref, *, sink0):
    del mix_ref
    j = pl.program_id(1)
    n_blocks = DEC_SEQ // C_BLOCK

    @pl.when(j == 0)
    def _():
        for kk in range(C_KV_HEADS):
            c0 = kk * HEAD_DIM
            kctx_ref[:, c0:c0 + HEAD_DIM] = kc_ref[kk].T.astype(BF16)
            vctx_ref[:, c0:c0 + HEAD_DIM] = vc_ref[kk].T.astype(BF16)

    n_q = C_GROUP * C_BLOCK
    qi = lax.broadcasted_iota(jnp.int32, (C_BLOCK, n_q), 1) % C_BLOCK
    jl = lax.broadcasted_iota(jnp.int32, (C_BLOCK, n_q), 0)
    valid_prev = (jl >= qi) & (j > 0)
    valid_next = (jl <= qi) & (j < n_blocks - 1)
    rows_prev = pl.ds(pl.multiple_of(jnp.maximum(j - 1, 0) * C_BLOCK, C_BLOCK), C_BLOCK)
    rows_cur = pl.ds(pl.multiple_of(j * C_BLOCK, C_BLOCK), C_BLOCK)
    rows_next = pl.ds(pl.multiple_of(jnp.minimum(j + 1, n_blocks - 1) * C_BLOCK, C_BLOCK),
                      C_BLOCK)

    def stacked_q(kk):
        return jnp.concatenate(
            [q_ref[:, h * HEAD_DIM:(h + 1) * HEAD_DIM]
             for h in range(kk * C_GROUP, (kk + 1) * C_GROUP)], axis=0)

    def band(ref, kk):
        cols = slice(kk * HEAD_DIM, (kk + 1) * HEAD_DIM)
        return jnp.concatenate(
            [ref[rows_prev, cols], ref[rows_cur, cols], ref[rows_next, cols]], axis=0)

    def ctx_scores(kk):
        return _dot_nt(kctx_ref[:, kk * HEAD_DIM:(kk + 1) * HEAD_DIM], stacked_q(kk))

    def band_scores(kk):
        s_raw = _dot_nt(band(k_ref, kk), stacked_q(kk))
        return jnp.concatenate(
            [jnp.where(valid_prev, s_raw[:C_BLOCK], NEG_INF),
             s_raw[C_BLOCK:2 * C_BLOCK],
             jnp.where(valid_next, s_raw[2 * C_BLOCK:], NEG_INF)], axis=0)

    kv_heads = range(C_KV_HEADS)
    outs = _softmax_pv_t(
        [[functools.partial(ctx_scores, kk), functools.partial(band_scores, kk)]
         for kk in kv_heads],
        [_sink_row(sink_ref, sink0 + kk * C_GROUP, C_BLOCK) for kk in kv_heads],
        [[vctx_ref[:, kk * HEAD_DIM:(kk + 1) * HEAD_DIM], band(v_ref, kk)] for kk in kv_heads],
        [sc_ref, sb_ref], [pc_ref, pb_ref])
    outs_t = [o_t[:, g * C_BLOCK:(g + 1) * C_BLOCK] for o_t in outs for g in range(C_GROUP)]
    o_ref[...] = jnp.concatenate(outs_t, axis=0).T


def _lat_attn_c(qkv, cache_k, cache_v, sink, j_layer, mix):
    n_blocks = DEC_SEQ // C_BLOCK
    q_row0 = N_CTX_TOK // C_BLOCK
    kv_row0 = N_CTX_TOK // DEC_SEQ
    k_col = C_Q_WIDTH // C_KV_WIDTH
    n_q = C_GROUP * C_BLOCK
    return pl.pallas_call(
        functools.partial(_lat_attn_c_kernel, sink0=j_layer * C_HEADS),
        out_shape=jax.ShapeDtypeStruct((N_TOK, D_MODEL), F32),
        grid=(DEC_BATCH, n_blocks),
        in_specs=[
            pl.BlockSpec(memory_space=pltpu.SMEM),
            pl.BlockSpec((C_BLOCK, C_Q_WIDTH), lambda b, j: (q_row0 + b * n_blocks + j, 0)),
            pl.BlockSpec((DEC_SEQ, C_KV_WIDTH), lambda b, j: (kv_row0 + b, k_col)),
            pl.BlockSpec((DEC_SEQ, C_KV_WIDTH), lambda b, j: (kv_row0 + b, k_col + 1)),
            pl.BlockSpec((None, None, C_KV_HEADS, HEAD_DIM, PAST_LEN),
                         lambda b, j: (b, j_layer, 0, 0, 0)),
            pl.BlockSpec((None, None, C_KV_HEADS, HEAD_DIM, PAST_LEN),
                         lambda b, j: (b, j_layer, 0, 0, 0)),
            pl.BlockSpec(memory_space=pl.ANY),
        ],
        out_specs=pl.BlockSpec((C_BLOCK, C_Q_WIDTH),
                               lambda b, j: (q_row0 + b * n_blocks + j, 0)),
        input_output_aliases={6: 0},
        scratch_shapes=[pltpu.VMEM((PAST_LEN, C_KV_WIDTH), BF16),
                        pltpu.VMEM((PAST_LEN, C_KV_WIDTH), BF16),
                        pltpu.VMEM((C_KV_HEADS, PAST_LEN, n_q), F32),
                        pltpu.VMEM((C_KV_HEADS, 3 * C_BLOCK, n_q), F32),
                        pltpu.VMEM((C_KV_HEADS, PAST_LEN, n_q), BF16),
                        pltpu.VMEM((C_KV_HEADS, 3 * C_BLOCK, n_q), BF16)],
        compiler_params=_params(2),
        name="lat_attn_c",
    )(sink.reshape(-1), qkv, qkv, qkv, cache_k, cache_v, mix)


def _rope_tables():
    t = np.arange(DEC_SEQ)
    pos = np.stack([t // GRID_W, t % GRID_W], axis=-1).astype(np.float64)
    half = HEAD_DIM // 4
    inv = ROPE_BASE ** (-np.arange(half, dtype=np.float64) / half)
    ang = pos[:, :, None] * inv
    cos = np.cos(ang)
    sin = np.sin(ang)
    cos64 = np.stack([cos, cos], axis=2).reshape(DEC_SEQ, HEAD_DIM)
    sin64 = np.stack([-sin, sin], axis=2).reshape(DEC_SEQ, HEAD_DIM)
    reps = ROPE_TABLE_WIDTH // HEAD_DIM
    return (jnp.asarray(np.tile(cos64, (1, reps)), F32),
            jnp.asarray(np.tile(sin64, (1, reps)), F32))


FFN_CHUNK = 256
N_FFN_CHUNKS = FFN_HIDDEN // FFN_CHUNK
N_WO_PIECES = D_MODEL // FFN_CHUNK


def _post_mixer_kernel(*refs, layer, wo_idx, tm, final):
    x_ref, mix_ref, g_ref, mod_ref = refs[:4]
    n_in = 5 if final else 7
    wo_hbm, wgu_hbm, wd_hbm = refs[n_in:n_in + 3]
    n_out = 2
    out_refs = refs[n_in + 3:n_in + 3 + n_out]
    (wo_s, wg_s, wu_s, wd_s, act_s, stage_col, stage_row, sem_col,
     sem_row) = refs[n_in + 3 + n_out:]
    n_row_pieces = N_WO_PIECES + N_FFN_CHUNKS

    def col_copy(which, c):
        src = wgu_hbm.at[layer, :, pl.ds(which * FFN_HIDDEN + c * FFN_CHUNK, FFN_CHUNK)]
        return pltpu.make_async_copy(src, stage_col.at[which, c % 2], sem_col.at[which, c % 2])

    def row_copy(p):
        if p < N_WO_PIECES:
            src = wo_hbm.at[wo_idx, pl.ds(p * FFN_CHUNK, FFN_CHUNK), :]
        else:
            src = wd_hbm.at[layer, pl.ds((p - N_WO_PIECES) * FFN_CHUNK, FFN_CHUNK), :]
        return pltpu.make_async_copy(src, stage_row.at[p % 2], sem_row.at[p % 2])

    def take_row_piece(p, dst_ref, row0):
        if p + 1 < n_row_pieces:
            row_copy(p + 1).start()
        row_copy(p).wait()
        dst_ref[row0:row0 + FFN_CHUNK, :] = stage_row[p % 2].astype(BF16)

    def tile(load_weights):
        if load_weights:
            row_copy(0).start()
            for which in range(2):
                col_copy(which, 0).start()
            for p in range(N_WO_PIECES):
                take_row_piece(p, wo_s, p * FFN_CHUNK)
        if not final:
            out_refs[1][...] = _modulation_block(*refs[4:7])
        gate1 = mod_ref[:, 2 * D_MODEL:3 * D_MODEL]
        x1 = x_ref[...] + gate1 * _dot(mix_ref[...].astype(BF16), wo_s[...])
        h = _norm_mod(x1, g_ref[...], mod_ref, 3).astype(BF16)
        for c in range(N_FFN_CHUNKS):
            if load_weights:
                for which, dst in ((0, wg_s), (1, wu_s)):
                    if c + 1 < N_FFN_CHUNKS:
                        col_copy(which, c + 1).start()
                    col_copy(which, c).wait()
                    dst[c] = stage_col[which, c % 2].astype(BF16)
            gate = _dot(h, wg_s[c])
            up = _dot(h, wu_s[c])
            act = gate / (1.0 + jnp.exp(-gate)) * up
            act_s[:, c * FFN_CHUNK:(c + 1) * FFN_CHUNK] = act.astype(BF16)
            if load_weights:
                take_row_piece(N_WO_PIECES + c, wd_s, c * FFN_CHUNK)
        gate2 = mod_ref[:, 5 * D_MODEL:6 * D_MODEL]
        return x1 + gate2 * _dot(act_s[...], wd_s[...])

    def emit(x2):
        if not final:
            out_refs[0][...] = x2
            return
        var = jnp.mean(x2 * x2, axis=-1, keepdims=True)
        y = x2 * lax.rsqrt(var + EPS) * refs[4][...]
        is_ctx = pl.program_id(0) < N_CTX_TOK // tm

        @pl.when(is_ctx)
        def _():
            out_refs[0][...] = y

        @pl.when(jnp.logical_not(is_ctx))
        def _():
            out_refs[1][...] = y

    first = pl.program_id(0) == 0

    @pl.when(first)
    def _():
        emit(tile(True))

    @pl.when(jnp.logical_not(first))
    def _():
        emit(tile(False))


def _post_mixer(x, mix, g_all, mod_l, layer, w_out_all, wo_idx, w_gate_up, w_down,
                norm_final=None, next_mod_inputs=None):
    tm = 512
    n_tiles = N_TOK // tm
    n_ctx_tiles = N_CTX_TOK // tm
    final = norm_final is not None
    row_spec = pl.BlockSpec((tm, D_MODEL), lambda i: (i, 0))
    hbm = pl.BlockSpec(memory_space=pl.ANY)
    if final:
        extra_in = [norm_final.reshape(1, D_MODEL)]
        extra_specs = [pl.BlockSpec((1, D_MODEL), lambda i: (0, 0))]
        out_shape = (jax.ShapeDtypeStruct((N_CTX_TOK, D_MODEL), F32),
                     jax.ShapeDtypeStruct((N_LAT_TOK, D_MODEL), F32))
        out_specs = (
            pl.BlockSpec((tm, D_MODEL), lambda i: (jnp.minimum(i, n_ctx_tiles - 1), 0)),
            pl.BlockSpec((tm, D_MODEL), lambda i: (jnp.maximum(i - n_ctx_tiles, 0), 0)))
    else:
        tn = 6 * D_MODEL // n_tiles
        extra_in = list(next_mod_inputs)
        extra_specs = [
            pl.BlockSpec((N_GROUPS_PAD, D_MODEL), lambda i: (0, 0)),
            pl.BlockSpec((None, D_MODEL, tn), lambda i: (layer + 1, 0, i)),
            pl.BlockSpec((None, 1, tn), lambda i: (layer + 1, 0, i)),
        ]
        out_shape = (jax.ShapeDtypeStruct((N_TOK, D_MODEL), F32),
                     jax.ShapeDtypeStruct((N_GROUPS_PAD, 6 * D_MODEL), F32))
        out_specs = (row_spec, pl.BlockSpec((N_GROUPS_PAD, tn), lambda i: (0, i)))
    return pl.pallas_call(
        functools.partial(_post_mixer_kernel, layer=layer, wo_idx=wo_idx, tm=tm, final=final),
        out_shape=out_shape,
        grid=(N_TOK // tm,),
        in_specs=[
            row_spec,
            row_spec,
            pl.BlockSpec((None, 1, D_MODEL), lambda i: (layer, 0, 0)),
            pl.BlockSpec((None, 1, 6 * D_MODEL), lambda i: (_group_of_tile(i, tm), 0, 0)),
        ] + extra_specs + [hbm, hbm, hbm],
        out_specs=out_specs,
        scratch_shapes=[
            pltpu.VMEM((D_MODEL, D_MODEL), BF16),
            pltpu.VMEM((N_FFN_CHUNKS, D_MODEL, FFN_CHUNK), BF16),
            pltpu.VMEM((N_FFN_CHUNKS, D_MODEL, FFN_CHUNK), BF16),
            pltpu.VMEM((FFN_HIDDEN, D_MODEL), BF16),
            pltpu.VMEM((tm, FFN_HIDDEN), BF16),
            pltpu.VMEM((2, 2, D_MODEL, FFN_CHUNK), F32),
            pltpu.VMEM((2, FFN_CHUNK, D_MODEL), F32),
            pltpu.SemaphoreType.DMA((2, 2)),
            pltpu.SemaphoreType.DMA((2,)),
        ],
        compiler_params=_params(1),
        name="post_mixer",
    )(x, mix, g_all.reshape(DEPTH, 1, D_MODEL), mod_l, *extra_in, w_out_all, w_gate_up, w_down)


def kernel(x_prompt, x_sample, cache_a_k, cache_a_v, cache_c_k, cache_c_v, c, c_ctx, w_mod, b_mod, norm_mix, norm_ffn, w_in_ab, rpb_a, w_pool, pool_scale, w_out_ab, w_in_c, sink_c, w_out_c, w_gate_up, w_down, norm_final):
    xs = (x_prompt.reshape(N_CTX_TOK, D_MODEL), x_sample.reshape(N_LAT_TOK, D_MODEL))
    cond8 = jnp.concatenate(
        [c_ctx[None], c, jnp.zeros((N_GROUPS_PAD - 1 - DEC_BATCH, D_MODEL), F32)], axis=0)
    b_mod3 = b_mod.reshape(DEPTH, 1, 6 * D_MODEL)
    mod, bias_tiles = _modulation0_and_bias(cond8, w_mod, b_mod3, rpb_a)

    n_ab = cache_a_k.shape[1]
    n_c = cache_c_k.shape[1]
    cache_a_k, cache_a_v, cache_c_k, cache_c_v = (
        jnp.transpose(t, (0, 1, 3, 4, 2)) for t in (cache_a_k, cache_a_v, cache_c_k, cache_c_v))
    rope_tables = _rope_tables()

    new_a = []
    new_c = []
    for l in range(DEPTH):
        mod_l = mod.reshape(N_GROUPS_PAD, 1, 6 * D_MODEL)
        if l % 2 == 0:
            i = l // 2
            qkv, *rest = _inproj(xs, norm_mix, mod_l, l, w_in_ab, i, A_HEADS, A_WIDTH, n_ab,
                                 new_a)
            new_a, u = rest[:2], rest[2]
            if len(xs) == 2:
                xs = (rest[3],)
            mix = _ctx_attn_a(qkv)
            mix = _lat_attn_a(qkv, cache_a_k, cache_a_v, bias_tiles, i, mix)
            mix = _pool(u, w_pool, pool_scale, i, mix)
            w_out, wo_idx = w_out_ab, i
        else:
            j = l // 2
            qkv, *new_c = _inproj(xs, norm_mix, mod_l, l, w_in_c, j, C_KV_HEADS, C_Q_WIDTH, n_c,
                                  new_c, rope_tables)
            mix = _ctx_attn_c(qkv, sink_c, j)
            mix = _lat_attn_c(qkv, cache_c_k, cache_c_v, sink_c, j, mix)
            w_out, wo_idx = w_out_c, j
        if l + 1 < DEPTH:
            x, mod = _post_mixer(xs[0], mix, norm_ffn, mod_l, l, w_out, wo_idx, w_gate_up,
                                 w_down, next_mod_inputs=(cond8, w_mod, b_mod3))
            xs = (x,)
        else:
            y_ctx, y_lat = _post_mixer(xs[0], mix, norm_ffn, mod_l, l, w_out, wo_idx, w_gate_up,
                                       w_down, norm_final)

    new_caches = [jnp.transpose(t, (0, 1, 4, 2, 3)) for t in (*new_a, *new_c)]
    return (y_ctx.reshape(BATCH, SEQ, D_MODEL), y_lat.reshape(DEC_BATCH, DEC_SEQ, D_MODEL),
            *new_caches)
```

```python
import functools

import jax
import jax.numpy as jnp
import numpy as np
from jax import lax
from jax.experimental import pallas as pl
from jax.experimental.pallas import tpu as pltpu

D_MODEL = 1024
BATCH = 16
SEQ = 256
DEPTH = 4
DEC_BATCH = 2
DEC_SEQ = 1024
PAST_LEN = 512
GRID_W = 64
HEAD_DIM = 64
A_WIDTH = 512
A_HEADS = 8
B_WIDTH = 512
POOL_WINDOWS = (2, 4, 8, 16)
B_GROUP_DIM = 128
NA_ROWS = 8
NA_COLS = 16
C_HEADS = 16
C_KV_HEADS = 4
C_GROUP = C_HEADS // C_KV_HEADS
C_Q_WIDTH = 1024
C_KV_WIDTH = 256
C_BLOCK = 128
FFN_HIDDEN = 2816
ROPE_BASE = 10000.0
EPS = 1e-6
NEG_INF = -1e30

N_CTX_TOK = BATCH * SEQ
N_LAT_TOK = DEC_BATCH * DEC_SEQ
N_TOK = N_CTX_TOK + N_LAT_TOK
GRID_ROWS = DEC_SEQ // GRID_W
N_GROUPS_PAD = 8

VMEM_LIMIT = 56 * 1024 * 1024

F32 = jnp.float32
BF16 = jnp.bfloat16


def _params(n_axes):
    return pltpu.CompilerParams(dimension_semantics=("arbitrary",) * n_axes,
                                vmem_limit_bytes=VMEM_LIMIT)


def _group_of_tile(i, tm):
    row0 = i * tm
    return jnp.where(row0 < N_CTX_TOK, 0, 1 + (row0 - N_CTX_TOK) // DEC_SEQ)


def _dot_nt(a, b):
    return lax.dot_general(a, b, (((1,), (1,)), ((), ())), preferred_element_type=F32)


def _dot(a, b):
    return jnp.dot(a, b, preferred_element_type=F32)


def _dot_tn(a, b):
    return lax.dot_general(a, b, (((0,), (0,)), ((), ())), preferred_element_type=F32)


def _softmax_pv_t(score_fns, sink_rows, values, s_refs, p_refs):
    n_groups, n_blocks = len(score_fns), len(s_refs)
    for g in range(n_groups):
        for b in range(n_blocks):
            s_refs[b][g] = score_fns[g][b]()
    inv_l = []
    for g in range(n_groups):
        m = sink_rows[g]
        for b in range(n_blocks):
            m = jnp.maximum(m, jnp.max(s_refs[b][g], axis=0, keepdims=True))
        l = jnp.exp(sink_rows[g] - m)
        for b in range(n_blocks):
            p = jnp.exp(s_refs[b][g] - m)
            l = l + jnp.sum(p, axis=0, keepdims=True)
            p_refs[b][g] = p.astype(BF16)
        inv_l.append(1.0 / l)
    outs = []
    for g in range(n_groups):
        o = None
        for b in range(n_blocks):
            pv = _dot_tn(values[g][b], p_refs[b][g])
            o = pv if o is None else o + pv
        outs.append(o * inv_l[g])
    return outs


def _modulation_block(cond_ref, w_ref, b_ref):
    c = cond_ref[...]
    s = c / (1.0 + jnp.exp(-c))
    return _dot(s.astype(BF16), w_ref[...].astype(BF16)) + b_ref[...]


def _mod_bias_kernel(rpb_ref, cond_ref, w_ref, b_ref, o_ref, bias_ref, *, heads_per_step):
    o_ref[...] = _modulation_block(cond_ref, w_ref, b_ref)
    pair0 = pl.program_id(0) * heads_per_step
    for t in range(heads_per_step):
        _write_bias_tiles(rpb_ref, (pair0 + t) * (N_DROW * N_DCOL), bias_ref.at[t])


def _modulation0_and_bias(cond8, w_mod, b_mod3, rpb_a):
    n_col_blocks = 4
    tn = 6 * D_MODEL // n_col_blocks
    n_bias_layers = rpb_a.shape[0]
    heads_per_step = n_bias_layers * A_HEADS // n_col_blocks
    steps_per_layer = A_HEADS // heads_per_step
    return pl.pallas_call(
        functools.partial(_mod_bias_kernel, heads_per_step=heads_per_step),
        out_shape=(
            jax.ShapeDtypeStruct((N_GROUPS_PAD, 6 * D_MODEL), F32),
            jax.ShapeDtypeStruct(
                (n_bias_layers, A_HEADS, N_BIAS_TILES, GRID_W, 2 * GRID_W), F32)),
        grid=(n_col_blocks,),
        in_specs=[
            pl.BlockSpec(memory_space=pltpu.SMEM),
            pl.BlockSpec((N_GROUPS_PAD, D_MODEL), lambda j: (0, 0)),
            pl.BlockSpec((None, D_MODEL, tn), lambda j: (0, 0, j)),
            pl.BlockSpec((None, 1, tn), lambda j: (0, 0, j)),
        ],
        out_specs=(
            pl.BlockSpec((N_GROUPS_PAD, tn), lambda j: (0, j)),
            pl.BlockSpec((None, heads_per_step, N_BIAS_TILES, GRID_W, 2 * GRID_W),
                         lambda j: (j // steps_per_layer, j % steps_per_layer, 0, 0, 0))),
        compiler_params=_params(1),
        name="modulation_bias",
    )(rpb_a.reshape(-1), cond8, w_mod, b_mod3)


def _norm_mod(x, g, mod_ref, shift_idx):
    var = jnp.mean(x * x, axis=-1, keepdims=True)
    y = x * lax.rsqrt(var + EPS) * g
    shift = mod_ref[:, shift_idx * D_MODEL:(shift_idx + 1) * D_MODEL]
    scale = mod_ref[:, (shift_idx + 1) * D_MODEL:(shift_idx + 2) * D_MODEL]
    return y * (1.0 + scale) + shift


QKV_WIDTH = 3 * A_WIDTH
Q_SCALE = HEAD_DIM ** -0.5
PROJ_CHUNK = 4 * HEAD_DIM
ROPE_TABLE_WIDTH = PROJ_CHUNK


def _rope(x, cos, sin_signed):
    n = x.shape[-1]
    lane = lax.broadcasted_iota(jnp.int32, x.shape, x.ndim - 1)
    first = (lane % 32) < 16
    partner = jnp.where(first, pltpu.roll(x, n - 16, axis=x.ndim - 1),
                        pltpu.roll(x, 16, axis=x.ndim - 1))
    return x * cos + partner * sin_signed


def _inproj_kernel(*refs, tm, n_heads, q_width, n_prev, split_x, rope, has_u):
    n_x = 2 if split_x else 1
    g_ref, mod_ref, w_ref = refs[n_x:n_x + 3]
    n_in = n_x + 3 + (2 if rope else 0) + n_prev
    qkv_ref, ck_ref, cv_ref = refs[n_in:n_in + 3]
    extra_out = refs[n_in + 3:-3]
    wbf_ref, h_ref, res_ref = refs[-3:]
    i = pl.program_id(0)
    is_ctx = i < N_CTX_TOK // tm
    kv_width = (QKV_WIDTH - q_width) // 2
    k_col, v_col = q_width, q_width + kv_width
    n_out = res_ref.shape[1]
    chunk = PROJ_CHUNK

    @pl.when(i == 0)
    def _():
        wbf_ref[...] = w_ref[...].astype(BF16)

    def tile(ctx):
        if split_x:
            x = (refs[0] if ctx else refs[1])[...]
            extra_out[-1][...] = x
        else:
            x = refs[0][...]
        h_ref[...] = _norm_mod(x, g_ref[...], mod_ref, 0).astype(BF16)
        def matmul(c0):
            cols = slice(c0, c0 + chunk)
            res_ref[:, cols] = _dot(h_ref[...], wbf_ref[:, cols])

        def epilogue(c0):
            cols = slice(c0, c0 + chunk)
            if c0 >= QKV_WIDTH:
                extra_out[0][:, c0 - QKV_WIDTH:c0 - QKV_WIDTH + chunk] = res_ref[:, cols]
                return
            r = res_ref[:, cols]
            if rope and not ctx and c0 < v_col:
                cos_ref, sin_ref = refs[n_x + 3:n_x + 5]
                r = _rope(r, cos_ref[...], sin_ref[...])
            if c0 < q_width:
                r = r * Q_SCALE
            qkv_ref[:, cols] = r.astype(BF16)
            if ctx and c0 >= k_col:
                c_ref, col0 = (ck_ref, k_col) if c0 < v_col else (cv_ref, v_col)
                for req in range(tm // SEQ):
                    for hc in range(chunk // HEAD_DIM):
                        hd = (c0 - col0) // HEAD_DIM + hc
                        c_ref[req, hd] = res_ref[
                            req * SEQ:(req + 1) * SEQ,
                            c0 + hc * HEAD_DIM:c0 + (hc + 1) * HEAD_DIM].T

        for c0 in range(0, n_out, chunk):
            matmul(c0)
            epilogue(c0)

    pl.when(is_ctx)(lambda: tile(True))
    pl.when(jnp.logical_not(is_ctx))(lambda: tile(False))


def _inproj(xs, g_all, mod_l, layer, w_all, w_idx, n_heads, q_width, n_slots, prev_caches,
            rope_tables=None):
    tm = 512
    n_out = w_all.shape[2]
    n_ctx_tiles = N_CTX_TOK // tm
    split_x = len(xs) == 2
    rope = rope_tables is not None
    has_u = n_out > QKV_WIDTH
    cache_shape = jax.ShapeDtypeStruct((BATCH, n_slots, n_heads, HEAD_DIM, SEQ), F32)
    cache_spec = pl.BlockSpec(
        (tm // SEQ, None, n_heads, HEAD_DIM, SEQ),
        lambda i: (jnp.minimum(i, n_ctx_tiles - 1), w_idx, 0, 0, 0))
    row_spec = pl.BlockSpec((tm, D_MODEL), lambda i: (i, 0))
    if split_x:
        x_specs = [
            pl.BlockSpec((tm, D_MODEL), lambda i: (jnp.minimum(i, n_ctx_tiles - 1), 0)),
            pl.BlockSpec((tm, D_MODEL), lambda i: (jnp.maximum(i - n_ctx_tiles, 0), 0)),
        ]
    else:
        x_specs = [row_spec]
    out_shape = [jax.ShapeDtypeStruct((N_TOK, QKV_WIDTH), BF16), cache_shape, cache_shape]
    out_specs = [pl.BlockSpec((tm, QKV_WIDTH), lambda i: (i, 0)), cache_spec, cache_spec]
    if has_u:
        out_shape.append(jax.ShapeDtypeStruct((N_TOK, n_out - QKV_WIDTH), F32))
        out_specs.append(pl.BlockSpec((tm, n_out - QKV_WIDTH), lambda i: (i, 0)))
    if split_x:
        out_shape.append(jax.ShapeDtypeStruct((N_TOK, D_MODEL), F32))
        out_specs.append(row_spec)
    rope_in, rope_specs = [], []
    if rope:
        tiles_per_seq = DEC_SEQ // tm
        rope_spec = pl.BlockSpec(
            (tm, ROPE_TABLE_WIDTH),
            lambda i: (jnp.maximum(i - n_ctx_tiles, 0) % tiles_per_seq, 0))
        rope_in, rope_specs = list(rope_tables), [rope_spec, rope_spec]
    n_prev = len(prev_caches)
    n_before = len(xs) + 3 + len(rope_in)
    return pl.pallas_call(
        functools.partial(_inproj_kernel, tm=tm, n_heads=n_heads, q_width=q_width,
                          n_prev=n_prev, split_x=split_x, rope=rope, has_u=has_u),
        out_shape=out_shape,
        grid=(N_TOK // tm,),
        in_specs=x_specs + [
            pl.BlockSpec((None, 1, D_MODEL), lambda i: (layer, 0, 0)),
            pl.BlockSpec((None, 1, 6 * D_MODEL), lambda i: (_group_of_tile(i, tm), 0, 0)),
            pl.BlockSpec((None, D_MODEL, n_out), lambda i: (w_idx, 0, 0),
                         pipeline_mode=pl.Buffered(1)),
        ] + rope_specs + [pl.BlockSpec(memory_space=pl.ANY)] * n_prev,
        out_specs=out_specs,
        scratch_shapes=[pltpu.VMEM((D_MODEL, n_out), BF16), pltpu.VMEM((tm, D_MODEL), BF16),
                        pltpu.VMEM((tm, n_out), F32)],
        input_output_aliases={n_before + k: 1 + k for k in range(n_prev)},
        compiler_params=_params(1),
        name="inproj",
    )(*xs, g_all.reshape(DEPTH, 1, D_MODEL), mod_l, w_all, *rope_in, *prev_caches)


def _ctx_attn_a_kernel(p_ref, o_ref, s_ref, p_scr):
    for h in range(A_HEADS):
        c0 = h * HEAD_DIM
        s_ref[h] = _dot_nt(p_ref[:, c0:c0 + HEAD_DIM],
                           p_ref[:, A_WIDTH + c0:A_WIDTH + c0 + HEAD_DIM])
    inv_l = []
    for h in range(A_HEADS):
        s = s_ref[h]
        p = jnp.exp(s - jnp.max(s, axis=-1, keepdims=True))
        inv_l.append(1.0 / jnp.sum(p, axis=-1, keepdims=True))
        p_scr[h] = p.astype(BF16)
    for h in range(A_HEADS):
        c0 = h * HEAD_DIM
        v = p_ref[:, 2 * A_WIDTH + c0:2 * A_WIDTH + c0 + HEAD_DIM]
        o_ref[:, c0:c0 + HEAD_DIM] = _dot(p_scr[h], v) * inv_l[h]


def _ctx_attn_a(qkv):
    return pl.pallas_call(
        _ctx_attn_a_kernel,
        out_shape=jax.ShapeDtypeStruct((N_TOK, D_MODEL), F32),
        grid=(BATCH,),
        in_specs=[pl.BlockSpec((SEQ, QKV_WIDTH), lambda b: (b, 0))],
        out_specs=pl.BlockSpec((SEQ, A_WIDTH), lambda b: (b, 0)),
        scratch_shapes=[pltpu.VMEM((A_HEADS, SEQ, SEQ), F32),
                        pltpu.VMEM((A_HEADS, SEQ, SEQ), BF16)],
        compiler_params=_params(1),
        name="ctx_attn_a",
    )(qkv)


N_DROW = 2 * NA_ROWS - 1
N_DCOL = 2 * NA_COLS - 1
N_BIAS_TILES = 16
BIAS_TILE_LEFT_PAD = 14
BIAS_TILE_RIGHT_PAD = 15
MID_DROW = NA_ROWS - 1 - NA_ROWS // 2


def _write_bias_tiles(rpb_ref, base, o_ref):
    qi = lax.broadcasted_iota(jnp.int32, (GRID_W, 2 * GRID_W), 0)
    lane = lax.broadcasted_iota(jnp.int32, (GRID_W, 2 * GRID_W), 1)
    right = lane >= GRID_W
    kc = jnp.where(right, lane - GRID_W, lane)
    qstart = jnp.clip(qi - NA_COLS // 2, 0, GRID_W - NA_COLS)
    valid = (kc >= qstart) & (kc < qstart + NA_COLS)

    offs = lax.broadcasted_iota(jnp.int32, (1, 2 * GRID_W), 1) & (GRID_W - 1)
    shift = 2 * GRID_W - (NA_COLS - 1)
    rows = []
    for dr in range(N_DROW):
        vec = jnp.zeros((1, 2 * GRID_W), F32)
        for d in range(N_DCOL):
            vec = jnp.where(offs == d, rpb_ref[base + dr * N_DCOL + d], vec)
        rows.append(pltpu.roll(jnp.broadcast_to(vec, (GRID_W, 2 * GRID_W)), shift, axis=1,
                               stride=1, stride_axis=0))

    for t in range(N_DROW - 1):
        o_ref[t] = jnp.where(valid, jnp.where(right, rows[t + 1], rows[t]), NEG_INF)
    o_ref[BIAS_TILE_LEFT_PAD] = jnp.where(valid & right, rows[MID_DROW], NEG_INF)
    o_ref[BIAS_TILE_RIGHT_PAD] = jnp.where(valid & jnp.logical_not(right),
                                           rows[MID_DROW + NA_ROWS - 1], NEG_INF)


NA_GROUP_ROWS = 4


def _na_window(r):
    start = min(max(r - NA_ROWS // 2, 0), GRID_ROWS - NA_ROWS)
    first_drow = start - r + NA_ROWS - 1
    if start % 2 == 0:
        return start, [first_drow + 2 * p for p in range(NA_ROWS // 2)]
    assert first_drow == MID_DROW
    inner = [first_drow + 1 + 2 * p for p in range(NA_ROWS // 2 - 1)]
    return start - 1, [BIAS_TILE_LEFT_PAD] + inner + [BIAS_TILE_RIGHT_PAD]


def _lat_attn_a_kernel(q_ref, k_ref, v_ref, kc_ref, vc_ref, bias_ref, mix_ref, o_ref,
                       s_loc, s_ctx, p_loc, p_ctx):
    del mix_ref
    hp = pl.program_id(1)

    @pl.when((pl.program_id(0) == 0) & (hp == 0))
    def _():
        p_loc[...] = jnp.zeros_like(p_loc)

    groups = []
    for g0 in range(0, GRID_ROWS, NA_GROUP_ROWS):
        windows = [_na_window(r) for r in range(g0, g0 + NA_GROUP_ROWS)]
        lo = min(first for first, _ in windows)
        hi = max(first + 2 * len(tiles) for first, tiles in windows)
        groups.append((slice(g0 * GRID_W, (g0 + NA_GROUP_ROWS) * GRID_W),
                       slice(lo * GRID_W, hi * GRID_W)))
    for hh in range(2):
        c0 = hh * HEAD_DIM
        for q_rows, keys in groups:
            s_loc[hh, q_rows, keys] = _dot_nt(q_ref[q_rows, c0:c0 + HEAD_DIM],
                                              k_ref[keys, c0:c0 + HEAD_DIM])
        s_ctx[hh] = _dot(q_ref[:, c0:c0 + HEAD_DIM],
                         kc_ref[hh].astype(BF16))
    inv_l = [[], []]
    for hh in range(2):
        for r in range(GRID_ROWS):
            rows = slice(r * GRID_W, (r + 1) * GRID_W)
            first_row, tiles = _na_window(r)
            cols = slice(first_row * GRID_W, (first_row + 2 * len(tiles)) * GRID_W)
            bias = jnp.concatenate([bias_ref[hh, t] for t in tiles], axis=1)
            sc = s_ctx[hh, rows, :]
            sl = s_loc[hh, rows, cols] + bias
            m = jnp.maximum(jnp.max(sc, axis=-1, keepdims=True),
                            jnp.max(sl, axis=-1, keepdims=True))
            pc = jnp.exp(sc - m)
            pw = jnp.exp(sl - m)
            inv_l[hh].append(1.0 / (jnp.sum(pc, axis=-1, keepdims=True)
                                    + jnp.sum(pw, axis=-1, keepdims=True)))
            p_ctx[hh, rows, :] = pc.astype(BF16)
            p_loc[hh, rows, cols] = pw.astype(BF16)
    for hh in range(2):
        c0 = hh * HEAD_DIM
        o_loc = jnp.concatenate(
            [_dot(p_loc[hh, q_rows, keys], v_ref[keys, c0:c0 + HEAD_DIM])
             for q_rows, keys in groups], axis=0)
        o = _dot_nt(p_ctx[hh], vc_ref[hh].astype(BF16)) + o_loc
        o_ref[:, c0:c0 + HEAD_DIM] = o * jnp.concatenate(inv_l[hh], axis=0)


def _lat_attn_a(qkv, cache_k, cache_v, bias_tiles, i_layer, mix):
    row0 = N_CTX_TOK // DEC_SEQ
    pair = 2 * HEAD_DIM
    k_col0 = A_WIDTH // pair
    v_col0 = 2 * A_WIDTH // pair
    return pl.pallas_call(
        _lat_attn_a_kernel,
        out_shape=jax.ShapeDtypeStruct((N_TOK, D_MODEL), F32),
        grid=(DEC_BATCH, A_HEADS // 2),
        in_specs=[
            pl.BlockSpec((DEC_SEQ, pair), lambda b, hp: (row0 + b, hp)),
            pl.BlockSpec((DEC_SEQ, pair), lambda b, hp: (row0 + b, k_col0 + hp)),
            pl.BlockSpec((DEC_SEQ, pair), lambda b, hp: (row0 + b, v_col0 + hp)),
            pl.BlockSpec((None, None, 2, HEAD_DIM, PAST_LEN),
                         lambda b, hp: (b, i_layer, hp, 0, 0)),
            pl.BlockSpec((None, None, 2, HEAD_DIM, PAST_LEN),
                         lambda b, hp: (b, i_layer, hp, 0, 0)),
            pl.BlockSpec((None, 2, N_BIAS_TILES, GRID_W, 2 * GRID_W),
                         lambda b, hp: (i_layer, hp, 0, 0, 0)),
            pl.BlockSpec(memory_space=pl.ANY),
        ],
        out_specs=pl.BlockSpec((DEC_SEQ, pair), lambda b, hp: (row0 + b, hp)),
        scratch_shapes=[pltpu.VMEM((2, DEC_SEQ, DEC_SEQ), F32),
                        pltpu.VMEM((2, DEC_SEQ, PAST_LEN), F32),
                        pltpu.VMEM((2, DEC_SEQ, DEC_SEQ), BF16),
                        pltpu.VMEM((2, DEC_SEQ, PAST_LEN), BF16)],
        input_output_aliases={6: 0},
        compiler_params=_params(2),
        name="lat_attn_a",
    )(qkv, qkv, qkv, cache_k, cache_v, bias_tiles, mix)


def _pool_kernel(u_ref, wp_ref, ps_ref, mix_ref, o_ref, *, rows):
    del mix_ref
    n = jnp.where(pl.program_id(0) < N_CTX_TOK // rows, SEQ, DEC_SEQ)
    t = lax.broadcasted_iota(jnp.int32, (rows, B_GROUP_DIM), 0) & (n - 1)

    def earlier(x, k):
        return jnp.where(t >= k, pltpu.roll(x, k, axis=0), 0.0)

    def later(x, k):
        return jnp.where(t < n - k, pltpu.roll(x, rows - k, axis=0), 0.0)

    tokens_before = t.astype(F32)
    tokens_from = (n - t).astype(F32)

    for g, w in enumerate(POOL_WINDOWS):
        c0 = g * B_GROUP_DIM
        half = w // 2
        u = u_ref[:, c0:c0 + B_GROUP_DIM]
        before, after = u, u
        k = 1
        while k < half:
            before = before + earlier(before, k)
            after = after + later(after, k)
            k *= 2
        total = earlier(before, 1) + after
        count = jnp.minimum(tokens_before, half) + jnp.minimum(tokens_from, half)
        pooled = total / count - u
        y = _dot(pooled.astype(BF16), wp_ref[g].astype(BF16))
        o_ref[:, c0:c0 + B_GROUP_DIM] = y * ps_ref[:, c0:c0 + B_GROUP_DIM]


def _pool(u, w_pool, pool_scale, i_layer, mix):
    n_groups = len(POOL_WINDOWS)
    rows = DEC_SEQ
    return pl.pallas_call(
        functools.partial(_pool_kernel, rows=rows),
        out_shape=jax.ShapeDtypeStruct((N_TOK, D_MODEL), F32),
        grid=(N_TOK // rows,),
        in_specs=[
            pl.BlockSpec((rows, B_WIDTH), lambda b: (b, 0)),
            pl.BlockSpec((None, n_groups, B_GROUP_DIM, B_GROUP_DIM),
                         lambda b: (i_layer, 0, 0, 0)),
            pl.BlockSpec((None, 1, B_WIDTH), lambda b: (i_layer, 0, 0)),
            pl.BlockSpec(memory_space=pl.ANY),
        ],
        out_specs=pl.BlockSpec((rows, B_WIDTH), lambda b: (b, 1)),
        input_output_aliases={3: 0},
        compiler_params=_params(1),
        name="pool_mixer",
    )(u, w_pool, pool_scale.reshape(-1, 1, B_WIDTH), mix)


def _sink_row(sink_ref, head0, queries_per_head):
    lane = lax.broadcasted_iota(jnp.int32, (1, C_GROUP * queries_per_head), 1)
    row = jnp.full((1, C_GROUP * queries_per_head), sink_ref[head0], F32)
    for g in range(1, C_GROUP):
        row = jnp.where(lane >= g * queries_per_head, sink_ref[head0 + g], row)
    return row


CTX_C_REQS_PER_STEP = 2


def _ctx_attn_c_kernel(sink_ref, p_ref, o_ref, s_ref, p_scr, *, sink0):
    for req in range(CTX_C_REQS_PER_STEP):
        rows = slice(req * SEQ, (req + 1) * SEQ)

        def scores(kk):
            q = jnp.concatenate(
                [p_ref[rows, h * HEAD_DIM:(h + 1) * HEAD_DIM]
                 for h in range(kk * C_GROUP, (kk + 1) * C_GROUP)], axis=0)
            k0 = C_Q_WIDTH + kk * HEAD_DIM
            return _dot_nt(p_ref[rows, k0:k0 + HEAD_DIM], q)

        v0 = C_Q_WIDTH + C_KV_WIDTH
        outs = _softmax_pv_t(
            [[functools.partial(scores, kk)] for kk in range(C_KV_HEADS)],
            [_sink_row(sink_ref, sink0 + kk * C_GROUP, SEQ) for kk in range(C_KV_HEADS)],
            [[p_ref[rows, v0 + kk * HEAD_DIM:v0 + (kk + 1) * HEAD_DIM]]
             for kk in range(C_KV_HEADS)],
            [s_ref], [p_scr])
        outs_t = [o_t[:, g * SEQ:(g + 1) * SEQ] for o_t in outs for g in range(C_GROUP)]
        o_ref[rows, :] = jnp.concatenate(outs_t, axis=0).T


def _ctx_attn_c(qkv, sink_all, j_layer):
    return pl.pallas_call(
        functools.partial(_ctx_attn_c_kernel, sink0=j_layer * C_HEADS),
        out_shape=jax.ShapeDtypeStruct((N_TOK, D_MODEL), F32),
        grid=(BATCH // CTX_C_REQS_PER_STEP,),
        in_specs=[
            pl.BlockSpec(memory_space=pltpu.SMEM),
            pl.BlockSpec((CTX_C_REQS_PER_STEP * SEQ, QKV_WIDTH), lambda b: (b, 0)),
        ],
        out_specs=pl.BlockSpec((CTX_C_REQS_PER_STEP * SEQ, C_Q_WIDTH), lambda b: (b, 0)),
        scratch_shapes=[pltpu.VMEM((C_KV_HEADS, SEQ, C_GROUP * SEQ), F32),
                        pltpu.VMEM((C_KV_HEADS, SEQ, C_GROUP * SEQ), BF16)],
        compiler_params=_params(1),
        name="ctx_attn_c",
    )(sink_all.reshape(-1), qkv)


def _lat_attn_c_kernel(sink_ref, q_ref, k_ref, v_ref, kc_ref, vc_ref, mix_ref, o_ref,
                       kctx_ref, vctx_ref, sc_ref, sb_ref, pc_ref, pb_ref, *, sink0):
    del mix_ref
    j = pl.program_id(1)
    n_blocks = DEC_SEQ // C_BLOCK

    @pl.when(j == 0)
    def _():
        for kk in range(C_KV_HEADS):
            c0 = kk * HEAD_DIM
            kctx_ref[:, c0:c0 + HEAD_DIM] = kc_ref[kk].T.astype(BF16)
            vctx_ref[:, c0:c0 + HEAD_DIM] = vc_ref[kk].T.astype(BF16)

    n_q = C_GROUP * C_BLOCK
    qi = lax.broadcasted_iota(jnp.int32, (C_BLOCK, n_q), 1) % C_BLOCK
    jl = lax.broadcasted_iota(jnp.int32, (C_BLOCK, n_q), 0)
    valid_prev = (jl >= qi) & (j > 0)
    valid_next = (jl <= qi) & (j < n_blocks - 1)
    rows_prev = pl.ds(pl.multiple_of(jnp.maximum(j - 1, 0) * C_BLOCK, C_BLOCK), C_BLOCK)
    rows_cur = pl.ds(pl.multiple_of(j * C_BLOCK, C_BLOCK), C_BLOCK)
    rows_next = pl.ds(pl.multiple_of(jnp.minimum(j + 1, n_blocks - 1) * C_BLOCK, C_BLOCK),
                      C_BLOCK)

    def stacked_q(kk):
        return jnp.concatenate(
            [q_ref[:, h * HEAD_DIM:(h + 1) * HEAD_DIM]
             for h in range(kk * C_GROUP, (kk + 1) * C_GROUP)], axis=0)

    def band(ref, kk):
        cols = slice(kk * HEAD_DIM, (kk + 1) * HEAD_DIM)
        return jnp.concatenate(
            [ref[rows_prev, cols], ref[rows_cur, cols], ref[rows_next, cols]], axis=0)

    def ctx_scores(kk):
        return _dot_nt(kctx_ref[:, kk * HEAD_DIM:(kk + 1) * HEAD_DIM], stacked_q(kk))

    def band_scores(kk):
        s_raw = _dot_nt(band(k_ref, kk), stacked_q(kk))
        return jnp.concatenate(
            [jnp.where(valid_prev, s_raw[:C_BLOCK], NEG_INF),
             s_raw[C_BLOCK:2 * C_BLOCK],
             jnp.where(valid_next, s_raw[2 * C_BLOCK:], NEG_INF)], axis=0)

    kv_heads = range(C_KV_HEADS)
    outs = _softmax_pv_t(
        [[functools.partial(ctx_scores, kk), functools.partial(band_scores, kk)]
         for kk in kv_heads],
        [_sink_row(sink_ref, sink0 + kk * C_GROUP, C_BLOCK) for kk in kv_heads],
        [[vctx_ref[:, kk * HEAD_DIM:(kk + 1) * HEAD_DIM], band(v_ref, kk)] for kk in kv_heads],
        [sc_ref, sb_ref], [pc_ref, pb_ref])
    outs_t = [o_t[:, g * C_BLOCK:(g + 1) * C_BLOCK] for o_t in outs for g in range(C_GROUP)]
    o_ref[...] = jnp.concatenate(outs_t, axis=0).T


def _lat_attn_c(qkv, cache_k, cache_v, sink, j_layer, mix):
    n_blocks = DEC_SEQ // C_BLOCK
    q_row0 = N_CTX_TOK // C_BLOCK
    kv_row0 = N_CTX_TOK // DEC_SEQ
    k_col = C_Q_WIDTH // C_KV_WIDTH
    n_q = C_GROUP * C_BLOCK
    return pl.pallas_call(
        functools.partial(_lat_attn_c_kernel, sink0=j_layer * C_HEADS),
        out_shape=jax.ShapeDtypeStruct((N_TOK, D_MODEL), F32),
        grid=(DEC_BATCH, n_blocks),
        in_specs=[
            pl.BlockSpec(memory_space=pltpu.SMEM),
            pl.BlockSpec((C_BLOCK, C_Q_WIDTH), lambda b, j: (q_row0 + b * n_blocks + j, 0)),
            pl.BlockSpec((DEC_SEQ, C_KV_WIDTH), lambda b, j: (kv_row0 + b, k_col)),
            pl.BlockSpec((DEC_SEQ, C_KV_WIDTH), lambda b, j: (kv_row0 + b, k_col + 1)),
            pl.BlockSpec((None, None, C_KV_HEADS, HEAD_DIM, PAST_LEN),
                         lambda b, j: (b, j_layer, 0, 0, 0)),
            pl.BlockSpec((None, None, C_KV_HEADS, HEAD_DIM, PAST_LEN),
                         lambda b, j: (b, j_layer, 0, 0, 0)),
            pl.BlockSpec(memory_space=pl.ANY),
        ],
        out_specs=pl.BlockSpec((C_BLOCK, C_Q_WIDTH),
                               lambda b, j: (q_row0 + b * n_blocks + j, 0)),
        input_output_aliases={6: 0},
        scratch_shapes=[pltpu.VMEM((PAST_LEN, C_KV_WIDTH), BF16),
                        pltpu.VMEM((PAST_LEN, C_KV_WIDTH), BF16),
                        pltpu.VMEM((C_KV_HEADS, PAST_LEN, n_q), F32),
                        pltpu.VMEM((C_KV_HEADS, 3 * C_BLOCK, n_q), F32),
                        pltpu.VMEM((C_KV_HEADS, PAST_LEN, n_q), BF16),
                        pltpu.VMEM((C_KV_HEADS, 3 * C_BLOCK, n_q), BF16)],
        compiler_params=_params(2),
        name="lat_attn_c",
    )(sink.reshape(-1), qkv, qkv, qkv, cache_k, cache_v, mix)


def _rope_tables():
    t = np.arange(DEC_SEQ)
    pos = np.stack([t // GRID_W, t % GRID_W], axis=-1).astype(np.float64)
    half = HEAD_DIM // 4
    inv = ROPE_BASE ** (-np.arange(half, dtype=np.float64) / half)
    ang = pos[:, :, None] * inv
    cos = np.cos(ang)
    sin = np.sin(ang)
    cos64 = np.stack([cos, cos], axis=2).reshape(DEC_SEQ, HEAD_DIM)
    sin64 = np.stack([-sin, sin], axis=2).reshape(DEC_SEQ, HEAD_DIM)
    reps = ROPE_TABLE_WIDTH // HEAD_DIM
    return (jnp.asarray(np.tile(cos64, (1, reps)), F32),
            jnp.asarray(np.tile(sin64, (1, reps)), F32))


FFN_CHUNK = 256
N_FFN_CHUNKS = FFN_HIDDEN // FFN_CHUNK
N_WO_PIECES = D_MODEL // FFN_CHUNK


def _post_mixer_kernel(*refs, layer, wo_idx, tm, final):
    x_ref, mix_ref, g_ref, mod_ref = refs[:4]
    n_in = 5 if final else 7
    wo_hbm, wgu_hbm, wd_hbm = refs[n_in:n_in + 3]
    n_out = 2
    out_refs = refs[n_in + 3:n_in + 3 + n_out]
    (wo_s, wg_s, wu_s, wd_s, act_s, stage_col, stage_row, sem_col,
     sem_row) = refs[n_in + 3 + n_out:]
    n_row_pieces = N_WO_PIECES + N_FFN_CHUNKS

    def col_copy(which, c):
        src = wgu_hbm.at[layer, :, pl.ds(which * FFN_HIDDEN + c * FFN_CHUNK, FFN_CHUNK)]
        return pltpu.make_async_copy(src, stage_col.at[which, c % 2], sem_col.at[which, c % 2])

    def row_copy(p):
        if p < N_WO_PIECES:
            src = wo_hbm.at[wo_idx, pl.ds(p * FFN_CHUNK, FFN_CHUNK), :]
        else:
            src = wd_hbm.at[layer, pl.ds((p - N_WO_PIECES) * FFN_CHUNK, FFN_CHUNK), :]
        return pltpu.make_async_copy(src, stage_row.at[p % 2], sem_row.at[p % 2])

    def take_row_piece(p, dst_ref, row0):
        if p + 1 < n_row_pieces:
            row_copy(p + 1).start()
        row_copy(p).wait()
        dst_ref[row0:row0 + FFN_CHUNK, :] = stage_row[p % 2].astype(BF16)

    def tile(load_weights):
        if load_weights:
            row_copy(0).start()
            for which in range(2):
                col_copy(which, 0).start()
            for p in range(N_WO_PIECES):
                take_row_piece(p, wo_s, p * FFN_CHUNK)
        if not final:
            out_refs[1][...] = _modulation_block(*refs[4:7])
        gate1 = mod_ref[:, 2 * D_MODEL:3 * D_MODEL]
        x1 = x_ref[...] + gate1 * _dot(mix_ref[...].astype(BF16), wo_s[...])
        h = _norm_mod(x1, g_ref[...], mod_ref, 3).astype(BF16)
        for c in range(N_FFN_CHUNKS):
            if load_weights:
                for which, dst in ((0, wg_s), (1, wu_s)):
                    if c + 1 < N_FFN_CHUNKS:
                        col_copy(which, c + 1).start()
                    col_copy(which, c).wait()
                    dst[c] = stage_col[which, c % 2].astype(BF16)
            gate = _dot(h, wg_s[c])
            up = _dot(h, wu_s[c])
            act = gate / (1.0 + jnp.exp(-gate)) * up
            act_s[:, c * FFN_CHUNK:(c + 1) * FFN_CHUNK] = act.astype(BF16)
            if load_weights:
                take_row_piece(N_WO_PIECES + c, wd_s, c * FFN_CHUNK)
        gate2 = mod_ref[:, 5 * D_MODEL:6 * D_MODEL]
        return x1 + gate2 * _dot(act_s[...], wd_s[...])

    def emit(x2):
        if not final:
            out_refs[0][...] = x2
            return
        var = jnp.mean(x2 * x2, axis=-1, keepdims=True)
        y = x2 * lax.rsqrt(var + EPS) * refs[4][...]
        is_ctx = pl.program_id(0) < N_CTX_TOK // tm

        @pl.when(is_ctx)
        def _():
            out_refs[0][...] = y

        @pl.when(jnp.logical_not(is_ctx))
        def _():
            out_refs[1][...] = y

    first = pl.program_id(0) == 0

    @pl.when(first)
    def _():
        emit(tile(True))

    @pl.when(jnp.logical_not(first))
    def _():
        emit(tile(False))


def _post_mixer(x, mix, g_all, mod_l, layer, w_out_all, wo_idx, w_gate_up, w_down,
                norm_final=None, next_mod_inputs=None):
    tm = 512
    n_tiles = N_TOK // tm
    n_ctx_tiles = N_CTX_TOK // tm
    final = norm_final is not None
    row_spec = pl.BlockSpec((tm, D_MODEL), lambda i: (i, 0))
    hbm = pl.BlockSpec(memory_space=pl.ANY)
    if final:
        extra_in = [norm_final.reshape(1, D_MODEL)]
        extra_specs = [pl.BlockSpec((1, D_MODEL), lambda i: (0, 0))]
        out_shape = (jax.ShapeDtypeStruct((N_CTX_TOK, D_MODEL), F32),
                     jax.ShapeDtypeStruct((N_LAT_TOK, D_MODEL), F32))
        out_specs = (
            pl.BlockSpec((tm, D_MODEL), lambda i: (jnp.minimum(i, n_ctx_tiles - 1), 0)),
            pl.BlockSpec((tm, D_MODEL), lambda i: (jnp.maximum(i - n_ctx_tiles, 0), 0)))
    else:
        tn = 6 * D_MODEL // n_tiles
        extra_in = list(next_mod_inputs)
        extra_specs = [
            pl.BlockSpec((N_GROUPS_PAD, D_MODEL), lambda i: (0, 0)),
            pl.BlockSpec((None, D_MODEL, tn), lambda i: (layer + 1, 0, i)),
            pl.BlockSpec((None, 1, tn), lambda i: (layer + 1, 0, i)),
        ]
        out_shape = (jax.ShapeDtypeStruct((N_TOK, D_MODEL), F32),
                     jax.ShapeDtypeStruct((N_GROUPS_PAD, 6 * D_MODEL), F32))
        out_specs = (row_spec, pl.BlockSpec((N_GROUPS_PAD, tn), lambda i: (0, i)))
    return pl.pallas_call(
        functools.partial(_post_mixer_kernel, layer=layer, wo_idx=wo_idx, tm=tm, final=final),
        out_shape=out_shape,
        grid=(N_TOK // tm,),
        in_specs=[
            row_spec,
            row_spec,
            pl.BlockSpec((None, 1, D_MODEL), lambda i: (layer, 0, 0)),
            pl.BlockSpec((None, 1, 6 * D_MODEL), lambda i: (_group_of_tile(i, tm), 0, 0)),
        ] + extra_specs + [hbm, hbm, hbm],
        out_specs=out_specs,
        scratch_shapes=[
            pltpu.VMEM((D_MODEL, D_MODEL), BF16),
            pltpu.VMEM((N_FFN_CHUNKS, D_MODEL, FFN_CHUNK), BF16),
            pltpu.VMEM((N_FFN_CHUNKS, D_MODEL, FFN_CHUNK), BF16),
            pltpu.VMEM((FFN_HIDDEN, D_MODEL), BF16),
            pltpu.VMEM((tm, FFN_HIDDEN), BF16),
            pltpu.VMEM((2, 2, D_MODEL, FFN_CHUNK), F32),
            pltpu.VMEM((2, FFN_CHUNK, D_MODEL), F32),
            pltpu.SemaphoreType.DMA((2, 2)),
            pltpu.SemaphoreType.DMA((2,)),
        ],
        compiler_params=_params(1),
        name="post_mixer",
    )(x, mix, g_all.reshape(DEPTH, 1, D_MODEL), mod_l, *extra_in, w_out_all, w_gate_up, w_down)


def kernel(x_prompt, x_sample, cache_a_k, cache_a_v, cache_c_k, cache_c_v, c, c_ctx, w_mod, b_mod, norm_mix, norm_ffn, w_in_ab, rpb_a, w_pool, pool_scale, w_out_ab, w_in_c, sink_c, w_out_c, w_gate_up, w_down, norm_final):
    xs = (x_prompt.reshape(N_CTX_TOK, D_MODEL), x_sample.reshape(N_LAT_TOK, D_MODEL))
    cond8 = jnp.concatenate(
        [c_ctx[None], c, jnp.zeros((N_GROUPS_PAD - 1 - DEC_BATCH, D_MODEL), F32)], axis=0)
    b_mod3 = b_mod.reshape(DEPTH, 1, 6 * D_MODEL)
    mod, bias_tiles = _modulation0_and_bias(cond8, w_mod, b_mod3, rpb_a)

    n_ab = cache_a_k.shape[1]
    n_c = cache_c_k.shape[1]
    cache_a_k, cache_a_v, cache_c_k, cache_c_v = (
        jnp.transpose(t, (0, 1, 3, 4, 2)) for t in (cache_a_k, cache_a_v, cache_c_k, cache_c_v))
    rope_tables = _rope_tables()

    new_a = []
    new_c = []
    for l in range(DEPTH):
        mod_l = mod.reshape(N_GROUPS_PAD, 1, 6 * D_MODEL)
        if l % 2 == 0:
            i = l // 2
            qkv, *rest = _inproj(xs, norm_mix, mod_l, l, w_in_ab, i, A_HEADS, A_WIDTH, n_ab,
                                 new_a)
            new_a, u = rest[:2], rest[2]
            if len(xs) == 2:
                xs = (rest[3],)
            mix = _ctx_attn_a(qkv)
            mix = _lat_attn_a(qkv, cache_a_k, cache_a_v, bias_tiles, i, mix)
            mix = _pool(u, w_pool, pool_scale, i, mix)
            w_out, wo_idx = w_out_ab, i
        else:
            j = l // 2
            qkv, *new_c = _inproj(xs, norm_mix, mod_l, l, w_in_c, j, C_KV_HEADS, C_Q_WIDTH, n_c,
                                  new_c, rope_tables)
            mix = _ctx_attn_c(qkv, sink_c, j)
            mix = _lat_attn_c(qkv, cache_c_k, cache_c_v, sink_c, j, mix)
            w_out, wo_idx = w_out_c, j
        if l + 1 < DEPTH:
            x, mod = _post_mixer(xs[0], mix, norm_ffn, mod_l, l, w_out, wo_idx, w_gate_up,
                                 w_down, next_mod_inputs=(cond8, w_mod, b_mod3))
            xs = (x,)
        else:
            y_ctx, y_lat = _post_mixer(xs[0], mix, norm_ffn, mod_l, l, w_out, wo_idx, w_gate_up,
                                       w_down, norm_final)

    new_caches = [jnp.transpose(t, (0, 1, 4, 2, 3)) for t in (*new_a, *new_c)]
    return (y_ctx.reshape(BATCH, SEQ, D_MODEL), y_lat.reshape(DEC_BATCH, DEC_SEQ, D_MODEL),
            *new_caches)
```

```python
import functools

import jax
import jax.numpy as jnp
import numpy as np
from jax import lax
from jax.experimental import pallas as pl
from jax.experimental.pallas import tpu as pltpu

D_MODEL = 1024
BATCH = 16
SEQ = 256
DEPTH = 4
DEC_BATCH = 2
DEC_SEQ = 1024
PAST_LEN = 512
GRID_W = 64
HEAD_DIM = 64
A_WIDTH = 512
A_HEADS = 8
B_WIDTH = 512
POOL_WINDOWS = (2, 4, 8, 16)
B_GROUP_DIM = 128
NA_ROWS = 8
NA_COLS = 16
C_HEADS = 16
C_KV_HEADS = 4
C_GROUP = C_HEADS // C_KV_HEADS
C_Q_WIDTH = 1024
C_KV_WIDTH = 256
C_BLOCK = 128
FFN_HIDDEN = 2816
ROPE_BASE = 10000.0
EPS = 1e-6
NEG_INF = -1e30

N_CTX_TOK = BATCH * SEQ
N_LAT_TOK = DEC_BATCH * DEC_SEQ
N_TOK = N_CTX_TOK + N_LAT_TOK
GRID_ROWS = DEC_SEQ // GRID_W
N_GROUPS_PAD = 8

VMEM_LIMIT = 56 * 1024 * 1024

F32 = jnp.float32
BF16 = jnp.bfloat16


def _params(n_axes):
    return pltpu.CompilerParams(dimension_semantics=("arbitrary",) * n_axes,
                                vmem_limit_bytes=VMEM_LIMIT)


def _group_of_tile(i, tm):
    row0 = i * tm
    return jnp.where(row0 < N_CTX_TOK, 0, 1 + (row0 - N_CTX_TOK) // DEC_SEQ)


def _dot_nt(a, b):
    return lax.dot_general(a, b, (((1,), (1,)), ((), ())), preferred_element_type=F32)


def _dot(a, b):
    return jnp.dot(a, b, preferred_element_type=F32)


def _dot_tn(a, b):
    return lax.dot_general(a, b, (((0,), (0,)), ((), ())), preferred_element_type=F32)


def _softmax_pv_t(score_fns, sink_rows, values, s_refs, p_refs):
    n_groups, n_blocks = len(score_fns), len(s_refs)
    for g in range(n_groups):
        for b in range(n_blocks):
            s_refs[b][g] = score_fns[g][b]()
    inv_l = []
    for g in range(n_groups):
        m = sink_rows[g]
        for b in range(n_blocks):
            m = jnp.maximum(m, jnp.max(s_refs[b][g], axis=0, keepdims=True))
        l = jnp.exp(sink_rows[g] - m)
        for b in range(n_blocks):
            p = jnp.exp(s_refs[b][g] - m)
            l = l + jnp.sum(p, axis=0, keepdims=True)
            p_refs[b][g] = p.astype(BF16)
        inv_l.append(1.0 / l)
    outs = []
    for g in range(n_groups):
        o = None
        for b in range(n_blocks):
            pv = _dot_tn(values[g][b], p_refs[b][g])
            o = pv if o is None else o + pv
        outs.append(o * inv_l[g])
    return outs


def _modulation_block(cond_ref, w_ref, b_ref):
    c = cond_ref[...]
    s = c / (1.0 + jnp.exp(-c))
    return _dot(s.astype(BF16), w_ref[...].astype(BF16)) + b_ref[...]


def _mod_bias_kernel(rpb_ref, cond_ref, w_ref, b_ref, o_ref, bias_ref, *, heads_per_step):
    o_ref[...] = _modulation_block(cond_ref, w_ref, b_ref)
    pair0 = pl.program_id(0) * heads_per_step
    for t in range(heads_per_step):
        _write_bias_tiles(rpb_ref, (pair0 + t) * (N_DROW * N_DCOL), bias_ref.at[t])


def _modulation0_and_bias(cond8, w_mod, b_mod3, rpb_a):
    n_col_blocks = 4
    tn = 6 * D_MODEL // n_col_blocks
    n_bias_layers = rpb_a.shape[0]
    heads_per_step = n_bias_layers * A_HEADS // n_col_blocks
    steps_per_layer = A_HEADS // heads_per_step
    return pl.pallas_call(
        functools.partial(_mod_bias_kernel, heads_per_step=heads_per_step),
        out_shape=(
            jax.ShapeDtypeStruct((N_GROUPS_PAD, 6 * D_MODEL), F32),
            jax.ShapeDtypeStruct(
                (n_bias_layers, A_HEADS, N_BIAS_TILES, GRID_W, 2 * GRID_W), F32)),
        grid=(n_col_blocks,),
        in_specs=[
            pl.BlockSpec(memory_space=pltpu.SMEM),
            pl.BlockSpec((N_GROUPS_PAD, D_MODEL), lambda j: (0, 0)),
            pl.BlockSpec((None, D_MODEL, tn), lambda j: (0, 0, j)),
            pl.BlockSpec((None, 1, tn), lambda j: (0, 0, j)),
        ],
        out_specs=(
            pl.BlockSpec((N_GROUPS_PAD, tn), lambda j: (0, j)),
            pl.BlockSpec((None, heads_per_step, N_BIAS_TILES, GRID_W, 2 * GRID_W),
                         lambda j: (j // steps_per_layer, j % steps_per_layer, 0, 0, 0))),
        compiler_params=_params(1),
        name="modulation_bias",
    )(rpb_a.reshape(-1), cond8, w_mod, b_mod3)


def _norm_mod(x, g, mod_ref, shift_idx):
    var = jnp.mean(x * x, axis=-1, keepdims=True)
    y = x * lax.rsqrt(var + EPS) * g
    shift = mod_ref[:, shift_idx * D_MODEL:(shift_idx + 1) * D_MODEL]
    scale = mod_ref[:, (shift_idx + 1) * D_MODEL:(shift_idx + 2) * D_MODEL]
    return y * (1.0 + scale) + shift


QKV_WIDTH = 3 * A_WIDTH
Q_SCALE = HEAD_DIM ** -0.5
PROJ_CHUNK = 4 * HEAD_DIM
ROPE_TABLE_WIDTH = PROJ_CHUNK


def _rope(x, cos, sin_signed):
    n = x.shape[-1]
    lane = lax.broadcasted_iota(jnp.int32, x.shape, x.ndim - 1)
    first = (lane % 32) < 16
    partner = jnp.where(first, pltpu.roll(x, n - 16, axis=x.ndim - 1),
                        pltpu.roll(x, 16, axis=x.ndim - 1))
    return x * cos + partner * sin_signed


def _inproj_kernel(*refs, tm, n_heads, q_width, n_prev, split_x, rope, has_u):
    n_x = 2 if split_x else 1
    g_ref, mod_ref, w_ref = refs[n_x:n_x + 3]
    n_in = n_x + 3 + (2 if rope else 0) + n_prev
    qkv_ref, ck_ref, cv_ref = refs[n_in:n_in + 3]
    extra_out = refs[n_in + 3:-3]
    wbf_ref, h_ref, res_ref = refs[-3:]
    i = pl.program_id(0)
    is_ctx = i < N_CTX_TOK // tm
    kv_width = (QKV_WIDTH - q_width) // 2
    k_col, v_col = q_width, q_width + kv_width
    n_out = res_ref.shape[1]
    chunk = PROJ_CHUNK

    @pl.when(i == 0)
    def _():
        wbf_ref[...] = w_ref[...].astype(BF16)

    def tile(ctx):
        if split_x:
            x = (refs[0] if ctx else refs[1])[...]
            extra_out[-1][...] = x
        else:
            x = refs[0][...]
        h_ref[...] = _norm_mod(x, g_ref[...], mod_ref, 0).astype(BF16)
        def matmul(c0):
            cols = slice(c0, c0 + chunk)
            res_ref[:, cols] = _dot(h_ref[...], wbf_ref[:, cols])

        def epilogue(c0):
            cols = slice(c0, c0 + chunk)
            if c0 >= QKV_WIDTH:
                extra_out[0][:, c0 - QKV_WIDTH:c0 - QKV_WIDTH + chunk] = res_ref[:, cols]
                return
            r = res_ref[:, cols]
            if rope and not ctx and c0 < v_col:
                cos_ref, sin_ref = refs[n_x + 3:n_x + 5]
                r = _rope(r, cos_ref[...], sin_ref[...])
            if c0 < q_width:
                r = r * Q_SCALE
            qkv_ref[:, cols] = r.astype(BF16)
            if ctx and c0 >= k_col:
                c_ref, col0 = (ck_ref, k_col) if c0 < v_col else (cv_ref, v_col)
                for req in range(tm // SEQ):
                    for hc in range(chunk // HEAD_DIM):
                        hd = (c0 - col0) // HEAD_DIM + hc
                        c_ref[req, hd] = res_ref[
                            req * SEQ:(req + 1) * SEQ,
                            c0 + hc * HEAD_DIM:c0 + (hc + 1) * HEAD_DIM].T

        for c0 in range(0, n_out, chunk):
            matmul(c0)
            epilogue(c0)

    pl.when(is_ctx)(lambda: tile(True))
    pl.when(jnp.logical_not(is_ctx))(lambda: tile(False))


def _inproj(xs, g_all, mod_l, layer, w_all, w_idx, n_heads, q_width, n_slots, prev_caches,
            rope_tables=None):
    tm = 512
    n_out = w_all.shape[2]
    n_ctx_tiles = N_CTX_TOK // tm
    split_x = len(xs) == 2
    rope = rope_tables is not None
    has_u = n_out > QKV_WIDTH
    cache_shape = jax.ShapeDtypeStruct((BATCH, n_slots, n_heads, HEAD_DIM, SEQ), F32)
    cache_spec = pl.BlockSpec(
        (tm // SEQ, None, n_heads, HEAD_DIM, SEQ),
        lambda i: (jnp.minimum(i, n_ctx_tiles - 1), w_idx, 0, 0, 0))
    row_spec = pl.BlockSpec((tm, D_MODEL), lambda i: (i, 0))
    if split_x:
        x_specs = [
            pl.BlockSpec((tm, D_MODEL), lambda i: (jnp.minimum(i, n_ctx_tiles - 1), 0)),
            pl.BlockSpec((tm, D_MODEL), lambda i: (jnp.maximum(i - n_ctx_tiles, 0), 0)),
        ]
    else:
        x_specs = [row_spec]
    out_shape = [jax.ShapeDtypeStruct((N_TOK, QKV_WIDTH), BF16), cache_shape, cache_shape]
    out_specs = [pl.BlockSpec((tm, QKV_WIDTH), lambda i: (i, 0)), cache_spec, cache_spec]
    if has_u:
        out_shape.append(jax.ShapeDtypeStruct((N_TOK, n_out - QKV_WIDTH), F32))
        out_specs.append(pl.BlockSpec((tm, n_out - QKV_WIDTH), lambda i: (i, 0)))
    if split_x:
        out_shape.append(jax.ShapeDtypeStruct((N_TOK, D_MODEL), F32))
        out_specs.append(row_spec)
    rope_in, rope_specs = [], []
    if rope:
        tiles_per_seq = DEC_SEQ // tm
        rope_spec = pl.BlockSpec(
            (tm, ROPE_TABLE_WIDTH),
            lambda i: (jnp.maximum(i - n_ctx_tiles, 0) % tiles_per_seq, 0))
        rope_in, rope_specs = list(rope_tables), [rope_spec, rope_spec]
    n_prev = len(prev_caches)
    n_before = len(xs) + 3 + len(rope_in)
    return pl.pallas_call(
        functools.partial(_inproj_kernel, tm=tm, n_heads=n_heads, q_width=q_width,
                          n_prev=n_prev, split_x=split_x, rope=rope, has_u=has_u),
        out_shape=out_shape,
        grid=(N_TOK // tm,),
        in_specs=x_specs + [
            pl.BlockSpec((None, 1, D_MODEL), lambda i: (layer, 0, 0)),
            pl.BlockSpec((None, 1, 6 * D_MODEL), lambda i: (_group_of_tile(i, tm), 0, 0)),
            pl.BlockSpec((None, D_MODEL, n_out), lambda i: (w_idx, 0, 0),
                         pipeline_mode=pl.Buffered(1)),
        ] + rope_specs + [pl.BlockSpec(memory_space=pl.ANY)] * n_prev,
        out_specs=out_specs,
        scratch_shapes=[pltpu.VMEM((D_MODEL, n_out), BF16), pltpu.VMEM((tm, D_MODEL), BF16),
                        pltpu.VMEM((tm, n_out), F32)],
        input_output_aliases={n_before + k: 1 + k for k in range(n_prev)},
        compiler_params=_params(1),
        name="inproj",
    )(*xs, g_all.reshape(DEPTH, 1, D_MODEL), mod_l, w_all, *rope_in, *prev_caches)


def _ctx_attn_a_kernel(p_ref, o_ref, s_ref, p_scr):
    for h in range(A_HEADS):
        c0 = h * HEAD_DIM
        s_ref[h] = _dot_nt(p_ref[:, c0:c0 + HEAD_DIM],
                           p_ref[:, A_WIDTH + c0:A_WIDTH + c0 + HEAD_DIM])
    inv_l = []
    for h in range(A_HEADS):
        s = s_ref[h]
        p = jnp.exp(s - jnp.max(s, axis=-1, keepdims=True))
        inv_l.append(1.0 / jnp.sum(p, axis=-1, keepdims=True))
        p_scr[h] = p.astype(BF16)
    for h in range(A_HEADS):
        c0 = h * HEAD_DIM
        v = p_ref[:, 2 * A_WIDTH + c0:2 * A_WIDTH + c0 + HEAD_DIM]
        o_ref[:, c0:c0 + HEAD_DIM] = _dot(p_scr[h], v) * inv_l[h]


def _ctx_attn_a(qkv):
    return pl.pallas_call(
        _ctx_attn_a_kernel,
        out_shape=jax.ShapeDtypeStruct((N_TOK, D_MODEL), F32),
        grid=(BATCH,),
        in_specs=[pl.BlockSpec((SEQ, QKV_WIDTH), lambda b: (b, 0))],
        out_specs=pl.BlockSpec((SEQ, A_WIDTH), lambda b: (b, 0)),
        scratch_shapes=[pltpu.VMEM((A_HEADS, SEQ, SEQ), F32),
                        pltpu.VMEM((A_HEADS, SEQ, SEQ), BF16)],
        compiler_params=_params(1),
        name="ctx_attn_a",
    )(qkv)


N_DROW = 2 * NA_ROWS - 1
N_DCOL = 2 * NA_COLS - 1
N_BIAS_TILES = 16
BIAS_TILE_LEFT_PAD = 14
BIAS_TILE_RIGHT_PAD = 15
MID_DROW = NA_ROWS - 1 - NA_ROWS // 2


def _write_bias_tiles(rpb_ref, base, o_ref):
    qi = lax.broadcasted_iota(jnp.int32, (GRID_W, 2 * GRID_W), 0)
    lane = lax.broadcasted_iota(jnp.int32, (GRID_W, 2 * GRID_W), 1)
    right = lane >= GRID_W
    kc = jnp.where(right, lane - GRID_W, lane)
    qstart = jnp.clip(qi - NA_COLS // 2, 0, GRID_W - NA_COLS)
    valid = (kc >= qstart) & (kc < qstart + NA_COLS)

    offs = lax.broadcasted_iota(jnp.int32, (1, 2 * GRID_W), 1) & (GRID_W - 1)
    shift = 2 * GRID_W - (NA_COLS - 1)
    rows = []
    for dr in range(N_DROW):
        vec = jnp.zeros((1, 2 * GRID_W), F32)
        for d in range(N_DCOL):
            vec = jnp.where(offs == d, rpb_ref[base + dr * N_DCOL + d], vec)
        rows.append(pltpu.roll(jnp.broadcast_to(vec, (GRID_W, 2 * GRID_W)), shift, axis=1,
                               stride=1, stride_axis=0))

    for t in range(N_DROW - 1):
        o_ref[t] = jnp.where(valid, jnp.where(right, rows[t + 1], rows[t]), NEG_INF)
    o_ref[BIAS_TILE_LEFT_PAD] = jnp.where(valid & right, rows[MID_DROW], NEG_INF)
    o_ref[BIAS_TILE_RIGHT_PAD] = jnp.where(valid & jnp.logical_not(right),
                                           rows[MID_DROW + NA_ROWS - 1], NEG_INF)


NA_GROUP_ROWS = 4


def _na_window(r):
    start = min(max(r - NA_ROWS // 2, 0), GRID_ROWS - NA_ROWS)
    first_drow = start - r + NA_ROWS - 1
    if start % 2 == 0:
        return start, [first_drow + 2 * p for p in range(NA_ROWS // 2)]
    assert first_drow == MID_DROW
    inner = [first_drow + 1 + 2 * p for p in range(NA_ROWS // 2 - 1)]
    return start - 1, [BIAS_TILE_LEFT_PAD] + inner + [BIAS_TILE_RIGHT_PAD]


def _lat_attn_a_kernel(q_ref, k_ref, v_ref, kc_ref, vc_ref, bias_ref, mix_ref, o_ref,
                       s_loc, s_ctx, p_loc, p_ctx):
    del mix_ref
    hp = pl.program_id(1)

    @pl.when((pl.program_id(0) == 0) & (hp == 0))
    def _():
        p_loc[...] = jnp.zeros_like(p_loc)

    groups = []
    for g0 in range(0, GRID_ROWS, NA_GROUP_ROWS):
        windows = [_na_window(r) for r in range(g0, g0 + NA_GROUP_ROWS)]
        lo = min(first for first, _ in windows)
        hi = max(first + 2 * len(tiles) for first, tiles in windows)
        groups.append((slice(g0 * GRID_W, (g0 + NA_GROUP_ROWS) * GRID_W),
                       slice(lo * GRID_W, hi * GRID_W)))
    for hh in range(2):
        c0 = hh * HEAD_DIM
        for q_rows, keys in groups:
            s_loc[hh, q_rows, keys] = _dot_nt(q_ref[q_rows, c0:c0 + HEAD_DIM],
                                              k_ref[keys, c0:c0 + HEAD_DIM])
        s_ctx[hh] = _dot(q_ref[:, c0:c0 + HEAD_DIM],
                         kc_ref[hh].astype(BF16))
    inv_l = [[], []]
    for hh in range(2):
        for r in range(GRID_ROWS):
            rows = slice(r * GRID_W, (r + 1) * GRID_W)
            first_row, tiles = _na_window(r)
            cols = slice(first_row * GRID_W, (first_row + 2 * len(tiles)) * GRID_W)
            bias = jnp.concatenate([bias_ref[hh, t] for t in tiles], axis=1)
            sc = s_ctx[hh, rows, :]
            sl = s_loc[hh, rows, cols] + bias
            m = jnp.maximum(jnp.max(sc, axis=-1, keepdims=True),
                            jnp.max(sl, axis=-1, keepdims=True))
            pc = jnp.exp(sc - m)
            pw = jnp.exp(sl - m)
            inv_l[hh].append(1.0 / (jnp.sum(pc, axis=-1, keepdims=True)
                                    + jnp.sum(pw, axis=-1, keepdims=True)))
            p_ctx[hh, rows, :] = pc.astype(BF16)
            p_loc[hh, rows, cols] = pw.astype(BF16)
    for hh in range(2):
        c0 = hh * HEAD_DIM
        o_loc = jnp.concatenate(
            [_dot(p_loc[hh, q_rows, keys], v_ref[keys, c0:c0 + HEAD_DIM])
             for q_rows, keys in groups], axis=0)
        o = _dot_nt(p_ctx[hh], vc_ref[hh].astype(BF16)) + o_loc
        o_ref[:, c0:c0 + HEAD_DIM] = o * jnp.concatenate(inv_l[hh], axis=0)


def _lat_attn_a(qkv, cache_k, cache_v, bias_tiles, i_layer, mix):
    row0 = N_CTX_TOK // DEC_SEQ
    pair = 2 * HEAD_DIM
    k_col0 = A_WIDTH // pair
    v_col0 = 2 * A_WIDTH // pair
    return pl.pallas_call(
        _lat_attn_a_kernel,
        out_shape=jax.ShapeDtypeStruct((N_TOK, D_MODEL), F32),
        grid=(DEC_BATCH, A_HEADS // 2),
        in_specs=[
            pl.BlockSpec((DEC_SEQ, pair), lambda b, hp: (row0 + b, hp)),
            pl.BlockSpec((DEC_SEQ, pair), lambda b, hp: (row0 + b, k_col0 + hp)),
            pl.BlockSpec((DEC_SEQ, pair), lambda b, hp: (row0 + b, v_col0 + hp)),
            pl.BlockSpec((None, None, 2, HEAD_DIM, PAST_LEN),
                         lambda b, hp: (b, i_layer, hp, 0, 0)),
            pl.BlockSpec((None, None, 2, HEAD_DIM, PAST_LEN),
                         lambda b, hp: (b, i_layer, hp, 0, 0)),
            pl.BlockSpec((None, 2, N_BIAS_TILES, GRID_W, 2 * GRID_W),
                         lambda b, hp: (i_layer, hp, 0, 0, 0)),
            pl.BlockSpec(memory_space=pl.ANY),
        ],
        out_specs=pl.BlockSpec((DEC_SEQ, pair), lambda b, hp: (row0 + b, hp)),
        scratch_shapes=[pltpu.VMEM((2, DEC_SEQ, DEC_SEQ), F32),
                        pltpu.VMEM((2, DEC_SEQ, PAST_LEN), F32),
                        pltpu.VMEM((2, DEC_SEQ, DEC_SEQ), BF16),
                        pltpu.VMEM((2, DEC_SEQ, PAST_LEN), BF16)],
        input_output_aliases={6: 0},
        compiler_params=_params(2),
        name="lat_attn_a",
    )(qkv, qkv, qkv, cache_k, cache_v, bias_tiles, mix)


def _pool_kernel(u_ref, wp_ref, ps_ref, mix_ref, o_ref, *, rows):
    del mix_ref
    n = jnp.where(pl.program_id(0) < N_CTX_TOK // rows, SEQ, DEC_SEQ)
    t = lax.broadcasted_iota(jnp.int32, (rows, B_GROUP_DIM), 0) & (n - 1)

    def earlier(x, k):
        return jnp.where(t >= k, pltpu.roll(x, k, axis=0), 0.0)

    def later(x, k):
        return jnp.where(t < n - k, pltpu.roll(x, rows - k, axis=0), 0.0)

    tokens_before = t.astype(F32)
    tokens_from = (n - t).astype(F32)

    for g, w in enumerate(POOL_WINDOWS):
        c0 = g * B_GROUP_DIM
        half = w // 2
        u = u_ref[:, c0:c0 + B_GROUP_DIM]
        before, after = u, u
        k = 1
        while k < half:
            before = before + earlier(before, k)
            after = after + later(after, k)
            k *= 2
        total = earlier(before, 1) + after
        count = jnp.minimum(tokens_before, half) + jnp.minimum(tokens_from, half)
        pooled = total / count - u
        y = _dot(pooled.astype(BF16), wp_ref[g].astype(BF16))
        o_ref[:, c0:c0 + B_GROUP_DIM] = y * ps_ref[:, c0:c0 + B_GROUP_DIM]


def _pool(u, w_pool, pool_scale, i_layer, mix):
    n_groups = len(POOL_WINDOWS)
    rows = DEC_SEQ
    return pl.pallas_call(
        functools.partial(_pool_kernel, rows=rows),
        out_shape=jax.ShapeDtypeStruct((N_TOK, D_MODEL), F32),
        grid=(N_TOK // rows,),
        in_specs=[
            pl.BlockSpec((rows, B_WIDTH), lambda b: (b, 0)),
            pl.BlockSpec((None, n_groups, B_GROUP_DIM, B_GROUP_DIM),
                         lambda b: (i_layer, 0, 0, 0)),
            pl.BlockSpec((None, 1, B_WIDTH), lambda b: (i_layer, 0, 0)),
            pl.BlockSpec(memory_space=pl.ANY),
        ],
        out_specs=pl.BlockSpec((rows, B_WIDTH), lambda b: (b, 1)),
        input_output_aliases={3: 0},
        compiler_params=_params(1),
        name="pool_mixer",
    )(u, w_pool, pool_scale.reshape(-1, 1, B_WIDTH), mix)


def _sink_row(sink_ref, head0, queries_per_head):
    lane = lax.broadcasted_iota(jnp.int32, (1, C_GROUP * queries_per_head), 1)
    row = jnp.full((1, C_GROUP * queries_per_head), sink_ref[head0], F32)
    for g in range(1, C_GROUP):
        row = jnp.where(lane >= g * queries_per_head, sink_ref[head0 + g], row)
    return row


CTX_REQS_PER_STEP = 2


def _ctx_attn_c_kernel(sink_ref, p_ref, o_ref, s_ref, p_scr, *, sink0):
    for req in range(CTX_REQS_PER_STEP):
        rows = slice(req * SEQ, (req + 1) * SEQ)

        def scores(kk):
            q = jnp.concatenate(
                [p_ref[rows, h * HEAD_DIM:(h + 1) * HEAD_DIM]
                 for h in range(kk * C_GROUP, (kk + 1) * C_GROUP)], axis=0)
            k0 = C_Q_WIDTH + kk * HEAD_DIM
            return _dot_nt(p_ref[rows, k0:k0 + HEAD_DIM], q)

        v0 = C_Q_WIDTH + C_KV_WIDTH
        outs = _softmax_pv_t(
            [[functools.partial(scores, kk)] for kk in range(C_KV_HEADS)],
            [_sink_row(sink_ref, sink0 + kk * C_GROUP, SEQ) for kk in range(C_KV_HEADS)],
            [[p_ref[rows, v0 + kk * HEAD_DIM:v0 + (kk + 1) * HEAD_DIM]]
             for kk in range(C_KV_HEADS)],
            [s_ref], [p_scr])
        outs_t = [o_t[:, g * SEQ:(g + 1) * SEQ] for o_t in outs for g in range(C_GROUP)]
        o_ref[rows, :] = jnp.concatenate(outs_t, axis=0).T


def _ctx_attn_c(qkv, sink_all, j_layer):
    return pl.pallas_call(
        functools.partial(_ctx_attn_c_kernel, sink0=j_layer * C_HEADS),
        out_shape=jax.ShapeDtypeStruct((N_TOK, D_MODEL), F32),
        grid=(BATCH // CTX_REQS_PER_STEP,),
        in_specs=[
            pl.BlockSpec(memory_space=pltpu.SMEM),
            pl.BlockSpec((CTX_REQS_PER_STEP * SEQ, QKV_WIDTH), lambda b: (b, 0)),
        ],
        out_specs=pl.BlockSpec((CTX_REQS_PER_STEP * SEQ, C_Q_WIDTH), lambda b: (b, 0)),
        scratch_shapes=[pltpu.VMEM((C_KV_HEADS, SEQ, C_GROUP * SEQ), F32),
                        pltpu.VMEM((C_KV_HEADS, SEQ, C_GROUP * SEQ), BF16)],
        compiler_params=_params(1),
        name="ctx_attn_c",
    )(sink_all.reshape(-1), qkv)


def _lat_attn_c_kernel(sink_ref, q_ref, k_ref, v_ref, kc_ref, vc_ref, mix_ref, o_ref,
                       kctx_ref, vctx_ref, sc_ref, sb_ref, pc_ref, pb_ref, *, sink0):
    del mix_ref
    j = pl.program_id(1)
    n_blocks = DEC_SEQ // C_BLOCK

    @pl.when(j == 0)
    def _():
        for kk in range(C_KV_HEADS):
            c0 = kk * HEAD_DIM
            kctx_ref[:, c0:c0 + HEAD_DIM] = kc_ref[kk].T.astype(BF16)
            vctx_ref[:, c0:c0 + HEAD_DIM] = vc_ref[kk].T.astype(BF16)

    n_q = C_GROUP * C_BLOCK
    qi = lax.broadcasted_iota(jnp.int32, (C_BLOCK, n_q), 1) % C_BLOCK
    jl = lax.broadcasted_iota(jnp.int32, (C_BLOCK, n_q), 0)
    valid_prev = (jl >= qi) & (j > 0)
    valid_next = (jl <= qi) & (j < n_blocks - 1)
    rows_prev = pl.ds(pl.multiple_of(jnp.maximum(j - 1, 0) * C_BLOCK, C_BLOCK), C_BLOCK)
    rows_cur = pl.ds(pl.multiple_of(j * C_BLOCK, C_BLOCK), C_BLOCK)
    rows_next = pl.ds(pl.multiple_of(jnp.minimum(j + 1, n_blocks - 1) * C_BLOCK, C_BLOCK),
                      C_BLOCK)

    def stacked_q(kk):
        return jnp.concatenate(
            [q_ref[:, h * HEAD_DIM:(h + 1) * HEAD_DIM]
             for h in range(kk * C_GROUP, (kk + 1) * C_GROUP)], axis=0)

    def band(ref, kk):
        cols = slice(kk * HEAD_DIM, (kk + 1) * HEAD_DIM)
        return jnp.concatenate(
            [ref[rows_prev, cols], ref[rows_cur, cols], ref[rows_next, cols]], axis=0)

    def ctx_scores(kk):
        return _dot_nt(kctx_ref[:, kk * HEAD_DIM:(kk + 1) * HEAD_DIM], stacked_q(kk))

    def band_scores(kk):
        s_raw = _dot_nt(band(k_ref, kk), stacked_q(kk))
        return jnp.concatenate(
            [jnp.where(valid_prev, s_raw[:C_BLOCK], NEG_INF),
             s_raw[C_BLOCK:2 * C_BLOCK],
             jnp.where(valid_next, s_raw[2 * C_BLOCK:], NEG_INF)], axis=0)

    kv_heads = range(C_KV_HEADS)
    outs = _softmax_pv_t(
        [[functools.partial(ctx_scores, kk), functools.partial(band_scores, kk)]
         for kk in kv_heads],
        [_sink_row(sink_ref, sink0 + kk * C_GROUP, C_BLOCK) for kk in kv_heads],
        [[vctx_ref[:, kk * HEAD_DIM:(kk + 1) * HEAD_DIM], band(v_ref, kk)] for kk in kv_heads],
        [sc_ref, sb_ref], [pc_ref, pb_ref])
    outs_t = [o_t[:, g * C_BLOCK:(g + 1) * C_BLOCK] for o_t in outs for g in range(C_GROUP)]
    o_ref[...] = jnp.concatenate(outs_t, axis=0).T


def _lat_attn_c(qkv, cache_k, cache_v, sink, j_layer, mix):
    n_blocks = DEC_SEQ // C_BLOCK
    q_row0 = N_CTX_TOK // C_BLOCK
    kv_row0 = N_CTX_TOK // DEC_SEQ
    k_col = C_Q_WIDTH // C_KV_WIDTH
    n_q = C_GROUP * C_BLOCK
    return pl.pallas_call(
        functools.partial(_lat_attn_c_kernel, sink0=j_layer * C_HEADS),
        out_shape=jax.ShapeDtypeStruct((N_TOK, D_MODEL), F32),
        grid=(DEC_BATCH, n_blocks),
        in_specs=[
            pl.BlockSpec(memory_space=pltpu.SMEM),
            pl.BlockSpec((C_BLOCK, C_Q_WIDTH), lambda b, j: (q_row0 + b * n_blocks + j, 0)),
            pl.BlockSpec((DEC_SEQ, C_KV_WIDTH), lambda b, j: (kv_row0 + b, k_col)),
            pl.BlockSpec((DEC_SEQ, C_KV_WIDTH), lambda b, j: (kv_row0 + b, k_col + 1)),
            pl.BlockSpec((None, None, C_KV_HEADS, HEAD_DIM, PAST_LEN),
                         lambda b, j: (b, j_layer, 0, 0, 0)),
            pl.BlockSpec((None, None, C_KV_HEADS, HEAD_DIM, PAST_LEN),
                         lambda b, j: (b, j_layer, 0, 0, 0)),
            pl.BlockSpec(memory_space=pl.ANY),
        ],
        out_specs=pl.BlockSpec((C_BLOCK, C_Q_WIDTH),
                               lambda b, j: (q_row0 + b * n_blocks + j, 0)),
        input_output_aliases={6: 0},
        scratch_shapes=[pltpu.VMEM((PAST_LEN, C_KV_WIDTH), BF16),
                        pltpu.VMEM((PAST_LEN, C_KV_WIDTH), BF16),
                        pltpu.VMEM((C_KV_HEADS, PAST_LEN, n_q), F32),
                        pltpu.VMEM((C_KV_HEADS, 3 * C_BLOCK, n_q), F32),
                        pltpu.VMEM((C_KV_HEADS, PAST_LEN, n_q), BF16),
                        pltpu.VMEM((C_KV_HEADS, 3 * C_BLOCK, n_q), BF16)],
        compiler_params=_params(2),
        name="lat_attn_c",
    )(sink.reshape(-1), qkv, qkv, qkv, cache_k, cache_v, mix)


def _rope_tables():
    t = np.arange(DEC_SEQ)
    pos = np.stack([t // GRID_W, t % GRID_W], axis=-1).astype(np.float64)
    half = HEAD_DIM // 4
    inv = ROPE_BASE ** (-np.arange(half, dtype=np.float64) / half)
    ang = pos[:, :, None] * inv
    cos = np.cos(ang)
    sin = np.sin(ang)
    cos64 = np.stack([cos, cos], axis=2).reshape(DEC_SEQ, HEAD_DIM)
    sin64 = np.stack([-sin, sin], axis=2).reshape(DEC_SEQ, HEAD_DIM)
    reps = ROPE_TABLE_WIDTH // HEAD_DIM
    return (jnp.asarray(np.tile(cos64, (1, reps)), F32),
            jnp.asarray(np.tile(sin64, (1, reps)), F32))


FFN_CHUNK = 256
N_FFN_CHUNKS = FFN_HIDDEN // FFN_CHUNK
N_WO_PIECES = D_MODEL // FFN_CHUNK


def _post_mixer_kernel(*refs, layer, wo_idx, tm, final):
    x_ref, mix_ref, g_ref, mod_ref = refs[:4]
    n_in = 5 if final else 7
    wo_hbm, wgu_hbm, wd_hbm = refs[n_in:n_in + 3]
    n_out = 2
    out_refs = refs[n_in + 3:n_in + 3 + n_out]
    (wo_s, wg_s, wu_s, wd_s, act_s, stage_col, stage_row, sem_col,
     sem_row) = refs[n_in + 3 + n_out:]
    n_row_pieces = N_WO_PIECES + N_FFN_CHUNKS

    def col_copy(which, c):
        src = wgu_hbm.at[layer, :, pl.ds(which * FFN_HIDDEN + c * FFN_CHUNK, FFN_CHUNK)]
        return pltpu.make_async_copy(src, stage_col.at[which, c % 2], sem_col.at[which, c % 2])

    def row_copy(p):
        if p < N_WO_PIECES:
            src = wo_hbm.at[wo_idx, pl.ds(p * FFN_CHUNK, FFN_CHUNK), :]
        else:
            src = wd_hbm.at[layer, pl.ds((p - N_WO_PIECES) * FFN_CHUNK, FFN_CHUNK), :]
        return pltpu.make_async_copy(src, stage_row.at[p % 2], sem_row.at[p % 2])

    def take_row_piece(p, dst_ref, row0):
        if p + 1 < n_row_pieces:
            row_copy(p + 1).start()
        row_copy(p).wait()
        dst_ref[row0:row0 + FFN_CHUNK, :] = stage_row[p % 2].astype(BF16)

    def tile(load_weights):
        if load_weights:
            row_copy(0).start()
            for which in range(2):
                col_copy(which, 0).start()
            for p in range(N_WO_PIECES):
                take_row_piece(p, wo_s, p * FFN_CHUNK)
        if not final:
            out_refs[1][...] = _modulation_block(*refs[4:7])
        halves = [slice(0, tm // 2), slice(tm // 2, tm)]
        gate1 = mod_ref[:, 2 * D_MODEL:3 * D_MODEL]
        x1 = [x_ref[rows, :] + gate1 * _dot(mix_ref[rows, :].astype(BF16), wo_s[...])
              for rows in halves]
        h = jnp.concatenate(
            [_norm_mod(part, g_ref[...], mod_ref, 3).astype(BF16) for part in x1], axis=0)
        for c in range(N_FFN_CHUNKS):
            if load_weights:
                for which, dst in ((0, wg_s), (1, wu_s)):
                    if c + 1 < N_FFN_CHUNKS:
                        col_copy(which, c + 1).start()
                    col_copy(which, c).wait()
                    dst[c] = stage_col[which, c % 2].astype(BF16)
            gate = _dot(h, wg_s[c])
            up = _dot(h, wu_s[c])
            act = gate / (1.0 + jnp.exp(-gate)) * up
            act_s[:, c * FFN_CHUNK:(c + 1) * FFN_CHUNK] = act.astype(BF16)
            if load_weights:
                take_row_piece(N_WO_PIECES + c, wd_s, c * FFN_CHUNK)
        gate2 = mod_ref[:, 5 * D_MODEL:6 * D_MODEL]
        return jnp.concatenate(
            [part + gate2 * _dot(act_s[rows, :], wd_s[...]) for part, rows in zip(x1, halves)],
            axis=0)

    def emit(x2):
        if not final:
            out_refs[0][...] = x2
            return
        var = jnp.mean(x2 * x2, axis=-1, keepdims=True)
        y = x2 * lax.rsqrt(var + EPS) * refs[4][...]
        is_ctx = pl.program_id(0) < N_CTX_TOK // tm

        @pl.when(is_ctx)
        def _():
            out_refs[0][...] = y

        @pl.when(jnp.logical_not(is_ctx))
        def _():
            out_refs[1][...] = y

    first = pl.program_id(0) == 0

    @pl.when(first)
    def _():
        emit(tile(True))

    @pl.when(jnp.logical_not(first))
    def _():
        emit(tile(False))


def _post_mixer(x, mix, g_all, mod_l, layer, w_out_all, wo_idx, w_gate_up, w_down,
                norm_final=None, next_mod_inputs=None):
    tm = 512
    n_tiles = N_TOK // tm
    n_ctx_tiles = N_CTX_TOK // tm
    final = norm_final is not None
    row_spec = pl.BlockSpec((tm, D_MODEL), lambda i: (i, 0))
    hbm = pl.BlockSpec(memory_space=pl.ANY)
    if final:
        extra_in = [norm_final.reshape(1, D_MODEL)]
        extra_specs = [pl.BlockSpec((1, D_MODEL), lambda i: (0, 0))]
        out_shape = (jax.ShapeDtypeStruct((N_CTX_TOK, D_MODEL), F32),
                     jax.ShapeDtypeStruct((N_LAT_TOK, D_MODEL), F32))
        out_specs = (
            pl.BlockSpec((tm, D_MODEL), lambda i: (jnp.minimum(i, n_ctx_tiles - 1), 0)),
            pl.BlockSpec((tm, D_MODEL), lambda i: (jnp.maximum(i - n_ctx_tiles, 0), 0)))
    else:
        tn = 6 * D_MODEL // n_tiles
        extra_in = list(next_mod_inputs)
        extra_specs = [
            pl.BlockSpec((N_GROUPS_PAD, D_MODEL), lambda i: (0, 0)),
            pl.BlockSpec((None, D_MODEL, tn), lambda i: (layer + 1, 0, i)),
            pl.BlockSpec((None, 1, tn), lambda i: (layer + 1, 0, i)),
        ]
        out_shape = (jax.ShapeDtypeStruct((N_TOK, D_MODEL), F32),
                     jax.ShapeDtypeStruct((N_GROUPS_PAD, 6 * D_MODEL), F32))
        out_specs = (row_spec, pl.BlockSpec((N_GROUPS_PAD, tn), lambda i: (0, i)))
    return pl.pallas_call(
        functools.partial(_post_mixer_kernel, layer=layer, wo_idx=wo_idx, tm=tm, final=final),
        out_shape=out_shape,
        grid=(N_TOK // tm,),
        in_specs=[
            row_spec,
            row_spec,
            pl.BlockSpec((None, 1, D_MODEL), lambda i: (layer, 0, 0)),
            pl.BlockSpec((None, 1, 6 * D_MODEL), lambda i: (_group_of_tile(i, tm), 0, 0)),
        ] + extra_specs + [hbm, hbm, hbm],
        out_specs=out_specs,
        scratch_shapes=[
            pltpu.VMEM((D_MODEL, D_MODEL), BF16),
            pltpu.VMEM((N_FFN_CHUNKS, D_MODEL, FFN_CHUNK), BF16),
            pltpu.VMEM((N_FFN_CHUNKS, D_MODEL, FFN_CHUNK), BF16),
            pltpu.VMEM((FFN_HIDDEN, D_MODEL), BF16),
            pltpu.VMEM((tm, FFN_HIDDEN), BF16),
            pltpu.VMEM((2, 2, D_MODEL, FFN_CHUNK), F32),
            pltpu.VMEM((2, FFN_CHUNK, D_MODEL), F32),
            pltpu.SemaphoreType.DMA((2, 2)),
            pltpu.SemaphoreType.DMA((2,)),
        ],
        compiler_params=_params(1),
        name="post_mixer",
    )(x, mix, g_all.reshape(DEPTH, 1, D_MODEL), mod_l, *extra_in, w_out_all, w_gate_up, w_down)


def kernel(x_prompt, x_sample, cache_a_k, cache_a_v, cache_c_k, cache_c_v, c, c_ctx, w_mod, b_mod, norm_mix, norm_ffn, w_in_ab, rpb_a, w_pool, pool_scale, w_out_ab, w_in_c, sink_c, w_out_c, w_gate_up, w_down, norm_final):
    xs = (x_prompt.reshape(N_CTX_TOK, D_MODEL), x_sample.reshape(N_LAT_TOK, D_MODEL))
    cond8 = jnp.concatenate(
        [c_ctx[None], c, jnp.zeros((N_GROUPS_PAD - 1 - DEC_BATCH, D_MODEL), F32)], axis=0)
    b_mod3 = b_mod.reshape(DEPTH, 1, 6 * D_MODEL)
    mod, bias_tiles = _modulation0_and_bias(cond8, w_mod, b_mod3, rpb_a)

    n_ab = cache_a_k.shape[1]
    n_c = cache_c_k.shape[1]
    cache_a_k, cache_a_v, cache_c_k, cache_c_v = (
        jnp.transpose(t, (0, 1, 3, 4, 2)) for t in (cache_a_k, cache_a_v, cache_c_k, cache_c_v))
    rope_tables = _rope_tables()

    new_a = []
    new_c = []
    for l in range(DEPTH):
        mod_l = mod.reshape(N_GROUPS_PAD, 1, 6 * D_MODEL)
        if l % 2 == 0:
            i = l // 2
            qkv, *rest = _inproj(xs, norm_mix, mod_l, l, w_in_ab, i, A_HEADS, A_WIDTH, n_ab,
                                 new_a)
            new_a, u = rest[:2], rest[2]
            if len(xs) == 2:
                xs = (rest[3],)
            mix = _ctx_attn_a(qkv)
            mix = _lat_attn_a(qkv, cache_a_k, cache_a_v, bias_tiles, i, mix)
            mix = _pool(u, w_pool, pool_scale, i, mix)
            w_out, wo_idx = w_out_ab, i
        else:
            j = l // 2
            qkv, *new_c = _inproj(xs, norm_mix, mod_l, l, w_in_c, j, C_KV_HEADS, C_Q_WIDTH, n_c,
                                  new_c, rope_tables)
            mix = _ctx_attn_c(qkv, sink_c, j)
            mix = _lat_attn_c(qkv, cache_c_k, cache_c_v, sink_c, j, mix)
            w_out, wo_idx = w_out_c, j
        if l + 1 < DEPTH:
            x, mod = _post_mixer(xs[0], mix, norm_ffn, mod_l, l, w_out, wo_idx, w_gate_up,
                                 w_down, next_mod_inputs=(cond8, w_mod, b_mod3))
            xs = (x,)
        else:
            y_ctx, y_lat = _post_mixer(xs[0], mix, norm_ffn, mod_l, l, w_out, wo_idx, w_gate_up,
                                       w_down, norm_final)

    new_caches = [jnp.transpose(t, (0, 1, 4, 2, 3)) for t in (*new_a, *new_c)]
    return (y_ctx.reshape(BATCH, SEQ, D_MODEL), y_lat.reshape(DEC_BATCH, DEC_SEQ, D_MODEL),
            *new_caches)
```

```python
import functools

import jax
import jax.numpy as jnp
import numpy as np
from jax import lax
from jax.experimental import pallas as pl
from jax.experimental.pallas import tpu as pltpu

D_MODEL = 1024
BATCH = 16
SEQ = 256
DEPTH = 4
DEC_BATCH = 2
DEC_SEQ = 1024
PAST_LEN = 512
GRID_W = 64
HEAD_DIM = 64
A_WIDTH = 512
A_HEADS = 8
B_WIDTH = 512
POOL_WINDOWS = (2, 4, 8, 16)
B_GROUP_DIM = 128
NA_ROWS = 8
NA_COLS = 16
C_HEADS = 16
C_KV_HEADS = 4
C_GROUP = C_HEADS // C_KV_HEADS
C_Q_WIDTH = 1024
C_KV_WIDTH = 256
C_BLOCK = 128
FFN_HIDDEN = 2816
ROPE_BASE = 10000.0
EPS = 1e-6
NEG_INF = -1e30

N_CTX_TOK = BATCH * SEQ
N_LAT_TOK = DEC_BATCH * DEC_SEQ
N_TOK = N_CTX_TOK + N_LAT_TOK
GRID_ROWS = DEC_SEQ // GRID_W
N_GROUPS_PAD = 8

VMEM_LIMIT = 56 * 1024 * 1024

F32 = jnp.float32
BF16 = jnp.bfloat16


def _params(n_axes):
    return pltpu.CompilerParams(dimension_semantics=("arbitrary",) * n_axes,
                                vmem_limit_bytes=VMEM_LIMIT)


def _group_of_tile(i, tm):
    row0 = i * tm
    return jnp.where(row0 < N_CTX_TOK, 0, 1 + (row0 - N_CTX_TOK) // DEC_SEQ)


def _dot_nt(a, b):
    return lax.dot_general(a, b, (((1,), (1,)), ((), ())), preferred_element_type=F32)


def _dot(a, b):
    return jnp.dot(a, b, preferred_element_type=F32)


def _dot_tn(a, b):
    return lax.dot_general(a, b, (((0,), (0,)), ((), ())), preferred_element_type=F32)


def _softmax_pv_t(score_fns, sink_rows, values, s_refs, p_refs):
    n_groups, n_blocks = len(score_fns), len(s_refs)
    for g in range(n_groups):
        for b in range(n_blocks):
            s_refs[b][g] = score_fns[g][b]()
    inv_l = []
    for g in range(n_groups):
        m = sink_rows[g]
        for b in range(n_blocks):
            m = jnp.maximum(m, jnp.max(s_refs[b][g], axis=0, keepdims=True))
        l = jnp.exp(sink_rows[g] - m)
        for b in range(n_blocks):
            p = jnp.exp(s_refs[b][g] - m)
            l = l + jnp.sum(p, axis=0, keepdims=True)
            p_refs[b][g] = p.astype(BF16)
        inv_l.append(1.0 / l)
    outs = []
    for g in range(n_groups):
        o = None
        for b in range(n_blocks):
            pv = _dot_tn(values[g][b], p_refs[b][g])
            o = pv if o is None else o + pv
        outs.append(o * inv_l[g])
    return outs


def _modulation_block(cond_ref, w_ref, b_ref):
    c = cond_ref[...]
    s = c / (1.0 + jnp.exp(-c))
    return _dot(s.astype(BF16), w_ref[...].astype(BF16)) + b_ref[...]


def _mod_bias_kernel(rpb_ref, cond_ref, w_ref, b_ref, o_ref, bias_ref, *, heads_per_step):
    o_ref[...] = _modulation_block(cond_ref, w_ref, b_ref)
    pair0 = pl.program_id(0) * heads_per_step
    for t in range(heads_per_step):
        _write_bias_tiles(rpb_ref, (pair0 + t) * (N_DROW * N_DCOL), bias_ref.at[t])


def _modulation0_and_bias(cond8, w_mod, b_mod3, rpb_a):
    n_col_blocks = 4
    tn = 6 * D_MODEL // n_col_blocks
    n_bias_layers = rpb_a.shape[0]
    heads_per_step = n_bias_layers * A_HEADS // n_col_blocks
    steps_per_layer = A_HEADS // heads_per_step
    return pl.pallas_call(
        functools.partial(_mod_bias_kernel, heads_per_step=heads_per_step),
        out_shape=(
            jax.ShapeDtypeStruct((N_GROUPS_PAD, 6 * D_MODEL), F32),
            jax.ShapeDtypeStruct(
                (n_bias_layers, A_HEADS, N_BIAS_TILES, GRID_W, 2 * GRID_W), F32)),
        grid=(n_col_blocks,),
        in_specs=[
            pl.BlockSpec(memory_space=pltpu.SMEM),
            pl.BlockSpec((N_GROUPS_PAD, D_MODEL), lambda j: (0, 0)),
            pl.BlockSpec((None, D_MODEL, tn), lambda j: (0, 0, j)),
            pl.BlockSpec((None, 1, tn), lambda j: (0, 0, j)),
        ],
        out_specs=(
            pl.BlockSpec((N_GROUPS_PAD, tn), lambda j: (0, j)),
            pl.BlockSpec((None, heads_per_step, N_BIAS_TILES, GRID_W, 2 * GRID_W),
                         lambda j: (j // steps_per_layer, j % steps_per_layer, 0, 0, 0))),
        compiler_params=_params(1),
        name="modulation_bias",
    )(rpb_a.reshape(-1), cond8, w_mod, b_mod3)


def _norm_mod(x, g, mod_ref, shift_idx):
    var = jnp.mean(x * x, axis=-1, keepdims=True)
    y = x * lax.rsqrt(var + EPS) * g
    shift = mod_ref[:, shift_idx * D_MODEL:(shift_idx + 1) * D_MODEL]
    scale = mod_ref[:, (shift_idx + 1) * D_MODEL:(shift_idx + 2) * D_MODEL]
    return y * (1.0 + scale) + shift


QKV_WIDTH = 3 * A_WIDTH
Q_SCALE = HEAD_DIM ** -0.5
PROJ_CHUNK = 4 * HEAD_DIM
ROPE_TABLE_WIDTH = PROJ_CHUNK


def _rope(x, cos, sin_signed):
    n = x.shape[-1]
    lane = lax.broadcasted_iota(jnp.int32, x.shape, x.ndim - 1)
    first = (lane % 32) < 16
    partner = jnp.where(first, pltpu.roll(x, n - 16, axis=x.ndim - 1),
                        pltpu.roll(x, 16, axis=x.ndim - 1))
    return x * cos + partner * sin_signed


def _inproj_kernel(*refs, tm, n_heads, q_width, n_prev, split_x, rope, has_u):
    n_x = 2 if split_x else 1
    g_ref, mod_ref, w_ref = refs[n_x:n_x + 3]
    n_in = n_x + 3 + (2 if rope else 0) + n_prev
    qkv_ref, ck_ref, cv_ref = refs[n_in:n_in + 3]
    extra_out = refs[n_in + 3:-3]
    wbf_ref, h_ref, res_ref = refs[-3:]
    i = pl.program_id(0)
    is_ctx = i < N_CTX_TOK // tm
    kv_width = (QKV_WIDTH - q_width) // 2
    k_col, v_col = q_width, q_width + kv_width
    n_out = res_ref.shape[1]
    chunk = PROJ_CHUNK

    @pl.when(i == 0)
    def _():
        wbf_ref[...] = w_ref[...].astype(BF16)

    def tile(ctx):
        if split_x:
            x = (refs[0] if ctx else refs[1])[...]
            extra_out[-1][...] = x
        else:
            x = refs[0][...]
        h_ref[...] = _norm_mod(x, g_ref[...], mod_ref, 0).astype(BF16)
        def matmul(c0):
            cols = slice(c0, c0 + chunk)
            res_ref[:, cols] = _dot(h_ref[...], wbf_ref[:, cols])

        def epilogue(c0):
            cols = slice(c0, c0 + chunk)
            if c0 >= QKV_WIDTH:
                extra_out[0][:, c0 - QKV_WIDTH:c0 - QKV_WIDTH + chunk] = res_ref[:, cols]
                return
            r = res_ref[:, cols]
            if rope and not ctx and c0 < v_col:
                cos_ref, sin_ref = refs[n_x + 3:n_x + 5]
                r = _rope(r, cos_ref[...], sin_ref[...])
            if c0 < q_width:
                r = r * Q_SCALE
            qkv_ref[:, cols] = r.astype(BF16)
            if ctx and c0 >= k_col:
                c_ref, col0 = (ck_ref, k_col) if c0 < v_col else (cv_ref, v_col)
                for req in range(tm // SEQ):
                    for hc in range(chunk // HEAD_DIM):
                        hd = (c0 - col0) // HEAD_DIM + hc
                        c_ref[req, hd] = res_ref[
                            req * SEQ:(req + 1) * SEQ,
                            c0 + hc * HEAD_DIM:c0 + (hc + 1) * HEAD_DIM].T

        for c0 in range(0, n_out, chunk):
            matmul(c0)
            epilogue(c0)

    pl.when(is_ctx)(lambda: tile(True))
    pl.when(jnp.logical_not(is_ctx))(lambda: tile(False))


def _inproj(xs, g_all, mod_l, layer, w_all, w_idx, n_heads, q_width, n_slots, prev_caches,
            rope_tables=None):
    tm = 512
    n_out = w_all.shape[2]
    n_ctx_tiles = N_CTX_TOK // tm
    split_x = len(xs) == 2
    rope = rope_tables is not None
    has_u = n_out > QKV_WIDTH
    cache_shape = jax.ShapeDtypeStruct((BATCH, n_slots, n_heads, HEAD_DIM, SEQ), F32)
    cache_spec = pl.BlockSpec(
        (tm // SEQ, None, n_heads, HEAD_DIM, SEQ),
        lambda i: (jnp.minimum(i, n_ctx_tiles - 1), w_idx, 0, 0, 0))
    row_spec = pl.BlockSpec((tm, D_MODEL), lambda i: (i, 0))
    if split_x:
        x_specs = [
            pl.BlockSpec((tm, D_MODEL), lambda i: (jnp.minimum(i, n_ctx_tiles - 1), 0)),
            pl.BlockSpec((tm, D_MODEL), lambda i: (jnp.maximum(i - n_ctx_tiles, 0), 0)),
        ]
    else:
        x_specs = [row_spec]
    out_shape = [jax.ShapeDtypeStruct((N_TOK, QKV_WIDTH), BF16), cache_shape, cache_shape]
    out_specs = [pl.BlockSpec((tm, QKV_WIDTH), lambda i: (i, 0)), cache_spec, cache_spec]
    if has_u:
        out_shape.append(jax.ShapeDtypeStruct((N_TOK, n_out - QKV_WIDTH), F32))
        out_specs.append(pl.BlockSpec((tm, n_out - QKV_WIDTH), lambda i: (i, 0)))
    if split_x:
        out_shape.append(jax.ShapeDtypeStruct((N_TOK, D_MODEL), F32))
        out_specs.append(row_spec)
    rope_in, rope_specs = [], []
    if rope:
        tiles_per_seq = DEC_SEQ // tm
        rope_spec = pl.BlockSpec(
            (tm, ROPE_TABLE_WIDTH),
            lambda i: (jnp.maximum(i - n_ctx_tiles, 0) % tiles_per_seq, 0))
        rope_in, rope_specs = list(rope_tables), [rope_spec, rope_spec]
    n_prev = len(prev_caches)
    n_before = len(xs) + 3 + len(rope_in)
    return pl.pallas_call(
        functools.partial(_inproj_kernel, tm=tm, n_heads=n_heads, q_width=q_width,
                          n_prev=n_prev, split_x=split_x, rope=rope, has_u=has_u),
        out_shape=out_shape,
        grid=(N_TOK // tm,),
        in_specs=x_specs + [
            pl.BlockSpec((None, 1, D_MODEL), lambda i: (layer, 0, 0)),
            pl.BlockSpec((None, 1, 6 * D_MODEL), lambda i: (_group_of_tile(i, tm), 0, 0)),
            pl.BlockSpec((None, D_MODEL, n_out), lambda i: (w_idx, 0, 0),
                         pipeline_mode=pl.Buffered(1)),
        ] + rope_specs + [pl.BlockSpec(memory_space=pl.ANY)] * n_prev,
        out_specs=out_specs,
        scratch_shapes=[pltpu.VMEM((D_MODEL, n_out), BF16), pltpu.VMEM((tm, D_MODEL), BF16),
                        pltpu.VMEM((tm, n_out), F32)],
        input_output_aliases={n_before + k: 1 + k for k in range(n_prev)},
        compiler_params=_params(1),
        name="inproj",
    )(*xs, g_all.reshape(DEPTH, 1, D_MODEL), mod_l, w_all, *rope_in, *prev_caches)


def _ctx_attn_a_kernel(p_ref, u_ref, wp_ref, ps_ref, o_ref, s_ref, p_scr):
    for h in range(A_HEADS):
        c0 = h * HEAD_DIM
        s_ref[h] = _dot_nt(p_ref[:, c0:c0 + HEAD_DIM],
                           p_ref[:, A_WIDTH + c0:A_WIDTH + c0 + HEAD_DIM])
    _pool_rows(u_ref, wp_ref, ps_ref, o_ref, n=SEQ, out_col0=A_WIDTH)
    inv_l = []
    for h in range(A_HEADS):
        s = s_ref[h]
        p = jnp.exp(s - jnp.max(s, axis=-1, keepdims=True))
        inv_l.append(1.0 / jnp.sum(p, axis=-1, keepdims=True))
        p_scr[h] = p.astype(BF16)
    for h in range(A_HEADS):
        c0 = h * HEAD_DIM
        v = p_ref[:, 2 * A_WIDTH + c0:2 * A_WIDTH + c0 + HEAD_DIM]
        o_ref[:, c0:c0 + HEAD_DIM] = _dot(p_scr[h], v) * inv_l[h]


def _ctx_attn_a(qkv, u, w_pool, pool_scale, i_layer):
    return pl.pallas_call(
        _ctx_attn_a_kernel,
        out_shape=jax.ShapeDtypeStruct((N_TOK, D_MODEL), F32),
        grid=(BATCH,),
        in_specs=[pl.BlockSpec((SEQ, QKV_WIDTH), lambda b: (b, 0)),
                  pl.BlockSpec((SEQ, B_WIDTH), lambda b: (b, 0))] + _pool_specs(i_layer),
        out_specs=pl.BlockSpec((SEQ, D_MODEL), lambda b: (b, 0)),
        scratch_shapes=[pltpu.VMEM((A_HEADS, SEQ, SEQ), F32),
                        pltpu.VMEM((A_HEADS, SEQ, SEQ), BF16)],
        compiler_params=_params(1),
        name="ctx_attn_a",
    )(qkv, u, w_pool, pool_scale.reshape(-1, 1, B_WIDTH))


N_DROW = 2 * NA_ROWS - 1
N_DCOL = 2 * NA_COLS - 1
N_BIAS_TILES = 16
BIAS_TILE_LEFT_PAD = 14
BIAS_TILE_RIGHT_PAD = 15
MID_DROW = NA_ROWS - 1 - NA_ROWS // 2


def _write_bias_tiles(rpb_ref, base, o_ref):
    qi = lax.broadcasted_iota(jnp.int32, (GRID_W, 2 * GRID_W), 0)
    lane = lax.broadcasted_iota(jnp.int32, (GRID_W, 2 * GRID_W), 1)
    right = lane >= GRID_W
    kc = jnp.where(right, lane - GRID_W, lane)
    qstart = jnp.clip(qi - NA_COLS // 2, 0, GRID_W - NA_COLS)
    valid = (kc >= qstart) & (kc < qstart + NA_COLS)

    offs = lax.broadcasted_iota(jnp.int32, (1, 2 * GRID_W), 1) & (GRID_W - 1)
    shift = 2 * GRID_W - (NA_COLS - 1)
    rows = []
    for dr in range(N_DROW):
        vec = jnp.zeros((1, 2 * GRID_W), F32)
        for d in range(N_DCOL):
            vec = jnp.where(offs == d, rpb_ref[base + dr * N_DCOL + d], vec)
        rows.append(pltpu.roll(jnp.broadcast_to(vec, (GRID_W, 2 * GRID_W)), shift, axis=1,
                               stride=1, stride_axis=0))

    for t in range(N_DROW - 1):
        o_ref[t] = jnp.where(valid, jnp.where(right, rows[t + 1], rows[t]), NEG_INF)
    o_ref[BIAS_TILE_LEFT_PAD] = jnp.where(valid & right, rows[MID_DROW], NEG_INF)
    o_ref[BIAS_TILE_RIGHT_PAD] = jnp.where(valid & jnp.logical_not(right),
                                           rows[MID_DROW + NA_ROWS - 1], NEG_INF)


NA_GROUP_ROWS = 4


def _na_window(r):
    start = min(max(r - NA_ROWS // 2, 0), GRID_ROWS - NA_ROWS)
    first_drow = start - r + NA_ROWS - 1
    if start % 2 == 0:
        return start, [first_drow + 2 * p for p in range(NA_ROWS // 2)]
    assert first_drow == MID_DROW
    inner = [first_drow + 1 + 2 * p for p in range(NA_ROWS // 2 - 1)]
    return start - 1, [BIAS_TILE_LEFT_PAD] + inner + [BIAS_TILE_RIGHT_PAD]


def _lat_attn_a_kernel(q_ref, k_ref, v_ref, kc_ref, vc_ref, bias_ref, mix_ref, o_ref,
                       s_loc, s_ctx, p_loc, p_ctx):
    del mix_ref
    hp = pl.program_id(1)

    @pl.when((pl.program_id(0) == 0) & (hp == 0))
    def _():
        p_loc[...] = jnp.zeros_like(p_loc)

    groups = []
    for g0 in range(0, GRID_ROWS, NA_GROUP_ROWS):
        windows = [_na_window(r) for r in range(g0, g0 + NA_GROUP_ROWS)]
        lo = min(first for first, _ in windows)
        hi = max(first + 2 * len(tiles) for first, tiles in windows)
        groups.append((slice(g0 * GRID_W, (g0 + NA_GROUP_ROWS) * GRID_W),
                       slice(lo * GRID_W, hi * GRID_W)))
    for hh in range(2):
        c0 = hh * HEAD_DIM
        for q_rows, keys in groups:
            s_loc[hh, q_rows, keys] = _dot_nt(q_ref[q_rows, c0:c0 + HEAD_DIM],
                                              k_ref[keys, c0:c0 + HEAD_DIM])
        s_ctx[hh] = _dot(q_ref[:, c0:c0 + HEAD_DIM],
                         kc_ref[hh].astype(BF16))
    inv_l = [[], []]
    for hh in range(2):
        for r in range(GRID_ROWS):
            rows = slice(r * GRID_W, (r + 1) * GRID_W)
            first_row, tiles = _na_window(r)
            cols = slice(first_row * GRID_W, (first_row + 2 * len(tiles)) * GRID_W)
            bias = jnp.concatenate([bias_ref[hh, t] for t in tiles], axis=1)
            sc = s_ctx[hh, rows, :]
            sl = s_loc[hh, rows, cols] + bias
            m = jnp.maximum(jnp.max(sc, axis=-1, keepdims=True),
                            jnp.max(sl, axis=-1, keepdims=True))
            pc = jnp.exp(sc - m)
            pw = jnp.exp(sl - m)
            inv_l[hh].append(1.0 / (jnp.sum(pc, axis=-1, keepdims=True)
                                    + jnp.sum(pw, axis=-1, keepdims=True)))
            p_ctx[hh, rows, :] = pc.astype(BF16)
            p_loc[hh, rows, cols] = pw.astype(BF16)
    for hh in range(2):
        c0 = hh * HEAD_DIM
        o_loc = jnp.concatenate(
            [_dot(p_loc[hh, q_rows, keys], v_ref[keys, c0:c0 + HEAD_DIM])
             for q_rows, keys in groups], axis=0)
        o = _dot_nt(p_ctx[hh], vc_ref[hh].astype(BF16)) + o_loc
        o_ref[:, c0:c0 + HEAD_DIM] = o * jnp.concatenate(inv_l[hh], axis=0)


def _lat_attn_a(qkv, cache_k, cache_v, bias_tiles, i_layer, mix):
    row0 = N_CTX_TOK // DEC_SEQ
    pair = 2 * HEAD_DIM
    k_col0 = A_WIDTH // pair
    v_col0 = 2 * A_WIDTH // pair
    return pl.pallas_call(
        _lat_attn_a_kernel,
        out_shape=jax.ShapeDtypeStruct((N_TOK, D_MODEL), F32),
        grid=(DEC_BATCH, A_HEADS // 2),
        in_specs=[
            pl.BlockSpec((DEC_SEQ, pair), lambda b, hp: (row0 + b, hp)),
            pl.BlockSpec((DEC_SEQ, pair), lambda b, hp: (row0 + b, k_col0 + hp)),
            pl.BlockSpec((DEC_SEQ, pair), lambda b, hp: (row0 + b, v_col0 + hp)),
            pl.BlockSpec((None, None, 2, HEAD_DIM, PAST_LEN),
                         lambda b, hp: (b, i_layer, hp, 0, 0)),
            pl.BlockSpec((None, None, 2, HEAD_DIM, PAST_LEN),
                         lambda b, hp: (b, i_layer, hp, 0, 0)),
            pl.BlockSpec((None, 2, N_BIAS_TILES, GRID_W, 2 * GRID_W),
                         lambda b, hp: (i_layer, hp, 0, 0, 0)),
            pl.BlockSpec(memory_space=pl.ANY),
        ],
        out_specs=pl.BlockSpec((DEC_SEQ, pair), lambda b, hp: (row0 + b, hp)),
        scratch_shapes=[pltpu.VMEM((2, DEC_SEQ, DEC_SEQ), F32),
                        pltpu.VMEM((2, DEC_SEQ, PAST_LEN), F32),
                        pltpu.VMEM((2, DEC_SEQ, DEC_SEQ), BF16),
                        pltpu.VMEM((2, DEC_SEQ, PAST_LEN), BF16)],
        input_output_aliases={6: 0},
        compiler_params=_params(2),
        name="lat_attn_a",
    )(qkv, qkv, qkv, cache_k, cache_v, bias_tiles, mix)


def _pool_rows(u_ref, wp_ref, ps_ref, o_ref, *, n, out_col0):
    rows = n
    t = lax.broadcasted_iota(jnp.int32, (rows, B_GROUP_DIM), 0)

    def earlier(x, k):
        return jnp.where(t >= k, pltpu.roll(x, k, axis=0), 0.0)

    def later(x, k):
        return jnp.where(t < n - k, pltpu.roll(x, rows - k, axis=0), 0.0)

    tokens_before = t.astype(F32)
    tokens_from = (n - t).astype(F32)

    for g, w in enumerate(POOL_WINDOWS):
        c0 = g * B_GROUP_DIM
        half = w // 2
        u = u_ref[:, c0:c0 + B_GROUP_DIM]
        before, after = u, u
        k = 1
        while k < half:
            before = before + earlier(before, k)
            after = after + later(after, k)
            k *= 2
        total = earlier(before, 1) + after
        count = jnp.minimum(tokens_before, half) + jnp.minimum(tokens_from, half)
        pooled = total / count - u
        y = _dot(pooled.astype(BF16), wp_ref[g].astype(BF16))
        o_ref[:, out_col0 + c0:out_col0 + c0 + B_GROUP_DIM] = (
            y * ps_ref[:, c0:c0 + B_GROUP_DIM])


def _pool_specs(i_layer):
    zeros = (0,) * 3
    return [
        pl.BlockSpec((None, len(POOL_WINDOWS), B_GROUP_DIM, B_GROUP_DIM),
                     lambda *_: (i_layer,) + zeros),
        pl.BlockSpec((None, 1, B_WIDTH), lambda *_: (i_layer, 0, 0)),
    ]


def _lat_pool_kernel(u_ref, wp_ref, ps_ref, mix_ref, o_ref):
    del mix_ref
    _pool_rows(u_ref, wp_ref, ps_ref, o_ref, n=DEC_SEQ, out_col0=0)


def _lat_pool(u, w_pool, pool_scale, i_layer, mix):
    row0 = N_CTX_TOK // DEC_SEQ
    return pl.pallas_call(
        _lat_pool_kernel,
        out_shape=jax.ShapeDtypeStruct((N_TOK, D_MODEL), F32),
        grid=(DEC_BATCH,),
        in_specs=[pl.BlockSpec((DEC_SEQ, B_WIDTH), lambda b: (row0 + b, 0))]
        + _pool_specs(i_layer) + [pl.BlockSpec(memory_space=pl.ANY)],
        out_specs=pl.BlockSpec((DEC_SEQ, B_WIDTH), lambda b: (row0 + b, 1)),
        input_output_aliases={3: 0},
        compiler_params=_params(1),
        name="pool_mixer",
    )(u, w_pool, pool_scale.reshape(-1, 1, B_WIDTH), mix)


def _sink_row(sink_ref, head0, queries_per_head):
    lane = lax.broadcasted_iota(jnp.int32, (1, C_GROUP * queries_per_head), 1)
    row = jnp.full((1, C_GROUP * queries_per_head), sink_ref[head0], F32)
    for g in range(1, C_GROUP):
        row = jnp.where(lane >= g * queries_per_head, sink_ref[head0 + g], row)
    return row


CTX_REQS_PER_STEP = 2


def _ctx_attn_c_kernel(sink_ref, p_ref, o_ref, s_ref, p_scr, *, sink0):
    for req in range(CTX_REQS_PER_STEP):
        rows = slice(req * SEQ, (req + 1) * SEQ)

        def scores(kk):
            q = jnp.concatenate(
                [p_ref[rows, h * HEAD_DIM:(h + 1) * HEAD_DIM]
                 for h in range(kk * C_GROUP, (kk + 1) * C_GROUP)], axis=0)
            k0 = C_Q_WIDTH + kk * HEAD_DIM
            return _dot_nt(p_ref[rows, k0:k0 + HEAD_DIM], q)

        v0 = C_Q_WIDTH + C_KV_WIDTH
        outs = _softmax_pv_t(
            [[functools.partial(scores, kk)] for kk in range(C_KV_HEADS)],
            [_sink_row(sink_ref, sink0 + kk * C_GROUP, SEQ) for kk in range(C_KV_HEADS)],
            [[p_ref[rows, v0 + kk * HEAD_DIM:v0 + (kk + 1) * HEAD_DIM]]
             for kk in range(C_KV_HEADS)],
            [s_ref], [p_scr])
        outs_t = [o_t[:, g * SEQ:(g + 1) * SEQ] for o_t in outs for g in range(C_GROUP)]
        o_ref[rows, :] = jnp.concatenate(outs_t, axis=0).T


def _ctx_attn_c(qkv, sink_all, j_layer):
    return pl.pallas_call(
        functools.partial(_ctx_attn_c_kernel, sink0=j_layer * C_HEADS),
        out_shape=jax.ShapeDtypeStruct((N_TOK, D_MODEL), F32),
        grid=(BATCH // CTX_REQS_PER_STEP,),
        in_specs=[
            pl.BlockSpec(memory_space=pltpu.SMEM),
            pl.BlockSpec((CTX_REQS_PER_STEP * SEQ, QKV_WIDTH), lambda b: (b, 0)),
        ],
        out_specs=pl.BlockSpec((CTX_REQS_PER_STEP * SEQ, C_Q_WIDTH), lambda b: (b, 0)),
        scratch_shapes=[pltpu.VMEM((C_KV_HEADS, SEQ, C_GROUP * SEQ), F32),
                        pltpu.VMEM((C_KV_HEADS, SEQ, C_GROUP * SEQ), BF16)],
        compiler_params=_params(1),
        name="ctx_attn_c",
    )(sink_all.reshape(-1), qkv)


def _lat_attn_c_kernel(sink_ref, q_ref, k_ref, v_ref, kc_ref, vc_ref, mix_ref, o_ref,
                       kctx_ref, vctx_ref, sc_ref, sb_ref, pc_ref, pb_ref, *, sink0):
    del mix_ref
    j = pl.program_id(1)
    n_blocks = DEC_SEQ // C_BLOCK

    @pl.when(j == 0)
    def _():
        for kk in range(C_KV_HEADS):
            c0 = kk * HEAD_DIM
            kctx_ref[:, c0:c0 + HEAD_DIM] = kc_ref[kk].T.astype(BF16)
            vctx_ref[:, c0:c0 + HEAD_DIM] = vc_ref[kk].T.astype(BF16)

    n_q = C_GROUP * C_BLOCK
    qi = lax.broadcasted_iota(jnp.int32, (C_BLOCK, n_q), 1) % C_BLOCK
    jl = lax.broadcasted_iota(jnp.int32, (C_BLOCK, n_q), 0)
    valid_prev = (jl >= qi) & (j > 0)
    valid_next = (jl <= qi) & (j < n_blocks - 1)
    rows_prev = pl.ds(pl.multiple_of(jnp.maximum(j - 1, 0) * C_BLOCK, C_BLOCK), C_BLOCK)
    rows_cur = pl.ds(pl.multiple_of(j * C_BLOCK, C_BLOCK), C_BLOCK)
    rows_next = pl.ds(pl.multiple_of(jnp.minimum(j + 1, n_blocks - 1) * C_BLOCK, C_BLOCK),
                      C_BLOCK)

    def stacked_q(kk):
        return jnp.concatenate(
            [q_ref[:, h * HEAD_DIM:(h + 1) * HEAD_DIM]
             for h in range(kk * C_GROUP, (kk + 1) * C_GROUP)], axis=0)

    def band(ref, kk):
        cols = slice(kk * HEAD_DIM, (kk + 1) * HEAD_DIM)
        return jnp.concatenate(
            [ref[rows_prev, cols], ref[rows_cur, cols], ref[rows_next, cols]], axis=0)

    def ctx_scores(kk):
        return _dot_nt(kctx_ref[:, kk * HEAD_DIM:(kk + 1) * HEAD_DIM], stacked_q(kk))

    def band_scores(kk):
        s_raw = _dot_nt(band(k_ref, kk), stacked_q(kk))
        return jnp.concatenate(
            [jnp.where(valid_prev, s_raw[:C_BLOCK], NEG_INF),
             s_raw[C_BLOCK:2 * C_BLOCK],
             jnp.where(valid_next, s_raw[2 * C_BLOCK:], NEG_INF)], axis=0)

    kv_heads = range(C_KV_HEADS)
    outs = _softmax_pv_t(
        [[functools.partial(ctx_scores, kk), functools.partial(band_scores, kk)]
         for kk in kv_heads],
        [_sink_row(sink_ref, sink0 + kk * C_GROUP, C_BLOCK) for kk in kv_heads],
        [[vctx_ref[:, kk * HEAD_DIM:(kk + 1) * HEAD_DIM], band(v_ref, kk)] for kk in kv_heads],
        [sc_ref, sb_ref], [pc_ref, pb_ref])
    outs_t = [o_t[:, g * C_BLOCK:(g + 1) * C_BLOCK] for o_t in outs for g in range(C_GROUP)]
    o_ref[...] = jnp.concatenate(outs_t, axis=0).T


def _lat_attn_c(qkv, cache_k, cache_v, sink, j_layer, mix):
    n_blocks = DEC_SEQ // C_BLOCK
    q_row0 = N_CTX_TOK // C_BLOCK
    kv_row0 = N_CTX_TOK // DEC_SEQ
    k_col = C_Q_WIDTH // C_KV_WIDTH
    n_q = C_GROUP * C_BLOCK
    return pl.pallas_call(
        functools.partial(_lat_attn_c_kernel, sink0=j_layer * C_HEADS),
        out_shape=jax.ShapeDtypeStruct((N_TOK, D_MODEL), F32),
        grid=(DEC_BATCH, n_blocks),
        in_specs=[
            pl.BlockSpec(memory_space=pltpu.SMEM),
            pl.BlockSpec((C_BLOCK, C_Q_WIDTH), lambda b, j: (q_row0 + b * n_blocks + j, 0)),
            pl.BlockSpec((DEC_SEQ, C_KV_WIDTH), lambda b, j: (kv_row0 + b, k_col)),
            pl.BlockSpec((DEC_SEQ, C_KV_WIDTH), lambda b, j: (kv_row0 + b, k_col + 1)),
            pl.BlockSpec((None, None, C_KV_HEADS, HEAD_DIM, PAST_LEN),
                         lambda b, j: (b, j_layer, 0, 0, 0)),
            pl.BlockSpec((None, None, C_KV_HEADS, HEAD_DIM, PAST_LEN),
                         lambda b, j: (b, j_layer, 0, 0, 0)),
            pl.BlockSpec(memory_space=pl.ANY),
        ],
        out_specs=pl.BlockSpec((C_BLOCK, C_Q_WIDTH),
                               lambda b, j: (q_row0 + b * n_blocks + j, 0)),
        input_output_aliases={6: 0},
        scratch_shapes=[pltpu.VMEM((PAST_LEN, C_KV_WIDTH), BF16),
                        pltpu.VMEM((PAST_LEN, C_KV_WIDTH), BF16),
                        pltpu.VMEM((C_KV_HEADS, PAST_LEN, n_q), F32),
                        pltpu.VMEM((C_KV_HEADS, 3 * C_BLOCK, n_q), F32),
                        pltpu.VMEM((C_KV_HEADS, PAST_LEN, n_q), BF16),
                        pltpu.VMEM((C_KV_HEADS, 3 * C_BLOCK, n_q), BF16)],
        compiler_params=_params(2),
        name="lat_attn_c",
    )(sink.reshape(-1), qkv, qkv, qkv, cache_k, cache_v, mix)


def _rope_tables():
    t = np.arange(DEC_SEQ)
    pos = np.stack([t // GRID_W, t % GRID_W], axis=-1).astype(np.float64)
    half = HEAD_DIM // 4
    inv = ROPE_BASE ** (-np.arange(half, dtype=np.float64) / half)
    ang = pos[:, :, None] * inv
    cos = np.cos(ang)
    sin = np.sin(ang)
    cos64 = np.stack([cos, cos], axis=2).reshape(DEC_SEQ, HEAD_DIM)
    sin64 = np.stack([-sin, sin], axis=2).reshape(DEC_SEQ, HEAD_DIM)
    reps = ROPE_TABLE_WIDTH // HEAD_DIM
    return (jnp.asarray(np.tile(cos64, (1, reps)), F32),
            jnp.asarray(np.tile(sin64, (1, reps)), F32))


FFN_CHUNK = 256
N_FFN_CHUNKS = FFN_HIDDEN // FFN_CHUNK
N_WO_PIECES = D_MODEL // FFN_CHUNK


def _post_mixer_kernel(*refs, layer, wo_idx, tm, final):
    x_ref, mix_ref, g_ref, mod_ref = refs[:4]
    n_in = 5 if final else 7
    wo_hbm, wgu_hbm, wd_hbm = refs[n_in:n_in + 3]
    n_out = 2
    out_refs = refs[n_in + 3:n_in + 3 + n_out]
    (wo_s, wg_s, wu_s, wd_s, act_s, stage_col, stage_row, sem_col,
     sem_row) = refs[n_in + 3 + n_out:]
    n_row_pieces = N_WO_PIECES + N_FFN_CHUNKS

    def col_copy(which, c):
        src = wgu_hbm.at[layer, :, pl.ds(which * FFN_HIDDEN + c * FFN_CHUNK, FFN_CHUNK)]
        return pltpu.make_async_copy(src, stage_col.at[which, c % 2], sem_col.at[which, c % 2])

    def row_copy(p):
        if p < N_WO_PIECES:
            src = wo_hbm.at[wo_idx, pl.ds(p * FFN_CHUNK, FFN_CHUNK), :]
        else:
            src = wd_hbm.at[layer, pl.ds((p - N_WO_PIECES) * FFN_CHUNK, FFN_CHUNK), :]
        return pltpu.make_async_copy(src, stage_row.at[p % 2], sem_row.at[p % 2])

    def take_row_piece(p, dst_ref, row0):
        if p + 1 < n_row_pieces:
            row_copy(p + 1).start()
        row_copy(p).wait()
        dst_ref[row0:row0 + FFN_CHUNK, :] = stage_row[p % 2].astype(BF16)

    def tile(load_weights):
        if load_weights:
            row_copy(0).start()
            for which in range(2):
                col_copy(which, 0).start()
            for p in range(N_WO_PIECES):
                take_row_piece(p, wo_s, p * FFN_CHUNK)
        if not final:
            out_refs[1][...] = _modulation_block(*refs[4:7])
        gate1 = mod_ref[:, 2 * D_MODEL:3 * D_MODEL]
        x1 = x_ref[...] + gate1 * _dot(mix_ref[...].astype(BF16), wo_s[...])
        h = _norm_mod(x1, g_ref[...], mod_ref, 3).astype(BF16)
        for c in range(N_FFN_CHUNKS):
            if load_weights:
                for which, dst in ((0, wg_s), (1, wu_s)):
                    if c + 1 < N_FFN_CHUNKS:
                        col_copy(which, c + 1).start()
                    col_copy(which, c).wait()
                    dst[c] = stage_col[which, c % 2].astype(BF16)
            gate = _dot(h, wg_s[c])
            up = _dot(h, wu_s[c])
            act = gate / (1.0 + jnp.exp(-gate)) * up
            act_s[:, c * FFN_CHUNK:(c + 1) * FFN_CHUNK] = act.astype(BF16)
            if load_weights:
                take_row_piece(N_WO_PIECES + c, wd_s, c * FFN_CHUNK)
        gate2 = mod_ref[:, 5 * D_MODEL:6 * D_MODEL]
        return x1 + gate2 * _dot(act_s[...], wd_s[...])

    def emit(x2):
        if not final:
            out_refs[0][...] = x2
            return
        var = jnp.mean(x2 * x2, axis=-1, keepdims=True)
        y = x2 * lax.rsqrt(var + EPS) * refs[4][...]
        is_ctx = pl.program_id(0) < N_CTX_TOK // tm

        @pl.when(is_ctx)
        def _():
            out_refs[0][...] = y

        @pl.when(jnp.logical_not(is_ctx))
        def _():
            out_refs[1][...] = y

    first = pl.program_id(0) == 0

    @pl.when(first)
    def _():
        emit(tile(True))

    @pl.when(jnp.logical_not(first))
    def _():
        emit(tile(False))


def _post_mixer(x, mix, g_all, mod_l, layer, w_out_all, wo_idx, w_gate_up, w_down,
                norm_final=None, next_mod_inputs=None):
    tm = 512
    n_tiles = N_TOK // tm
    n_ctx_tiles = N_CTX_TOK // tm
    final = norm_final is not None
    row_spec = pl.BlockSpec((tm, D_MODEL), lambda i: (i, 0))
    hbm = pl.BlockSpec(memory_space=pl.ANY)
    if final:
        extra_in = [norm_final.reshape(1, D_MODEL)]
        extra_specs = [pl.BlockSpec((1, D_MODEL), lambda i: (0, 0))]
        out_shape = (jax.ShapeDtypeStruct((N_CTX_TOK, D_MODEL), F32),
                     jax.ShapeDtypeStruct((N_LAT_TOK, D_MODEL), F32))
        out_specs = (
            pl.BlockSpec((tm, D_MODEL), lambda i: (jnp.minimum(i, n_ctx_tiles - 1), 0)),
            pl.BlockSpec((tm, D_MODEL), lambda i: (jnp.maximum(i - n_ctx_tiles, 0), 0)))
    else:
        tn = 6 * D_MODEL // n_tiles
        extra_in = list(next_mod_inputs)
        extra_specs = [
            pl.BlockSpec((N_GROUPS_PAD, D_MODEL), lambda i: (0, 0)),
            pl.BlockSpec((None, D_MODEL, tn), lambda i: (layer + 1, 0, i)),
            pl.BlockSpec((None, 1, tn), lambda i: (layer + 1, 0, i)),
        ]
        out_shape = (jax.ShapeDtypeStruct((N_TOK, D_MODEL), F32),
                     jax.ShapeDtypeStruct((N_GROUPS_PAD, 6 * D_MODEL), F32))
        out_specs = (row_spec, pl.BlockSpec((N_GROUPS_PAD, tn), lambda i: (0, i)))
    return pl.pallas_call(
        functools.partial(_post_mixer_kernel, layer=layer, wo_idx=wo_idx, tm=tm, final=final),
        out_shape=out_shape,
        grid=(N_TOK // tm,),
        in_specs=[
            row_spec,
            row_spec,
            pl.BlockSpec((None, 1, D_MODEL), lambda i: (layer, 0, 0)),
            pl.BlockSpec((None, 1, 6 * D_MODEL), lambda i: (_group_of_tile(i, tm), 0, 0)),
        ] + extra_specs + [hbm, hbm, hbm],
        out_specs=out_specs,
        scratch_shapes=[
            pltpu.VMEM((D_MODEL, D_MODEL), BF16),
            pltpu.VMEM((N_FFN_CHUNKS, D_MODEL, FFN_CHUNK), BF16),
            pltpu.VMEM((N_FFN_CHUNKS, D_MODEL, FFN_CHUNK), BF16),
            pltpu.VMEM((FFN_HIDDEN, D_MODEL), BF16),
            pltpu.VMEM((tm, FFN_HIDDEN), BF16),
            pltpu.VMEM((2, 2, D_MODEL, FFN_CHUNK), F32),
            pltpu.VMEM((2, FFN_CHUNK, D_MODEL), F32),
            pltpu.SemaphoreType.DMA((2, 2)),
            pltpu.SemaphoreType.DMA((2,)),
        ],
        compiler_params=_params(1),
        name="post_mixer",
    )(x, mix, g_all.reshape(DEPTH, 1, D_MODEL), mod_l, *extra_in, w_out_all, w_gate_up, w_down)


def kernel(x_prompt, x_sample, cache_a_k, cache_a_v, cache_c_k, cache_c_v, c, c_ctx, w_mod, b_mod, norm_mix, norm_ffn, w_in_ab, rpb_a, w_pool, pool_scale, w_out_ab, w_in_c, sink_c, w_out_c, w_gate_up, w_down, norm_final):
    xs = (x_prompt.reshape(N_CTX_TOK, D_MODEL), x_sample.reshape(N_LAT_TOK, D_MODEL))
    cond8 = jnp.concatenate(
        [c_ctx[None], c, jnp.zeros((N_GROUPS_PAD - 1 - DEC_BATCH, D_MODEL), F32)], axis=0)
    b_mod3 = b_mod.reshape(DEPTH, 1, 6 * D_MODEL)
    mod, bias_tiles = _modulation0_and_bias(cond8, w_mod, b_mod3, rpb_a)

    n_ab = cache_a_k.shape[1]
    n_c = cache_c_k.shape[1]
    cache_a_k, cache_a_v, cache_c_k, cache_c_v = (
        jnp.transpose(t, (0, 1, 3, 4, 2)) for t in (cache_a_k, cache_a_v, cache_c_k, cache_c_v))
    rope_tables = _rope_tables()

    new_a = []
    new_c = []
    for l in range(DEPTH):
        mod_l = mod.reshape(N_GROUPS_PAD, 1, 6 * D_MODEL)
        if l % 2 == 0:
            i = l // 2
            qkv, *rest = _inproj(xs, norm_mix, mod_l, l, w_in_ab, i, A_HEADS, A_WIDTH, n_ab,
                                 new_a)
            new_a, u = rest[:2], rest[2]
            if len(xs) == 2:
                xs = (rest[3],)
            mix = _ctx_attn_a(qkv, u, w_pool, pool_scale, i)
            mix = _lat_attn_a(qkv, cache_a_k, cache_a_v, bias_tiles, i, mix)
            mix = _lat_pool(u, w_pool, pool_scale, i, mix)
            w_out, wo_idx = w_out_ab, i
        else:
            j = l // 2
            qkv, *new_c = _inproj(xs, norm_mix, mod_l, l, w_in_c, j, C_KV_HEADS, C_Q_WIDTH, n_c,
                                  new_c, rope_tables)
            mix = _ctx_attn_c(qkv, sink_c, j)
            mix = _lat_attn_c(qkv, cache_c_k, cache_c_v, sink_c, j, mix)
            w_out, wo_idx = w_out_c, j
        if l + 1 < DEPTH:
            x, mod = _post_mixer(xs[0], mix, norm_ffn, mod_l, l, w_out, wo_idx, w_gate_up,
                                 w_down, next_mod_inputs=(cond8, w_mod, b_mod3))
            xs = (x,)
        else:
            y_ctx, y_lat = _post_mixer(xs[0], mix, norm_ffn, mod_l, l, w_out, wo_idx, w_gate_up,
                                       w_down, norm_final)

    new_caches = [jnp.transpose(t, (0, 1, 4, 2, 3)) for t in (*new_a, *new_c)]
    return (y_ctx.reshape(BATCH, SEQ, D_MODEL), y_lat.reshape(DEC_BATCH, DEC_SEQ, D_MODEL),
            *new_caches)
```

```python
import functools

import jax
import jax.numpy as jnp
import numpy as np
from jax import lax
from jax.experimental import pallas as pl
from jax.experimental.pallas import tpu as pltpu

D_MODEL = 1024
BATCH = 16
SEQ = 256
DEPTH = 4
DEC_BATCH = 2
DEC_SEQ = 1024
PAST_LEN = 512
GRID_W = 64
HEAD_DIM = 64
A_WIDTH = 512
A_HEADS = 8
B_WIDTH = 512
POOL_WINDOWS = (2, 4, 8, 16)
B_GROUP_DIM = 128
NA_ROWS = 8
NA_COLS = 16
C_HEADS = 16
C_KV_HEADS = 4
C_GROUP = C_HEADS // C_KV_HEADS
C_Q_WIDTH = 1024
C_KV_WIDTH = 256
C_BLOCK = 128
FFN_HIDDEN = 2816
ROPE_BASE = 10000.0
EPS = 1e-6
NEG_INF = -1e30

N_CTX_TOK = BATCH * SEQ
N_LAT_TOK = DEC_BATCH * DEC_SEQ
N_TOK = N_CTX_TOK + N_LAT_TOK
GRID_ROWS = DEC_SEQ // GRID_W
N_GROUPS_PAD = 8

VMEM_LIMIT = 56 * 1024 * 1024

F32 = jnp.float32
BF16 = jnp.bfloat16


def _params(n_axes):
    return pltpu.CompilerParams(dimension_semantics=("arbitrary",) * n_axes,
                                vmem_limit_bytes=VMEM_LIMIT)


def _group_of_tile(i, tm):
    row0 = i * tm
    return jnp.where(row0 < N_CTX_TOK, 0, 1 + (row0 - N_CTX_TOK) // DEC_SEQ)


def _dot_nt(a, b):
    return lax.dot_general(a, b, (((1,), (1,)), ((), ())), preferred_element_type=F32)


def _dot(a, b):
    return jnp.dot(a, b, preferred_element_type=F32)


def _dot_tn(a, b):
    return lax.dot_general(a, b, (((0,), (0,)), ((), ())), preferred_element_type=F32)


def _softmax_pv_t(score_fns, sink_rows, values, s_refs, p_refs):
    n_groups, n_blocks = len(score_fns), len(s_refs)
    for g in range(n_groups):
        for b in range(n_blocks):
            s_refs[b][g] = score_fns[g][b]()
    inv_l = []
    for g in range(n_groups):
        m = sink_rows[g]
        for b in range(n_blocks):
            m = jnp.maximum(m, jnp.max(s_refs[b][g], axis=0, keepdims=True))
        l = jnp.exp(sink_rows[g] - m)
        for b in range(n_blocks):
            p = jnp.exp(s_refs[b][g] - m)
            l = l + jnp.sum(p, axis=0, keepdims=True)
            p_refs[b][g] = p.astype(BF16)
        inv_l.append(1.0 / l)
    outs = []
    for g in range(n_groups):
        o = None
        for b in range(n_blocks):
            pv = _dot_tn(values[g][b], p_refs[b][g])
            o = pv if o is None else o + pv
        outs.append(o * inv_l[g])
    return outs


def _modulation_block(cond_ref, w_ref, b_ref):
    c = cond_ref[...]
    s = c / (1.0 + jnp.exp(-c))
    return _dot(s.astype(BF16), w_ref[...].astype(BF16)) + b_ref[...]


def _mod_bias_kernel(rpb_ref, cond_ref, w_ref, b_ref, o_ref, bias_ref, *, heads_per_step):
    o_ref[...] = _modulation_block(cond_ref, w_ref, b_ref)
    pair0 = pl.program_id(0) * heads_per_step
    for t in range(heads_per_step):
        _write_bias_tiles(rpb_ref, (pair0 + t) * (N_DROW * N_DCOL), bias_ref.at[t])


def _modulation0_and_bias(cond8, w_mod, b_mod3, rpb_a):
    n_col_blocks = 4
    tn = 6 * D_MODEL // n_col_blocks
    n_bias_layers = rpb_a.shape[0]
    heads_per_step = n_bias_layers * A_HEADS // n_col_blocks
    steps_per_layer = A_HEADS // heads_per_step
    return pl.pallas_call(
        functools.partial(_mod_bias_kernel, heads_per_step=heads_per_step),
        out_shape=(
            jax.ShapeDtypeStruct((N_GROUPS_PAD, 6 * D_MODEL), F32),
            jax.ShapeDtypeStruct(
                (n_bias_layers, A_HEADS, N_BIAS_TILES, GRID_W, 2 * GRID_W), F32)),
        grid=(n_col_blocks,),
        in_specs=[
            pl.BlockSpec(memory_space=pltpu.SMEM),
            pl.BlockSpec((N_GROUPS_PAD, D_MODEL), lambda j: (0, 0)),
            pl.BlockSpec((None, D_MODEL, tn), lambda j: (0, 0, j)),
            pl.BlockSpec((None, 1, tn), lambda j: (0, 0, j)),
        ],
        out_specs=(
            pl.BlockSpec((N_GROUPS_PAD, tn), lambda j: (0, j)),
            pl.BlockSpec((None, heads_per_step, N_BIAS_TILES, GRID_W, 2 * GRID_W),
                         lambda j: (j // steps_per_layer, j % steps_per_layer, 0, 0, 0))),
        compiler_params=_params(1),
        name="modulation_bias",
    )(rpb_a.reshape(-1), cond8, w_mod, b_mod3)


def _norm_mod(x, g, mod_ref, shift_idx):
    var = jnp.mean(x * x, axis=-1, keepdims=True)
    y = x * lax.rsqrt(var + EPS) * g
    shift = mod_ref[:, shift_idx * D_MODEL:(shift_idx + 1) * D_MODEL]
    scale = mod_ref[:, (shift_idx + 1) * D_MODEL:(shift_idx + 2) * D_MODEL]
    return y * (1.0 + scale) + shift


QKV_WIDTH = 3 * A_WIDTH
Q_SCALE = HEAD_DIM ** -0.5
PROJ_CHUNK = 4 * HEAD_DIM
ROPE_TABLE_WIDTH = PROJ_CHUNK


def _rope(x, cos, sin_signed):
    n = x.shape[-1]
    lane = lax.broadcasted_iota(jnp.int32, x.shape, x.ndim - 1)
    first = (lane % 32) < 16
    partner = jnp.where(first, pltpu.roll(x, n - 16, axis=x.ndim - 1),
                        pltpu.roll(x, 16, axis=x.ndim - 1))
    return x * cos + partner * sin_signed


def _inproj_kernel(*refs, tm, n_heads, q_width, n_prev, split_x, rope, has_u):
    n_x = 2 if split_x else 1
    g_ref, mod_ref, w_ref = refs[n_x:n_x + 3]
    n_in = n_x + 3 + (2 if rope else 0) + n_prev
    qkv_ref, ck_ref, cv_ref = refs[n_in:n_in + 3]
    extra_out = refs[n_in + 3:-3]
    wbf_ref, h_ref, res_ref = refs[-3:]
    i = pl.program_id(0)
    is_ctx = i < N_CTX_TOK // tm
    kv_width = (QKV_WIDTH - q_width) // 2
    k_col, v_col = q_width, q_width + kv_width
    n_out = res_ref.shape[1]
    chunk = PROJ_CHUNK

    @pl.when(i == 0)
    def _():
        wbf_ref[...] = w_ref[...].astype(BF16)

    def tile(ctx):
        if split_x:
            x = (refs[0] if ctx else refs[1])[...]
            extra_out[-1][...] = x
        else:
            x = refs[0][...]
        h_ref[...] = _norm_mod(x, g_ref[...], mod_ref, 0).astype(BF16)
        def matmul(c0):
            cols = slice(c0, c0 + chunk)
            res_ref[:, cols] = _dot(h_ref[...], wbf_ref[:, cols])

        def epilogue(c0):
            cols = slice(c0, c0 + chunk)
            if c0 >= QKV_WIDTH:
                extra_out[0][:, c0 - QKV_WIDTH:c0 - QKV_WIDTH + chunk] = res_ref[:, cols]
                return
            r = res_ref[:, cols]
            if rope and not ctx and c0 < v_col:
                cos_ref, sin_ref = refs[n_x + 3:n_x + 5]
                r = _rope(r, cos_ref[...], sin_ref[...])
            if c0 < q_width:
                r = r * Q_SCALE
            qkv_ref[:, cols] = r.astype(BF16)
            if ctx and c0 >= k_col:
                c_ref, col0 = (ck_ref, k_col) if c0 < v_col else (cv_ref, v_col)
                for req in range(tm // SEQ):
                    for hc in range(chunk // HEAD_DIM):
                        hd = (c0 - col0) // HEAD_DIM + hc
                        c_ref[req, hd] = res_ref[
                            req * SEQ:(req + 1) * SEQ,
                            c0 + hc * HEAD_DIM:c0 + (hc + 1) * HEAD_DIM].T

        for c0 in range(0, n_out, chunk):
            matmul(c0)
            epilogue(c0)

    pl.when(is_ctx)(lambda: tile(True))
    pl.when(jnp.logical_not(is_ctx))(lambda: tile(False))


def _inproj(xs, g_all, mod_l, layer, w_all, w_idx, n_heads, q_width, n_slots, prev_caches,
            rope_tables=None):
    tm = 512
    n_out = w_all.shape[2]
    n_ctx_tiles = N_CTX_TOK // tm
    split_x = len(xs) == 2
    rope = rope_tables is not None
    has_u = n_out > QKV_WIDTH
    cache_shape = jax.ShapeDtypeStruct((BATCH, n_slots, n_heads, HEAD_DIM, SEQ), F32)
    cache_spec = pl.BlockSpec(
        (tm // SEQ, None, n_heads, HEAD_DIM, SEQ),
        lambda i: (jnp.minimum(i, n_ctx_tiles - 1), w_idx, 0, 0, 0))
    row_spec = pl.BlockSpec((tm, D_MODEL), lambda i: (i, 0))
    if split_x:
        x_specs = [
            pl.BlockSpec((tm, D_MODEL), lambda i: (jnp.minimum(i, n_ctx_tiles - 1), 0)),
            pl.BlockSpec((tm, D_MODEL), lambda i: (jnp.maximum(i - n_ctx_tiles, 0), 0)),
        ]
    else:
        x_specs = [row_spec]
    out_shape = [jax.ShapeDtypeStruct((N_TOK, QKV_WIDTH), BF16), cache_shape, cache_shape]
    out_specs = [pl.BlockSpec((tm, QKV_WIDTH), lambda i: (i, 0)), cache_spec, cache_spec]
    if has_u:
        out_shape.append(jax.ShapeDtypeStruct((N_TOK, n_out - QKV_WIDTH), F32))
        out_specs.append(pl.BlockSpec((tm, n_out - QKV_WIDTH), lambda i: (i, 0)))
    if split_x:
        out_shape.append(jax.ShapeDtypeStruct((N_TOK, D_MODEL), F32))
        out_specs.append(row_spec)
    rope_in, rope_specs = [], []
    if rope:
        tiles_per_seq = DEC_SEQ // tm
        rope_spec = pl.BlockSpec(
            (tm, ROPE_TABLE_WIDTH),
            lambda i: (jnp.maximum(i - n_ctx_tiles, 0) % tiles_per_seq, 0))
        rope_in, rope_specs = list(rope_tables), [rope_spec, rope_spec]
    n_prev = len(prev_caches)
    n_before = len(xs) + 3 + len(rope_in)
    return pl.pallas_call(
        functools.partial(_inproj_kernel, tm=tm, n_heads=n_heads, q_width=q_width,
                          n_prev=n_prev, split_x=split_x, rope=rope, has_u=has_u),
        out_shape=out_shape,
        grid=(N_TOK // tm,),
        in_specs=x_specs + [
            pl.BlockSpec((None, 1, D_MODEL), lambda i: (layer, 0, 0)),
            pl.BlockSpec((None, 1, 6 * D_MODEL), lambda i: (_group_of_tile(i, tm), 0, 0)),
            pl.BlockSpec((None, D_MODEL, n_out), lambda i: (w_idx, 0, 0),
                         pipeline_mode=pl.Buffered(1)),
        ] + rope_specs + [pl.BlockSpec(memory_space=pl.ANY)] * n_prev,
        out_specs=out_specs,
        scratch_shapes=[pltpu.VMEM((D_MODEL, n_out), BF16), pltpu.VMEM((tm, D_MODEL), BF16),
                        pltpu.VMEM((tm, n_out), F32)],
        input_output_aliases={n_before + k: 1 + k for k in range(n_prev)},
        compiler_params=_params(1),
        name="inproj",
    )(*xs, g_all.reshape(DEPTH, 1, D_MODEL), mod_l, w_all, *rope_in, *prev_caches)


def _ctx_attn_a_kernel(p_ref, u_ref, wp_ref, ps_ref, o_ref, s_ref, p_scr):
    for h in range(A_HEADS):
        c0 = h * HEAD_DIM
        s_ref[h] = _dot_nt(p_ref[:, c0:c0 + HEAD_DIM],
                           p_ref[:, A_WIDTH + c0:A_WIDTH + c0 + HEAD_DIM])
    _pool_rows(u_ref, wp_ref, ps_ref, o_ref, n=SEQ, out_col0=A_WIDTH)
    inv_l = []
    for h in range(A_HEADS):
        s = s_ref[h]
        p = jnp.exp(s - jnp.max(s, axis=-1, keepdims=True))
        inv_l.append(1.0 / jnp.sum(p, axis=-1, keepdims=True))
        p_scr[h] = p.astype(BF16)
    for h in range(A_HEADS):
        c0 = h * HEAD_DIM
        v = p_ref[:, 2 * A_WIDTH + c0:2 * A_WIDTH + c0 + HEAD_DIM]
        o_ref[:, c0:c0 + HEAD_DIM] = _dot(p_scr[h], v) * inv_l[h]


def _ctx_attn_a(qkv, u, w_pool, pool_scale, i_layer):
    return pl.pallas_call(
        _ctx_attn_a_kernel,
        out_shape=jax.ShapeDtypeStruct((N_TOK, D_MODEL), F32),
        grid=(BATCH,),
        in_specs=[pl.BlockSpec((SEQ, QKV_WIDTH), lambda b: (b, 0)),
                  pl.BlockSpec((SEQ, B_WIDTH), lambda b: (b, 0))] + _pool_specs(i_layer),
        out_specs=pl.BlockSpec((SEQ, D_MODEL), lambda b: (b, 0)),
        scratch_shapes=[pltpu.VMEM((A_HEADS, SEQ, SEQ), F32),
                        pltpu.VMEM((A_HEADS, SEQ, SEQ), BF16)],
        compiler_params=_params(1),
        name="ctx_attn_a",
    )(qkv, u, w_pool, pool_scale.reshape(-1, 1, B_WIDTH))


N_DROW = 2 * NA_ROWS - 1
N_DCOL = 2 * NA_COLS - 1
N_BIAS_TILES = 16
BIAS_TILE_LEFT_PAD = 14
BIAS_TILE_RIGHT_PAD = 15
MID_DROW = NA_ROWS - 1 - NA_ROWS // 2


def _write_bias_tiles(rpb_ref, base, o_ref):
    qi = lax.broadcasted_iota(jnp.int32, (GRID_W, 2 * GRID_W), 0)
    lane = lax.broadcasted_iota(jnp.int32, (GRID_W, 2 * GRID_W), 1)
    right = lane >= GRID_W
    kc = jnp.where(right, lane - GRID_W, lane)
    qstart = jnp.clip(qi - NA_COLS // 2, 0, GRID_W - NA_COLS)
    valid = (kc >= qstart) & (kc < qstart + NA_COLS)

    offs = lax.broadcasted_iota(jnp.int32, (1, 2 * GRID_W), 1) & (GRID_W - 1)
    shift = 2 * GRID_W - (NA_COLS - 1)
    rows = []
    for dr in range(N_DROW):
        vec = jnp.zeros((1, 2 * GRID_W), F32)
        for d in range(N_DCOL):
            vec = jnp.where(offs == d, rpb_ref[base + dr * N_DCOL + d], vec)
        rows.append(pltpu.roll(jnp.broadcast_to(vec, (GRID_W, 2 * GRID_W)), shift, axis=1,
                               stride=1, stride_axis=0))

    for t in range(N_DROW - 1):
        o_ref[t] = jnp.where(valid, jnp.where(right, rows[t + 1], rows[t]), NEG_INF)
    o_ref[BIAS_TILE_LEFT_PAD] = jnp.where(valid & right, rows[MID_DROW], NEG_INF)
    o_ref[BIAS_TILE_RIGHT_PAD] = jnp.where(valid & jnp.logical_not(right),
                                           rows[MID_DROW + NA_ROWS - 1], NEG_INF)


NA_GROUP_ROWS = 4


def _na_window(r):
    start = min(max(r - NA_ROWS // 2, 0), GRID_ROWS - NA_ROWS)
    first_drow = start - r + NA_ROWS - 1
    if start % 2 == 0:
        return start, [first_drow + 2 * p for p in range(NA_ROWS // 2)]
    assert first_drow == MID_DROW
    inner = [first_drow + 1 + 2 * p for p in range(NA_ROWS // 2 - 1)]
    return start - 1, [BIAS_TILE_LEFT_PAD] + inner + [BIAS_TILE_RIGHT_PAD]


def _lat_attn_a_kernel(qkv_ref, u_ref, wp_ref, ps_ref, kc_ref, vc_ref, bias_ref, mix_ref, o_ref,
                       s_loc, s_ctx, p_loc, p_ctx):
    del mix_ref

    @pl.when(pl.program_id(0) == 0)
    def _():
        p_loc[...] = jnp.zeros_like(p_loc)

    groups = []
    for g0 in range(0, GRID_ROWS, NA_GROUP_ROWS):
        windows = [_na_window(r) for r in range(g0, g0 + NA_GROUP_ROWS)]
        lo = min(first for first, _ in windows)
        hi = max(first + 2 * len(tiles) for first, tiles in windows)
        groups.append((slice(g0 * GRID_W, (g0 + NA_GROUP_ROWS) * GRID_W),
                       slice(lo * GRID_W, hi * GRID_W)))

    def head_pair(hp):
        heads = (2 * hp, 2 * hp + 1)
        for hh, head in enumerate(heads):
            q0, k0 = head * HEAD_DIM, A_WIDTH + head * HEAD_DIM
            for q_rows, keys in groups:
                s_loc[hh, q_rows, keys] = _dot_nt(qkv_ref[q_rows, q0:q0 + HEAD_DIM],
                                                  qkv_ref[keys, k0:k0 + HEAD_DIM])
            s_ctx[hh] = _dot(qkv_ref[:, q0:q0 + HEAD_DIM],
                             kc_ref[head].astype(BF16))
        inv_l = [[], []]
        for hh, head in enumerate(heads):
            for r in range(GRID_ROWS):
                rows = slice(r * GRID_W, (r + 1) * GRID_W)
                first_row, tiles = _na_window(r)
                cols = slice(first_row * GRID_W, (first_row + 2 * len(tiles)) * GRID_W)
                bias = jnp.concatenate([bias_ref[head, t] for t in tiles], axis=1)
                sc = s_ctx[hh, rows, :]
                sl = s_loc[hh, rows, cols] + bias
                m = jnp.maximum(jnp.max(sc, axis=-1, keepdims=True),
                                jnp.max(sl, axis=-1, keepdims=True))
                pc = jnp.exp(sc - m)
                pw = jnp.exp(sl - m)
                inv_l[hh].append(1.0 / (jnp.sum(pc, axis=-1, keepdims=True)
                                        + jnp.sum(pw, axis=-1, keepdims=True)))
                p_ctx[hh, rows, :] = pc.astype(BF16)
                p_loc[hh, rows, cols] = pw.astype(BF16)
        for hh, head in enumerate(heads):
            v0 = 2 * A_WIDTH + head * HEAD_DIM
            o_loc = jnp.concatenate(
                [_dot(p_loc[hh, q_rows, keys], qkv_ref[keys, v0:v0 + HEAD_DIM])
                 for q_rows, keys in groups], axis=0)
            o = _dot_nt(p_ctx[hh], vc_ref[head].astype(BF16)) + o_loc
            o_ref[:, head * HEAD_DIM:(head + 1) * HEAD_DIM] = (
                o * jnp.concatenate(inv_l[hh], axis=0))

    _pool_rows(u_ref, wp_ref, ps_ref, o_ref, n=DEC_SEQ, out_col0=A_WIDTH)
    for hp in range(A_HEADS // 2):
        head_pair(hp)


def _lat_attn_a(qkv, u, w_pool, pool_scale, cache_k, cache_v, bias_tiles, i_layer, mix):
    row0 = N_CTX_TOK // DEC_SEQ
    cache_spec = pl.BlockSpec((None, None, A_HEADS, HEAD_DIM, PAST_LEN),
                              lambda b: (b, i_layer, 0, 0, 0))
    return pl.pallas_call(
        _lat_attn_a_kernel,
        out_shape=jax.ShapeDtypeStruct((N_TOK, D_MODEL), F32),
        grid=(DEC_BATCH,),
        in_specs=[
            pl.BlockSpec((DEC_SEQ, QKV_WIDTH), lambda b: (row0 + b, 0)),
            pl.BlockSpec((DEC_SEQ, B_WIDTH), lambda b: (row0 + b, 0)),
        ] + _pool_specs(i_layer) + [
            cache_spec,
            cache_spec,
            pl.BlockSpec((None, A_HEADS, N_BIAS_TILES, GRID_W, 2 * GRID_W),
                         lambda b: (i_layer, 0, 0, 0, 0)),
            pl.BlockSpec(memory_space=pl.ANY),
        ],
        out_specs=pl.BlockSpec((DEC_SEQ, D_MODEL), lambda b: (row0 + b, 0)),
        scratch_shapes=[pltpu.VMEM((2, DEC_SEQ, DEC_SEQ), F32),
                        pltpu.VMEM((2, DEC_SEQ, PAST_LEN), F32),
                        pltpu.VMEM((2, DEC_SEQ, DEC_SEQ), BF16),
                        pltpu.VMEM((2, DEC_SEQ, PAST_LEN), BF16)],
        input_output_aliases={7: 0},
        compiler_params=_params(1),
        name="lat_attn_a",
    )(qkv, u, w_pool, pool_scale.reshape(-1, 1, B_WIDTH), cache_k, cache_v, bias_tiles, mix)


def _pool_rows(u_ref, wp_ref, ps_ref, o_ref, *, n, out_col0):
    rows = n
    t = lax.broadcasted_iota(jnp.int32, (rows, B_GROUP_DIM), 0)

    def earlier(x, k):
        return jnp.where(t >= k, pltpu.roll(x, k, axis=0), 0.0)

    def later(x, k):
        return jnp.where(t < n - k, pltpu.roll(x, rows - k, axis=0), 0.0)

    tokens_before = t.astype(F32)
    tokens_from = (n - t).astype(F32)

    for g, w in enumerate(POOL_WINDOWS):
        c0 = g * B_GROUP_DIM
        half = w // 2
        u = u_ref[:, c0:c0 + B_GROUP_DIM]
        before, after = u, u
        k = 1
        while k < half:
            before = before + earlier(before, k)
            after = after + later(after, k)
            k *= 2
        total = earlier(before, 1) + after
        count = jnp.minimum(tokens_before, half) + jnp.minimum(tokens_from, half)
        pooled = total / count - u
        y = _dot(pooled.astype(BF16), wp_ref[g].astype(BF16))
        o_ref[:, out_col0 + c0:out_col0 + c0 + B_GROUP_DIM] = (
            y * ps_ref[:, c0:c0 + B_GROUP_DIM])


def _pool_specs(i_layer):
    zeros = (0,) * 3
    return [
        pl.BlockSpec((None, len(POOL_WINDOWS), B_GROUP_DIM, B_GROUP_DIM),
                     lambda *_: (i_layer,) + zeros),
        pl.BlockSpec((None, 1, B_WIDTH), lambda *_: (i_layer, 0, 0)),
    ]


def _sink_row(sink_ref, head0, queries_per_head):
    lane = lax.broadcasted_iota(jnp.int32, (1, C_GROUP * queries_per_head), 1)
    row = jnp.full((1, C_GROUP * queries_per_head), sink_ref[head0], F32)
    for g in range(1, C_GROUP):
        row = jnp.where(lane >= g * queries_per_head, sink_ref[head0 + g], row)
    return row


CTX_REQS_PER_STEP = 2


def _ctx_attn_c_kernel(sink_ref, p_ref, o_ref, s_ref, p_scr, *, sink0):
    for req in range(CTX_REQS_PER_STEP):
        rows = slice(req * SEQ, (req + 1) * SEQ)

        def scores(kk):
            q = jnp.concatenate(
                [p_ref[rows, h * HEAD_DIM:(h + 1) * HEAD_DIM]
                 for h in range(kk * C_GROUP, (kk + 1) * C_GROUP)], axis=0)
            k0 = C_Q_WIDTH + kk * HEAD_DIM
            return _dot_nt(p_ref[rows, k0:k0 + HEAD_DIM], q)

        v0 = C_Q_WIDTH + C_KV_WIDTH
        outs = _softmax_pv_t(
            [[functools.partial(scores, kk)] for kk in range(C_KV_HEADS)],
            [_sink_row(sink_ref, sink0 + kk * C_GROUP, SEQ) for kk in range(C_KV_HEADS)],
            [[p_ref[rows, v0 + kk * HEAD_DIM:v0 + (kk + 1) * HEAD_DIM]]
             for kk in range(C_KV_HEADS)],
            [s_ref], [p_scr])
        outs_t = [o_t[:, g * SEQ:(g + 1) * SEQ] for o_t in outs for g in range(C_GROUP)]
        o_ref[rows, :] = jnp.concatenate(outs_t, axis=0).T


def _ctx_attn_c(qkv, sink_all, j_layer):
    return pl.pallas_call(
        functools.partial(_ctx_attn_c_kernel, sink0=j_layer * C_HEADS),
        out_shape=jax.ShapeDtypeStruct((N_TOK, D_MODEL), F32),
        grid=(BATCH // CTX_REQS_PER_STEP,),
        in_specs=[
            pl.BlockSpec(memory_space=pltpu.SMEM),
            pl.BlockSpec((CTX_REQS_PER_STEP * SEQ, QKV_WIDTH), lambda b: (b, 0)),
        ],
        out_specs=pl.BlockSpec((CTX_REQS_PER_STEP * SEQ, C_Q_WIDTH), lambda b: (b, 0)),
        scratch_shapes=[pltpu.VMEM((C_KV_HEADS, SEQ, C_GROUP * SEQ), F32),
                        pltpu.VMEM((C_KV_HEADS, SEQ, C_GROUP * SEQ), BF16)],
        compiler_params=_params(1),
        name="ctx_attn_c",
    )(sink_all.reshape(-1), qkv)


def _lat_attn_c_kernel(sink_ref, q_ref, k_ref, v_ref, kc_ref, vc_ref, mix_ref, o_ref,
                       kctx_ref, vctx_ref, sc_ref, sb_ref, pc_ref, pb_ref, *, sink0):
    del mix_ref
    j = pl.program_id(1)
    n_blocks = DEC_SEQ // C_BLOCK

    @pl.when(j == 0)
    def _():
        for kk in range(C_KV_HEADS):
            c0 = kk * HEAD_DIM
            kctx_ref[:, c0:c0 + HEAD_DIM] = kc_ref[kk].T.astype(BF16)
            vctx_ref[:, c0:c0 + HEAD_DIM] = vc_ref[kk].T.astype(BF16)

    n_q = C_GROUP * C_BLOCK
    qi = lax.broadcasted_iota(jnp.int32, (C_BLOCK, n_q), 1) % C_BLOCK
    jl = lax.broadcasted_iota(jnp.int32, (C_BLOCK, n_q), 0)
    valid_prev = (jl >= qi) & (j > 0)
    valid_next = (jl <= qi) & (j < n_blocks - 1)
    rows_prev = pl.ds(pl.multiple_of(jnp.maximum(j - 1, 0) * C_BLOCK, C_BLOCK), C_BLOCK)
    rows_cur = pl.ds(pl.multiple_of(j * C_BLOCK, C_BLOCK), C_BLOCK)
    rows_next = pl.ds(pl.multiple_of(jnp.minimum(j + 1, n_blocks - 1) * C_BLOCK, C_BLOCK),
                      C_BLOCK)

    def stacked_q(kk):
        return jnp.concatenate(
            [q_ref[:, h * HEAD_DIM:(h + 1) * HEAD_DIM]
             for h in range(kk * C_GROUP, (kk + 1) * C_GROUP)], axis=0)

    def band(ref, kk):
        cols = slice(kk * HEAD_DIM, (kk + 1) * HEAD_DIM)
        return jnp.concatenate(
            [ref[rows_prev, cols], ref[rows_cur, cols], ref[rows_next, cols]], axis=0)

    def ctx_scores(kk):
        return _dot_nt(kctx_ref[:, kk * HEAD_DIM:(kk + 1) * HEAD_DIM], stacked_q(kk))

    def band_scores(kk):
        s_raw = _dot_nt(band(k_ref, kk), stacked_q(kk))
        return jnp.concatenate(
            [jnp.where(valid_prev, s_raw[:C_BLOCK], NEG_INF),
             s_raw[C_BLOCK:2 * C_BLOCK],
             jnp.where(valid_next, s_raw[2 * C_BLOCK:], NEG_INF)], axis=0)

    kv_heads = range(C_KV_HEADS)
    outs = _softmax_pv_t(
        [[functools.partial(ctx_scores, kk), functools.partial(band_scores, kk)]
         for kk in kv_heads],
        [_sink_row(sink_ref, sink0 + kk * C_GROUP, C_BLOCK) for kk in kv_heads],
        [[vctx_ref[:, kk * HEAD_DIM:(kk + 1) * HEAD_DIM], band(v_ref, kk)] for kk in kv_heads],
        [sc_ref, sb_ref], [pc_ref, pb_ref])
    outs_t = [o_t[:, g * C_BLOCK:(g + 1) * C_BLOCK] for o_t in outs for g in range(C_GROUP)]
    o_ref[...] = jnp.concatenate(outs_t, axis=0).T


def _lat_attn_c(qkv, cache_k, cache_v, sink, j_layer, mix):
    n_blocks = DEC_SEQ // C_BLOCK
    q_row0 = N_CTX_TOK // C_BLOCK
    kv_row0 = N_CTX_TOK // DEC_SEQ
    k_col = C_Q_WIDTH // C_KV_WIDTH
    n_q = C_GROUP * C_BLOCK
    return pl.pallas_call(
        functools.partial(_lat_attn_c_kernel, sink0=j_layer * C_HEADS),
        out_shape=jax.ShapeDtypeStruct((N_TOK, D_MODEL), F32),
        grid=(DEC_BATCH, n_blocks),
        in_specs=[
            pl.BlockSpec(memory_space=pltpu.SMEM),
            pl.BlockSpec((C_BLOCK, C_Q_WIDTH), lambda b, j: (q_row0 + b * n_blocks + j, 0)),
            pl.BlockSpec((DEC_SEQ, C_KV_WIDTH), lambda b, j: (kv_row0 + b, k_col)),
            pl.BlockSpec((DEC_SEQ, C_KV_WIDTH), lambda b, j: (kv_row0 + b, k_col + 1)),
            pl.BlockSpec((None, None, C_KV_HEADS, HEAD_DIM, PAST_LEN),
                         lambda b, j: (b, j_layer, 0, 0, 0)),
            pl.BlockSpec((None, None, C_KV_HEADS, HEAD_DIM, PAST_LEN),
                         lambda b, j: (b, j_layer, 0, 0, 0)),
            pl.BlockSpec(memory_space=pl.ANY),
        ],
        out_specs=pl.BlockSpec((C_BLOCK, C_Q_WIDTH),
                               lambda b, j: (q_row0 + b * n_blocks + j, 0)),
        input_output_aliases={6: 0},
        scratch_shapes=[pltpu.VMEM((PAST_LEN, C_KV_WIDTH), BF16),
                        pltpu.VMEM((PAST_LEN, C_KV_WIDTH), BF16),
                        pltpu.VMEM((C_KV_HEADS, PAST_LEN, n_q), F32),
                        pltpu.VMEM((C_KV_HEADS, 3 * C_BLOCK, n_q), F32),
                        pltpu.VMEM((C_KV_HEADS, PAST_LEN, n_q), BF16),
                        pltpu.VMEM((C_KV_HEADS, 3 * C_BLOCK, n_q), BF16)],
        compiler_params=_params(2),
        name="lat_attn_c",
    )(sink.reshape(-1), qkv, qkv, qkv, cache_k, cache_v, mix)


def _rope_tables():
    t = np.arange(DEC_SEQ)
    pos = np.stack([t // GRID_W, t % GRID_W], axis=-1).astype(np.float64)
    half = HEAD_DIM // 4
    inv = ROPE_BASE ** (-np.arange(half, dtype=np.float64) / half)
    ang = pos[:, :, None] * inv
    cos = np.cos(ang)
    sin = np.sin(ang)
    cos64 = np.stack([cos, cos], axis=2).reshape(DEC_SEQ, HEAD_DIM)
    sin64 = np.stack([-sin, sin], axis=2).reshape(DEC_SEQ, HEAD_DIM)
    reps = ROPE_TABLE_WIDTH // HEAD_DIM
    return (jnp.asarray(np.tile(cos64, (1, reps)), F32),
            jnp.asarray(np.tile(sin64, (1, reps)), F32))


FFN_CHUNK = 256
N_FFN_CHUNKS = FFN_HIDDEN // FFN_CHUNK
N_WO_PIECES = D_MODEL // FFN_CHUNK


def _post_mixer_kernel(*refs, layer, wo_idx, tm, final):
    x_ref, mix_ref, g_ref, mod_ref = refs[:4]
    n_in = 5 if final else 7
    wo_hbm, wgu_hbm, wd_hbm = refs[n_in:n_in + 3]
    n_out = 2
    out_refs = refs[n_in + 3:n_in + 3 + n_out]
    (wo_s, wg_s, wu_s, wd_s, act_s, stage_col, stage_row, sem_col,
     sem_row) = refs[n_in + 3 + n_out:]
    n_row_pieces = N_WO_PIECES + N_FFN_CHUNKS

    def col_copy(which, c):
        src = wgu_hbm.at[layer, :, pl.ds(which * FFN_HIDDEN + c * FFN_CHUNK, FFN_CHUNK)]
        return pltpu.make_async_copy(src, stage_col.at[which, c % 2], sem_col.at[which, c % 2])

    def row_copy(p):
        if p < N_WO_PIECES:
            src = wo_hbm.at[wo_idx, pl.ds(p * FFN_CHUNK, FFN_CHUNK), :]
        else:
            src = wd_hbm.at[layer, pl.ds((p - N_WO_PIECES) * FFN_CHUNK, FFN_CHUNK), :]
        return pltpu.make_async_copy(src, stage_row.at[p % 2], sem_row.at[p % 2])

    def take_row_piece(p, dst_ref, row0):
        if p + 1 < n_row_pieces:
            row_copy(p + 1).start()
        row_copy(p).wait()
        dst_ref[row0:row0 + FFN_CHUNK, :] = stage_row[p % 2].astype(BF16)

    def tile(load_weights):
        if load_weights:
            row_copy(0).start()
            for which in range(2):
                col_copy(which, 0).start()
            for p in range(N_WO_PIECES):
                take_row_piece(p, wo_s, p * FFN_CHUNK)
        if not final:
            out_refs[1][...] = _modulation_block(*refs[4:7])
        gate1 = mod_ref[:, 2 * D_MODEL:3 * D_MODEL]
        x1 = x_ref[...] + gate1 * _dot(mix_ref[...].astype(BF16), wo_s[...])
        h = _norm_mod(x1, g_ref[...], mod_ref, 3).astype(BF16)
        for c in range(N_FFN_CHUNKS):
            if load_weights:
                for which, dst in ((0, wg_s), (1, wu_s)):
                    if c + 1 < N_FFN_CHUNKS:
                        col_copy(which, c + 1).start()
                    col_copy(which, c).wait()
                    dst[c] = stage_col[which, c % 2].astype(BF16)
            gate = _dot(h, wg_s[c])
            up = _dot(h, wu_s[c])
            act = gate / (1.0 + jnp.exp(-gate)) * up
            act_s[:, c * FFN_CHUNK:(c + 1) * FFN_CHUNK] = act.astype(BF16)
            if load_weights:
                take_row_piece(N_WO_PIECES + c, wd_s, c * FFN_CHUNK)
        gate2 = mod_ref[:, 5 * D_MODEL:6 * D_MODEL]
        return x1 + gate2 * _dot(act_s[...], wd_s[...])

    def emit(x2):
        if not final:
            out_refs[0][...] = x2
            return
        var = jnp.mean(x2 * x2, axis=-1, keepdims=True)
        y = x2 * lax.rsqrt(var + EPS) * refs[4][...]
        is_ctx = pl.program_id(0) < N_CTX_TOK // tm

        @pl.when(is_ctx)
        def _():
            out_refs[0][...] = y

        @pl.when(jnp.logical_not(is_ctx))
        def _():
            out_refs[1][...] = y

    first = pl.program_id(0) == 0

    @pl.when(first)
    def _():
        emit(tile(True))

    @pl.when(jnp.logical_not(first))
    def _():
        emit(tile(False))


def _post_mixer(x, mix, g_all, mod_l, layer, w_out_all, wo_idx, w_gate_up, w_down,
                norm_final=None, next_mod_inputs=None):
    tm = 512
    n_tiles = N_TOK // tm
    n_ctx_tiles = N_CTX_TOK // tm
    final = norm_final is not None
    row_spec = pl.BlockSpec((tm, D_MODEL), lambda i: (i, 0))
    hbm = pl.BlockSpec(memory_space=pl.ANY)
    if final:
        extra_in = [norm_final.reshape(1, D_MODEL)]
        extra_specs = [pl.BlockSpec((1, D_MODEL), lambda i: (0, 0))]
        out_shape = (jax.ShapeDtypeStruct((N_CTX_TOK, D_MODEL), F32),
                     jax.ShapeDtypeStruct((N_LAT_TOK, D_MODEL), F32))
        out_specs = (
            pl.BlockSpec((tm, D_MODEL), lambda i: (jnp.minimum(i, n_ctx_tiles - 1), 0)),
            pl.BlockSpec((tm, D_MODEL), lambda i: (jnp.maximum(i - n_ctx_tiles, 0), 0)))
    else:
        tn = 6 * D_MODEL // n_tiles
        extra_in = list(next_mod_inputs)
        extra_specs = [
            pl.BlockSpec((N_GROUPS_PAD, D_MODEL), lambda i: (0, 0)),
            pl.BlockSpec((None, D_MODEL, tn), lambda i: (layer + 1, 0, i)),
            pl.BlockSpec((None, 1, tn), lambda i: (layer + 1, 0, i)),
        ]
        out_shape = (jax.ShapeDtypeStruct((N_TOK, D_MODEL), F32),
                     jax.ShapeDtypeStruct((N_GROUPS_PAD, 6 * D_MODEL), F32))
        out_specs = (row_spec, pl.BlockSpec((N_GROUPS_PAD, tn), lambda i: (0, i)))
    return pl.pallas_call(
        functools.partial(_post_mixer_kernel, layer=layer, wo_idx=wo_idx, tm=tm, final=final),
        out_shape=out_shape,
        grid=(N_TOK // tm,),
        in_specs=[
            row_spec,
            row_spec,
            pl.BlockSpec((None, 1, D_MODEL), lambda i: (layer, 0, 0)),
            pl.BlockSpec((None, 1, 6 * D_MODEL), lambda i: (_group_of_tile(i, tm), 0, 0)),
        ] + extra_specs + [hbm, hbm, hbm],
        out_specs=out_specs,
        scratch_shapes=[
            pltpu.VMEM((D_MODEL, D_MODEL), BF16),
            pltpu.VMEM((N_FFN_CHUNKS, D_MODEL, FFN_CHUNK), BF16),
            pltpu.VMEM((N_FFN_CHUNKS, D_MODEL, FFN_CHUNK), BF16),
            pltpu.VMEM((FFN_HIDDEN, D_MODEL), BF16),
            pltpu.VMEM((tm, FFN_HIDDEN), BF16),
            pltpu.VMEM((2, 2, D_MODEL, FFN_CHUNK), F32),
            pltpu.VMEM((2, FFN_CHUNK, D_MODEL), F32),
            pltpu.SemaphoreType.DMA((2, 2)),
            pltpu.SemaphoreType.DMA((2,)),
        ],
        compiler_params=_params(1),
        name="post_mixer",
    )(x, mix, g_all.reshape(DEPTH, 1, D_MODEL), mod_l, *extra_in, w_out_all, w_gate_up, w_down)


def kernel(x_prompt, x_sample, cache_a_k, cache_a_v, cache_c_k, cache_c_v, c, c_ctx, w_mod, b_mod, norm_mix, norm_ffn, w_in_ab, rpb_a, w_pool, pool_scale, w_out_ab, w_in_c, sink_c, w_out_c, w_gate_up, w_down, norm_final):
    xs = (x_prompt.reshape(N_CTX_TOK, D_MODEL), x_sample.reshape(N_LAT_TOK, D_MODEL))
    cond8 = jnp.concatenate(
        [c_ctx[None], c, jnp.zeros((N_GROUPS_PAD - 1 - DEC_BATCH, D_MODEL), F32)], axis=0)
    b_mod3 = b_mod.reshape(DEPTH, 1, 6 * D_MODEL)
    mod, bias_tiles = _modulation0_and_bias(cond8, w_mod, b_mod3, rpb_a)

    n_ab = cache_a_k.shape[1]
    n_c = cache_c_k.shape[1]
    cache_a_k, cache_a_v, cache_c_k, cache_c_v = (
        jnp.transpose(t, (0, 1, 3, 4, 2)) for t in (cache_a_k, cache_a_v, cache_c_k, cache_c_v))
    rope_tables = _rope_tables()

    new_a = []
    new_c = []
    for l in range(DEPTH):
        mod_l = mod.reshape(N_GROUPS_PAD, 1, 6 * D_MODEL)
        if l % 2 == 0:
            i = l // 2
            qkv, *rest = _inproj(xs, norm_mix, mod_l, l, w_in_ab, i, A_HEADS, A_WIDTH, n_ab,
                                 new_a)
            new_a, u = rest[:2], rest[2]
            if len(xs) == 2:
                xs = (rest[3],)
            mix = _ctx_attn_a(qkv, u, w_pool, pool_scale, i)
            mix = _lat_attn_a(qkv, u, w_pool, pool_scale, cache_a_k, cache_a_v, bias_tiles, i,
                              mix)
            w_out, wo_idx = w_out_ab, i
        else:
            j = l // 2
            qkv, *new_c = _inproj(xs, norm_mix, mod_l, l, w_in_c, j, C_KV_HEADS, C_Q_WIDTH, n_c,
                                  new_c, rope_tables)
            mix = _ctx_attn_c(qkv, sink_c, j)
            mix = _lat_attn_c(qkv, cache_c_k, cache_c_v, sink_c, j, mix)
            w_out, wo_idx = w_out_c, j
        if l + 1 < DEPTH:
            x, mod = _post_mixer(xs[0], mix, norm_ffn, mod_l, l, w_out, wo_idx, w_gate_up,
                                 w_down, next_mod_inputs=(cond8, w_mod, b_mod3))
            xs = (x,)
        else:
            y_ctx, y_lat = _post_mixer(xs[0], mix, norm_ffn, mod_l, l, w_out, wo_idx, w_gate_up,
                                       w_down, norm_final)

    new_caches = [jnp.transpose(t, (0, 1, 4, 2, 3)) for t in (*new_a, *new_c)]
    return (y_ctx.reshape(BATCH, SEQ, D_MODEL), y_lat.reshape(DEC_BATCH, DEC_SEQ, D_MODEL),
            *new_caches)
```

```python
import functools

import jax
import jax.numpy as jnp
import numpy as np
from jax import lax
from jax.experimental import pallas as pl
from jax.experimental.pallas import tpu as pltpu

D_MODEL = 1024
BATCH = 16
SEQ = 256
DEPTH = 4
DEC_BATCH = 2
DEC_SEQ = 1024
PAST_LEN = 512
GRID_W = 64
HEAD_DIM = 64
A_WIDTH = 512
A_HEADS = 8
B_WIDTH = 512
POOL_WINDOWS = (2, 4, 8, 16)
B_GROUP_DIM = 128
NA_ROWS = 8
NA_COLS = 16
C_HEADS = 16
C_KV_HEADS = 4
C_GROUP = C_HEADS // C_KV_HEADS
C_Q_WIDTH = 1024
C_KV_WIDTH = 256
C_BLOCK = 128
FFN_HIDDEN = 2816
ROPE_BASE = 10000.0
EPS = 1e-6
NEG_INF = -1e30

N_CTX_TOK = BATCH * SEQ
N_LAT_TOK = DEC_BATCH * DEC_SEQ
N_TOK = N_CTX_TOK + N_LAT_TOK
GRID_ROWS = DEC_SEQ // GRID_W
N_GROUPS_PAD = 8

VMEM_LIMIT = 56 * 1024 * 1024

F32 = jnp.float32
BF16 = jnp.bfloat16


def _params(n_axes):
    return pltpu.CompilerParams(dimension_semantics=("arbitrary",) * n_axes,
                                vmem_limit_bytes=VMEM_LIMIT)


def _group_of_tile(i, tm):
    row0 = i * tm
    return jnp.where(row0 < N_CTX_TOK, 0, 1 + (row0 - N_CTX_TOK) // DEC_SEQ)


def _dot_nt(a, b):
    return lax.dot_general(a, b, (((1,), (1,)), ((), ())), preferred_element_type=F32)


def _dot(a, b):
    return jnp.dot(a, b, preferred_element_type=F32)


def _dot_tn(a, b):
    return lax.dot_general(a, b, (((0,), (0,)), ((), ())), preferred_element_type=F32)


def _softmax_pv_t(score_fns, sink_rows, values, s_refs, p_refs):
    n_groups, n_blocks = len(score_fns), len(s_refs)
    for g in range(n_groups):
        for b in range(n_blocks):
            s_refs[b][g] = score_fns[g][b]()
    inv_l = []
    for g in range(n_groups):
        m = sink_rows[g]
        for b in range(n_blocks):
            m = jnp.maximum(m, jnp.max(s_refs[b][g], axis=0, keepdims=True))
        l = jnp.exp2(sink_rows[g] - m)
        for b in range(n_blocks):
            p = jnp.exp2(s_refs[b][g] - m)
            l = l + jnp.sum(p, axis=0, keepdims=True)
            p_refs[b][g] = p.astype(BF16)
        inv_l.append(1.0 / l)
    outs = []
    for g in range(n_groups):
        o = None
        for b in range(n_blocks):
            pv = _dot_tn(values[g][b], p_refs[b][g])
            o = pv if o is None else o + pv
        outs.append(o * inv_l[g])
    return outs


def _modulation_block(cond_ref, w_ref, b_ref):
    c = cond_ref[...]
    s = c / (1.0 + jnp.exp(-c))
    return _dot(s.astype(BF16), w_ref[...].astype(BF16)) + b_ref[...]


def _mod_bias_kernel(rpb_ref, cond_ref, w_ref, b_ref, o_ref, bias_ref, *, heads_per_step):
    o_ref[...] = _modulation_block(cond_ref, w_ref, b_ref)
    pair0 = pl.program_id(0) * heads_per_step
    for t in range(heads_per_step):
        _write_bias_tiles(rpb_ref, (pair0 + t) * (N_DROW * N_DCOL), bias_ref.at[t])


def _modulation0_and_bias(cond8, w_mod, b_mod3, rpb_a):
    n_col_blocks = 4
    tn = 6 * D_MODEL // n_col_blocks
    n_bias_layers = rpb_a.shape[0]
    heads_per_step = n_bias_layers * A_HEADS // n_col_blocks
    steps_per_layer = A_HEADS // heads_per_step
    return pl.pallas_call(
        functools.partial(_mod_bias_kernel, heads_per_step=heads_per_step),
        out_shape=(
            jax.ShapeDtypeStruct((N_GROUPS_PAD, 6 * D_MODEL), F32),
            jax.ShapeDtypeStruct(
                (n_bias_layers, A_HEADS, N_BIAS_TILES, GRID_W, 2 * GRID_W), F32)),
        grid=(n_col_blocks,),
        in_specs=[
            pl.BlockSpec(memory_space=pltpu.SMEM),
            pl.BlockSpec((N_GROUPS_PAD, D_MODEL), lambda j: (0, 0)),
            pl.BlockSpec((None, D_MODEL, tn), lambda j: (0, 0, j)),
            pl.BlockSpec((None, 1, tn), lambda j: (0, 0, j)),
        ],
        out_specs=(
            pl.BlockSpec((N_GROUPS_PAD, tn), lambda j: (0, j)),
            pl.BlockSpec((None, heads_per_step, N_BIAS_TILES, GRID_W, 2 * GRID_W),
                         lambda j: (j // steps_per_layer, j % steps_per_layer, 0, 0, 0))),
        compiler_params=_params(1),
        name="modulation_bias",
    )(rpb_a.reshape(-1), cond8, w_mod, b_mod3)


def _norm_mod(x, g, mod_ref, shift_idx):
    var = jnp.mean(x * x, axis=-1, keepdims=True)
    y = x * lax.rsqrt(var + EPS) * g
    shift = mod_ref[:, shift_idx * D_MODEL:(shift_idx + 1) * D_MODEL]
    scale = mod_ref[:, (shift_idx + 1) * D_MODEL:(shift_idx + 2) * D_MODEL]
    return y * (1.0 + scale) + shift


QKV_WIDTH = 3 * A_WIDTH
LOG2_E = 1.4426950408889634
Q_SCALE = HEAD_DIM ** -0.5 * LOG2_E
PROJ_CHUNK = 4 * HEAD_DIM
ROPE_TABLE_WIDTH = PROJ_CHUNK


def _rope(x, cos, sin_signed):
    n = x.shape[-1]
    lane = lax.broadcasted_iota(jnp.int32, x.shape, x.ndim - 1)
    first = (lane % 32) < 16
    partner = jnp.where(first, pltpu.roll(x, n - 16, axis=x.ndim - 1),
                        pltpu.roll(x, 16, axis=x.ndim - 1))
    return x * cos + partner * sin_signed


def _inproj_kernel(*refs, tm, n_heads, q_width, n_prev, split_x, rope, has_u):
    n_x = 2 if split_x else 1
    g_ref, mod_ref, w_ref = refs[n_x:n_x + 3]
    n_in = n_x + 3 + (2 if rope else 0) + n_prev
    qkv_ref, ck_ref, cv_ref = refs[n_in:n_in + 3]
    extra_out = refs[n_in + 3:-3]
    wbf_ref, h_ref, res_ref = refs[-3:]
    i = pl.program_id(0)
    is_ctx = i < N_CTX_TOK // tm
    kv_width = (QKV_WIDTH - q_width) // 2
    k_col, v_col = q_width, q_width + kv_width
    n_out = res_ref.shape[1]
    chunk = PROJ_CHUNK

    @pl.when(i == 0)
    def _():
        wbf_ref[...] = w_ref[...].astype(BF16)

    def tile(ctx):
        if split_x:
            x = (refs[0] if ctx else refs[1])[...]
            extra_out[-1][...] = x
        else:
            x = refs[0][...]
        h_ref[...] = _norm_mod(x, g_ref[...], mod_ref, 0).astype(BF16)

        def matmul(c0):
            cols = slice(c0, c0 + chunk)
            res_ref[:, cols] = _dot(h_ref[...], wbf_ref[:, cols])

        def epilogue(c0):
            cols = slice(c0, c0 + chunk)
            if c0 >= QKV_WIDTH:
                extra_out[0][:, c0 - QKV_WIDTH:c0 - QKV_WIDTH + chunk] = res_ref[:, cols]
                return
            r = res_ref[:, cols]
            if rope and not ctx and c0 < v_col:
                cos_ref, sin_ref = refs[n_x + 3:n_x + 5]
                r = _rope(r, cos_ref[...], sin_ref[...])
            if c0 < q_width:
                r = r * Q_SCALE
            qkv_ref[:, cols] = r.astype(BF16)
            if ctx and c0 >= k_col:
                c_ref, col0 = (ck_ref, k_col) if c0 < v_col else (cv_ref, v_col)
                for req in range(tm // SEQ):
                    for hc in range(chunk // HEAD_DIM):
                        hd = (c0 - col0) // HEAD_DIM + hc
                        c_ref[req, hd] = res_ref[
                            req * SEQ:(req + 1) * SEQ,
                            c0 + hc * HEAD_DIM:c0 + (hc + 1) * HEAD_DIM].T

        for c0 in range(0, n_out, chunk):
            matmul(c0)
            epilogue(c0)

    pl.when(is_ctx)(lambda: tile(True))
    pl.when(jnp.logical_not(is_ctx))(lambda: tile(False))


def _inproj(xs, g_all, mod_l, layer, w_all, w_idx, n_heads, q_width, n_slots, prev_caches,
            rope_tables=None):
    split_x = len(xs) == 2
    tm = 512 if split_x else 1024
    n_out = w_all.shape[2]
    n_ctx_tiles = N_CTX_TOK // tm
    rope = rope_tables is not None
    has_u = n_out > QKV_WIDTH
    cache_shape = jax.ShapeDtypeStruct((BATCH, n_slots, n_heads, HEAD_DIM, SEQ), F32)
    cache_spec = pl.BlockSpec(
        (tm // SEQ, None, n_heads, HEAD_DIM, SEQ),
        lambda i: (jnp.minimum(i, n_ctx_tiles - 1), w_idx, 0, 0, 0))
    row_spec = pl.BlockSpec((tm, D_MODEL), lambda i: (i, 0))
    if split_x:
        x_specs = [
            pl.BlockSpec((tm, D_MODEL), lambda i: (jnp.minimum(i, n_ctx_tiles - 1), 0)),
            pl.BlockSpec((tm, D_MODEL), lambda i: (jnp.maximum(i - n_ctx_tiles, 0), 0)),
        ]
    else:
        x_specs = [row_spec]
    out_shape = [jax.ShapeDtypeStruct((N_TOK, QKV_WIDTH), BF16), cache_shape, cache_shape]
    out_specs = [pl.BlockSpec((tm, QKV_WIDTH), lambda i: (i, 0)), cache_spec, cache_spec]
    if has_u:
        out_shape.append(jax.ShapeDtypeStruct((N_TOK, n_out - QKV_WIDTH), F32))
        out_specs.append(pl.BlockSpec((tm, n_out - QKV_WIDTH), lambda i: (i, 0)))
    if split_x:
        out_shape.append(jax.ShapeDtypeStruct((N_TOK, D_MODEL), F32))
        out_specs.append(row_spec)
    rope_in, rope_specs = [], []
    if rope:
        tiles_per_seq = DEC_SEQ // tm
        rope_spec = pl.BlockSpec(
            (tm, ROPE_TABLE_WIDTH),
            lambda i: (jnp.maximum(i - n_ctx_tiles, 0) % tiles_per_seq, 0))
        rope_in, rope_specs = list(rope_tables), [rope_spec, rope_spec]
    n_prev = len(prev_caches)
    n_before = len(xs) + 3 + len(rope_in)
    return pl.pallas_call(
        functools.partial(_inproj_kernel, tm=tm, n_heads=n_heads, q_width=q_width,
                          n_prev=n_prev, split_x=split_x, rope=rope, has_u=has_u),
        out_shape=out_shape,
        grid=(N_TOK // tm,),
        in_specs=x_specs + [
            pl.BlockSpec((None, 1, D_MODEL), lambda i: (layer, 0, 0)),
            pl.BlockSpec((None, 1, 6 * D_MODEL), lambda i: (_group_of_tile(i, tm), 0, 0)),
            pl.BlockSpec((None, D_MODEL, n_out), lambda i: (w_idx, 0, 0),
                         pipeline_mode=pl.Buffered(1)),
        ] + rope_specs + [pl.BlockSpec(memory_space=pl.ANY)] * n_prev,
        out_specs=out_specs,
        scratch_shapes=[pltpu.VMEM((D_MODEL, n_out), BF16), pltpu.VMEM((tm, D_MODEL), BF16),
                        pltpu.VMEM((tm, n_out), F32)],
        input_output_aliases={n_before + k: 1 + k for k in range(n_prev)},
        compiler_params=_params(1),
        name="inproj",
    )(*xs, g_all.reshape(DEPTH, 1, D_MODEL), mod_l, w_all, *rope_in, *prev_caches)


def _ctx_attn_a_kernel(p_ref, u_ref, wp_ref, ps_ref, o_ref, s_ref, p_scr):
    for h in range(A_HEADS):
        c0 = h * HEAD_DIM
        s_ref[h] = _dot_nt(p_ref[:, c0:c0 + HEAD_DIM],
                           p_ref[:, A_WIDTH + c0:A_WIDTH + c0 + HEAD_DIM])
    _pool_rows(u_ref, wp_ref, ps_ref, o_ref, n=SEQ, out_col0=A_WIDTH)
    inv_l = []
    for h in range(A_HEADS):
        s = s_ref[h]
        p = jnp.exp2(s - jnp.max(s, axis=-1, keepdims=True))
        inv_l.append(1.0 / jnp.sum(p, axis=-1, keepdims=True))
        p_scr[h] = p.astype(BF16)
    for h in range(A_HEADS):
        c0 = h * HEAD_DIM
        v = p_ref[:, 2 * A_WIDTH + c0:2 * A_WIDTH + c0 + HEAD_DIM]
        o_ref[:, c0:c0 + HEAD_DIM] = _dot(p_scr[h], v) * inv_l[h]


def _ctx_attn_a(qkv, u, w_pool, pool_scale, i_layer):
    return pl.pallas_call(
        _ctx_attn_a_kernel,
        out_shape=jax.ShapeDtypeStruct((N_TOK, D_MODEL), F32),
        grid=(BATCH,),
        in_specs=[pl.BlockSpec((SEQ, QKV_WIDTH), lambda b: (b, 0)),
                  pl.BlockSpec((SEQ, B_WIDTH), lambda b: (b, 0))] + _pool_specs(i_layer),
        out_specs=pl.BlockSpec((SEQ, D_MODEL), lambda b: (b, 0)),
        scratch_shapes=[pltpu.VMEM((A_HEADS, SEQ, SEQ), F32),
                        pltpu.VMEM((A_HEADS, SEQ, SEQ), BF16)],
        compiler_params=_params(1),
        name="ctx_attn_a",
    )(qkv, u, w_pool, pool_scale.reshape(-1, 1, B_WIDTH))


N_DROW = 2 * NA_ROWS - 1
N_DCOL = 2 * NA_COLS - 1
N_BIAS_TILES = 16
BIAS_TILE_LEFT_PAD = 14
BIAS_TILE_RIGHT_PAD = 15
MID_DROW = NA_ROWS - 1 - NA_ROWS // 2


def _write_bias_tiles(rpb_ref, base, o_ref):
    qi = lax.broadcasted_iota(jnp.int32, (GRID_W, 2 * GRID_W), 0)
    lane = lax.broadcasted_iota(jnp.int32, (GRID_W, 2 * GRID_W), 1)
    right = lane >= GRID_W
    kc = jnp.where(right, lane - GRID_W, lane)
    qstart = jnp.clip(qi - NA_COLS // 2, 0, GRID_W - NA_COLS)
    valid = (kc >= qstart) & (kc < qstart + NA_COLS)

    offs = lax.broadcasted_iota(jnp.int32, (1, 2 * GRID_W), 1) & (GRID_W - 1)
    shift = 2 * GRID_W - (NA_COLS - 1)
    rows = []
    for dr in range(N_DROW):
        vec = jnp.zeros((1, 2 * GRID_W), F32)
        for d in range(N_DCOL):
            vec = jnp.where(offs == d, rpb_ref[base + dr * N_DCOL + d] * LOG2_E, vec)
        rows.append(pltpu.roll(jnp.broadcast_to(vec, (GRID_W, 2 * GRID_W)), shift, axis=1,
                               stride=1, stride_axis=0))

    for t in range(N_DROW - 1):
        o_ref[t] = jnp.where(valid, jnp.where(right, rows[t + 1], rows[t]), NEG_INF)
    o_ref[BIAS_TILE_LEFT_PAD] = jnp.where(valid & right, rows[MID_DROW], NEG_INF)
    o_ref[BIAS_TILE_RIGHT_PAD] = jnp.where(valid & jnp.logical_not(right),
                                           rows[MID_DROW + NA_ROWS - 1], NEG_INF)


NA_GROUP_ROWS = 4


def _na_window(r):
    start = min(max(r - NA_ROWS // 2, 0), GRID_ROWS - NA_ROWS)
    first_drow = start - r + NA_ROWS - 1
    if start % 2 == 0:
        return start, [first_drow + 2 * p for p in range(NA_ROWS // 2)]
    assert first_drow == MID_DROW
    inner = [first_drow + 1 + 2 * p for p in range(NA_ROWS // 2 - 1)]
    return start - 1, [BIAS_TILE_LEFT_PAD] + inner + [BIAS_TILE_RIGHT_PAD]


def _lat_attn_a_kernel(q_ref, k_ref, v_ref, kc_ref, vc_ref, bias_ref, mix_ref, o_ref,
                       s_loc, s_ctx, p_loc, p_ctx):
    del mix_ref
    hp = pl.program_id(1)

    @pl.when((pl.program_id(0) == 0) & (hp == 0))
    def _():
        p_loc[...] = jnp.zeros_like(p_loc)

    groups = []
    for g0 in range(0, GRID_ROWS, NA_GROUP_ROWS):
        windows = [_na_window(r) for r in range(g0, g0 + NA_GROUP_ROWS)]
        lo = min(first for first, _ in windows)
        hi = max(first + 2 * len(tiles) for first, tiles in windows)
        groups.append((slice(g0 * GRID_W, (g0 + NA_GROUP_ROWS) * GRID_W),
                       slice(lo * GRID_W, hi * GRID_W)))

    for hh in range(2):
        c0 = hh * HEAD_DIM
        for q_rows, keys in groups:
            s_loc[hh, q_rows, keys] = _dot_nt(q_ref[q_rows, c0:c0 + HEAD_DIM],
                                              k_ref[keys, c0:c0 + HEAD_DIM])
        s_ctx[hh] = _dot(q_ref[:, c0:c0 + HEAD_DIM],
                         kc_ref[hh].astype(BF16))
    inv_l = [[], []]
    for hh in range(2):
        for r in range(GRID_ROWS):
            rows = slice(r * GRID_W, (r + 1) * GRID_W)
            first_row, tiles = _na_window(r)
            cols = slice(first_row * GRID_W, (first_row + 2 * len(tiles)) * GRID_W)
            bias = jnp.concatenate([bias_ref[hh, t] for t in tiles], axis=1)
            sc = s_ctx[hh, rows, :]
            sl = s_loc[hh, rows, cols] + bias
            m = jnp.maximum(jnp.max(sc, axis=-1, keepdims=True),
                            jnp.max(sl, axis=-1, keepdims=True))
            pc = jnp.exp2(sc - m)
            pw = jnp.exp2(sl - m)
            inv_l[hh].append(1.0 / (jnp.sum(pc, axis=-1, keepdims=True)
                                    + jnp.sum(pw, axis=-1, keepdims=True)))
            p_ctx[hh, rows, :] = pc.astype(BF16)
            p_loc[hh, rows, cols] = pw.astype(BF16)
    for hh in range(2):
        c0 = hh * HEAD_DIM
        o_loc = jnp.concatenate(
            [_dot(p_loc[hh, q_rows, keys], v_ref[keys, c0:c0 + HEAD_DIM])
             for q_rows, keys in groups], axis=0)
        o = _dot_nt(p_ctx[hh], vc_ref[hh].astype(BF16)) + o_loc
        o_ref[:, c0:c0 + HEAD_DIM] = o * jnp.concatenate(inv_l[hh], axis=0)


def _lat_attn_a(qkv, cache_k, cache_v, bias_tiles, i_layer, mix):
    row0 = N_CTX_TOK // DEC_SEQ
    pair = 2 * HEAD_DIM
    k_col0 = A_WIDTH // pair
    v_col0 = 2 * A_WIDTH // pair
    return pl.pallas_call(
        _lat_attn_a_kernel,
        out_shape=jax.ShapeDtypeStruct((N_TOK, D_MODEL), F32),
        grid=(DEC_BATCH, A_HEADS // 2),
        in_specs=[
            pl.BlockSpec((DEC_SEQ, pair), lambda b, hp: (row0 + b, hp)),
            pl.BlockSpec((DEC_SEQ, pair), lambda b, hp: (row0 + b, k_col0 + hp)),
            pl.BlockSpec((DEC_SEQ, pair), lambda b, hp: (row0 + b, v_col0 + hp)),
            pl.BlockSpec((None, None, 2, HEAD_DIM, PAST_LEN),
                         lambda b, hp: (b, i_layer, hp, 0, 0)),
            pl.BlockSpec((None, None, 2, HEAD_DIM, PAST_LEN),
                         lambda b, hp: (b, i_layer, hp, 0, 0)),
            pl.BlockSpec((None, 2, N_BIAS_TILES, GRID_W, 2 * GRID_W),
                         lambda b, hp: (i_layer, hp, 0, 0, 0)),
            pl.BlockSpec(memory_space=pl.ANY),
        ],
        out_specs=pl.BlockSpec((DEC_SEQ, pair), lambda b, hp: (row0 + b, hp)),
        scratch_shapes=[pltpu.VMEM((2, DEC_SEQ, DEC_SEQ), F32),
                        pltpu.VMEM((2, DEC_SEQ, PAST_LEN), F32),
                        pltpu.VMEM((2, DEC_SEQ, DEC_SEQ), BF16),
                        pltpu.VMEM((2, DEC_SEQ, PAST_LEN), BF16)],
        input_output_aliases={6: 0},
        compiler_params=_params(2),
        name="lat_attn_a",
    )(qkv, qkv, qkv, cache_k, cache_v, bias_tiles, mix)


def _pool_rows(u_ref, wp_ref, ps_ref, o_ref, *, n, out_col0):
    rows = n
    t = lax.broadcasted_iota(jnp.int32, (rows, B_GROUP_DIM), 0)

    def earlier(x, k):
        return jnp.where(t >= k, pltpu.roll(x, k, axis=0), 0.0)

    def later(x, k):
        return jnp.where(t < n - k, pltpu.roll(x, rows - k, axis=0), 0.0)

    tokens_before = t.astype(F32)
    tokens_from = (n - t).astype(F32)

    for g, w in enumerate(POOL_WINDOWS):
        c0 = g * B_GROUP_DIM
        half = w // 2
        u = u_ref[:, c0:c0 + B_GROUP_DIM]
        before, after = u, u
        k = 1
        while k < half:
            before = before + earlier(before, k)
            after = after + later(after, k)
            k *= 2
        total = earlier(before, 1) + after
        count = jnp.minimum(tokens_before, half) + jnp.minimum(tokens_from, half)
        pooled = total / count - u
        y = _dot(pooled.astype(BF16), wp_ref[g].astype(BF16))
        o_ref[:, out_col0 + c0:out_col0 + c0 + B_GROUP_DIM] = (
            y * ps_ref[:, c0:c0 + B_GROUP_DIM])


def _pool_specs(i_layer):
    zeros = (0,) * 3
    return [
        pl.BlockSpec((None, len(POOL_WINDOWS), B_GROUP_DIM, B_GROUP_DIM),
                     lambda *_: (i_layer,) + zeros),
        pl.BlockSpec((None, 1, B_WIDTH), lambda *_: (i_layer, 0, 0)),
    ]


def _lat_pool_kernel(u_ref, wp_ref, ps_ref, mix_ref, o_ref):
    del mix_ref
    _pool_rows(u_ref, wp_ref, ps_ref, o_ref, n=DEC_SEQ, out_col0=0)


def _lat_pool(u, w_pool, pool_scale, i_layer, mix):
    row0 = N_CTX_TOK // DEC_SEQ
    return pl.pallas_call(
        _lat_pool_kernel,
        out_shape=jax.ShapeDtypeStruct((N_TOK, D_MODEL), F32),
        grid=(DEC_BATCH,),
        in_specs=[pl.BlockSpec((DEC_SEQ, B_WIDTH), lambda b: (row0 + b, 0))]
        + _pool_specs(i_layer) + [pl.BlockSpec(memory_space=pl.ANY)],
        out_specs=pl.BlockSpec((DEC_SEQ, B_WIDTH), lambda b: (row0 + b, 1)),
        input_output_aliases={3: 0},
        compiler_params=_params(1),
        name="pool_mixer",
    )(u, w_pool, pool_scale.reshape(-1, 1, B_WIDTH), mix)


def _sink_row(sink_ref, head0, queries_per_head):
    lane = lax.broadcasted_iota(jnp.int32, (1, C_GROUP * queries_per_head), 1)
    row = jnp.full((1, C_GROUP * queries_per_head), sink_ref[head0], F32)
    for g in range(1, C_GROUP):
        row = jnp.where(lane >= g * queries_per_head, sink_ref[head0 + g], row)
    return row * LOG2_E


CTX_REQS_PER_STEP = 4


def _ctx_attn_c_kernel(sink_ref, p_ref, o_ref, s_ref, p_scr, *, sink0):
    for req in range(CTX_REQS_PER_STEP):
        rows = slice(req * SEQ, (req + 1) * SEQ)

        def scores(kk):
            q = jnp.concatenate(
                [p_ref[rows, h * HEAD_DIM:(h + 1) * HEAD_DIM]
                 for h in range(kk * C_GROUP, (kk + 1) * C_GROUP)], axis=0)
            k0 = C_Q_WIDTH + kk * HEAD_DIM
            return _dot_nt(p_ref[rows, k0:k0 + HEAD_DIM], q)

        v0 = C_Q_WIDTH + C_KV_WIDTH
        outs = _softmax_pv_t(
            [[functools.partial(scores, kk)] for kk in range(C_KV_HEADS)],
            [_sink_row(sink_ref, sink0 + kk * C_GROUP, SEQ) for kk in range(C_KV_HEADS)],
            [[p_ref[rows, v0 + kk * HEAD_DIM:v0 + (kk + 1) * HEAD_DIM]]
             for kk in range(C_KV_HEADS)],
            [s_ref], [p_scr])
        outs_t = [o_t[:, g * SEQ:(g + 1) * SEQ] for o_t in outs for g in range(C_GROUP)]
        o_ref[rows, :] = jnp.concatenate(outs_t, axis=0).T


def _ctx_attn_c(qkv, sink_all, j_layer):
    return pl.pallas_call(
        functools.partial(_ctx_attn_c_kernel, sink0=j_layer * C_HEADS),
        out_shape=jax.ShapeDtypeStruct((N_TOK, D_MODEL), F32),
        grid=(BATCH // CTX_REQS_PER_STEP,),
        in_specs=[
            pl.BlockSpec(memory_space=pltpu.SMEM),
            pl.BlockSpec((CTX_REQS_PER_STEP * SEQ, QKV_WIDTH), lambda b: (b, 0)),
        ],
        out_specs=pl.BlockSpec((CTX_REQS_PER_STEP * SEQ, C_Q_WIDTH), lambda b: (b, 0)),
        scratch_shapes=[pltpu.VMEM((C_KV_HEADS, SEQ, C_GROUP * SEQ), F32),
                        pltpu.VMEM((C_KV_HEADS, SEQ, C_GROUP * SEQ), BF16)],
        compiler_params=_params(1),
        name="ctx_attn_c",
    )(sink_all.reshape(-1), qkv)


def _lat_attn_c_kernel(sink_ref, q_ref, k_ref, v_ref, kc_ref, vc_ref, mix_ref, o_ref,
                       kctx_ref, vctx_ref, sc_ref, sb_ref, pc_ref, pb_ref, *, sink0):
    del mix_ref
    j = pl.program_id(1)
    n_blocks = DEC_SEQ // C_BLOCK

    @pl.when(j == 0)
    def _():
        for kk in range(C_KV_HEADS):
            c0 = kk * HEAD_DIM
            kctx_ref[:, c0:c0 + HEAD_DIM] = kc_ref[kk].T.astype(BF16)
            vctx_ref[:, c0:c0 + HEAD_DIM] = vc_ref[kk].T.astype(BF16)

    n_q = C_GROUP * C_BLOCK
    qi = lax.broadcasted_iota(jnp.int32, (C_BLOCK, n_q), 1) % C_BLOCK
    jl = lax.broadcasted_iota(jnp.int32, (C_BLOCK, n_q), 0)
    valid_prev = (jl >= qi) & (j > 0)
    valid_next = (jl <= qi) & (j < n_blocks - 1)
    rows_prev = pl.ds(pl.multiple_of(jnp.maximum(j - 1, 0) * C_BLOCK, C_BLOCK), C_BLOCK)
    rows_cur = pl.ds(pl.multiple_of(j * C_BLOCK, C_BLOCK), C_BLOCK)
    rows_next = pl.ds(pl.multiple_of(jnp.minimum(j + 1, n_blocks - 1) * C_BLOCK, C_BLOCK),
                      C_BLOCK)

    def stacked_q(kk):
        return jnp.concatenate(
            [q_ref[:, h * HEAD_DIM:(h + 1) * HEAD_DIM]
             for h in range(kk * C_GROUP, (kk + 1) * C_GROUP)], axis=0)

    def band(ref, kk):
        cols = slice(kk * HEAD_DIM, (kk + 1) * HEAD_DIM)
        return jnp.concatenate(
            [ref[rows_prev, cols], ref[rows_cur, cols], ref[rows_next, cols]], axis=0)

    def ctx_scores(kk):
        return _dot_nt(kctx_ref[:, kk * HEAD_DIM:(kk + 1) * HEAD_DIM], stacked_q(kk))

    def band_scores(kk):
        s_raw = _dot_nt(band(k_ref, kk), stacked_q(kk))
        return jnp.concatenate(
            [jnp.where(valid_prev, s_raw[:C_BLOCK], NEG_INF),
             s_raw[C_BLOCK:2 * C_BLOCK],
             jnp.where(valid_next, s_raw[2 * C_BLOCK:], NEG_INF)], axis=0)

    kv_heads = range(C_KV_HEADS)
    outs = _softmax_pv_t(
        [[functools.partial(ctx_scores, kk), functools.partial(band_scores, kk)]
         for kk in kv_heads],
        [_sink_row(sink_ref, sink0 + kk * C_GROUP, C_BLOCK) for kk in kv_heads],
        [[vctx_ref[:, kk * HEAD_DIM:(kk + 1) * HEAD_DIM], band(v_ref, kk)] for kk in kv_heads],
        [sc_ref, sb_ref], [pc_ref, pb_ref])
    outs_t = [o_t[:, g * C_BLOCK:(g + 1) * C_BLOCK] for o_t in outs for g in range(C_GROUP)]
    o_ref[...] = jnp.concatenate(outs_t, axis=0).T


def _lat_attn_c(qkv, cache_k, cache_v, sink, j_layer, mix):
    n_blocks = DEC_SEQ // C_BLOCK
    q_row0 = N_CTX_TOK // C_BLOCK
    kv_row0 = N_CTX_TOK // DEC_SEQ
    k_col = C_Q_WIDTH // C_KV_WIDTH
    n_q = C_GROUP * C_BLOCK
    return pl.pallas_call(
        functools.partial(_lat_attn_c_kernel, sink0=j_layer * C_HEADS),
        out_shape=jax.ShapeDtypeStruct((N_TOK, D_MODEL), F32),
        grid=(DEC_BATCH, n_blocks),
        in_specs=[
            pl.BlockSpec(memory_space=pltpu.SMEM),
            pl.BlockSpec((C_BLOCK, C_Q_WIDTH), lambda b, j: (q_row0 + b * n_blocks + j, 0)),
            pl.BlockSpec((DEC_SEQ, C_KV_WIDTH), lambda b, j: (kv_row0 + b, k_col)),
            pl.BlockSpec((DEC_SEQ, C_KV_WIDTH), lambda b, j: (kv_row0 + b, k_col + 1)),
            pl.BlockSpec((None, None, C_KV_HEADS, HEAD_DIM, PAST_LEN),
                         lambda b, j: (b, j_layer, 0, 0, 0)),
            pl.BlockSpec((None, None, C_KV_HEADS, HEAD_DIM, PAST_LEN),
                         lambda b, j: (b, j_layer, 0, 0, 0)),
            pl.BlockSpec(memory_space=pl.ANY),
        ],
        out_specs=pl.BlockSpec((C_BLOCK, C_Q_WIDTH),
                               lambda b, j: (q_row0 + b * n_blocks + j, 0)),
        input_output_aliases={6: 0},
        scratch_shapes=[pltpu.VMEM((PAST_LEN, C_KV_WIDTH), BF16),
                        pltpu.VMEM((PAST_LEN, C_KV_WIDTH), BF16),
                        pltpu.VMEM((C_KV_HEADS, PAST_LEN, n_q), F32),
                        pltpu.VMEM((C_KV_HEADS, 3 * C_BLOCK, n_q), F32),
                        pltpu.VMEM((C_KV_HEADS, PAST_LEN, n_q), BF16),
                        pltpu.VMEM((C_KV_HEADS, 3 * C_BLOCK, n_q), BF16)],
        compiler_params=_params(2),
        name="lat_attn_c",
    )(sink.reshape(-1), qkv, qkv, qkv, cache_k, cache_v, mix)


def _rope_tables():
    t = np.arange(DEC_SEQ)
    pos = np.stack([t // GRID_W, t % GRID_W], axis=-1).astype(np.float64)
    half = HEAD_DIM // 4
    inv = ROPE_BASE ** (-np.arange(half, dtype=np.float64) / half)
    ang = pos[:, :, None] * inv
    cos = np.cos(ang)
    sin = np.sin(ang)
    cos64 = np.stack([cos, cos], axis=2).reshape(DEC_SEQ, HEAD_DIM)
    sin64 = np.stack([-sin, sin], axis=2).reshape(DEC_SEQ, HEAD_DIM)
    reps = ROPE_TABLE_WIDTH // HEAD_DIM
    return (jnp.asarray(np.tile(cos64, (1, reps)), F32),
            jnp.asarray(np.tile(sin64, (1, reps)), F32))


FFN_CHUNK = 256
N_FFN_CHUNKS = FFN_HIDDEN // FFN_CHUNK
N_WO_PIECES = D_MODEL // FFN_CHUNK


def _post_mixer_kernel(*refs, layer, wo_idx, tm, final):
    x_ref, mix_ref, g_ref, mod_ref = refs[:4]
    n_in = 5 if final else 7
    wo_hbm, wgu_hbm, wd_hbm = refs[n_in:n_in + 3]
    n_out = 2
    out_refs = refs[n_in + 3:n_in + 3 + n_out]
    (wo_s, wg_s, wu_s, wd_s, act_s, stage_col, stage_row, sem_col,
     sem_row) = refs[n_in + 3 + n_out:]
    n_row_pieces = N_WO_PIECES + N_FFN_CHUNKS

    def col_copy(which, c):
        src = wgu_hbm.at[layer, :, pl.ds(which * FFN_HIDDEN + c * FFN_CHUNK, FFN_CHUNK)]
        return pltpu.make_async_copy(src, stage_col.at[which, c % 2], sem_col.at[which, c % 2])

    def row_copy(p):
        if p < N_WO_PIECES:
            src = wo_hbm.at[wo_idx, pl.ds(p * FFN_CHUNK, FFN_CHUNK), :]
        else:
            src = wd_hbm.at[layer, pl.ds((p - N_WO_PIECES) * FFN_CHUNK, FFN_CHUNK), :]
        return pltpu.make_async_copy(src, stage_row.at[p % 2], sem_row.at[p % 2])

    def take_row_piece(p, dst_ref, row0):
        if p + 1 < n_row_pieces:
            row_copy(p + 1).start()
        row_copy(p).wait()
        dst_ref[row0:row0 + FFN_CHUNK, :] = stage_row[p % 2].astype(BF16)

    def tile(load_weights):
        if load_weights:
            row_copy(0).start()
            for which in range(2):
                col_copy(which, 0).start()
            for p in range(N_WO_PIECES):
                take_row_piece(p, wo_s, p * FFN_CHUNK)
        if not final:
            out_refs[1][...] = _modulation_block(*refs[4:7])
        gate1 = mod_ref[:, 2 * D_MODEL:3 * D_MODEL]
        x1 = x_ref[...] + gate1 * _dot(mix_ref[...].astype(BF16), wo_s[...])
        h = _norm_mod(x1, g_ref[...], mod_ref, 3).astype(BF16)
        for c in range(N_FFN_CHUNKS):
            if load_weights:
                for which, dst in ((0, wg_s), (1, wu_s)):
                    if c + 1 < N_FFN_CHUNKS:
                        col_copy(which, c + 1).start()
                    col_copy(which, c).wait()
                    dst[c] = stage_col[which, c % 2].astype(BF16)
            gate = _dot(h, wg_s[c])
            up = _dot(h, wu_s[c])
            act = gate / (1.0 + jnp.exp(-gate)) * up
            act_s[:, c * FFN_CHUNK:(c + 1) * FFN_CHUNK] = act.astype(BF16)
            if load_weights:
                take_row_piece(N_WO_PIECES + c, wd_s, c * FFN_CHUNK)
        gate2 = mod_ref[:, 5 * D_MODEL:6 * D_MODEL]
        return x1 + gate2 * _dot(act_s[...], wd_s[...])

    def emit(x2):
        if not final:
            out_refs[0][...] = x2
            return
        var = jnp.mean(x2 * x2, axis=-1, keepdims=True)
        y = x2 * lax.rsqrt(var + EPS) * refs[4][...]
        is_ctx = pl.program_id(0) < N_CTX_TOK // tm

        @pl.when(is_ctx)
        def _():
            out_refs[0][...] = y

        @pl.when(jnp.logical_not(is_ctx))
        def _():
            out_refs[1][...] = y

    first = pl.program_id(0) == 0

    @pl.when(first)
    def _():
        emit(tile(True))

    @pl.when(jnp.logical_not(first))
    def _():
        emit(tile(False))


def _post_mixer(x, mix, g_all, mod_l, layer, w_out_all, wo_idx, w_gate_up, w_down,
                norm_final=None, next_mod_inputs=None):
    tm = 512
    n_tiles = N_TOK // tm
    n_ctx_tiles = N_CTX_TOK // tm
    final = norm_final is not None
    row_spec = pl.BlockSpec((tm, D_MODEL), lambda i: (i, 0))
    hbm = pl.BlockSpec(memory_space=pl.ANY)
    if final:
        extra_in = [norm_final.reshape(1, D_MODEL)]
        extra_specs = [pl.BlockSpec((1, D_MODEL), lambda i: (0, 0))]
        out_shape = (jax.ShapeDtypeStruct((N_CTX_TOK, D_MODEL), F32),
                     jax.ShapeDtypeStruct((N_LAT_TOK, D_MODEL), F32))
        out_specs = (
            pl.BlockSpec((tm, D_MODEL), lambda i: (jnp.minimum(i, n_ctx_tiles - 1), 0)),
            pl.BlockSpec((tm, D_MODEL), lambda i: (jnp.maximum(i - n_ctx_tiles, 0), 0)))
    else:
        tn = 6 * D_MODEL // n_tiles
        extra_in = list(next_mod_inputs)
        extra_specs = [
            pl.BlockSpec((N_GROUPS_PAD, D_MODEL), lambda i: (0, 0)),
            pl.BlockSpec((None, D_MODEL, tn), lambda i: (layer + 1, 0, i)),
            pl.BlockSpec((None, 1, tn), lambda i: (layer + 1, 0, i)),
        ]
        out_shape = (jax.ShapeDtypeStruct((N_TOK, D_MODEL), F32),
                     jax.ShapeDtypeStruct((N_GROUPS_PAD, 6 * D_MODEL), F32))
        out_specs = (row_spec, pl.BlockSpec((N_GROUPS_PAD, tn), lambda i: (0, i)))
    return pl.pallas_call(
        functools.partial(_post_mixer_kernel, layer=layer, wo_idx=wo_idx, tm=tm, final=final),
        out_shape=out_shape,
        grid=(N_TOK // tm,),
        in_specs=[
            row_spec,
            row_spec,
            pl.BlockSpec((None, 1, D_MODEL), lambda i: (layer, 0, 0)),
            pl.BlockSpec((None, 1, 6 * D_MODEL), lambda i: (_group_of_tile(i, tm), 0, 0)),
        ] + extra_specs + [hbm, hbm, hbm],
        out_specs=out_specs,
        scratch_shapes=[
            pltpu.VMEM((D_MODEL, D_MODEL), BF16),
            pltpu.VMEM((N_FFN_CHUNKS, D_MODEL, FFN_CHUNK), BF16),
            pltpu.VMEM((N_FFN_CHUNKS, D_MODEL, FFN_CHUNK), BF16),
            pltpu.VMEM((FFN_HIDDEN, D_MODEL), BF16),
            pltpu.VMEM((tm, FFN_HIDDEN), BF16),
            pltpu.VMEM((2, 2, D_MODEL, FFN_CHUNK), F32),
            pltpu.VMEM((2, FFN_CHUNK, D_MODEL), F32),
            pltpu.SemaphoreType.DMA((2, 2)),
            pltpu.SemaphoreType.DMA((2,)),
        ],
        compiler_params=_params(1),
        name="post_mixer",
    )(x, mix, g_all.reshape(DEPTH, 1, D_MODEL), mod_l, *extra_in, w_out_all, w_gate_up, w_down)


def kernel(x_prompt, x_sample, cache_a_k, cache_a_v, cache_c_k, cache_c_v, c, c_ctx, w_mod, b_mod, norm_mix, norm_ffn, w_in_ab, rpb_a, w_pool, pool_scale, w_out_ab, w_in_c, sink_c, w_out_c, w_gate_up, w_down, norm_final):
    xs = (x_prompt.reshape(N_CTX_TOK, D_MODEL), x_sample.reshape(N_LAT_TOK, D_MODEL))
    cond8 = jnp.concatenate(
        [c_ctx[None], c, jnp.zeros((N_GROUPS_PAD - 1 - DEC_BATCH, D_MODEL), F32)], axis=0)
    b_mod3 = b_mod.reshape(DEPTH, 1, 6 * D_MODEL)
    mod, bias_tiles = _modulation0_and_bias(cond8, w_mod, b_mod3, rpb_a)

    n_ab = cache_a_k.shape[1]
    n_c = cache_c_k.shape[1]
    cache_a_k, cache_a_v, cache_c_k, cache_c_v = (
        jnp.transpose(t, (0, 1, 3, 4, 2)) for t in (cache_a_k, cache_a_v, cache_c_k, cache_c_v))
    rope_tables = _rope_tables()

    new_a = []
    new_c = []
    for l in range(DEPTH):
        mod_l = mod.reshape(N_GROUPS_PAD, 1, 6 * D_MODEL)
        if l % 2 == 0:
            i = l // 2
            qkv, *rest = _inproj(xs, norm_mix, mod_l, l, w_in_ab, i, A_HEADS, A_WIDTH, n_ab,
                                 new_a)
            new_a, u = rest[:2], rest[2]
            if len(xs) == 2:
                xs = (rest[3],)
            mix = _ctx_attn_a(qkv, u, w_pool, pool_scale, i)
            mix = _lat_pool(u, w_pool, pool_scale, i, mix)
            mix = _lat_attn_a(qkv, cache_a_k, cache_a_v, bias_tiles, i, mix)
            w_out, wo_idx = w_out_ab, i
        else:
            j = l // 2
            qkv, *new_c = _inproj(xs, norm_mix, mod_l, l, w_in_c, j, C_KV_HEADS, C_Q_WIDTH, n_c,
                                  new_c, rope_tables)
            mix = _ctx_attn_c(qkv, sink_c, j)
            mix = _lat_attn_c(qkv, cache_c_k, cache_c_v, sink_c, j, mix)
            w_out, wo_idx = w_out_c, j
        if l + 1 < DEPTH:
            x, mod = _post_mixer(xs[0], mix, norm_ffn, mod_l, l, w_out, wo_idx, w_gate_up,
                                 w_down, next_mod_inputs=(cond8, w_mod, b_mod3))
            xs = (x,)
        else:
            y_ctx, y_lat = _post_mixer(xs[0], mix, norm_ffn, mod_l, l, w_out, wo_idx, w_gate_up,
                                       w_down, norm_final)

    new_caches = [jnp.transpose(t, (0, 1, 4, 2, 3)) for t in (*new_a, *new_c)]
    return (y_ctx.reshape(BATCH, SEQ, D_MODEL), y_lat.reshape(DEC_BATCH, DEC_SEQ, D_MODEL),
            *new_caches)
```

```python
import functools

import jax
import jax.numpy as jnp
import numpy as np
from jax import lax
from jax.experimental import pallas as pl
from jax.experimental.pallas import tpu as pltpu

D_MODEL = 1024
BATCH = 16
SEQ = 256
DEPTH = 4
DEC_BATCH = 2
DEC_SEQ = 1024
PAST_LEN = 512
GRID_W = 64
HEAD_DIM = 64
A_WIDTH = 512
A_HEADS = 8
B_WIDTH = 512
POOL_WINDOWS = (2, 4, 8, 16)
B_GROUP_DIM = 128
NA_ROWS = 8
NA_COLS = 16
C_HEADS = 16
C_KV_HEADS = 4
C_GROUP = C_HEADS // C_KV_HEADS
C_Q_WIDTH = 1024
C_KV_WIDTH = 256
C_BLOCK = 128
FFN_HIDDEN = 2816
ROPE_BASE = 10000.0
EPS = 1e-6
NEG_INF = -1e30

N_CTX_TOK = BATCH * SEQ
N_LAT_TOK = DEC_BATCH * DEC_SEQ
N_TOK = N_CTX_TOK + N_LAT_TOK
GRID_ROWS = DEC_SEQ // GRID_W
N_GROUPS_PAD = 8

VMEM_LIMIT = 56 * 1024 * 1024

F32 = jnp.float32
BF16 = jnp.bfloat16


def _params(n_axes):
    return pltpu.CompilerParams(dimension_semantics=("arbitrary",) * n_axes,
                                vmem_limit_bytes=VMEM_LIMIT)


def _group_of_tile(i, tm):
    row0 = i * tm
    return jnp.where(row0 < N_CTX_TOK, 0, 1 + (row0 - N_CTX_TOK) // DEC_SEQ)


def _dot_nt(a, b):
    return lax.dot_general(a, b, (((1,), (1,)), ((), ())), preferred_element_type=F32)


def _dot(a, b):
    return jnp.dot(a, b, preferred_element_type=F32)


def _dot_tn(a, b):
    return lax.dot_general(a, b, (((0,), (0,)), ((), ())), preferred_element_type=F32)


def _softmax_pv_t(score_fns, sink_rows, values, s_refs, p_refs):
    n_groups, n_blocks = len(score_fns), len(s_refs)
    for g in range(n_groups):
        for b in range(n_blocks):
            s_refs[b][g] = score_fns[g][b]()
    inv_l = []
    for g in range(n_groups):
        m = sink_rows[g]
        for b in range(n_blocks):
            m = jnp.maximum(m, jnp.max(s_refs[b][g], axis=0, keepdims=True))
        l = jnp.exp2(sink_rows[g] - m)
        for b in range(n_blocks):
            p = jnp.exp2(s_refs[b][g] - m)
            l = l + jnp.sum(p, axis=0, keepdims=True)
            p_refs[b][g] = p.astype(BF16)
        inv_l.append(1.0 / l)
    outs = []
    for g in range(n_groups):
        o = None
        for b in range(n_blocks):
            pv = _dot_tn(values[g][b], p_refs[b][g])
            o = pv if o is None else o + pv
        outs.append(o * inv_l[g])
    return outs


def _modulation_block(cond_ref, w_ref, b_ref):
    c = cond_ref[...]
    s = c / (1.0 + jnp.exp(-c))
    return _dot(s.astype(BF16), w_ref[...].astype(BF16)) + b_ref[...]


def _mod_bias_kernel(rpb_ref, cond_ref, w_ref, b_ref, o_ref, bias_ref, *, heads_per_step):
    o_ref[...] = _modulation_block(cond_ref, w_ref, b_ref)
    pair0 = pl.program_id(0) * heads_per_step
    for t in range(heads_per_step):
        _write_bias_tiles(rpb_ref, (pair0 + t) * (N_DROW * N_DCOL), bias_ref.at[t])


def _modulation0_and_bias(cond8, w_mod, b_mod3, rpb_a):
    n_col_blocks = 4
    tn = 6 * D_MODEL // n_col_blocks
    n_bias_layers = rpb_a.shape[0]
    heads_per_step = n_bias_layers * A_HEADS // n_col_blocks
    steps_per_layer = A_HEADS // heads_per_step
    return pl.pallas_call(
        functools.partial(_mod_bias_kernel, heads_per_step=heads_per_step),
        out_shape=(
            jax.ShapeDtypeStruct((N_GROUPS_PAD, 6 * D_MODEL), F32),
            jax.ShapeDtypeStruct(
                (n_bias_layers, A_HEADS, N_BIAS_TILES, GRID_W, 2 * GRID_W), F32)),
        grid=(n_col_blocks,),
        in_specs=[
            pl.BlockSpec(memory_space=pltpu.SMEM),
            pl.BlockSpec((N_GROUPS_PAD, D_MODEL), lambda j: (0, 0)),
            pl.BlockSpec((None, D_MODEL, tn), lambda j: (0, 0, j)),
            pl.BlockSpec((None, 1, tn), lambda j: (0, 0, j)),
        ],
        out_specs=(
            pl.BlockSpec((N_GROUPS_PAD, tn), lambda j: (0, j)),
            pl.BlockSpec((None, heads_per_step, N_BIAS_TILES, GRID_W, 2 * GRID_W),
                         lambda j: (j // steps_per_layer, j % steps_per_layer, 0, 0, 0))),
        compiler_params=_params(1),
        name="modulation_bias",
    )(rpb_a.reshape(-1), cond8, w_mod, b_mod3)


def _norm_mod(x, g, mod_ref, shift_idx):
    var = jnp.mean(x * x, axis=-1, keepdims=True)
    y = x * lax.rsqrt(var + EPS) * g
    shift = mod_ref[:, shift_idx * D_MODEL:(shift_idx + 1) * D_MODEL]
    scale = mod_ref[:, (shift_idx + 1) * D_MODEL:(shift_idx + 2) * D_MODEL]
    return y * (1.0 + scale) + shift


QKV_WIDTH = 3 * A_WIDTH
LOG2_E = 1.4426950408889634
Q_SCALE = HEAD_DIM ** -0.5 * LOG2_E
PROJ_CHUNK = 4 * HEAD_DIM
ROPE_TABLE_WIDTH = PROJ_CHUNK


def _rope(x, cos, sin_signed):
    n = x.shape[-1]
    lane = lax.broadcasted_iota(jnp.int32, x.shape, x.ndim - 1)
    first = (lane % 32) < 16
    partner = jnp.where(first, pltpu.roll(x, n - 16, axis=x.ndim - 1),
                        pltpu.roll(x, 16, axis=x.ndim - 1))
    return x * cos + partner * sin_signed


def _inproj_kernel(*refs, tm, n_heads, q_width, n_prev, split_x, rope, has_u):
    n_x = 2 if split_x else 1
    g_ref, mod_ref, w_ref = refs[n_x:n_x + 3]
    n_in = n_x + 3 + (2 if rope else 0) + n_prev
    qkv_ref, ck_ref, cv_ref = refs[n_in:n_in + 3]
    extra_out = refs[n_in + 3:-3]
    wbf_ref, h_ref, res_ref = refs[-3:]
    i = pl.program_id(0)
    is_ctx = i < N_CTX_TOK // tm
    kv_width = (QKV_WIDTH - q_width) // 2
    k_col, v_col = q_width, q_width + kv_width
    n_out = res_ref.shape[1]
    chunk = PROJ_CHUNK

    @pl.when(i == 0)
    def _():
        wbf_ref[...] = w_ref[...].astype(BF16)

    def tile(ctx):
        if split_x:
            x = (refs[0] if ctx else refs[1])[...]
            extra_out[-1][...] = x
        else:
            x = refs[0][...]
        h_ref[...] = _norm_mod(x, g_ref[...], mod_ref, 0).astype(BF16)

        def matmul(c0):
            cols = slice(c0, c0 + chunk)
            res_ref[:, cols] = _dot(h_ref[...], wbf_ref[:, cols])

        def epilogue(c0):
            cols = slice(c0, c0 + chunk)
            if c0 >= QKV_WIDTH:
                extra_out[0][:, c0 - QKV_WIDTH:c0 - QKV_WIDTH + chunk] = res_ref[:, cols]
                return
            r = res_ref[:, cols]
            if rope and not ctx and c0 < v_col:
                cos_ref, sin_ref = refs[n_x + 3:n_x + 5]
                r = _rope(r, cos_ref[...], sin_ref[...])
            if c0 < q_width:
                r = r * Q_SCALE
            qkv_ref[:, cols] = r.astype(BF16)
            if ctx and c0 >= k_col:
                c_ref, col0 = (ck_ref, k_col) if c0 < v_col else (cv_ref, v_col)
                heads = slice((c0 - col0) // HEAD_DIM, (c0 - col0 + chunk) // HEAD_DIM)
                for req in range(tm // SEQ):
                    blk_t = res_ref[req * SEQ:(req + 1) * SEQ, cols].T
                    c_ref[req, heads] = blk_t.reshape(chunk // HEAD_DIM, HEAD_DIM, SEQ)

        for c0 in range(0, n_out, chunk):
            matmul(c0)
            epilogue(c0)

    pl.when(is_ctx)(lambda: tile(True))
    pl.when(jnp.logical_not(is_ctx))(lambda: tile(False))


def _inproj(xs, g_all, mod_l, layer, w_all, w_idx, n_heads, q_width, n_slots, prev_caches,
            rope_tables=None):
    split_x = len(xs) == 2
    tm = 512 if split_x else 1024
    n_out = w_all.shape[2]
    n_ctx_tiles = N_CTX_TOK // tm
    rope = rope_tables is not None
    has_u = n_out > QKV_WIDTH
    cache_shape = jax.ShapeDtypeStruct((BATCH, n_slots, n_heads, HEAD_DIM, SEQ), F32)
    cache_spec = pl.BlockSpec(
        (tm // SEQ, None, n_heads, HEAD_DIM, SEQ),
        lambda i: (jnp.minimum(i, n_ctx_tiles - 1), w_idx, 0, 0, 0))
    row_spec = pl.BlockSpec((tm, D_MODEL), lambda i: (i, 0))
    if split_x:
        x_specs = [
            pl.BlockSpec((tm, D_MODEL), lambda i: (jnp.minimum(i, n_ctx_tiles - 1), 0)),
            pl.BlockSpec((tm, D_MODEL), lambda i: (jnp.maximum(i - n_ctx_tiles, 0), 0)),
        ]
    else:
        x_specs = [row_spec]
    out_shape = [jax.ShapeDtypeStruct((N_TOK, QKV_WIDTH), BF16), cache_shape, cache_shape]
    out_specs = [pl.BlockSpec((tm, QKV_WIDTH), lambda i: (i, 0)), cache_spec, cache_spec]
    if has_u:
        out_shape.append(jax.ShapeDtypeStruct((N_TOK, n_out - QKV_WIDTH), F32))
        out_specs.append(pl.BlockSpec((tm, n_out - QKV_WIDTH), lambda i: (i, 0)))
    if split_x:
        out_shape.append(jax.ShapeDtypeStruct((N_TOK, D_MODEL), F32))
        out_specs.append(row_spec)
    rope_in, rope_specs = [], []
    if rope:
        tiles_per_seq = DEC_SEQ // tm
        rope_spec = pl.BlockSpec(
            (tm, ROPE_TABLE_WIDTH),
            lambda i: (jnp.maximum(i - n_ctx_tiles, 0) % tiles_per_seq, 0))
        rope_in, rope_specs = list(rope_tables), [rope_spec, rope_spec]
    n_prev = len(prev_caches)
    n_before = len(xs) + 3 + len(rope_in)
    return pl.pallas_call(
        functools.partial(_inproj_kernel, tm=tm, n_heads=n_heads, q_width=q_width,
                          n_prev=n_prev, split_x=split_x, rope=rope, has_u=has_u),
        out_shape=out_shape,
        grid=(N_TOK // tm,),
        in_specs=x_specs + [
            pl.BlockSpec((None, 1, D_MODEL), lambda i: (layer, 0, 0)),
            pl.BlockSpec((None, 1, 6 * D_MODEL), lambda i: (_group_of_tile(i, tm), 0, 0)),
            pl.BlockSpec((None, D_MODEL, n_out), lambda i: (w_idx, 0, 0),
                         pipeline_mode=pl.Buffered(1)),
        ] + rope_specs + [pl.BlockSpec(memory_space=pl.ANY)] * n_prev,
        out_specs=out_specs,
        scratch_shapes=[pltpu.VMEM((D_MODEL, n_out), BF16), pltpu.VMEM((tm, D_MODEL), BF16),
                        pltpu.VMEM((tm, n_out), F32)],
        input_output_aliases={n_before + k: 1 + k for k in range(n_prev)},
        compiler_params=_params(1),
        name="inproj",
    )(*xs, g_all.reshape(DEPTH, 1, D_MODEL), mod_l, w_all, *rope_in, *prev_caches)


def _ctx_attn_a_kernel(p_ref, u_ref, wp_ref, ps_ref, o_ref, s_ref, p_scr):
    for h in range(A_HEADS):
        c0 = h * HEAD_DIM
        s_ref[h] = _dot_nt(p_ref[:, c0:c0 + HEAD_DIM],
                           p_ref[:, A_WIDTH + c0:A_WIDTH + c0 + HEAD_DIM])
    _pool_rows(u_ref, wp_ref, ps_ref, o_ref, n=SEQ, out_col0=A_WIDTH)
    inv_l = []
    for h in range(A_HEADS):
        s = s_ref[h]
        p = jnp.exp2(s - jnp.max(s, axis=-1, keepdims=True))
        inv_l.append(1.0 / jnp.sum(p, axis=-1, keepdims=True))
        p_scr[h] = p.astype(BF16)
    for h in range(A_HEADS):
        c0 = h * HEAD_DIM
        v = p_ref[:, 2 * A_WIDTH + c0:2 * A_WIDTH + c0 + HEAD_DIM]
        o_ref[:, c0:c0 + HEAD_DIM] = _dot(p_scr[h], v) * inv_l[h]


def _ctx_attn_a(qkv, u, w_pool, pool_scale, i_layer):
    return pl.pallas_call(
        _ctx_attn_a_kernel,
        out_shape=jax.ShapeDtypeStruct((N_TOK, D_MODEL), F32),
        grid=(BATCH,),
        in_specs=[pl.BlockSpec((SEQ, QKV_WIDTH), lambda b: (b, 0)),
                  pl.BlockSpec((SEQ, B_WIDTH), lambda b: (b, 0))] + _pool_specs(i_layer),
        out_specs=pl.BlockSpec((SEQ, D_MODEL), lambda b: (b, 0)),
        scratch_shapes=[pltpu.VMEM((A_HEADS, SEQ, SEQ), F32),
                        pltpu.VMEM((A_HEADS, SEQ, SEQ), BF16)],
        compiler_params=_params(1),
        name="ctx_attn_a",
    )(qkv, u, w_pool, pool_scale.reshape(-1, 1, B_WIDTH))


N_DROW = 2 * NA_ROWS - 1
N_DCOL = 2 * NA_COLS - 1
N_BIAS_TILES = 16
BIAS_TILE_LEFT_PAD = 14
BIAS_TILE_RIGHT_PAD = 15
MID_DROW = NA_ROWS - 1 - NA_ROWS // 2


def _write_bias_tiles(rpb_ref, base, o_ref):
    qi = lax.broadcasted_iota(jnp.int32, (GRID_W, 2 * GRID_W), 0)
    lane = lax.broadcasted_iota(jnp.int32, (GRID_W, 2 * GRID_W), 1)
    right = lane >= GRID_W
    kc = jnp.where(right, lane - GRID_W, lane)
    qstart = jnp.clip(qi - NA_COLS // 2, 0, GRID_W - NA_COLS)
    valid = (kc >= qstart) & (kc < qstart + NA_COLS)

    offs = lax.broadcasted_iota(jnp.int32, (1, 2 * GRID_W), 1) & (GRID_W - 1)
    shift = 2 * GRID_W - (NA_COLS - 1)
    rows = []
    for dr in range(N_DROW):
        vec = jnp.zeros((1, 2 * GRID_W), F32)
        for d in range(N_DCOL):
            vec = jnp.where(offs == d, rpb_ref[base + dr * N_DCOL + d] * LOG2_E, vec)
        rows.append(pltpu.roll(jnp.broadcast_to(vec, (GRID_W, 2 * GRID_W)), shift, axis=1,
                               stride=1, stride_axis=0))

    for t in range(N_DROW - 1):
        o_ref[t] = jnp.where(valid, jnp.where(right, rows[t + 1], rows[t]), NEG_INF)
    o_ref[BIAS_TILE_LEFT_PAD] = jnp.where(valid & right, rows[MID_DROW], NEG_INF)
    o_ref[BIAS_TILE_RIGHT_PAD] = jnp.where(valid & jnp.logical_not(right),
                                           rows[MID_DROW + NA_ROWS - 1], NEG_INF)


NA_GROUP_ROWS = 4


def _na_window(r):
    start = min(max(r - NA_ROWS // 2, 0), GRID_ROWS - NA_ROWS)
    first_drow = start - r + NA_ROWS - 1
    if start % 2 == 0:
        return start, [first_drow + 2 * p for p in range(NA_ROWS // 2)]
    assert first_drow == MID_DROW
    inner = [first_drow + 1 + 2 * p for p in range(NA_ROWS // 2 - 1)]
    return start - 1, [BIAS_TILE_LEFT_PAD] + inner + [BIAS_TILE_RIGHT_PAD]


def _lat_attn_a_kernel(q_ref, k_ref, v_ref, kc_ref, vc_ref, bias_ref, mix_ref, o_ref,
                       s_loc, s_ctx, p_loc, p_ctx):
    del mix_ref
    hp = pl.program_id(1)

    @pl.when((pl.program_id(0) == 0) & (hp == 0))
    def _():
        p_loc[...] = jnp.zeros_like(p_loc)

    groups = []
    for g0 in range(0, GRID_ROWS, NA_GROUP_ROWS):
        windows = [_na_window(r) for r in range(g0, g0 + NA_GROUP_ROWS)]
        lo = min(first for first, _ in windows)
        hi = max(first + 2 * len(tiles) for first, tiles in windows)
        groups.append((slice(g0 * GRID_W, (g0 + NA_GROUP_ROWS) * GRID_W),
                       slice(lo * GRID_W, hi * GRID_W)))

    for hh in range(2):
        c0 = hh * HEAD_DIM
        for q_rows, keys in groups:
            s_loc[hh, q_rows, keys] = _dot_nt(q_ref[q_rows, c0:c0 + HEAD_DIM],
                                              k_ref[keys, c0:c0 + HEAD_DIM])
        s_ctx[hh] = _dot(q_ref[:, c0:c0 + HEAD_DIM],
                         kc_ref[hh].astype(BF16))
    inv_l = [[], []]
    for hh in range(2):
        for r in range(GRID_ROWS):
            rows = slice(r * GRID_W, (r + 1) * GRID_W)
            first_row, tiles = _na_window(r)
            cols = slice(first_row * GRID_W, (first_row + 2 * len(tiles)) * GRID_W)
            bias = jnp.concatenate([bias_ref[hh, t] for t in tiles], axis=1)
            sc = s_ctx[hh, rows, :]
            sl = s_loc[hh, rows, cols] + bias
            m = jnp.maximum(jnp.max(sc, axis=-1, keepdims=True),
                            jnp.max(sl, axis=-1, keepdims=True))
            pc = jnp.exp2(sc - m)
            pw = jnp.exp2(sl - m)
            inv_l[hh].append(1.0 / (jnp.sum(pc, axis=-1, keepdims=True)
                                    + jnp.sum(pw, axis=-1, keepdims=True)))
            p_ctx[hh, rows, :] = pc.astype(BF16)
            p_loc[hh, rows, cols] = pw.astype(BF16)
    for hh in range(2):
        c0 = hh * HEAD_DIM
        o_loc = jnp.concatenate(
            [_dot(p_loc[hh, q_rows, keys], v_ref[keys, c0:c0 + HEAD_DIM])
             for q_rows, keys in groups], axis=0)
        o = _dot_nt(p_ctx[hh], vc_ref[hh].astype(BF16)) + o_loc
        o_ref[:, c0:c0 + HEAD_DIM] = o * jnp.concatenate(inv_l[hh], axis=0)


def _lat_attn_a(qkv, cache_k, cache_v, bias_tiles, i_layer, mix):
    row0 = N_CTX_TOK // DEC_SEQ
    pair = 2 * HEAD_DIM
    k_col0 = A_WIDTH // pair
    v_col0 = 2 * A_WIDTH // pair
    return pl.pallas_call(
        _lat_attn_a_kernel,
        out_shape=jax.ShapeDtypeStruct((N_TOK, D_MODEL), F32),
        grid=(DEC_BATCH, A_HEADS // 2),
        in_specs=[
            pl.BlockSpec((DEC_SEQ, pair), lambda b, hp: (row0 + b, hp)),
            pl.BlockSpec((DEC_SEQ, pair), lambda b, hp: (row0 + b, k_col0 + hp)),
            pl.BlockSpec((DEC_SEQ, pair), lambda b, hp: (row0 + b, v_col0 + hp)),
            pl.BlockSpec((None, None, 2, HEAD_DIM, PAST_LEN),
                         lambda b, hp: (b, i_layer, hp, 0, 0)),
            pl.BlockSpec((None, None, 2, HEAD_DIM, PAST_LEN),
                         lambda b, hp: (b, i_layer, hp, 0, 0)),
            pl.BlockSpec((None, 2, N_BIAS_TILES, GRID_W, 2 * GRID_W),
                         lambda b, hp: (i_layer, hp, 0, 0, 0)),
            pl.BlockSpec(memory_space=pl.ANY),
        ],
        out_specs=pl.BlockSpec((DEC_SEQ, pair), lambda b, hp: (row0 + b, hp)),
        scratch_shapes=[pltpu.VMEM((2, DEC_SEQ, DEC_SEQ), F32),
                        pltpu.VMEM((2, DEC_SEQ, PAST_LEN), F32),
                        pltpu.VMEM((2, DEC_SEQ, DEC_SEQ), BF16),
                        pltpu.VMEM((2, DEC_SEQ, PAST_LEN), BF16)],
        input_output_aliases={6: 0},
        compiler_params=_params(2),
        name="lat_attn_a",
    )(qkv, qkv, qkv, cache_k, cache_v, bias_tiles, mix)


def _pool_rows(u_ref, wp_ref, ps_ref, o_ref, *, n, out_col0):
    rows = n
    t = lax.broadcasted_iota(jnp.int32, (rows, B_GROUP_DIM), 0)

    def earlier(x, k):
        return jnp.where(t >= k, pltpu.roll(x, k, axis=0), 0.0)

    def later(x, k):
        return jnp.where(t < n - k, pltpu.roll(x, rows - k, axis=0), 0.0)

    tokens_before = t.astype(F32)
    tokens_from = (n - t).astype(F32)

    for g, w in enumerate(POOL_WINDOWS):
        c0 = g * B_GROUP_DIM
        half = w // 2
        u = u_ref[:, c0:c0 + B_GROUP_DIM]
        before, after = u, u
        k = 1
        while k < half:
            before = before + earlier(before, k)
            after = after + later(after, k)
            k *= 2
        total = earlier(before, 1) + after
        count = jnp.minimum(tokens_before, half) + jnp.minimum(tokens_from, half)
        pooled = total / count - u
        y = _dot(pooled.astype(BF16), wp_ref[g].astype(BF16))
        o_ref[:, out_col0 + c0:out_col0 + c0 + B_GROUP_DIM] = (
            y * ps_ref[:, c0:c0 + B_GROUP_DIM])


def _pool_specs(i_layer):
    zeros = (0,) * 3
    return [
        pl.BlockSpec((None, len(POOL_WINDOWS), B_GROUP_DIM, B_GROUP_DIM),
                     lambda *_: (i_layer,) + zeros),
        pl.BlockSpec((None, 1, B_WIDTH), lambda *_: (i_layer, 0, 0)),
    ]


def _lat_pool_kernel(u_ref, wp_ref, ps_ref, mix_ref, o_ref):
    del mix_ref
    _pool_rows(u_ref, wp_ref, ps_ref, o_ref, n=DEC_SEQ, out_col0=0)


def _lat_pool(u, w_pool, pool_scale, i_layer, mix):
    row0 = N_CTX_TOK // DEC_SEQ
    return pl.pallas_call(
        _lat_pool_kernel,
        out_shape=jax.ShapeDtypeStruct((N_TOK, D_MODEL), F32),
        grid=(DEC_BATCH,),
        in_specs=[pl.BlockSpec((DEC_SEQ, B_WIDTH), lambda b: (row0 + b, 0))]
        + _pool_specs(i_layer) + [pl.BlockSpec(memory_space=pl.ANY)],
        out_specs=pl.BlockSpec((DEC_SEQ, B_WIDTH), lambda b: (row0 + b, 1)),
        input_output_aliases={3: 0},
        compiler_params=_params(1),
        name="pool_mixer",
    )(u, w_pool, pool_scale.reshape(-1, 1, B_WIDTH), mix)


def _sink_row(sink_ref, head0, queries_per_head):
    lane = lax.broadcasted_iota(jnp.int32, (1, C_GROUP * queries_per_head), 1)
    row = jnp.full((1, C_GROUP * queries_per_head), sink_ref[head0], F32)
    for g in range(1, C_GROUP):
        row = jnp.where(lane >= g * queries_per_head, sink_ref[head0 + g], row)
    return row * LOG2_E


CTX_REQS_PER_STEP = 4


def _ctx_attn_c_kernel(sink_ref, p_ref, o_ref, s_ref, p_scr, *, sink0):
    for req in range(CTX_REQS_PER_STEP):
        rows = slice(req * SEQ, (req + 1) * SEQ)

        def scores(kk):
            q = jnp.concatenate(
                [p_ref[rows, h * HEAD_DIM:(h + 1) * HEAD_DIM]
                 for h in range(kk * C_GROUP, (kk + 1) * C_GROUP)], axis=0)
            k0 = C_Q_WIDTH + kk * HEAD_DIM
            return _dot_nt(p_ref[rows, k0:k0 + HEAD_DIM], q)

        v0 = C_Q_WIDTH + C_KV_WIDTH
        outs = _softmax_pv_t(
            [[functools.partial(scores, kk)] for kk in range(C_KV_HEADS)],
            [_sink_row(sink_ref, sink0 + kk * C_GROUP, SEQ) for kk in range(C_KV_HEADS)],
            [[p_ref[rows, v0 + kk * HEAD_DIM:v0 + (kk + 1) * HEAD_DIM]]
             for kk in range(C_KV_HEADS)],
            [s_ref], [p_scr])
        outs_t = [o_t[:, g * SEQ:(g + 1) * SEQ] for o_t in outs for g in range(C_GROUP)]
        o_ref[rows, :] = jnp.concatenate(outs_t, axis=0).T


def _ctx_attn_c(qkv, sink_all, j_layer):
    return pl.pallas_call(
        functools.partial(_ctx_attn_c_kernel, sink0=j_layer * C_HEADS),
        out_shape=jax.ShapeDtypeStruct((N_TOK, D_MODEL), F32),
        grid=(BATCH // CTX_REQS_PER_STEP,),
        in_specs=[
            pl.BlockSpec(memory_space=pltpu.SMEM),
            pl.BlockSpec((CTX_REQS_PER_STEP * SEQ, QKV_WIDTH), lambda b: (b, 0)),
        ],
        out_specs=pl.BlockSpec((CTX_REQS_PER_STEP * SEQ, C_Q_WIDTH), lambda b: (b, 0)),
        scratch_shapes=[pltpu.VMEM((C_KV_HEADS, SEQ, C_GROUP * SEQ), F32),
                        pltpu.VMEM((C_KV_HEADS, SEQ, C_GROUP * SEQ), BF16)],
        compiler_params=_params(1),
        name="ctx_attn_c",
    )(sink_all.reshape(-1), qkv)


def _lat_attn_c_kernel(sink_ref, q_ref, k_ref, v_ref, kc_ref, vc_ref, mix_ref, o_ref,
                       kctx_ref, vctx_ref, sc_ref, sb_ref, pc_ref, pb_ref, *, sink0):
    del mix_ref
    j = pl.program_id(1)
    n_blocks = DEC_SEQ // C_BLOCK

    @pl.when(j == 0)
    def _():
        for kk in range(C_KV_HEADS):
            c0 = kk * HEAD_DIM
            kctx_ref[:, c0:c0 + HEAD_DIM] = kc_ref[kk].T.astype(BF16)
            vctx_ref[:, c0:c0 + HEAD_DIM] = vc_ref[kk].T.astype(BF16)

    n_q = C_GROUP * C_BLOCK
    qi = lax.broadcasted_iota(jnp.int32, (C_BLOCK, n_q), 1) % C_BLOCK
    jl = lax.broadcasted_iota(jnp.int32, (C_BLOCK, n_q), 0)
    valid_prev = (jl >= qi) & (j > 0)
    valid_next = (jl <= qi) & (j < n_blocks - 1)
    rows_prev = pl.ds(pl.multiple_of(jnp.maximum(j - 1, 0) * C_BLOCK, C_BLOCK), C_BLOCK)
    rows_cur = pl.ds(pl.multiple_of(j * C_BLOCK, C_BLOCK), C_BLOCK)
    rows_next = pl.ds(pl.multiple_of(jnp.minimum(j + 1, n_blocks - 1) * C_BLOCK, C_BLOCK),
                      C_BLOCK)

    def stacked_q(kk):
        return jnp.concatenate(
            [q_ref[:, h * HEAD_DIM:(h + 1) * HEAD_DIM]
             for h in range(kk * C_GROUP, (kk + 1) * C_GROUP)], axis=0)

    def band(ref, kk):
        cols = slice(kk * HEAD_DIM, (kk + 1) * HEAD_DIM)
        return jnp.concatenate(
            [ref[rows_prev, cols], ref[rows_cur, cols], ref[rows_next, cols]], axis=0)

    def ctx_scores(kk):
        return _dot_nt(kctx_ref[:, kk * HEAD_DIM:(kk + 1) * HEAD_DIM], stacked_q(kk))

    def band_scores(kk):
        s_raw = _dot_nt(band(k_ref, kk), stacked_q(kk))
        return jnp.concatenate(
            [jnp.where(valid_prev, s_raw[:C_BLOCK], NEG_INF),
             s_raw[C_BLOCK:2 * C_BLOCK],
             jnp.where(valid_next, s_raw[2 * C_BLOCK:], NEG_INF)], axis=0)

    kv_heads = range(C_KV_HEADS)
    outs = _softmax_pv_t(
        [[functools.partial(ctx_scores, kk), functools.partial(band_scores, kk)]
         for kk in kv_heads],
        [_sink_row(sink_ref, sink0 + kk * C_GROUP, C_BLOCK) for kk in kv_heads],
        [[vctx_ref[:, kk * HEAD_DIM:(kk + 1) * HEAD_DIM], band(v_ref, kk)] for kk in kv_heads],
        [sc_ref, sb_ref], [pc_ref, pb_ref])
    outs_t = [o_t[:, g * C_BLOCK:(g + 1) * C_BLOCK] for o_t in outs for g in range(C_GROUP)]
    o_ref[...] = jnp.concatenate(outs_t, axis=0).T


def _lat_attn_c(qkv, cache_k, cache_v, sink, j_layer, mix):
    n_blocks = DEC_SEQ // C_BLOCK
    q_row0 = N_CTX_TOK // C_BLOCK
    kv_row0 = N_CTX_TOK // DEC_SEQ
    k_col = C_Q_WIDTH // C_KV_WIDTH
    n_q = C_GROUP * C_BLOCK
    return pl.pallas_call(
        functools.partial(_lat_attn_c_kernel, sink0=j_layer * C_HEADS),
        out_shape=jax.ShapeDtypeStruct((N_TOK, D_MODEL), F32),
        grid=(DEC_BATCH, n_blocks),
        in_specs=[
            pl.BlockSpec(memory_space=pltpu.SMEM),
            pl.BlockSpec((C_BLOCK, C_Q_WIDTH), lambda b, j: (q_row0 + b * n_blocks + j, 0)),
            pl.BlockSpec((DEC_SEQ, C_KV_WIDTH), lambda b, j: (kv_row0 + b, k_col)),
            pl.BlockSpec((DEC_SEQ, C_KV_WIDTH), lambda b, j: (kv_row0 + b, k_col + 1)),
            pl.BlockSpec((None, None, C_KV_HEADS, HEAD_DIM, PAST_LEN),
                         lambda b, j: (b, j_layer, 0, 0, 0)),
            pl.BlockSpec((None, None, C_KV_HEADS, HEAD_DIM, PAST_LEN),
                         lambda b, j: (b, j_layer, 0, 0, 0)),
            pl.BlockSpec(memory_space=pl.ANY),
        ],
        out_specs=pl.BlockSpec((C_BLOCK, C_Q_WIDTH),
                               lambda b, j: (q_row0 + b * n_blocks + j, 0)),
        input_output_aliases={6: 0},
        scratch_shapes=[pltpu.VMEM((PAST_LEN, C_KV_WIDTH), BF16),
                        pltpu.VMEM((PAST_LEN, C_KV_WIDTH), BF16),
                        pltpu.VMEM((C_KV_HEADS, PAST_LEN, n_q), F32),
                        pltpu.VMEM((C_KV_HEADS, 3 * C_BLOCK, n_q), F32),
                        pltpu.VMEM((C_KV_HEADS, PAST_LEN, n_q), BF16),
                        pltpu.VMEM((C_KV_HEADS, 3 * C_BLOCK, n_q), BF16)],
        compiler_params=_params(2),
        name="lat_attn_c",
    )(sink.reshape(-1), qkv, qkv, qkv, cache_k, cache_v, mix)


def _rope_tables():
    t = np.arange(DEC_SEQ)
    pos = np.stack([t // GRID_W, t % GRID_W], axis=-1).astype(np.float64)
    half = HEAD_DIM // 4
    inv = ROPE_BASE ** (-np.arange(half, dtype=np.float64) / half)
    ang = pos[:, :, None] * inv
    cos = np.cos(ang)
    sin = np.sin(ang)
    cos64 = np.stack([cos, cos], axis=2).reshape(DEC_SEQ, HEAD_DIM)
    sin64 = np.stack([-sin, sin], axis=2).reshape(DEC_SEQ, HEAD_DIM)
    reps = ROPE_TABLE_WIDTH // HEAD_DIM
    return (jnp.asarray(np.tile(cos64, (1, reps)), F32),
            jnp.asarray(np.tile(sin64, (1, reps)), F32))


FFN_CHUNK = 256
N_FFN_CHUNKS = FFN_HIDDEN // FFN_CHUNK
N_WO_PIECES = D_MODEL // FFN_CHUNK


def _post_mixer_kernel(*refs, layer, wo_idx, tm, final):
    x_ref, mix_ref, g_ref, mod_ref = refs[:4]
    n_in = 5 if final else 7
    wo_hbm, wgu_hbm, wd_hbm = refs[n_in:n_in + 3]
    n_out = 2
    out_refs = refs[n_in + 3:n_in + 3 + n_out]
    (wo_s, wg_s, wu_s, wd_s, act_s, stage_col, stage_row, sem_col,
     sem_row) = refs[n_in + 3 + n_out:]
    n_row_pieces = N_WO_PIECES + N_FFN_CHUNKS

    def col_copy(which, c):
        src = wgu_hbm.at[layer, :, pl.ds(which * FFN_HIDDEN + c * FFN_CHUNK, FFN_CHUNK)]
        return pltpu.make_async_copy(src, stage_col.at[which, c % 2], sem_col.at[which, c % 2])

    def row_copy(p):
        if p < N_WO_PIECES:
            src = wo_hbm.at[wo_idx, pl.ds(p * FFN_CHUNK, FFN_CHUNK), :]
        else:
            src = wd_hbm.at[layer, pl.ds((p - N_WO_PIECES) * FFN_CHUNK, FFN_CHUNK), :]
        return pltpu.make_async_copy(src, stage_row.at[p % 2], sem_row.at[p % 2])

    def take_row_piece(p, dst_ref, row0):
        if p + 1 < n_row_pieces:
            row_copy(p + 1).start()
        row_copy(p).wait()
        dst_ref[row0:row0 + FFN_CHUNK, :] = stage_row[p % 2].astype(BF16)

    def tile(load_weights):
        if load_weights:
            row_copy(0).start()
            for which in range(2):
                col_copy(which, 0).start()
            for p in range(N_WO_PIECES):
                take_row_piece(p, wo_s, p * FFN_CHUNK)
        if not final:
            out_refs[1][...] = _modulation_block(*refs[4:7])
        gate1 = mod_ref[:, 2 * D_MODEL:3 * D_MODEL]
        x1 = x_ref[...] + gate1 * _dot(mix_ref[...].astype(BF16), wo_s[...])
        h = _norm_mod(x1, g_ref[...], mod_ref, 3).astype(BF16)
        for c in range(N_FFN_CHUNKS):
            if load_weights:
                for which, dst in ((0, wg_s), (1, wu_s)):
                    if c + 1 < N_FFN_CHUNKS:
                        col_copy(which, c + 1).start()
                    col_copy(which, c).wait()
                    dst[c] = stage_col[which, c % 2].astype(BF16)
            gate = _dot(h, wg_s[c])
            up = _dot(h, wu_s[c])
            act = gate / (1.0 + jnp.exp(-gate)) * up
            act_s[:, c * FFN_CHUNK:(c + 1) * FFN_CHUNK] = act.astype(BF16)
            if load_weights:
                take_row_piece(N_WO_PIECES + c, wd_s, c * FFN_CHUNK)
        gate2 = mod_ref[:, 5 * D_MODEL:6 * D_MODEL]
        return x1 + gate2 * _dot(act_s[...], wd_s[...])

    def emit(x2):
        if not final:
            out_refs[0][...] = x2
            return
        var = jnp.mean(x2 * x2, axis=-1, keepdims=True)
        y = x2 * lax.rsqrt(var + EPS) * refs[4][...]
        is_ctx = pl.program_id(0) < N_CTX_TOK // tm

        @pl.when(is_ctx)
        def _():
            out_refs[0][...] = y

        @pl.when(jnp.logical_not(is_ctx))
        def _():
            out_refs[1][...] = y

    first = pl.program_id(0) == 0

    @pl.when(first)
    def _():
        emit(tile(True))

    @pl.when(jnp.logical_not(first))
    def _():
        emit(tile(False))


def _post_mixer(x, mix, g_all, mod_l, layer, w_out_all, wo_idx, w_gate_up, w_down,
                norm_final=None, next_mod_inputs=None):
    tm = 512
    n_tiles = N_TOK // tm
    n_ctx_tiles = N_CTX_TOK // tm
    final = norm_final is not None
    row_spec = pl.BlockSpec((tm, D_MODEL), lambda i: (i, 0))
    hbm = pl.BlockSpec(memory_space=pl.ANY)
    if final:
        extra_in = [norm_final.reshape(1, D_MODEL)]
        extra_specs = [pl.BlockSpec((1, D_MODEL), lambda i: (0, 0))]
        out_shape = (jax.ShapeDtypeStruct((N_CTX_TOK, D_MODEL), F32),
                     jax.ShapeDtypeStruct((N_LAT_TOK, D_MODEL), F32))
        out_specs = (
            pl.BlockSpec((tm, D_MODEL), lambda i: (jnp.minimum(i, n_ctx_tiles - 1), 0)),
            pl.BlockSpec((tm, D_MODEL), lambda i: (jnp.maximum(i - n_ctx_tiles, 0), 0)))
    else:
        tn = 6 * D_MODEL // n_tiles
        extra_in = list(next_mod_inputs)
        extra_specs = [
            pl.BlockSpec((N_GROUPS_PAD, D_MODEL), lambda i: (0, 0)),
            pl.BlockSpec((None, D_MODEL, tn), lambda i: (layer + 1, 0, i)),
            pl.BlockSpec((None, 1, tn), lambda i: (layer + 1, 0, i)),
        ]
        out_shape = (jax.ShapeDtypeStruct((N_TOK, D_MODEL), F32),
                     jax.ShapeDtypeStruct((N_GROUPS_PAD, 6 * D_MODEL), F32))
        out_specs = (row_spec, pl.BlockSpec((N_GROUPS_PAD, tn), lambda i: (0, i)))
    return pl.pallas_call(
        functools.partial(_post_mixer_kernel, layer=layer, wo_idx=wo_idx, tm=tm, final=final),
        out_shape=out_shape,
        grid=(N_TOK // tm,),
        in_specs=[
            row_spec,
            row_spec,
            pl.BlockSpec((None, 1, D_MODEL), lambda i: (layer, 0, 0)),
            pl.BlockSpec((None, 1, 6 * D_MODEL), lambda i: (_group_of_tile(i, tm), 0, 0)),
        ] + extra_specs + [hbm, hbm, hbm],
        out_specs=out_specs,
        scratch_shapes=[
            pltpu.VMEM((D_MODEL, D_MODEL), BF16),
            pltpu.VMEM((N_FFN_CHUNKS, D_MODEL, FFN_CHUNK), BF16),
            pltpu.VMEM((N_FFN_CHUNKS, D_MODEL, FFN_CHUNK), BF16),
            pltpu.VMEM((FFN_HIDDEN, D_MODEL), BF16),
            pltpu.VMEM((tm, FFN_HIDDEN), BF16),
            pltpu.VMEM((2, 2, D_MODEL, FFN_CHUNK), F32),
            pltpu.VMEM((2, FFN_CHUNK, D_MODEL), F32),
            pltpu.SemaphoreType.DMA((2, 2)),
            pltpu.SemaphoreType.DMA((2,)),
        ],
        compiler_params=_params(1),
        name="post_mixer",
    )(x, mix, g_all.reshape(DEPTH, 1, D_MODEL), mod_l, *extra_in, w_out_all, w_gate_up, w_down)


def kernel(x_prompt, x_sample, cache_a_k, cache_a_v, cache_c_k, cache_c_v, c, c_ctx, w_mod, b_mod, norm_mix, norm_ffn, w_in_ab, rpb_a, w_pool, pool_scale, w_out_ab, w_in_c, sink_c, w_out_c, w_gate_up, w_down, norm_final):
    xs = (x_prompt.reshape(N_CTX_TOK, D_MODEL), x_sample.reshape(N_LAT_TOK, D_MODEL))
    cond8 = jnp.concatenate(
        [c_ctx[None], c, jnp.zeros((N_GROUPS_PAD - 1 - DEC_BATCH, D_MODEL), F32)], axis=0)
    b_mod3 = b_mod.reshape(DEPTH, 1, 6 * D_MODEL)
    mod, bias_tiles = _modulation0_and_bias(cond8, w_mod, b_mod3, rpb_a)

    n_ab = cache_a_k.shape[1]
    n_c = cache_c_k.shape[1]
    cache_a_k, cache_a_v, cache_c_k, cache_c_v = (
        jnp.transpose(t, (0, 1, 3, 4, 2)) for t in (cache_a_k, cache_a_v, cache_c_k, cache_c_v))
    rope_tables = _rope_tables()

    new_a = []
    new_c = []
    for l in range(DEPTH):
        mod_l = mod.reshape(N_GROUPS_PAD, 1, 6 * D_MODEL)
        if l % 2 == 0:
            i = l // 2
            qkv, *rest = _inproj(xs, norm_mix, mod_l, l, w_in_ab, i, A_HEADS, A_WIDTH, n_ab,
                                 new_a)
            new_a, u = rest[:2], rest[2]
            if len(xs) == 2:
                xs = (rest[3],)
            mix = _ctx_attn_a(qkv, u, w_pool, pool_scale, i)
            mix = _lat_pool(u, w_pool, pool_scale, i, mix)
            mix = _lat_attn_a(qkv, cache_a_k, cache_a_v, bias_tiles, i, mix)
            w_out, wo_idx = w_out_ab, i
        else:
            j = l // 2
            qkv, *new_c = _inproj(xs, norm_mix, mod_l, l, w_in_c, j, C_KV_HEADS, C_Q_WIDTH, n_c,
                                  new_c, rope_tables)
            mix = _ctx_attn_c(qkv, sink_c, j)
            mix = _lat_attn_c(qkv, cache_c_k, cache_c_v, sink_c, j, mix)
            w_out, wo_idx = w_out_c, j
        if l + 1 < DEPTH:
            x, mod = _post_mixer(xs[0], mix, norm_ffn, mod_l, l, w_out, wo_idx, w_gate_up,
                                 w_down, next_mod_inputs=(cond8, w_mod, b_mod3))
            xs = (x,)
        else:
            y_ctx, y_lat = _post_mixer(xs[0], mix, norm_ffn, mod_l, l, w_out, wo_idx, w_gate_up,
                                       w_down, norm_final)

    new_caches = [jnp.transpose(t, (0, 1, 4, 2, 3)) for t in (*new_a, *new_c)]
    return (y_ctx.reshape(BATCH, SEQ, D_MODEL), y_lat.reshape(DEC_BATCH, DEC_SEQ, D_MODEL),
            *new_caches)
```

```python
import functools

import jax
import jax.numpy as jnp
import numpy as np
from jax import lax
from jax.experimental import pallas as pl
from jax.experimental.pallas import tpu as pltpu

D_MODEL = 1024
BATCH = 16
SEQ = 256
DEPTH = 4
DEC_BATCH = 2
DEC_SEQ = 1024
PAST_LEN = 512
GRID_W = 64
HEAD_DIM = 64
A_WIDTH = 512
A_HEADS = 8
B_WIDTH = 512
POOL_WINDOWS = (2, 4, 8, 16)
B_GROUP_DIM = 128
NA_ROWS = 8
NA_COLS = 16
C_HEADS = 16
C_KV_HEADS = 4
C_GROUP = C_HEADS // C_KV_HEADS
C_Q_WIDTH = 1024
C_KV_WIDTH = 256
C_BLOCK = 128
FFN_HIDDEN = 2816
ROPE_BASE = 10000.0
EPS = 1e-6
NEG_INF = -1e30

N_CTX_TOK = BATCH * SEQ
N_LAT_TOK = DEC_BATCH * DEC_SEQ
N_TOK = N_CTX_TOK + N_LAT_TOK
GRID_ROWS = DEC_SEQ // GRID_W
N_GROUPS_PAD = 8

VMEM_LIMIT = 56 * 1024 * 1024

F32 = jnp.float32
BF16 = jnp.bfloat16


def _params(n_axes):
    return pltpu.CompilerParams(dimension_semantics=("arbitrary",) * n_axes,
                                vmem_limit_bytes=VMEM_LIMIT)


def _group_of_tile(i, tm):
    row0 = i * tm
    return jnp.where(row0 < N_CTX_TOK, 0, 1 + (row0 - N_CTX_TOK) // DEC_SEQ)


def _dot_nt(a, b):
    return lax.dot_general(a, b, (((1,), (1,)), ((), ())), preferred_element_type=F32)


def _dot(a, b):
    return jnp.dot(a, b, preferred_element_type=F32)


def _dot_tn(a, b):
    return lax.dot_general(a, b, (((0,), (0,)), ((), ())), preferred_element_type=F32)


def _softmax_pv_t(score_fns, sink_rows, values, s_refs, p_refs):
    n_groups, n_blocks = len(score_fns), len(s_refs)
    for g in range(n_groups):
        for b in range(n_blocks):
            s_refs[b][g] = score_fns[g][b]()
    inv_l = []
    for g in range(n_groups):
        m = sink_rows[g]
        for b in range(n_blocks):
            m = jnp.maximum(m, jnp.max(s_refs[b][g], axis=0, keepdims=True))
        l = jnp.exp2(sink_rows[g] - m)
        for b in range(n_blocks):
            p = jnp.exp2(s_refs[b][g] - m)
            l = l + jnp.sum(p, axis=0, keepdims=True)
            p_refs[b][g] = p.astype(BF16)
        inv_l.append(1.0 / l)
    outs = []
    for g in range(n_groups):
        o = None
        for b in range(n_blocks):
            pv = _dot_tn(values[g][b], p_refs[b][g])
            o = pv if o is None else o + pv
        outs.append(o * inv_l[g])
    return outs


def _modulation_block(cond_ref, w_ref, b_ref):
    c = cond_ref[...]
    s = c / (1.0 + jnp.exp(-c))
    return _dot(s.astype(BF16), w_ref[...].astype(BF16)) + b_ref[...]


def _mod_bias_kernel(rpb_ref, cond_ref, w_ref, b_ref, o_ref, bias_ref, *, heads_per_step):
    o_ref[...] = _modulation_block(cond_ref, w_ref, b_ref)
    pair0 = pl.program_id(0) * heads_per_step
    for t in range(heads_per_step):
        _write_bias_tiles(rpb_ref, (pair0 + t) * (N_DROW * N_DCOL), bias_ref.at[t])


def _modulation0_and_bias(cond8, w_mod, b_mod3, rpb_a):
    n_col_blocks = 4
    tn = 6 * D_MODEL // n_col_blocks
    n_bias_layers = rpb_a.shape[0]
    heads_per_step = n_bias_layers * A_HEADS // n_col_blocks
    steps_per_layer = A_HEADS // heads_per_step
    return pl.pallas_call(
        functools.partial(_mod_bias_kernel, heads_per_step=heads_per_step),
        out_shape=(
            jax.ShapeDtypeStruct((N_GROUPS_PAD, 6 * D_MODEL), F32),
            jax.ShapeDtypeStruct(
                (n_bias_layers, A_HEADS, N_BIAS_TILES, GRID_W, 2 * GRID_W), F32)),
        grid=(n_col_blocks,),
        in_specs=[
            pl.BlockSpec(memory_space=pltpu.SMEM),
            pl.BlockSpec((N_GROUPS_PAD, D_MODEL), lambda j: (0, 0)),
            pl.BlockSpec((None, D_MODEL, tn), lambda j: (0, 0, j)),
            pl.BlockSpec((None, 1, tn), lambda j: (0, 0, j)),
        ],
        out_specs=(
            pl.BlockSpec((N_GROUPS_PAD, tn), lambda j: (0, j)),
            pl.BlockSpec((None, heads_per_step, N_BIAS_TILES, GRID_W, 2 * GRID_W),
                         lambda j: (j // steps_per_layer, j % steps_per_layer, 0, 0, 0))),
        compiler_params=_params(1),
        name="modulation_bias",
    )(rpb_a.reshape(-1), cond8, w_mod, b_mod3)


def _norm_mod(x, g, mod_ref, shift_idx):
    var = jnp.mean(x * x, axis=-1, keepdims=True)
    y = x * lax.rsqrt(var + EPS) * g
    shift = mod_ref[:, shift_idx * D_MODEL:(shift_idx + 1) * D_MODEL]
    scale = mod_ref[:, (shift_idx + 1) * D_MODEL:(shift_idx + 2) * D_MODEL]
    return y * (1.0 + scale) + shift


QKV_WIDTH = 3 * A_WIDTH
LOG2_E = 1.4426950408889634
Q_SCALE = HEAD_DIM ** -0.5 * LOG2_E
PROJ_CHUNK = 4 * HEAD_DIM
ROPE_TABLE_WIDTH = PROJ_CHUNK


def _rope(x, cos, sin_signed):
    n = x.shape[-1]
    lane = lax.broadcasted_iota(jnp.int32, x.shape, x.ndim - 1)
    first = (lane % 32) < 16
    partner = jnp.where(first, pltpu.roll(x, n - 16, axis=x.ndim - 1),
                        pltpu.roll(x, 16, axis=x.ndim - 1))
    return x * cos + partner * sin_signed


def _inproj_kernel(*refs, tm, n_heads, q_width, n_prev, split_x, rope, has_u):
    n_x = 2 if split_x else 1
    g_ref, mod_ref, w_ref = refs[n_x:n_x + 3]
    n_in = n_x + 3 + (2 if rope else 0) + n_prev
    qkv_ref, ck_ref, cv_ref = refs[n_in:n_in + 3]
    extra_out = refs[n_in + 3:-3]
    wbf_ref, h_ref, res_ref = refs[-3:]
    i = pl.program_id(0)
    is_ctx = i < N_CTX_TOK // tm
    kv_width = (QKV_WIDTH - q_width) // 2
    k_col, v_col = q_width, q_width + kv_width
    n_out = res_ref.shape[1]
    chunk = PROJ_CHUNK

    @pl.when(i == 0)
    def _():
        wbf_ref[...] = w_ref[...].astype(BF16)

    def tile(ctx):
        if split_x:
            x = (refs[0] if ctx else refs[1])[...]
            extra_out[-1][...] = x
        else:
            x = refs[0][...]
        h_ref[...] = _norm_mod(x, g_ref[...], mod_ref, 0).astype(BF16)

        def matmul(c0):
            cols = slice(c0, c0 + chunk)
            res_ref[:, cols] = _dot(h_ref[...], wbf_ref[:, cols])

        def epilogue(c0):
            cols = slice(c0, c0 + chunk)
            if c0 >= QKV_WIDTH:
                extra_out[0][:, c0 - QKV_WIDTH:c0 - QKV_WIDTH + chunk] = res_ref[:, cols]
                return
            r = res_ref[:, cols]
            if rope and not ctx and c0 < v_col:
                cos_ref, sin_ref = refs[n_x + 3:n_x + 5]
                r = _rope(r, cos_ref[...], sin_ref[...])
            if c0 < q_width:
                r = r * Q_SCALE
            qkv_ref[:, cols] = r.astype(BF16)
            if ctx and c0 >= k_col:
                c_ref, col0 = (ck_ref, k_col) if c0 < v_col else (cv_ref, v_col)
                heads = slice((c0 - col0) // HEAD_DIM, (c0 - col0 + chunk) // HEAD_DIM)
                for req in range(tm // SEQ):
                    blk_t = res_ref[req * SEQ:(req + 1) * SEQ, cols].T
                    c_ref[req, heads] = blk_t.reshape(chunk // HEAD_DIM, HEAD_DIM, SEQ)

        for c0 in range(0, n_out, chunk):
            matmul(c0)
            epilogue(c0)

    pl.when(is_ctx)(lambda: tile(True))
    pl.when(jnp.logical_not(is_ctx))(lambda: tile(False))


def _inproj(xs, g_all, mod_l, layer, w_all, w_idx, n_heads, q_width, n_slots, prev_caches,
            rope_tables=None):
    split_x = len(xs) == 2
    tm = 512 if split_x else 1024
    n_out = w_all.shape[2]
    n_ctx_tiles = N_CTX_TOK // tm
    rope = rope_tables is not None
    has_u = n_out > QKV_WIDTH
    cache_shape = jax.ShapeDtypeStruct((BATCH, n_slots, n_heads, HEAD_DIM, SEQ), F32)
    cache_spec = pl.BlockSpec(
        (tm // SEQ, None, n_heads, HEAD_DIM, SEQ),
        lambda i: (jnp.minimum(i, n_ctx_tiles - 1), w_idx, 0, 0, 0))
    row_spec = pl.BlockSpec((tm, D_MODEL), lambda i: (i, 0))
    if split_x:
        x_specs = [
            pl.BlockSpec((tm, D_MODEL), lambda i: (jnp.minimum(i, n_ctx_tiles - 1), 0)),
            pl.BlockSpec((tm, D_MODEL), lambda i: (jnp.maximum(i - n_ctx_tiles, 0), 0)),
        ]
    else:
        x_specs = [row_spec]
    out_shape = [jax.ShapeDtypeStruct((N_TOK, QKV_WIDTH), BF16), cache_shape, cache_shape]
    out_specs = [pl.BlockSpec((tm, QKV_WIDTH), lambda i: (i, 0)), cache_spec, cache_spec]
    if has_u:
        out_shape.append(jax.ShapeDtypeStruct((N_TOK, n_out - QKV_WIDTH), F32))
        out_specs.append(pl.BlockSpec((tm, n_out - QKV_WIDTH), lambda i: (i, 0)))
    if split_x:
        out_shape.append(jax.ShapeDtypeStruct((N_TOK, D_MODEL), F32))
        out_specs.append(row_spec)
    rope_in, rope_specs = [], []
    if rope:
        tiles_per_seq = DEC_SEQ // tm
        rope_spec = pl.BlockSpec(
            (tm, ROPE_TABLE_WIDTH),
            lambda i: (jnp.maximum(i - n_ctx_tiles, 0) % tiles_per_seq, 0))
        rope_in, rope_specs = list(rope_tables), [rope_spec, rope_spec]
    n_prev = len(prev_caches)
    n_before = len(xs) + 3 + len(rope_in)
    return pl.pallas_call(
        functools.partial(_inproj_kernel, tm=tm, n_heads=n_heads, q_width=q_width,
                          n_prev=n_prev, split_x=split_x, rope=rope, has_u=has_u),
        out_shape=out_shape,
        grid=(N_TOK // tm,),
        in_specs=x_specs + [
            pl.BlockSpec((None, 1, D_MODEL), lambda i: (layer, 0, 0)),
            pl.BlockSpec((None, 1, 6 * D_MODEL), lambda i: (_group_of_tile(i, tm), 0, 0)),
            pl.BlockSpec((None, D_MODEL, n_out), lambda i: (w_idx, 0, 0),
                         pipeline_mode=pl.Buffered(1)),
        ] + rope_specs + [pl.BlockSpec(memory_space=pl.ANY)] * n_prev,
        out_specs=out_specs,
        scratch_shapes=[pltpu.VMEM((D_MODEL, n_out), BF16), pltpu.VMEM((tm, D_MODEL), BF16),
                        pltpu.VMEM((tm, n_out), F32)],
        input_output_aliases={n_before + k: 1 + k for k in range(n_prev)},
        compiler_params=_params(1),
        name="inproj",
    )(*xs, g_all.reshape(DEPTH, 1, D_MODEL), mod_l, w_all, *rope_in, *prev_caches)


def _ctx_attn_a_kernel(p_ref, u_ref, wp_ref, ps_ref, o_ref, s_ref, p_scr):
    for h in range(A_HEADS):
        c0 = h * HEAD_DIM
        s_ref[h] = _dot_nt(p_ref[:, c0:c0 + HEAD_DIM],
                           p_ref[:, A_WIDTH + c0:A_WIDTH + c0 + HEAD_DIM])
    _pool_rows(u_ref, wp_ref, ps_ref, o_ref, n=SEQ, out_col0=A_WIDTH)
    inv_l = []
    for h in range(A_HEADS):
        s = s_ref[h]
        p = jnp.exp2(s - jnp.max(s, axis=-1, keepdims=True))
        inv_l.append(1.0 / jnp.sum(p, axis=-1, keepdims=True))
        p_scr[h] = p.astype(BF16)
    left = lax.broadcasted_iota(jnp.int32, (SEQ, 2 * HEAD_DIM), 1) < HEAD_DIM
    for hp in range(A_HEADS // 2):
        c0 = hp * 2 * HEAD_DIM
        v_pair = p_ref[:, 2 * A_WIDTH + c0:2 * A_WIDTH + c0 + 2 * HEAD_DIM]
        zero = jnp.zeros_like(v_pair)
        v_bd = jnp.concatenate([jnp.where(left, v_pair, zero), jnp.where(left, zero, v_pair)],
                               axis=0)
        p_pair = jnp.concatenate([p_scr[2 * hp], p_scr[2 * hp + 1]], axis=1)
        scale = jnp.where(left, inv_l[2 * hp], inv_l[2 * hp + 1])
        o_ref[:, c0:c0 + 2 * HEAD_DIM] = _dot(p_pair, v_bd) * scale


def _ctx_attn_a(qkv, u, w_pool, pool_scale, i_layer):
    return pl.pallas_call(
        _ctx_attn_a_kernel,
        out_shape=jax.ShapeDtypeStruct((N_TOK, D_MODEL), F32),
        grid=(BATCH,),
        in_specs=[pl.BlockSpec((SEQ, QKV_WIDTH), lambda b: (b, 0)),
                  pl.BlockSpec((SEQ, B_WIDTH), lambda b: (b, 0))] + _pool_specs(i_layer),
        out_specs=pl.BlockSpec((SEQ, D_MODEL), lambda b: (b, 0)),
        scratch_shapes=[pltpu.VMEM((A_HEADS, SEQ, SEQ), F32),
                        pltpu.VMEM((A_HEADS, SEQ, SEQ), BF16)],
        compiler_params=_params(1),
        name="ctx_attn_a",
    )(qkv, u, w_pool, pool_scale.reshape(-1, 1, B_WIDTH))


N_DROW = 2 * NA_ROWS - 1
N_DCOL = 2 * NA_COLS - 1
N_BIAS_TILES = 16
BIAS_TILE_LEFT_PAD = 14
BIAS_TILE_RIGHT_PAD = 15
MID_DROW = NA_ROWS - 1 - NA_ROWS // 2


def _write_bias_tiles(rpb_ref, base, o_ref):
    qi = lax.broadcasted_iota(jnp.int32, (GRID_W, 2 * GRID_W), 0)
    lane = lax.broadcasted_iota(jnp.int32, (GRID_W, 2 * GRID_W), 1)
    right = lane >= GRID_W
    kc = jnp.where(right, lane - GRID_W, lane)
    qstart = jnp.clip(qi - NA_COLS // 2, 0, GRID_W - NA_COLS)
    valid = (kc >= qstart) & (kc < qstart + NA_COLS)

    offs = lax.broadcasted_iota(jnp.int32, (1, 2 * GRID_W), 1) & (GRID_W - 1)
    shift = 2 * GRID_W - (NA_COLS - 1)
    rows = []
    for dr in range(N_DROW):
        vec = jnp.zeros((1, 2 * GRID_W), F32)
        for d in range(N_DCOL):
            vec = jnp.where(offs == d, rpb_ref[base + dr * N_DCOL + d] * LOG2_E, vec)
        rows.append(pltpu.roll(jnp.broadcast_to(vec, (GRID_W, 2 * GRID_W)), shift, axis=1,
                               stride=1, stride_axis=0))

    for t in range(N_DROW - 1):
        o_ref[t] = jnp.where(valid, jnp.where(right, rows[t + 1], rows[t]), NEG_INF)
    o_ref[BIAS_TILE_LEFT_PAD] = jnp.where(valid & right, rows[MID_DROW], NEG_INF)
    o_ref[BIAS_TILE_RIGHT_PAD] = jnp.where(valid & jnp.logical_not(right),
                                           rows[MID_DROW + NA_ROWS - 1], NEG_INF)


NA_GROUP_ROWS = 4


def _na_window(r):
    start = min(max(r - NA_ROWS // 2, 0), GRID_ROWS - NA_ROWS)
    first_drow = start - r + NA_ROWS - 1
    if start % 2 == 0:
        return start, [first_drow + 2 * p for p in range(NA_ROWS // 2)]
    assert first_drow == MID_DROW
    inner = [first_drow + 1 + 2 * p for p in range(NA_ROWS // 2 - 1)]
    return start - 1, [BIAS_TILE_LEFT_PAD] + inner + [BIAS_TILE_RIGHT_PAD]


def _lat_attn_a_kernel(q_ref, k_ref, v_ref, kc_ref, vc_ref, bias_ref, mix_ref, o_ref,
                       s_loc, s_ctx, p_loc, p_ctx):
    del mix_ref
    hp = pl.program_id(1)

    @pl.when((pl.program_id(0) == 0) & (hp == 0))
    def _():
        p_loc[...] = jnp.zeros_like(p_loc)

    groups = []
    for g0 in range(0, GRID_ROWS, NA_GROUP_ROWS):
        windows = [_na_window(r) for r in range(g0, g0 + NA_GROUP_ROWS)]
        lo = min(first for first, _ in windows)
        hi = max(first + 2 * len(tiles) for first, tiles in windows)
        groups.append((slice(g0 * GRID_W, (g0 + NA_GROUP_ROWS) * GRID_W),
                       slice(lo * GRID_W, hi * GRID_W)))

    for hh in range(2):
        c0 = hh * HEAD_DIM
        for q_rows, keys in groups:
            s_loc[hh, q_rows, keys] = _dot_nt(q_ref[q_rows, c0:c0 + HEAD_DIM],
                                              k_ref[keys, c0:c0 + HEAD_DIM])
        s_ctx[hh] = _dot(q_ref[:, c0:c0 + HEAD_DIM],
                         kc_ref[hh].astype(BF16))
    inv_l = [[], []]
    for hh in range(2):
        for r in range(GRID_ROWS):
            rows = slice(r * GRID_W, (r + 1) * GRID_W)
            first_row, tiles = _na_window(r)
            cols = slice(first_row * GRID_W, (first_row + 2 * len(tiles)) * GRID_W)
            bias = jnp.concatenate([bias_ref[hh, t] for t in tiles], axis=1)
            sc = s_ctx[hh, rows, :]
            sl = s_loc[hh, rows, cols] + bias
            m = jnp.maximum(jnp.max(sc, axis=-1, keepdims=True),
                            jnp.max(sl, axis=-1, keepdims=True))
            pc = jnp.exp2(sc - m)
            pw = jnp.exp2(sl - m)
            inv_l[hh].append(1.0 / (jnp.sum(pc, axis=-1, keepdims=True)
                                    + jnp.sum(pw, axis=-1, keepdims=True)))
            p_ctx[hh, rows, :] = pc.astype(BF16)
            p_loc[hh, rows, cols] = pw.astype(BF16)
    for hh in range(2):
        c0 = hh * HEAD_DIM
        o_loc = jnp.concatenate(
            [_dot(p_loc[hh, q_rows, keys], v_ref[keys, c0:c0 + HEAD_DIM])
             for q_rows, keys in groups], axis=0)
        o = _dot_nt(p_ctx[hh], vc_ref[hh].astype(BF16)) + o_loc
        o_ref[:, c0:c0 + HEAD_DIM] = o * jnp.concatenate(inv_l[hh], axis=0)


def _lat_attn_a(qkv, cache_k, cache_v, bias_tiles, i_layer, mix):
    row0 = N_CTX_TOK // DEC_SEQ
    pair = 2 * HEAD_DIM
    k_col0 = A_WIDTH // pair
    v_col0 = 2 * A_WIDTH // pair
    return pl.pallas_call(
        _lat_attn_a_kernel,
        out_shape=jax.ShapeDtypeStruct((N_TOK, D_MODEL), F32),
        grid=(DEC_BATCH, A_HEADS // 2),
        in_specs=[
            pl.BlockSpec((DEC_SEQ, pair), lambda b, hp: (row0 + b, hp)),
            pl.BlockSpec((DEC_SEQ, pair), lambda b, hp: (row0 + b, k_col0 + hp)),
            pl.BlockSpec((DEC_SEQ, pair), lambda b, hp: (row0 + b, v_col0 + hp)),
            pl.BlockSpec((None, None, 2, HEAD_DIM, PAST_LEN),
                         lambda b, hp: (b, i_layer, hp, 0, 0)),
            pl.BlockSpec((None, None, 2, HEAD_DIM, PAST_LEN),
                         lambda b, hp: (b, i_layer, hp, 0, 0)),
            pl.BlockSpec((None, 2, N_BIAS_TILES, GRID_W, 2 * GRID_W),
                         lambda b, hp: (i_layer, hp, 0, 0, 0)),
            pl.BlockSpec(memory_space=pl.ANY),
        ],
        out_specs=pl.BlockSpec((DEC_SEQ, pair), lambda b, hp: (row0 + b, hp)),
        scratch_shapes=[pltpu.VMEM((2, DEC_SEQ, DEC_SEQ), F32),
                        pltpu.VMEM((2, DEC_SEQ, PAST_LEN), F32),
                        pltpu.VMEM((2, DEC_SEQ, DEC_SEQ), BF16),
                        pltpu.VMEM((2, DEC_SEQ, PAST_LEN), BF16)],
        input_output_aliases={6: 0},
        compiler_params=_params(2),
        name="lat_attn_a",
    )(qkv, qkv, qkv, cache_k, cache_v, bias_tiles, mix)


def _pool_rows(u_ref, wp_ref, ps_ref, o_ref, *, n, out_col0):
    rows = n
    t = lax.broadcasted_iota(jnp.int32, (rows, B_GROUP_DIM), 0)

    def earlier(x, k):
        return jnp.where(t >= k, pltpu.roll(x, k, axis=0), 0.0)

    def later(x, k):
        return jnp.where(t < n - k, pltpu.roll(x, rows - k, axis=0), 0.0)

    tokens_before = t.astype(F32)
    tokens_from = (n - t).astype(F32)

    for g, w in enumerate(POOL_WINDOWS):
        c0 = g * B_GROUP_DIM
        half = w // 2
        u = u_ref[:, c0:c0 + B_GROUP_DIM]
        before, after = u, u
        k = 1
        while k < half:
            before = before + earlier(before, k)
            after = after + later(after, k)
            k *= 2
        total = earlier(before, 1) + after
        count = jnp.minimum(tokens_before, half) + jnp.minimum(tokens_from, half)
        pooled = total / count - u
        y = _dot(pooled.astype(BF16), wp_ref[g].astype(BF16))
        o_ref[:, out_col0 + c0:out_col0 + c0 + B_GROUP_DIM] = (
            y * ps_ref[:, c0:c0 + B_GROUP_DIM])


def _pool_specs(i_layer):
    zeros = (0,) * 3
    return [
        pl.BlockSpec((None, len(POOL_WINDOWS), B_GROUP_DIM, B_GROUP_DIM),
                     lambda *_: (i_layer,) + zeros),
        pl.BlockSpec((None, 1, B_WIDTH), lambda *_: (i_layer, 0, 0)),
    ]


def _lat_pool_kernel(u_ref, wp_ref, ps_ref, mix_ref, o_ref):
    del mix_ref
    _pool_rows(u_ref, wp_ref, ps_ref, o_ref, n=DEC_SEQ, out_col0=0)


def _lat_pool(u, w_pool, pool_scale, i_layer, mix):
    row0 = N_CTX_TOK // DEC_SEQ
    return pl.pallas_call(
        _lat_pool_kernel,
        out_shape=jax.ShapeDtypeStruct((N_TOK, D_MODEL), F32),
        grid=(DEC_BATCH,),
        in_specs=[pl.BlockSpec((DEC_SEQ, B_WIDTH), lambda b: (row0 + b, 0))]
        + _pool_specs(i_layer) + [pl.BlockSpec(memory_space=pl.ANY)],
        out_specs=pl.BlockSpec((DEC_SEQ, B_WIDTH), lambda b: (row0 + b, 1)),
        input_output_aliases={3: 0},
        compiler_params=_params(1),
        name="pool_mixer",
    )(u, w_pool, pool_scale.reshape(-1, 1, B_WIDTH), mix)


def _sink_row(sink_ref, head0, queries_per_head):
    lane = lax.broadcasted_iota(jnp.int32, (1, C_GROUP * queries_per_head), 1)
    row = jnp.full((1, C_GROUP * queries_per_head), sink_ref[head0], F32)
    for g in range(1, C_GROUP):
        row = jnp.where(lane >= g * queries_per_head, sink_ref[head0 + g], row)
    return row * LOG2_E


CTX_REQS_PER_STEP = 4


def _ctx_attn_c_kernel(sink_ref, p_ref, o_ref, s_ref, p_scr, *, sink0):
    for req in range(CTX_REQS_PER_STEP):
        rows = slice(req * SEQ, (req + 1) * SEQ)

        def scores(kk):
            q = jnp.concatenate(
                [p_ref[rows, h * HEAD_DIM:(h + 1) * HEAD_DIM]
                 for h in range(kk * C_GROUP, (kk + 1) * C_GROUP)], axis=0)
            k0 = C_Q_WIDTH + kk * HEAD_DIM
            return _dot_nt(p_ref[rows, k0:k0 + HEAD_DIM], q)

        v0 = C_Q_WIDTH + C_KV_WIDTH
        outs = _softmax_pv_t(
            [[functools.partial(scores, kk)] for kk in range(C_KV_HEADS)],
            [_sink_row(sink_ref, sink0 + kk * C_GROUP, SEQ) for kk in range(C_KV_HEADS)],
            [[p_ref[rows, v0 + kk * HEAD_DIM:v0 + (kk + 1) * HEAD_DIM]]
             for kk in range(C_KV_HEADS)],
            [s_ref], [p_scr])
        outs_t = [o_t[:, g * SEQ:(g + 1) * SEQ] for o_t in outs for g in range(C_GROUP)]
        o_ref[rows, :] = jnp.concatenate(outs_t, axis=0).T


def _ctx_attn_c(qkv, sink_all, j_layer):
    return pl.pallas_call(
        functools.partial(_ctx_attn_c_kernel, sink0=j_layer * C_HEADS),
        out_shape=jax.ShapeDtypeStruct((N_TOK, D_MODEL), F32),
        grid=(BATCH // CTX_REQS_PER_STEP,),
        in_specs=[
            pl.BlockSpec(memory_space=pltpu.SMEM),
            pl.BlockSpec((CTX_REQS_PER_STEP * SEQ, QKV_WIDTH), lambda b: (b, 0)),
        ],
        out_specs=pl.BlockSpec((CTX_REQS_PER_STEP * SEQ, C_Q_WIDTH), lambda b: (b, 0)),
        scratch_shapes=[pltpu.VMEM((C_KV_HEADS, SEQ, C_GROUP * SEQ), F32),
                        pltpu.VMEM((C_KV_HEADS, SEQ, C_GROUP * SEQ), BF16)],
        compiler_params=_params(1),
        name="ctx_attn_c",
    )(sink_all.reshape(-1), qkv)


def _lat_attn_c_kernel(sink_ref, q_ref, k_ref, v_ref, kc_ref, vc_ref, mix_ref, o_ref,
                       kctx_ref, vctx_ref, sc_ref, sb_ref, pc_ref, pb_ref, *, sink0):
    del mix_ref
    j = pl.program_id(1)
    n_blocks = DEC_SEQ // C_BLOCK

    @pl.when(j == 0)
    def _():
        for kk in range(C_KV_HEADS):
            c0 = kk * HEAD_DIM
            kctx_ref[:, c0:c0 + HEAD_DIM] = kc_ref[kk].T.astype(BF16)
            vctx_ref[:, c0:c0 + HEAD_DIM] = vc_ref[kk].T.astype(BF16)

    n_q = C_GROUP * C_BLOCK
    qi = lax.broadcasted_iota(jnp.int32, (C_BLOCK, n_q), 1) % C_BLOCK
    jl = lax.broadcasted_iota(jnp.int32, (C_BLOCK, n_q), 0)
    valid_prev = (jl >= qi) & (j > 0)
    valid_next = (jl <= qi) & (j < n_blocks - 1)
    rows_prev = pl.ds(pl.multiple_of(jnp.maximum(j - 1, 0) * C_BLOCK, C_BLOCK), C_BLOCK)
    rows_cur = pl.ds(pl.multiple_of(j * C_BLOCK, C_BLOCK), C_BLOCK)
    rows_next = pl.ds(pl.multiple_of(jnp.minimum(j + 1, n_blocks - 1) * C_BLOCK, C_BLOCK),
                      C_BLOCK)

    def stacked_q(kk):
        return jnp.concatenate(
            [q_ref[:, h * HEAD_DIM:(h + 1) * HEAD_DIM]
             for h in range(kk * C_GROUP, (kk + 1) * C_GROUP)], axis=0)

    def band(ref, kk):
        cols = slice(kk * HEAD_DIM, (kk + 1) * HEAD_DIM)
        return jnp.concatenate(
            [ref[rows_prev, cols], ref[rows_cur, cols], ref[rows_next, cols]], axis=0)

    def ctx_scores(kk):
        return _dot_nt(kctx_ref[:, kk * HEAD_DIM:(kk + 1) * HEAD_DIM], stacked_q(kk))

    def band_scores(kk):
        s_raw = _dot_nt(band(k_ref, kk), stacked_q(kk))
        return jnp.concatenate(
            [jnp.where(valid_prev, s_raw[:C_BLOCK], NEG_INF),
             s_raw[C_BLOCK:2 * C_BLOCK],
             jnp.where(valid_next, s_raw[2 * C_BLOCK:], NEG_INF)], axis=0)

    kv_heads = range(C_KV_HEADS)
    outs = _softmax_pv_t(
        [[functools.partial(ctx_scores, kk), functools.partial(band_scores, kk)]
         for kk in kv_heads],
        [_sink_row(sink_ref, sink0 + kk * C_GROUP, C_BLOCK) for kk in kv_heads],
        [[vctx_ref[:, kk * HEAD_DIM:(kk + 1) * HEAD_DIM], band(v_ref, kk)] for kk in kv_heads],
        [sc_ref, sb_ref], [pc_ref, pb_ref])
    outs_t = [o_t[:, g * C_BLOCK:(g + 1) * C_BLOCK] for o_t in outs for g in range(C_GROUP)]
    o_ref[...] = jnp.concatenate(outs_t, axis=0).T


def _lat_attn_c(qkv, cache_k, cache_v, sink, j_layer, mix):
    n_blocks = DEC_SEQ // C_BLOCK
    q_row0 = N_CTX_TOK // C_BLOCK
    kv_row0 = N_CTX_TOK // DEC_SEQ
    k_col = C_Q_WIDTH // C_KV_WIDTH
    n_q = C_GROUP * C_BLOCK
    return pl.pallas_call(
        functools.partial(_lat_attn_c_kernel, sink0=j_layer * C_HEADS),
        out_shape=jax.ShapeDtypeStruct((N_TOK, D_MODEL), F32),
        grid=(DEC_BATCH, n_blocks),
        in_specs=[
            pl.BlockSpec(memory_space=pltpu.SMEM),
            pl.BlockSpec((C_BLOCK, C_Q_WIDTH), lambda b, j: (q_row0 + b * n_blocks + j, 0)),
            pl.BlockSpec((DEC_SEQ, C_KV_WIDTH), lambda b, j: (kv_row0 + b, k_col)),
            pl.BlockSpec((DEC_SEQ, C_KV_WIDTH), lambda b, j: (kv_row0 + b, k_col + 1)),
            pl.BlockSpec((None, None, C_KV_HEADS, HEAD_DIM, PAST_LEN),
                         lambda b, j: (b, j_layer, 0, 0, 0)),
            pl.BlockSpec((None, None, C_KV_HEADS, HEAD_DIM, PAST_LEN),
                         lambda b, j: (b, j_layer, 0, 0, 0)),
            pl.BlockSpec(memory_space=pl.ANY),
        ],
        out_specs=pl.BlockSpec((C_BLOCK, C_Q_WIDTH),
                               lambda b, j: (q_row0 + b * n_blocks + j, 0)),
        input_output_aliases={6: 0},
        scratch_shapes=[pltpu.VMEM((PAST_LEN, C_KV_WIDTH), BF16),
                        pltpu.VMEM((PAST_LEN, C_KV_WIDTH), BF16),
                        pltpu.VMEM((C_KV_HEADS, PAST_LEN, n_q), F32),
                        pltpu.VMEM((C_KV_HEADS, 3 * C_BLOCK, n_q), F32),
                        pltpu.VMEM((C_KV_HEADS, PAST_LEN, n_q), BF16),
                        pltpu.VMEM((C_KV_HEADS, 3 * C_BLOCK, n_q), BF16)],
        compiler_params=_params(2),
        name="lat_attn_c",
    )(sink.reshape(-1), qkv, qkv, qkv, cache_k, cache_v, mix)


def _rope_tables():
    t = np.arange(DEC_SEQ)
    pos = np.stack([t // GRID_W, t % GRID_W], axis=-1).astype(np.float64)
    half = HEAD_DIM // 4
    inv = ROPE_BASE ** (-np.arange(half, dtype=np.float64) / half)
    ang = pos[:, :, None] * inv
    cos = np.cos(ang)
    sin = np.sin(ang)
    cos64 = np.stack([cos, cos], axis=2).reshape(DEC_SEQ, HEAD_DIM)
    sin64 = np.stack([-sin, sin], axis=2).reshape(DEC_SEQ, HEAD_DIM)
    reps = ROPE_TABLE_WIDTH // HEAD_DIM
    return (jnp.asarray(np.tile(cos64, (1, reps)), F32),
            jnp.asarray(np.tile(sin64, (1, reps)), F32))


FFN_CHUNK = 256
N_FFN_CHUNKS = FFN_HIDDEN // FFN_CHUNK
N_WO_PIECES = D_MODEL // FFN_CHUNK


def _post_mixer_kernel(*refs, layer, wo_idx, tm, final):
    x_ref, mix_ref, g_ref, mod_ref = refs[:4]
    n_in = 5 if final else 7
    wo_hbm, wgu_hbm, wd_hbm = refs[n_in:n_in + 3]
    n_out = 2
    out_refs = refs[n_in + 3:n_in + 3 + n_out]
    (wo_s, wg_s, wu_s, wd_s, act_s, stage_col, stage_row, sem_col,
     sem_row) = refs[n_in + 3 + n_out:]
    n_row_pieces = N_WO_PIECES + N_FFN_CHUNKS

    def col_copy(which, c):
        src = wgu_hbm.at[layer, :, pl.ds(which * FFN_HIDDEN + c * FFN_CHUNK, FFN_CHUNK)]
        return pltpu.make_async_copy(src, stage_col.at[which, c % 2], sem_col.at[which, c % 2])

    def row_copy(p):
        if p < N_WO_PIECES:
            src = wo_hbm.at[wo_idx, pl.ds(p * FFN_CHUNK, FFN_CHUNK), :]
        else:
            src = wd_hbm.at[layer, pl.ds((p - N_WO_PIECES) * FFN_CHUNK, FFN_CHUNK), :]
        return pltpu.make_async_copy(src, stage_row.at[p % 2], sem_row.at[p % 2])

    def take_row_piece(p, dst_ref, row0):
        if p + 1 < n_row_pieces:
            row_copy(p + 1).start()
        row_copy(p).wait()
        dst_ref[row0:row0 + FFN_CHUNK, :] = stage_row[p % 2].astype(BF16)

    def tile(load_weights):
        if load_weights:
            row_copy(0).start()
            for which in range(2):
                col_copy(which, 0).start()
            for p in range(N_WO_PIECES):
                take_row_piece(p, wo_s, p * FFN_CHUNK)
        if not final:
            out_refs[1][...] = _modulation_block(*refs[4:7])
        gate1 = mod_ref[:, 2 * D_MODEL:3 * D_MODEL]
        x1 = x_ref[...] + gate1 * _dot(mix_ref[...].astype(BF16), wo_s[...])
        h = _norm_mod(x1, g_ref[...], mod_ref, 3).astype(BF16)
        for c in range(N_FFN_CHUNKS):
            if load_weights:
                for which, dst in ((0, wg_s), (1, wu_s)):
                    if c + 1 < N_FFN_CHUNKS:
                        col_copy(which, c + 1).start()
                    col_copy(which, c).wait()
                    dst[c] = stage_col[which, c % 2].astype(BF16)
            gate = _dot(h, wg_s[c])
            up = _dot(h, wu_s[c])
            act = gate / (1.0 + jnp.exp(-gate)) * up
            act_s[:, c * FFN_CHUNK:(c + 1) * FFN_CHUNK] = act.astype(BF16)
            if load_weights:
                take_row_piece(N_WO_PIECES + c, wd_s, c * FFN_CHUNK)
        gate2 = mod_ref[:, 5 * D_MODEL:6 * D_MODEL]
        return x1 + gate2 * _dot(act_s[...], wd_s[...])

    def emit(x2):
        if not final:
            out_refs[0][...] = x2
            return
        var = jnp.mean(x2 * x2, axis=-1, keepdims=True)
        y = x2 * lax.rsqrt(var + EPS) * refs[4][...]
        is_ctx = pl.program_id(0) < N_CTX_TOK // tm

        @pl.when(is_ctx)
        def _():
            out_refs[0][...] = y

        @pl.when(jnp.logical_not(is_ctx))
        def _():
            out_refs[1][...] = y

    first = pl.program_id(0) == 0

    @pl.when(first)
    def _():
        emit(tile(True))

    @pl.when(jnp.logical_not(first))
    def _():
        emit(tile(False))


def _post_mixer(x, mix, g_all, mod_l, layer, w_out_all, wo_idx, w_gate_up, w_down,
                norm_final=None, next_mod_inputs=None):
    tm = 512
    n_tiles = N_TOK // tm
    n_ctx_tiles = N_CTX_TOK // tm
    final = norm_final is not None
    row_spec = pl.BlockSpec((tm, D_MODEL), lambda i: (i, 0))
    hbm = pl.BlockSpec(memory_space=pl.ANY)
    if final:
        extra_in = [norm_final.reshape(1, D_MODEL)]
        extra_specs = [pl.BlockSpec((1, D_MODEL), lambda i: (0, 0))]
        out_shape = (jax.ShapeDtypeStruct((N_CTX_TOK, D_MODEL), F32),
                     jax.ShapeDtypeStruct((N_LAT_TOK, D_MODEL), F32))
        out_specs = (
            pl.BlockSpec((tm, D_MODEL), lambda i: (jnp.minimum(i, n_ctx_tiles - 1), 0)),
            pl.BlockSpec((tm, D_MODEL), lambda i: (jnp.maximum(i - n_ctx_tiles, 0), 0)))
    else:
        tn = 6 * D_MODEL // n_tiles
        extra_in = list(next_mod_inputs)
        extra_specs = [
            pl.BlockSpec((N_GROUPS_PAD, D_MODEL), lambda i: (0, 0)),
            pl.BlockSpec((None, D_MODEL, tn), lambda i: (layer + 1, 0, i)),
            pl.BlockSpec((None, 1, tn), lambda i: (layer + 1, 0, i)),
        ]
        out_shape = (jax.ShapeDtypeStruct((N_TOK, D_MODEL), F32),
                     jax.ShapeDtypeStruct((N_GROUPS_PAD, 6 * D_MODEL), F32))
        out_specs = (row_spec, pl.BlockSpec((N_GROUPS_PAD, tn), lambda i: (0, i)))
    return pl.pallas_call(
        functools.partial(_post_mixer_kernel, layer=layer, wo_idx=wo_idx, tm=tm, final=final),
        out_shape=out_shape,
        grid=(N_TOK // tm,),
        in_specs=[
            row_spec,
            row_spec,
            pl.BlockSpec((None, 1, D_MODEL), lambda i: (layer, 0, 0)),
            pl.BlockSpec((None, 1, 6 * D_MODEL), lambda i: (_group_of_tile(i, tm), 0, 0)),
        ] + extra_specs + [hbm, hbm, hbm],
        out_specs=out_specs,
        scratch_shapes=[
            pltpu.VMEM((D_MODEL, D_MODEL), BF16),
            pltpu.VMEM((N_FFN_CHUNKS, D_MODEL, FFN_CHUNK), BF16),
            pltpu.VMEM((N_FFN_CHUNKS, D_MODEL, FFN_CHUNK), BF16),
            pltpu.VMEM((FFN_HIDDEN, D_MODEL), BF16),
            pltpu.VMEM((tm, FFN_HIDDEN), BF16),
            pltpu.VMEM((2, 2, D_MODEL, FFN_CHUNK), F32),
            pltpu.VMEM((2, FFN_CHUNK, D_MODEL), F32),
            pltpu.SemaphoreType.DMA((2, 2)),
            pltpu.SemaphoreType.DMA((2,)),
        ],
        compiler_params=_params(1),
        name="post_mixer",
    )(x, mix, g_all.reshape(DEPTH, 1, D_MODEL), mod_l, *extra_in, w_out_all, w_gate_up, w_down)


def kernel(x_prompt, x_sample, cache_a_k, cache_a_v, cache_c_k, cache_c_v, c, c_ctx, w_mod, b_mod, norm_mix, norm_ffn, w_in_ab, rpb_a, w_pool, pool_scale, w_out_ab, w_in_c, sink_c, w_out_c, w_gate_up, w_down, norm_final):
    xs = (x_prompt.reshape(N_CTX_TOK, D_MODEL), x_sample.reshape(N_LAT_TOK, D_MODEL))
    cond8 = jnp.concatenate(
        [c_ctx[None], c, jnp.zeros((N_GROUPS_PAD - 1 - DEC_BATCH, D_MODEL), F32)], axis=0)
    b_mod3 = b_mod.reshape(DEPTH, 1, 6 * D_MODEL)
    mod, bias_tiles = _modulation0_and_bias(cond8, w_mod, b_mod3, rpb_a)

    n_ab = cache_a_k.shape[1]
    n_c = cache_c_k.shape[1]
    cache_a_k, cache_a_v, cache_c_k, cache_c_v = (
        jnp.transpose(t, (0, 1, 3, 4, 2)) for t in (cache_a_k, cache_a_v, cache_c_k, cache_c_v))
    rope_tables = _rope_tables()

    new_a = []
    new_c = []
    for l in range(DEPTH):
        mod_l = mod.reshape(N_GROUPS_PAD, 1, 6 * D_MODEL)
        if l % 2 == 0:
            i = l // 2
            qkv, *rest = _inproj(xs, norm_mix, mod_l, l, w_in_ab, i, A_HEADS, A_WIDTH, n_ab,
                                 new_a)
            new_a, u = rest[:2], rest[2]
            if len(xs) == 2:
                xs = (rest[3],)
            mix = _ctx_attn_a(qkv, u, w_pool, pool_scale, i)
            mix = _lat_pool(u, w_pool, pool_scale, i, mix)
            mix = _lat_attn_a(qkv, cache_a_k, cache_a_v, bias_tiles, i, mix)
            w_out, wo_idx = w_out_ab, i
        else:
            j = l // 2
            qkv, *new_c = _inproj(xs, norm_mix, mod_l, l, w_in_c, j, C_KV_HEADS, C_Q_WIDTH, n_c,
                                  new_c, rope_tables)
            mix = _ctx_attn_c(qkv, sink_c, j)
            mix = _lat_attn_c(qkv, cache_c_k, cache_c_v, sink_c, j, mix)
            w_out, wo_idx = w_out_c, j
        if l + 1 < DEPTH:
            x, mod = _post_mixer(xs[0], mix, norm_ffn, mod_l, l, w_out, wo_idx, w_gate_up,
                                 w_down, next_mod_inputs=(cond8, w_mod, b_mod3))
            xs = (x,)
        else:
            y_ctx, y_lat = _post_mixer(xs[0], mix, norm_ffn, mod_l, l, w_out, wo_idx, w_gate_up,
                                       w_down, norm_final)

    new_caches = [jnp.transpose(t, (0, 1, 4, 2, 3)) for t in (*new_a, *new_c)]
    return (y_ctx.reshape(BATCH, SEQ, D_MODEL), y_lat.reshape(DEC_BATCH, DEC_SEQ, D_MODEL),
            *new_caches)
```
